```python
import math
import jax
import jax.numpy as jnp
from jax import lax
import numpy as np

D_MODEL = 2048
BATCH = 4
SEQ = 8192
DEPTH = 1

CHUNK = 64
QBLK = 128
RMS_EPS = 1e-6
ROPE_THETA = 500000.0

A_HEADS = 8
A_KV_HEADS = 2
A_HEAD_DIM = 128
A_ROT_DIM = A_HEAD_DIM // 4
IDX_HEADS = 16
IDX_DIM = 64
IDX_ROT_DIM = IDX_DIM // 4
TOPK_MAX = 256

B_HEADS = 8
B_HEAD_DIM = 128

N_EXPERTS = 32
TOP_K_EXPERTS = 4
D_FF = D_MODEL
SWIGLU_LIMIT = 7.0
SWIGLU_ALPHA = 1.702
MOE_BLOCK = 128

A_Q_W = A_HEADS * A_HEAD_DIM
A_KV_W = A_KV_HEADS * A_HEAD_DIM
B_W = B_HEADS * B_HEAD_DIM
IDX_Q_W = IDX_HEADS * IDX_DIM
IN_SPLITS = (A_Q_W, A_KV_W, A_KV_W, B_W, B_W, B_W, IDX_Q_W, IDX_DIM, IDX_HEADS, D_MODEL, D_MODEL)
IN_W = A_Q_W + 2 * A_KV_W + 3 * B_W + IDX_Q_W + IDX_DIM + IDX_HEADS + 2 * D_MODEL

kernel_name = 'hybrid_dsa_stickbreak_moe_block'


def rms_norm(x, g):
    xf = x.astype(jnp.float32)
    y = xf * lax.rsqrt(jnp.mean(xf * xf, axis=-1, keepdims=True) + RMS_EPS)
    return (y * g.astype(jnp.float32)).astype(x.dtype)


def partial_rope(x, positions, rot_dim):
    half = rot_dim // 2
    inv_freq = ROPE_THETA ** (-jnp.arange(half, dtype=jnp.float32) / half)
    ang = positions.astype(jnp.float32)[..., None] * inv_freq
    cos = jnp.cos(ang)[:, :, None, :]
    sin = jnp.sin(ang)[:, :, None, :]
    xf = x.astype(jnp.float32)
    x1, x2, rest = xf[..., :half], xf[..., half:rot_dim], xf[..., rot_dim:]
    out = jnp.concatenate([x1 * cos - x2 * sin, x2 * cos + x1 * sin, rest], axis=-1)
    return out.astype(x.dtype)


def split_cols(z):
    outs = []
    off = 0
    for w in IN_SPLITS:
        outs.append(z[..., off:off + w])
        off += w
    return outs


def dsa_mixer(q, k, v, qi, ki, wi):
    bsz, seq = q.shape[0], q.shape[1]
    topk = min(TOPK_MAX, seq // 4)
    n_blk = seq // QBLK
    grp = A_HEADS // A_KV_HEADS
    key_pos = jnp.arange(seq)
    bidx = jnp.arange(bsz)[:, None, None]
    ki32 = ki.astype(jnp.float32)

    def block(i):
        t0 = i * QBLK
        t = t0 + jnp.arange(QBLK)
        limit = (t // CHUNK + 1) * CHUNK
        admissible = key_pos[None, :] < limit[:, None]
        qi_b = lax.dynamic_slice_in_dim(qi, t0, QBLK, axis=1).astype(jnp.float32)
        wi_b = lax.dynamic_slice_in_dim(wi, t0, QBLK, axis=1).astype(jnp.float32)
        dots = jnp.einsum('bqhd,bsd->bhqs', qi_b, ki32) * (IDX_DIM ** -0.5)
        score = jnp.einsum('bhqs,bqh->bqs', jax.nn.relu(dots), wi_b) * (IDX_HEADS ** -0.5)
        score = jnp.where(admissible[None], score, -jnp.inf)
        _, sel = lax.top_k(score, topk)
        sel_ok = sel < limit[None, :, None]
        kg = k[bidx, sel]
        vg = v[bidx, sel]
        q_b = lax.dynamic_slice_in_dim(q, t0, QBLK, axis=1).reshape(bsz, QBLK, A_KV_HEADS, grp, A_HEAD_DIM)
        logits = jnp.einsum('bqjgd,bqnjd->bqjgn', q_b, kg).astype(jnp.float32) * (A_HEAD_DIM ** -0.5)
        logits = jnp.where(sel_ok[:, :, None, None, :], logits, -jnp.inf)
        p = jax.nn.softmax(logits, axis=-1).astype(v.dtype)
        o = jnp.einsum('bqjgn,bqnjd->bqjgd', p, vg)
        return o.reshape(bsz, QBLK, A_Q_W)

    out = lax.map(block, jnp.arange(n_blk))
    return jnp.transpose(out, (1, 0, 2, 3)).reshape(bsz, seq, A_Q_W)


def stick_breaking_mixer(q, k, v):
    bsz, seq = q.shape[0], q.shape[1]
    n_blk = seq // QBLK
    key_pos = jnp.arange(seq)

    def block(i):
        t0 = i * QBLK
        t = t0 + jnp.arange(QBLK)
        causal = key_pos[None, :] < t[:, None]
        q_b = lax.dynamic_slice_in_dim(q, t0, QBLK, axis=1)
        z = jnp.einsum('bqhd,bshd->bhqs', q_b, k).astype(jnp.float32) * (B_HEAD_DIM ** -0.5)
        log_beta = jax.nn.log_sigmoid(z)
        log_keep = jnp.where(causal, log_beta - z, 0.0)
        log_gap = lax.cumsum(log_keep, axis=3, reverse=True) - log_keep
        attn = jnp.where(causal, jnp.exp(log_beta + log_gap), 0.0).astype(v.dtype)
        return jnp.einsum('bhqs,bshd->bqhd', attn, v).reshape(bsz, QBLK, B_W)

    out = lax.map(block, jnp.arange(n_blk))
    return jnp.transpose(out, (1, 0, 2, 3)).reshape(bsz, seq, B_W)


def clamped_swiglu(hid):
    glu = jnp.minimum(hid[..., ::2], SWIGLU_LIMIT)
    lin = jnp.clip(hid[..., 1::2], -SWIGLU_LIMIT, SWIGLU_LIMIT)
    return glu * jax.nn.sigmoid(SWIGLU_ALPHA * glu) * (lin + 1.0)


def moe_ffn(h, w_router, b_router, w1, b1, w2, b2):
    n_tok, d = h.shape
    logits = (h @ w_router).astype(jnp.float32) + b_router.astype(jnp.float32)
    top_logit, top_e = lax.top_k(logits, TOP_K_EXPERTS)
    top_w = jax.nn.softmax(top_logit, axis=-1).astype(h.dtype)
    n_slots = n_tok * TOP_K_EXPERTS
    flat_e = top_e.reshape(-1)
    flat_tok = jnp.arange(n_slots, dtype=jnp.int32) // TOP_K_EXPERTS
    flat_w = top_w.reshape(-1)
    order = jnp.argsort(flat_e)
    se, stok, sw = flat_e[order], flat_tok[order], flat_w[order]
    counts = jnp.bincount(flat_e, length=N_EXPERTS)
    padded = (counts + MOE_BLOCK - 1) // MOE_BLOCK * MOE_BLOCK
    pad_end = jnp.cumsum(padded)
    pad_start = pad_end - padded
    start = jnp.cumsum(counts) - counts
    dest = pad_start[se] + jnp.arange(n_slots, dtype=jnp.int32) - start[se]
    n_blocks = -(-n_slots // MOE_BLOCK) + N_EXPERTS
    n_rows = n_blocks * MOE_BLOCK
    buf_tok = jnp.full((n_rows,), n_tok, dtype=jnp.int32).at[dest].set(stok)
    buf_w = jnp.zeros((n_rows,), dtype=h.dtype).at[dest].set(sw)
    block_e = jnp.minimum(jnp.searchsorted(pad_end, jnp.arange(n_blocks) * MOE_BLOCK, side='right'), N_EXPERTS - 1)
    h_pad = jnp.concatenate([h, jnp.zeros((1, d), h.dtype)], axis=0)

    def expert_block(args):
        tok, wgt, e = args
        hid = h_pad[tok] @ w1[e] + b1[e]
        out = clamped_swiglu(hid) @ w2[e] + b2[e]
        return out * wgt[:, None]

    out = lax.map(expert_block, (buf_tok.reshape(n_blocks, MOE_BLOCK), buf_w.reshape(n_blocks, MOE_BLOCK), block_e))
    y = jax.ops.segment_sum(out.reshape(n_rows, d), buf_tok, num_segments=n_tok + 1)
    return y[:n_tok]


def setup_inputs(seed: int = 0) -> dict:
    key = jax.random.key(seed)
    ks = jax.random.split(key, 20)
    f32 = jnp.float32
    nrm = jax.random.normal
    x = nrm(ks[0], (BATCH, SEQ, D_MODEL), f32)
    c = nrm(ks[1], (BATCH, D_MODEL), f32)
    offset = jax.random.randint(ks[2], (BATCH, 1), 0, 4096, dtype=jnp.int32)
    positions = offset + jnp.arange(SEQ, dtype=jnp.int32)[None, :]
    w_ada = nrm(ks[3], (DEPTH, D_MODEL, 6 * D_MODEL), f32) * (0.5 * D_MODEL ** -0.5)
    b_ada = nrm(ks[4], (DEPTH, 6 * D_MODEL), f32) * 0.02
    g_pre_mix = 1.0 + 0.05 * nrm(ks[5], (DEPTH, D_MODEL), f32)
    g_post_mix = 1.0 + 0.05 * nrm(ks[6], (DEPTH, D_MODEL), f32)
    w_in = nrm(ks[7], (DEPTH, D_MODEL, IN_W), f32) * D_MODEL ** -0.5
    w_branch_a = nrm(ks[8], (DEPTH, A_Q_W, D_MODEL), f32) * A_Q_W ** -0.5
    w_branch_b = nrm(ks[9], (DEPTH, B_W, D_MODEL), f32) * B_W ** -0.5
    w_out = nrm(ks[10], (DEPTH, D_MODEL, D_MODEL), f32) * D_MODEL ** -0.5
    g_pre_ffn = 1.0 + 0.05 * nrm(ks[11], (DEPTH, D_MODEL), f32)
    g_post_ffn = 1.0 + 0.05 * nrm(ks[12], (DEPTH, D_MODEL), f32)
    w_router = nrm(ks[13], (DEPTH, D_MODEL, N_EXPERTS), f32) * D_MODEL ** -0.5
    b_router = nrm(ks[14], (DEPTH, N_EXPERTS), f32) * 0.01
    w1 = nrm(ks[15], (DEPTH, N_EXPERTS, D_MODEL, 2 * D_FF), f32) * D_MODEL ** -0.5
    b1 = nrm(ks[16], (DEPTH, N_EXPERTS, 2 * D_FF), f32) * 0.02
    w2 = nrm(ks[17], (DEPTH, N_EXPERTS, D_FF, D_MODEL), f32) * D_FF ** -0.5
    b2 = nrm(ks[18], (DEPTH, N_EXPERTS, D_MODEL), f32) * 0.02
    return {'x': x, 'c': c, 'positions': positions, 'w_ada': w_ada, 'b_ada': b_ada,
            'g_pre_mix': g_pre_mix, 'g_post_mix': g_post_mix, 'w_in': w_in,
            'w_branch_a': w_branch_a, 'w_branch_b': w_branch_b, 'w_out': w_out,
            'g_pre_ffn': g_pre_ffn, 'g_post_ffn': g_post_ffn, 'w_router': w_router,
            'b_router': b_router, 'w1': w1, 'b1': b1, 'w2': w2, 'b2': b2}


def reference(x, c, positions, w_ada, b_ada, g_pre_mix, g_post_mix, w_in, w_branch_a, w_branch_b, w_out, g_pre_ffn, g_post_ffn, w_router, b_router, w1, b1, w2, b2):
    bsz, seq, d = x.shape
    for l in range(DEPTH):
        mod = jax.nn.silu(c) @ w_ada[l] + b_ada[l]
        sh1, sc1, ga1, sh2, sc2, ga2 = jnp.split(mod, 6, axis=-1)

        h = rms_norm(x, g_pre_mix[l]) * (1.0 + sc1[:, None, :]) + sh1[:, None, :]
        z = h @ w_in[l]
        qa, ka, va, qb, kb, vb, qi, ki, wi, gate_a, gate_b = split_cols(z)
        qa = partial_rope(qa.reshape(bsz, seq, A_HEADS, A_HEAD_DIM), positions, A_ROT_DIM)
        ka = partial_rope(ka.reshape(bsz, seq, A_KV_HEADS, A_HEAD_DIM), positions, A_ROT_DIM)
        va = va.reshape(bsz, seq, A_KV_HEADS, A_HEAD_DIM)
        qi = partial_rope(qi.reshape(bsz, seq, IDX_HEADS, IDX_DIM), positions, IDX_ROT_DIM)
        ki = partial_rope(ki[:, :, None, :], positions, IDX_ROT_DIM)[:, :, 0, :]
        o_a = dsa_mixer(qa, ka, va, qi, ki, wi)
        o_b = stick_breaking_mixer(qb.reshape(bsz, seq, B_HEADS, B_HEAD_DIM),
                                   kb.reshape(bsz, seq, B_HEADS, B_HEAD_DIM),
                                   vb.reshape(bsz, seq, B_HEADS, B_HEAD_DIM))
        merged = jax.nn.sigmoid(gate_a) * (o_a @ w_branch_a[l]) + jax.nn.sigmoid(gate_b) * (o_b @ w_branch_b[l])
        mix = merged @ w_out[l]
        x = x + ga1[:, None, :] * rms_norm(mix, g_post_mix[l])

        h2 = rms_norm(x, g_pre_ffn[l]) * (1.0 + sc2[:, None, :]) + sh2[:, None, :]
        f = moe_ffn(h2.reshape(bsz * seq, d), w_router[l], b_router[l], w1[l], b1[l], w2[l], b2[l]).reshape(bsz, seq, d)
        x = x + ga2[:, None, :] * rms_norm(f, g_post_ffn[l])
    return x
```

```python
import functools

import jax
import jax.numpy as jnp
from jax import lax
from jax.experimental import pallas as pl
from jax.experimental.pallas import tpu as pltpu

F32 = jnp.float32
BF16 = jnp.bfloat16
I32 = jnp.int32

CHUNK = 64
RMS_EPS = 1e-6
ROPE_THETA = 500000.0
A_HEADS = 8
A_KV_HEADS = 2
A_HEAD_DIM = 128
A_ROT_DIM = A_HEAD_DIM // 4
IDX_HEADS = 16
IDX_DIM = 64
IDX_ROT_DIM = IDX_DIM // 4
TOPK_MAX = 256
B_HEADS = 8
B_HEAD_DIM = 128
N_EXPERTS = 32
TOP_K_EXPERTS = 4
SWIGLU_LIMIT = 7.0
SWIGLU_ALPHA = 1.702

A_Q_W = A_HEADS * A_HEAD_DIM
A_KV_W = A_KV_HEADS * A_HEAD_DIM
B_W = B_HEADS * B_HEAD_DIM
IDX_Q_W = IDX_HEADS * IDX_DIM

LANES = 128
VMEM_LIMIT = 56 * 1024 * 1024
NEG_BIG = -1e30
INT_MIN = -(2 ** 31)
MOE_ROWS = 512


def _cparams(sem):
    return pltpu.CompilerParams(dimension_semantics=sem, vmem_limit_bytes=VMEM_LIMIT)


def _tile(n, pref):
    t = min(n, pref)
    assert n % t == 0, (n, t)
    return t


def _mod_kernel(c_ref, w_ref, b_ref, o_ref):
    c = c_ref[...]
    s = c * jax.nn.sigmoid(c)
    o_ref[...] = jnp.dot(s, w_ref[...], preferred_element_type=F32,
                         precision=lax.Precision.HIGHEST) + b_ref[...]


def _modulation(c, w_ada, b_ada):
    bsz, d = c.shape
    n = w_ada.shape[1]
    rows = 8
    c_pad = jnp.zeros((rows, d), F32).at[:bsz].set(c)
    tn = _tile(n, 1024)
    out = pl.pallas_call(
        _mod_kernel,
        grid=(n // tn,),
        in_specs=[pl.BlockSpec((rows, d), lambda j: (0, 0)),
                  pl.BlockSpec((d, tn), lambda j: (0, j)),
                  pl.BlockSpec((1, tn), lambda j: (0, j))],
        out_specs=pl.BlockSpec((rows, tn), lambda j: (0, j)),
        out_shape=jax.ShapeDtypeStruct((rows, n), F32),
        compiler_params=_cparams(("arbitrary",)),
        name="modulation",
    )(c_pad, w_ada, b_ada.reshape(1, n))
    return out[:bsz]


def _rope_table_kernel(pos_ref, fa_ref, sa_ref, fi_ref, si_ref, ca_o, sa_o, ci_o, si_o):
    pos = pos_ref[...].astype(F32)
    ang_a = pos * fa_ref[...]
    ca_o[...] = jnp.cos(ang_a)
    sa_o[...] = jnp.sin(ang_a) * sa_ref[...]
    ang_i = pos * fi_ref[...]
    ci_o[...] = jnp.cos(ang_i)
    si_o[...] = jnp.sin(ang_i) * si_ref[...]


def _rope_pattern(rot_dim, head_dim):
    half = rot_dim // 2
    inv_freq = ROPE_THETA ** (-jnp.arange(half, dtype=F32) / half)
    freq = jnp.concatenate([inv_freq, inv_freq, jnp.zeros((head_dim - rot_dim,), F32)])
    sign = jnp.concatenate([-jnp.ones((half,), F32), jnp.ones((half,), F32),
                            jnp.zeros((head_dim - rot_dim,), F32)])
    reps = LANES // head_dim
    return jnp.tile(freq, reps).reshape(1, LANES), jnp.tile(sign, reps).reshape(1, LANES)


def _rope_tables(positions):
    t = positions.size
    pos = positions.reshape(t, 1).astype(I32)
    fa, sa = _rope_pattern(A_ROT_DIM, A_HEAD_DIM)
    fi, si = _rope_pattern(IDX_ROT_DIM, IDX_DIM)
    tm = _tile(t, 1024)
    pat = pl.BlockSpec((1, LANES), lambda i: (0, 0))
    tab = pl.BlockSpec((tm, LANES), lambda i: (i, 0))
    return pl.pallas_call(
        _rope_table_kernel,
        grid=(t // tm,),
        in_specs=[pl.BlockSpec((tm, 1), lambda i: (i, 0)), pat, pat, pat, pat],
        out_specs=[tab, tab, tab, tab],
        out_shape=[jax.ShapeDtypeStruct((t, LANES), F32)] * 4,
        compiler_params=_cparams(("arbitrary",)),
        name="rope_tables",
    )(pos, fa, sa, fi, si)


def _prenorm_kernel(x_ref, g_ref, sc_ref, sh_ref, o_ref):
    x = x_ref[...]
    y = x * lax.rsqrt(jnp.mean(x * x, axis=-1, keepdims=True) + RMS_EPS)
    h = (y * g_ref[...]) * (1.0 + sc_ref[0]) + sh_ref[0]
    o_ref[...] = h.astype(o_ref.dtype)


def _prenorm(x2, g, scale, shift, seq, out_dtype):
    t, d = x2.shape
    bsz = scale.shape[0]
    tm = _tile(seq, 512)
    per_b = seq // tm
    row = pl.BlockSpec((tm, d), lambda i: (i, 0))
    bvec = pl.BlockSpec((1, 1, d), lambda i: (i // per_b, 0, 0))
    return pl.pallas_call(
        _prenorm_kernel,
        grid=(t // tm,),
        in_specs=[row, pl.BlockSpec((1, d), lambda i: (0, 0)), bvec, bvec],
        out_specs=row,
        out_shape=jax.ShapeDtypeStruct((t, d), out_dtype),
        compiler_params=_cparams(("arbitrary",)),
        name="prenorm",
    )(x2, g.reshape(1, d), scale.reshape(bsz, 1, d), shift.reshape(bsz, 1, d))


def _rotate(x, cos, sin, period, half):
    lane = lax.broadcasted_iota(I32, x.shape, 1) & (period - 1)
    swapped = jnp.where(lane < half, pltpu.roll(x, LANES - half, 1), pltpu.roll(x, half, 1))
    return x * cos + swapped * sin


def _proj_kernel(*refs, epilogue):
    if epilogue in ("plain", "sigmoid"):
        h_ref, w_ref, o_ref = refs
    else:
        h_ref, w_ref, cos_ref, sin_ref, o_ref = refs
    acc = jnp.dot(h_ref[...], w_ref[...], preferred_element_type=F32)
    if epilogue == "plain":
        o_ref[...] = acc.astype(o_ref.dtype)
    elif epilogue == "sigmoid":
        o_ref[...] = jax.nn.sigmoid(acc).astype(o_ref.dtype)
    else:
        cos = cos_ref[...]
        sin = sin_ref[...]
        if epilogue == "rope_a":
            period, half = A_HEAD_DIM, A_ROT_DIM // 2
        else:
            period, half = IDX_DIM, IDX_ROT_DIM // 2
        if epilogue == "rope_key_idx":
            lane = lax.broadcasted_iota(I32, cos.shape, 1)
            cos = jnp.where(lane < IDX_DIM, cos, 1.0)
            sin = jnp.where(lane < IDX_DIM, sin, 0.0)
        for c in range(acc.shape[1] // LANES):
            sl = slice(c * LANES, (c + 1) * LANES)
            o_ref[:, sl] = _rotate(acc[:, sl], cos, sin, period, half).astype(o_ref.dtype)


def _project(h, w, epilogue, out_dtype, tables=None, tn_pref=512):
    t, d = h.shape
    n = w.shape[1]
    tm = _tile(t, 1024)
    tn = n
    for cand in (tn_pref, 256, 128):
        if n % cand == 0:
            tn = cand
            break
    in_specs = [pl.BlockSpec((tm, d), lambda i, j: (i, 0)),
                pl.BlockSpec((d, tn), lambda i, j: (0, j))]
    args = [h, w]
    if tables is not None:
        tab = pl.BlockSpec((tm, LANES), lambda i, j: (i, 0))
        in_specs += [tab, tab]
        args += list(tables)
    return pl.pallas_call(
        functools.partial(_proj_kernel, epilogue=epilogue),
        grid=(t // tm, n // tn),
        in_specs=in_specs,
        out_specs=pl.BlockSpec((tm, tn), lambda i, j: (i, j)),
        out_shape=jax.ShapeDtypeStruct((t, n), out_dtype),
        compiler_params=_cparams(("arbitrary", "arbitrary")),
        name="proj_" + epilogue,
    )(*args)


def _dsa_kernel(qi_ref, kwq_ref, kwk_ref, qa_ref, ka_ref, va_ref, o_ref,
                keys_ref, kib_ref, *, tq, tk, topk):
    i = pl.program_id(1)
    t0 = i * tq
    n_kt = (t0 + tq + tk - 1) // tk
    grp = A_HEADS // A_KV_HEADS

    @pl.when(i == 0)
    def _():
        kib_ref[...] = kwk_ref[:, :IDX_DIM].astype(BF16)

    qi = qi_ref[...]
    wq = kwq_ref[...]
    row = t0 + lax.broadcasted_iota(I32, (tq, 1), 0)
    limit = (row // CHUNK + 1) * CHUNK

    def score_tile(kt, carry):
        k0 = pl.multiple_of(kt * tk, tk)
        kb = kib_ref[pl.ds(k0, tk), :]
        acc = jnp.zeros((tq, tk), F32)
        for h in range(IDX_HEADS):
            d = lax.dot_general(qi[:, h * IDX_DIM:(h + 1) * IDX_DIM], kb,
                                (((1,), (1,)), ((), ())), preferred_element_type=F32)
            acc = acc + jnp.maximum(d, 0.0) * wq[:, IDX_DIM + h:IDX_DIM + h + 1]
        acc = acc + 0.0
        bits = pltpu.bitcast(acc, I32)
        key = jnp.where(bits < 0, bits ^ 0x7FFFFFFF, bits)
        kpos = k0 + lax.broadcasted_iota(I32, (tq, tk), 1)
        keys_ref[kt] = jnp.where(kpos < limit, key, INT_MIN)
        return carry

    lax.fori_loop(0, n_kt, score_tile, 0)

    def bit_step(bi, lo):
        cand = lo + lax.shift_left(jnp.int32(1), 31 - bi)

        def count_tile(kt, cnt):
            ge = (keys_ref[kt] >= cand).astype(I32)
            for c in range(tk // LANES):
                cnt = cnt + ge[:, c * LANES:(c + 1) * LANES]
            return cnt

        cnt = lax.fori_loop(0, n_kt, count_tile, jnp.zeros((tq, LANES), I32))
        total = jnp.sum(cnt, axis=1, keepdims=True)
        return jnp.where(total >= topk, cand, lo)

    lo = lax.fori_loop(0, 32, bit_step, jnp.full((tq, 1), INT_MIN, I32))
    tau = jnp.maximum(lo, INT_MIN + 1)

    scale = A_HEAD_DIM ** -0.5
    for j in range(A_KV_HEADS):
        q = jnp.concatenate(
            [qa_ref[:, (j * grp + g) * A_HEAD_DIM:(j * grp + g + 1) * A_HEAD_DIM]
             for g in range(grp)], axis=0)

        def attn_tile(kt, carry, j=j, q=q):
            m, l, acc = carry
            k0 = pl.multiple_of(kt * tk, tk)
            kk = ka_ref[pl.ds(k0, tk), j * A_HEAD_DIM:(j + 1) * A_HEAD_DIM]
            vv = va_ref[pl.ds(k0, tk), j * A_HEAD_DIM:(j + 1) * A_HEAD_DIM]
            s = lax.dot_general(q, kk, (((1,), (1,)), ((), ())),
                                preferred_element_type=F32) * scale
            bias = jnp.where(keys_ref[kt] >= tau, 0.0, NEG_BIG)
            s = (s.reshape(grp, tq, tk) + bias[None]).reshape(grp * tq, tk)
            m_new = jnp.maximum(m, jnp.max(s, axis=1, keepdims=True))
            alpha = jnp.exp(m - m_new)
            p = jnp.exp(s - m_new)
            l = alpha * l + jnp.sum(p, axis=1, keepdims=True)
            acc = alpha * acc + jnp.dot(p.astype(BF16), vv, preferred_element_type=F32)
            return m_new, l, acc

        m0 = jnp.full((grp * tq, 1), NEG_BIG, F32)
        l0 = jnp.zeros((grp * tq, 1), F32)
        a0 = jnp.zeros((grp * tq, A_HEAD_DIM), F32)
        _, l, acc = lax.fori_loop(0, n_kt, attn_tile, (m0, l0, a0))
        out = acc / l
        for g in range(grp):
            h = j * grp + g
            o_ref[:, h * A_HEAD_DIM:(h + 1) * A_HEAD_DIM] = out[g * tq:(g + 1) * tq].astype(o_ref.dtype)


def _dsa_mixer(z_a, z_p, z_i, z_k, bsz, seq):
    t = bsz * seq
    tq = _tile(seq, 128)
    tk = _tile(seq, 512)
    nq = seq // tq
    topk = min(TOPK_MAX, seq // 4)
    return pl.pallas_call(
        functools.partial(_dsa_kernel, tq=tq, tk=tk, topk=topk),
        grid=(bsz, nq),
        in_specs=[
            pl.BlockSpec((tq, IDX_Q_W), lambda b, i: (b * nq + i, 0)),
            pl.BlockSpec((tq, LANES), lambda b, i: (b * nq + i, 0)),
            pl.BlockSpec((seq, LANES), lambda b, i: (b, 0)),
            pl.BlockSpec((tq, A_Q_W), lambda b, i: (b * nq + i, 0)),
            pl.BlockSpec((seq, A_KV_W), lambda b, i: (b, A_Q_W // A_KV_W)),
            pl.BlockSpec((seq, A_KV_W), lambda b, i: (b, 0)),
        ],
        out_specs=pl.BlockSpec((tq, A_Q_W), lambda b, i: (b * nq + i, 0)),
        out_shape=jax.ShapeDtypeStruct((t, A_Q_W), BF16),
        scratch_shapes=[pltpu.VMEM((seq // tk, tq, tk), I32),
                        pltpu.VMEM((seq, IDX_DIM), BF16)],
        compiler_params=_cparams(("arbitrary", "arbitrary")),
        name="dsa_mixer",
    )(z_i, z_k, z_k, z_a, z_a, z_p)


def _stick_kernel(q_ref, k_ref, v_ref, o_ref, *, tq):
    i = pl.program_id(2)
    tk = tq
    q = q_ref[...]
    scale = B_HEAD_DIM ** -0.5
    r_io = lax.broadcasted_iota(I32, (tk, tk), 0)
    c_io = lax.broadcasted_iota(I32, (tk, tk), 1)
    upper = jnp.where(r_io > c_io, 1.0, 0.0).astype(BF16)
    tpos = i * tq + lax.broadcasted_iota(I32, (tq, tk), 0)

    def tile(n, carry):
        run, acc = carry
        kt = i - n
        k0 = pl.multiple_of(kt * tk, tk)
        kk = k_ref[pl.ds(k0, tk), :]
        vv = v_ref[pl.ds(k0, tk), :]
        z = lax.dot_general(q, kk, (((1,), (1,)), ((), ())),
                            preferred_element_type=F32) * scale
        causal = (k0 + lax.broadcasted_iota(I32, (tq, tk), 1)) < tpos
        soft = jnp.log1p(jnp.exp(-jnp.abs(z)))
        log_beta = jnp.minimum(z, 0.0) - soft
        log_keep = jnp.where(causal, log_beta - z, 0.0)
        hi = log_keep.astype(BF16)
        lo = (log_keep - hi.astype(F32)).astype(BF16)
        gap = (jnp.dot(hi, upper, preferred_element_type=F32)
               + jnp.dot(lo, upper, preferred_element_type=F32))
        attn = jnp.where(causal, jnp.exp(log_beta + gap + run), 0.0)
        acc = acc + jnp.dot(attn.astype(BF16), vv, preferred_element_type=F32)
        run = run + jnp.sum(log_keep, axis=1, keepdims=True)
        return run, acc

    _, acc = lax.fori_loop(0, i + 1, tile,
                           (jnp.zeros((tq, 1), F32), jnp.zeros((tq, B_HEAD_DIM), F32)))
    o_ref[...] = acc.astype(o_ref.dtype)


def _stick_mixer(z_p, bsz, seq):
    t = bsz * seq
    tq = _tile(seq, 256)
    nq = seq // tq
    q_off = A_KV_W // B_HEAD_DIM
    k_off = q_off + B_HEADS
    v_off = k_off + B_HEADS
    return pl.pallas_call(
        functools.partial(_stick_kernel, tq=tq),
        grid=(bsz, B_HEADS, nq),
        in_specs=[
            pl.BlockSpec((tq, B_HEAD_DIM), lambda b, h, i: (b * nq + i, q_off + h)),
            pl.BlockSpec((seq, B_HEAD_DIM), lambda b, h, i: (b, k_off + h)),
            pl.BlockSpec((seq, B_HEAD_DIM), lambda b, h, i: (b, v_off + h)),
        ],
        out_specs=pl.BlockSpec((tq, B_HEAD_DIM), lambda b, h, i: (b * nq + i, h)),
        out_shape=jax.ShapeDtypeStruct((t, B_W), BF16),
        compiler_params=_cparams(("arbitrary", "arbitrary", "arbitrary")),
        name="stick_mixer",
    )(z_p, z_p, z_p)


def _merge_kernel(oa_ref, ob_ref, g_ref, x_ref, wa_ref, wb_ref, wo_ref, gpost_ref, ga_ref, o_ref):
    d = x_ref.shape[1]
    a = jnp.dot(oa_ref[...], wa_ref[...], preferred_element_type=F32)
    b = jnp.dot(ob_ref[...], wb_ref[...], preferred_element_type=F32)
    merged = g_ref[:, :d].astype(F32) * a + g_ref[:, d:].astype(F32) * b
    mix = jnp.dot(merged.astype(BF16), wo_ref[...], preferred_element_type=F32)
    y = mix * lax.rsqrt(jnp.mean(mix * mix, axis=-1, keepdims=True) + RMS_EPS)
    o_ref[...] = x_ref[...] + ga_ref[0] * (y * gpost_ref[...])


def _merge(o_a, o_b, z_g, x2, w_a, w_b, w_o, g_post, gate, seq):
    t, d = x2.shape
    bsz = gate.shape[0]
    tm = _tile(seq, 256)
    per_b = seq // tm
    const = lambda i: (0, 0)
    return pl.pallas_call(
        _merge_kernel,
        grid=(t // tm,),
        in_specs=[
            pl.BlockSpec((tm, A_Q_W), lambda i: (i, 0)),
            pl.BlockSpec((tm, B_W), lambda i: (i, 0)),
            pl.BlockSpec((tm, 2 * d), lambda i: (i, 0)),
            pl.BlockSpec((tm, d), lambda i: (i, 0)),
            pl.BlockSpec((A_Q_W, d), const),
            pl.BlockSpec((B_W, d), const),
            pl.BlockSpec((d, d), const),
            pl.BlockSpec((1, d), const),
            pl.BlockSpec((1, 1, d), lambda i: (i // per_b, 0, 0)),
        ],
        out_specs=pl.BlockSpec((tm, d), lambda i: (i, 0)),
        out_shape=jax.ShapeDtypeStruct((t, d), F32),
        compiler_params=_cparams(("arbitrary",)),
        name="merge",
    )(o_a, o_b, z_g, x2, w_a, w_b, w_o, g_post.reshape(1, d), gate.reshape(bsz, 1, d))


def _router_kernel(h_ref, wr_ref, br_ref, e_ref, w_ref, r_ref, cnt_ref, base_ref):
    i = pl.program_id(0)
    tm = h_ref.shape[0]

    @pl.when(i == 0)
    def _():
        base_ref[...] = jnp.zeros_like(base_ref)

    logits = jnp.dot(h_ref[...], wr_ref[...], preferred_element_type=F32,
                     precision=lax.Precision.HIGHEST) + br_ref[...]
    lane = lax.broadcasted_iota(I32, (tm, LANES), 1)
    work = logits
    picks, vals = [], []
    for _ in range(TOP_K_EXPERTS):
        m = jnp.max(work, axis=1, keepdims=True)
        idx = jnp.min(jnp.where(work == m, lane, LANES), axis=1, keepdims=True)
        picks.append(idx)
        vals.append(m)
        work = jnp.where(lane == idx, -jnp.inf, work)
    exps = [jnp.exp(v - vals[0]) for v in vals]
    den = exps[0]
    for e in exps[1:]:
        den = den + e

    onehot = jnp.zeros((tm, LANES), F32)
    for idx in picks:
        onehot = onehot + jnp.where(lane == idx, 1.0, 0.0)
    r_io = lax.broadcasted_iota(I32, (tm, tm), 0)
    c_io = lax.broadcasted_iota(I32, (tm, tm), 1)
    lower = jnp.where(c_io < r_io, 1.0, 0.0).astype(BF16)
    prefix = jnp.dot(lower, onehot.astype(BF16), preferred_element_type=F32)
    total = prefix + base_ref[0:1, :]

    e_out = jnp.zeros((tm, LANES), I32)
    w_out = jnp.zeros((tm, LANES), F32)
    r_out = jnp.zeros((tm, LANES), I32)
    for k in range(TOP_K_EXPERTS):
        rank = jnp.sum(jnp.where(lane == picks[k], total, 0.0), axis=1, keepdims=True)
        e_out = jnp.where(lane == k, picks[k], e_out)
        w_out = jnp.where(lane == k, exps[k] / den, w_out)
        r_out = jnp.where(lane == k, rank.astype(I32), r_out)
    e_ref[...] = e_out
    w_ref[...] = w_out
    r_ref[...] = r_out
    new_base = base_ref[0:1, :] + jnp.sum(onehot, axis=0, keepdims=True)
    base_ref[...] = jnp.broadcast_to(new_base, base_ref.shape)
    cnt_ref[...] = jnp.broadcast_to(new_base, cnt_ref.shape)


def _route(h2, w_router, b_router):
    t, d = h2.shape
    tm = _tile(t, 512)
    wr = jnp.zeros((d, LANES), F32).at[:, :N_EXPERTS].set(w_router)
    br = jnp.full((1, LANES), NEG_BIG, F32).at[0, :N_EXPERTS].set(b_router)
    row = pl.BlockSpec((tm, LANES), lambda i: (i, 0))
    return pl.pallas_call(
        _router_kernel,
        grid=(t // tm,),
        in_specs=[pl.BlockSpec((tm, d), lambda i: (i, 0)),
                  pl.BlockSpec((d, LANES), lambda i: (0, 0)),
                  pl.BlockSpec((1, LANES), lambda i: (0, 0))],
        out_specs=[row, row, row, pl.BlockSpec((8, LANES), lambda i: (0, 0))],
        out_shape=[jax.ShapeDtypeStruct((t, LANES), I32),
                   jax.ShapeDtypeStruct((t, LANES), F32),
                   jax.ShapeDtypeStruct((t, LANES), I32),
                   jax.ShapeDtypeStruct((8, LANES), F32)],
        scratch_shapes=[pltpu.VMEM((8, LANES), F32)],
        compiler_params=_cparams(("arbitrary",)),
        name="router",
    )(h2, wr, br)


def _dispatch_kernel(dest_ref, h_ref, init_ref, xs_ref, sem):
    del init_ref
    tm = h_ref.shape[0]

    def copy(t, k):
        return pltpu.make_async_copy(h_ref.at[pl.ds(t, 1)],
                                     xs_ref.at[pl.ds(dest_ref[0, 0, t * TOP_K_EXPERTS + k], 1)], sem)

    def issue(t, c):
        for k in range(TOP_K_EXPERTS):
            copy(t, k).start()
        return c

    def drain(t, c):
        for k in range(TOP_K_EXPERTS):
            copy(t, k).wait()
        return c

    lax.fori_loop(0, tm, issue, 0)
    lax.fori_loop(0, tm, drain, 0)


def _dispatch(h2, dest, n_rows):
    t, d = h2.shape
    tm = _tile(t, 256)
    nt = t // tm
    dest3 = dest.reshape(nt, 1, tm * TOP_K_EXPERTS)
    init = jnp.zeros((n_rows, d), h2.dtype)
    return pl.pallas_call(
        _dispatch_kernel,
        grid=(nt,),
        in_specs=[pl.BlockSpec((1, 1, tm * TOP_K_EXPERTS), lambda i: (i, 0, 0), memory_space=pltpu.SMEM),
                  pl.BlockSpec((tm, d), lambda i: (i, 0)),
                  pl.BlockSpec(memory_space=pl.ANY)],
        out_specs=pl.BlockSpec(memory_space=pl.ANY),
        out_shape=jax.ShapeDtypeStruct((n_rows, d), h2.dtype),
        scratch_shapes=[pltpu.SemaphoreType.DMA(())],
        input_output_aliases={2: 0},
        compiler_params=pltpu.CompilerParams(dimension_semantics=("arbitrary",),
                                             vmem_limit_bytes=VMEM_LIMIT, has_side_effects=True),
        name="moe_dispatch",
    )(dest3, h2, init)


def _expert_kernel(be_ref, nu_ref, x_ref, w1g_ref, w1l_ref, b1g_ref, b1l_ref, w2_ref, b2_ref,
                   o_ref, xb_ref, acc_ref):
    i = pl.program_id(0)
    j = pl.program_id(1)
    nf = pl.num_programs(1)
    valid = i < nu_ref[0]

    @pl.when(jnp.logical_and(valid, j == 0))
    def _():
        xb_ref[...] = x_ref[...].astype(BF16)
        acc_ref[...] = jnp.zeros_like(acc_ref)

    @pl.when(valid)
    def _():
        xb = xb_ref[...]
        glu = jnp.dot(xb, w1g_ref[0], preferred_element_type=F32) + b1g_ref[0]
        lin = jnp.dot(xb, w1l_ref[0], preferred_element_type=F32) + b1l_ref[0]
        glu = jnp.minimum(glu, SWIGLU_LIMIT)
        lin = jnp.clip(lin, -SWIGLU_LIMIT, SWIGLU_LIMIT)
        act = glu * jax.nn.sigmoid(SWIGLU_ALPHA * glu) * (lin + 1.0)
        acc_ref[...] += jnp.dot(act.astype(BF16), w2_ref[0], preferred_element_type=F32)

    @pl.when(jnp.logical_and(valid, j == nf - 1))
    def _():
        o_ref[...] = acc_ref[...] + b2_ref[0]


def _experts(xs, block_e, n_used, w1g, w1l, b1g, b1l, w2, b2):
    n_rows, d = xs.shape
    n_e, _, f = w1g.shape
    n_blocks = n_rows // MOE_ROWS
    tf = _tile(f, 512)
    nf = f // tf

    def blk(i, nu):
        return jnp.minimum(i, nu[0] - 1)

    def ftile(i, j, nu):
        return jnp.where(i < nu[0], j, nf - 1)

    grid_spec = pltpu.PrefetchScalarGridSpec(
        num_scalar_prefetch=2,
        grid=(n_blocks, nf),
        in_specs=[
            pl.BlockSpec((MOE_ROWS, d), lambda i, j, be, nu: (blk(i, nu), 0)),
            pl.BlockSpec((1, d, tf), lambda i, j, be, nu: (be[blk(i, nu)], 0, ftile(i, j, nu))),
            pl.BlockSpec((1, d, tf), lambda i, j, be, nu: (be[blk(i, nu)], 0, ftile(i, j, nu))),
            pl.BlockSpec((1, 1, tf), lambda i, j, be, nu: (be[blk(i, nu)], 0, ftile(i, j, nu))),
            pl.BlockSpec((1, 1, tf), lambda i, j, be, nu: (be[blk(i, nu)], 0, ftile(i, j, nu))),
            pl.BlockSpec((1, tf, d), lambda i, j, be, nu: (be[blk(i, nu)], ftile(i, j, nu), 0)),
            pl.BlockSpec((1, 1, d), lambda i, j, be, nu: (be[blk(i, nu)], 0, 0)),
        ],
        out_specs=pl.BlockSpec((MOE_ROWS, d), lambda i, j, be, nu: (blk(i, nu), 0)),
        scratch_shapes=[pltpu.VMEM((MOE_ROWS, d), BF16), pltpu.VMEM((MOE_ROWS, d), F32)],
    )
    return pl.pallas_call(
        _expert_kernel,
        grid_spec=grid_spec,
        out_shape=jax.ShapeDtypeStruct((n_rows, d), F32),
        compiler_params=_cparams(("arbitrary", "arbitrary")),
        name="moe_experts",
    )(block_e, n_used, xs, w1g, w1l, b1g, b1l, w2, b2)


def _combine_kernel(dest_ref, tw_ref, x_ref, gpost_ref, ga_ref, ys_ref, o_ref, buf_ref, sem):
    tm = x_ref.shape[0]

    def copy(t, k):
        return pltpu.make_async_copy(ys_ref.at[pl.ds(dest_ref[0, 0, t * TOP_K_EXPERTS + k], 1)],
                                     buf_ref.at[k, pl.ds(t, 1)], sem)

    def issue(t, c):
        for k in range(TOP_K_EXPERTS):
            copy(t, k).start()
        return c

    def drain(t, c):
        for k in range(TOP_K_EXPERTS):
            copy(t, k).wait()
        return c

    lax.fori_loop(0, tm, issue, 0)
    lax.fori_loop(0, tm, drain, 0)

    tw = tw_ref[...]
    f = tw[:, 0:1] * buf_ref[0]
    for k in range(1, TOP_K_EXPERTS):
        f = f + tw[:, k:k + 1] * buf_ref[k]
    y = f * lax.rsqrt(jnp.mean(f * f, axis=-1, keepdims=True) + RMS_EPS)
    o_ref[...] = x_ref[...] + ga_ref[0] * (y * gpost_ref[...])


def _combine(ys, dest, top_w, x1, g_post, gate, seq):
    t, d = x1.shape
    bsz = gate.shape[0]
    tm = _tile(seq, 256)
    nt = t // tm
    per_b = seq // tm
    dest3 = dest.reshape(nt, 1, tm * TOP_K_EXPERTS)
    return pl.pallas_call(
        _combine_kernel,
        grid=(nt,),
        in_specs=[pl.BlockSpec((1, 1, tm * TOP_K_EXPERTS), lambda i: (i, 0, 0), memory_space=pltpu.SMEM),
                  pl.BlockSpec((tm, LANES), lambda i: (i, 0)),
                  pl.BlockSpec((tm, d), lambda i: (i, 0)),
                  pl.BlockSpec((1, d), lambda i: (0, 0)),
                  pl.BlockSpec((1, 1, d), lambda i: (i // per_b, 0, 0)),
                  pl.BlockSpec(memory_space=pl.ANY)],
        out_specs=pl.BlockSpec((tm, d), lambda i: (i, 0)),
        out_shape=jax.ShapeDtypeStruct((t, d), F32),
        scratch_shapes=[pltpu.VMEM((TOP_K_EXPERTS, tm, d), F32), pltpu.SemaphoreType.DMA(())],
        compiler_params=_cparams(("arbitrary",)),
        name="moe_combine",
    )(dest3, top_w, x1, g_post.reshape(1, d), gate.reshape(bsz, 1, d), ys)


def _moe(h2, x1, w_router, b_router, w1, b1, w2, b2, g_post, gate, seq):
    t, d = h2.shape
    top_e, top_w, rank, counts = _route(h2, w_router, b_router)
    top_e = top_e[:, :TOP_K_EXPERTS]
    rank = rank[:, :TOP_K_EXPERTS]
    counts = counts[0, :N_EXPERTS].astype(I32)

    n_blocks = (t * TOP_K_EXPERTS) // MOE_ROWS + N_EXPERTS
    padded = (counts + MOE_ROWS - 1) // MOE_ROWS * MOE_ROWS
    pad_end = jnp.cumsum(padded)
    pad_start = pad_end - padded
    dest = (pad_start[top_e] + rank).astype(I32)
    block_e = jnp.minimum(jnp.searchsorted(pad_end, jnp.arange(n_blocks, dtype=I32) * MOE_ROWS,
                                           side="right"), N_EXPERTS - 1).astype(I32)
    n_used = (pad_end[-1:] // MOE_ROWS).astype(I32)

    xs = _dispatch(h2, dest, n_blocks * MOE_ROWS)
    w1g = w1[:, :, 0::2].astype(BF16)
    w1l = w1[:, :, 1::2].astype(BF16)
    n_e, f2 = b1.shape
    b1g = b1[:, 0::2].reshape(n_e, 1, f2 // 2)
    b1l = b1[:, 1::2].reshape(n_e, 1, f2 // 2)
    ys = _experts(xs, block_e, n_used, w1g, w1l, b1g, b1l, w2.astype(BF16), b2.reshape(n_e, 1, d))
    return _combine(ys, dest, top_w, x1, g_post, gate, seq)


def _layer(x2, c, tables, bsz, seq, w_ada, b_ada, g_pre_mix, g_post_mix, w_in, w_branch_a,
           w_branch_b, w_out, g_pre_ffn, g_post_ffn, w_router, b_router, w1, b1, w2, b2):
    d = x2.shape[1]
    mod = _modulation(c, w_ada, b_ada)
    sh1, sc1, ga1, sh2, sc2, ga2 = [mod[:, k * d:(k + 1) * d] for k in range(6)]
    cos_a, sin_a, cos_i, sin_i = tables

    h = _prenorm(x2, g_pre_mix, sc1, sh1, seq, BF16)
    wb = w_in.astype(BF16)
    o = 0
    w_qka = wb[:, o:o + A_Q_W + A_KV_W]
    o += A_Q_W + A_KV_W
    w_plain = wb[:, o:o + A_KV_W + 3 * B_W]
    o += A_KV_W + 3 * B_W
    w_qi = wb[:, o:o + IDX_Q_W]
    o += IDX_Q_W
    w_kw = jnp.zeros((d, LANES), BF16).at[:, :IDX_DIM + IDX_HEADS].set(wb[:, o:o + IDX_DIM + IDX_HEADS])
    o += IDX_DIM + IDX_HEADS
    w_gate = wb[:, o:o + 2 * d]

    z_a = _project(h, w_qka, "rope_a", BF16, (cos_a, sin_a), tn_pref=640)
    z_p = _project(h, w_plain, "plain", BF16, tn_pref=1664)
    z_i = _project(h, w_qi, "rope_idx", BF16, (cos_i, sin_i))
    z_k = _project(h, w_kw, "rope_key_idx", F32, (cos_i, sin_i))
    z_g = _project(h, w_gate, "sigmoid", BF16, tn_pref=1024)

    o_a = _dsa_mixer(z_a, z_p, z_i, z_k, bsz, seq)
    o_b = _stick_mixer(z_p, bsz, seq)
    x1 = _merge(o_a, o_b, z_g, x2, w_branch_a.astype(BF16), w_branch_b.astype(BF16),
                w_out.astype(BF16), g_post_mix, ga1, seq)

    h2 = _prenorm(x1, g_pre_ffn, sc2, sh2, seq, F32)
    return _moe(h2, x1, w_router, b_router, w1, b1, w2, b2, g_post_ffn, ga2, seq)


def kernel(x, c, positions, w_ada, b_ada, g_pre_mix, g_post_mix, w_in, w_branch_a, w_branch_b, w_out, g_pre_ffn, g_post_ffn, w_router, b_router, w1, b1, w2, b2):
    bsz, seq, d = x.shape
    x2 = x.reshape(bsz * seq, d)
    tables = _rope_tables(positions)
    for l in range(w_ada.shape[0]):
        x2 = _layer(x2, c, tables, bsz, seq, w_ada[l], b_ada[l], g_pre_mix[l], g_post_mix[l], w_in[l],
                    w_branch_a[l], w_branch_b[l], w_out[l], g_pre_ffn[l], g_post_ffn[l],
                    w_router[l], b_router[l], w1[l], b1[l], w2[l], b2[l])
    return x2.reshape(bsz, seq, d)
```

```python
import functools

import jax
import jax.numpy as jnp
from jax import lax
from jax.experimental import pallas as pl
from jax.experimental.pallas import tpu as pltpu

F32 = jnp.float32
BF16 = jnp.bfloat16
I32 = jnp.int32

CHUNK = 64
RMS_EPS = 1e-6
ROPE_THETA = 500000.0
A_HEADS = 8
A_KV_HEADS = 2
A_HEAD_DIM = 128
A_ROT_DIM = A_HEAD_DIM // 4
IDX_HEADS = 16
IDX_DIM = 64
IDX_ROT_DIM = IDX_DIM // 4
TOPK_MAX = 256
B_HEADS = 8
B_HEAD_DIM = 128
N_EXPERTS = 32
TOP_K_EXPERTS = 4
SWIGLU_LIMIT = 7.0
SWIGLU_ALPHA = 1.702

A_Q_W = A_HEADS * A_HEAD_DIM
A_KV_W = A_KV_HEADS * A_HEAD_DIM
B_W = B_HEADS * B_HEAD_DIM
IDX_Q_W = IDX_HEADS * IDX_DIM

LANES = 128
VMEM_LIMIT = 56 * 1024 * 1024
NEG_BIG = -1e30
INT_MIN = -(2 ** 31)
MOE_ROWS = 512


def _cparams(sem):
    return pltpu.CompilerParams(dimension_semantics=sem, vmem_limit_bytes=VMEM_LIMIT)


def _tile(n, pref):
    t = min(n, pref)
    assert n % t == 0, (n, t)
    return t


def _mod_kernel(c_ref, w_ref, b_ref, o_ref):
    c = c_ref[...]
    s = c * jax.nn.sigmoid(c)
    o_ref[...] = jnp.dot(s, w_ref[...], preferred_element_type=F32,
                         precision=lax.Precision.HIGHEST) + b_ref[...]


def _modulation(c, w_ada, b_ada):
    bsz, d = c.shape
    n = w_ada.shape[1]
    rows = 8
    c_pad = jnp.zeros((rows, d), F32).at[:bsz].set(c)
    tn = _tile(n, 1024)
    out = pl.pallas_call(
        _mod_kernel,
        grid=(n // tn,),
        in_specs=[pl.BlockSpec((rows, d), lambda j: (0, 0)),
                  pl.BlockSpec((d, tn), lambda j: (0, j)),
                  pl.BlockSpec((1, tn), lambda j: (0, j))],
        out_specs=pl.BlockSpec((rows, tn), lambda j: (0, j)),
        out_shape=jax.ShapeDtypeStruct((rows, n), F32),
        compiler_params=_cparams(("arbitrary",)),
        name="modulation",
    )(c_pad, w_ada, b_ada.reshape(1, n))
    return out[:bsz]


def _rope_table_kernel(pos_ref, fa_ref, sa_ref, fi_ref, si_ref, ca_o, sa_o, ci_o, si_o):
    pos = pos_ref[...].astype(F32)
    ang_a = pos * fa_ref[...]
    ca_o[...] = jnp.cos(ang_a)
    sa_o[...] = jnp.sin(ang_a) * sa_ref[...]
    ang_i = pos * fi_ref[...]
    ci_o[...] = jnp.cos(ang_i)
    si_o[...] = jnp.sin(ang_i) * si_ref[...]


def _rope_pattern(rot_dim, head_dim):
    half = rot_dim // 2
    inv_freq = ROPE_THETA ** (-jnp.arange(half, dtype=F32) / half)
    freq = jnp.concatenate([inv_freq, inv_freq, jnp.zeros((head_dim - rot_dim,), F32)])
    sign = jnp.concatenate([-jnp.ones((half,), F32), jnp.ones((half,), F32),
                            jnp.zeros((head_dim - rot_dim,), F32)])
    reps = LANES // head_dim
    return jnp.tile(freq, reps).reshape(1, LANES), jnp.tile(sign, reps).reshape(1, LANES)


def _rope_tables(positions):
    t = positions.size
    pos = positions.reshape(t, 1).astype(I32)
    fa, sa = _rope_pattern(A_ROT_DIM, A_HEAD_DIM)
    fi, si = _rope_pattern(IDX_ROT_DIM, IDX_DIM)
    tm = _tile(t, 1024)
    pat = pl.BlockSpec((1, LANES), lambda i: (0, 0))
    tab = pl.BlockSpec((tm, LANES), lambda i: (i, 0))
    return pl.pallas_call(
        _rope_table_kernel,
        grid=(t // tm,),
        in_specs=[pl.BlockSpec((tm, 1), lambda i: (i, 0)), pat, pat, pat, pat],
        out_specs=[tab, tab, tab, tab],
        out_shape=[jax.ShapeDtypeStruct((t, LANES), F32)] * 4,
        compiler_params=_cparams(("arbitrary",)),
        name="rope_tables",
    )(pos, fa, sa, fi, si)


def _prenorm_kernel(x_ref, g_ref, sc_ref, sh_ref, o_ref):
    x = x_ref[...]
    y = x * lax.rsqrt(jnp.mean(x * x, axis=-1, keepdims=True) + RMS_EPS)
    h = (y * g_ref[...]) * (1.0 + sc_ref[0]) + sh_ref[0]
    o_ref[...] = h.astype(o_ref.dtype)


def _prenorm(x2, g, scale, shift, seq, out_dtype):
    t, d = x2.shape
    bsz = scale.shape[0]
    tm = _tile(seq, 512)
    per_b = seq // tm
    row = pl.BlockSpec((tm, d), lambda i: (i, 0))
    bvec = pl.BlockSpec((1, 1, d), lambda i: (i // per_b, 0, 0))
    return pl.pallas_call(
        _prenorm_kernel,
        grid=(t // tm,),
        in_specs=[row, pl.BlockSpec((1, d), lambda i: (0, 0)), bvec, bvec],
        out_specs=row,
        out_shape=jax.ShapeDtypeStruct((t, d), out_dtype),
        compiler_params=_cparams(("arbitrary",)),
        name="prenorm",
    )(x2, g.reshape(1, d), scale.reshape(bsz, 1, d), shift.reshape(bsz, 1, d))


def _rotate(x, cos, sin, period, half):
    lane = lax.broadcasted_iota(I32, x.shape, 1) & (period - 1)
    swapped = jnp.where(lane < half, pltpu.roll(x, LANES - half, 1), pltpu.roll(x, half, 1))
    return x * cos + swapped * sin


def _proj_kernel(*refs, epilogue):
    if epilogue in ("plain", "sigmoid"):
        h_ref, w_ref, o_ref = refs
    else:
        h_ref, w_ref, cos_ref, sin_ref, o_ref = refs
    acc = jnp.dot(h_ref[...], w_ref[...], preferred_element_type=F32)
    if epilogue == "plain":
        o_ref[...] = acc.astype(o_ref.dtype)
    elif epilogue == "sigmoid":
        o_ref[...] = jax.nn.sigmoid(acc).astype(o_ref.dtype)
    else:
        cos = cos_ref[...]
        sin = sin_ref[...]
        if epilogue == "rope_a":
            period, half = A_HEAD_DIM, A_ROT_DIM // 2
        else:
            period, half = IDX_DIM, IDX_ROT_DIM // 2
        if epilogue == "rope_key_idx":
            lane = lax.broadcasted_iota(I32, cos.shape, 1)
            cos = jnp.where(lane < IDX_DIM, cos, 1.0)
            sin = jnp.where(lane < IDX_DIM, sin, 0.0)
        for c in range(acc.shape[1] // LANES):
            sl = slice(c * LANES, (c + 1) * LANES)
            o_ref[:, sl] = _rotate(acc[:, sl], cos, sin, period, half).astype(o_ref.dtype)


def _project(h, w, epilogue, out_dtype, tables=None, tn_pref=512):
    t, d = h.shape
    n = w.shape[1]
    tm = _tile(t, 1024)
    tn = n
    for cand in (tn_pref, 256, 128):
        if n % cand == 0:
            tn = cand
            break
    in_specs = [pl.BlockSpec((tm, d), lambda i, j: (i, 0)),
                pl.BlockSpec((d, tn), lambda i, j: (0, j))]
    args = [h, w]
    if tables is not None:
        tab = pl.BlockSpec((tm, LANES), lambda i, j: (i, 0))
        in_specs += [tab, tab]
        args += list(tables)
    return pl.pallas_call(
        functools.partial(_proj_kernel, epilogue=epilogue),
        grid=(t // tm, n // tn),
        in_specs=in_specs,
        out_specs=pl.BlockSpec((tm, tn), lambda i, j: (i, j)),
        out_shape=jax.ShapeDtypeStruct((t, n), out_dtype),
        compiler_params=_cparams(("arbitrary", "arbitrary")),
        name="proj_" + epilogue,
    )(*args)


def _dsa_kernel(qi_ref, kwq_ref, kwk_ref, qa_ref, ka_ref, va_ref, o_ref,
                keys_ref, kib_ref, *, tq, tk, topk):
    i = pl.program_id(1)
    t0 = i * tq
    n_kt = (t0 + tq + tk - 1) // tk
    grp = A_HEADS // A_KV_HEADS

    @pl.when(i == 0)
    def _():
        kib_ref[...] = kwk_ref[:, :IDX_DIM].astype(BF16)

    qi = qi_ref[...]
    wq = kwq_ref[...]
    row = t0 + lax.broadcasted_iota(I32, (tq, 1), 0)
    limit = (row // CHUNK + 1) * CHUNK

    def score_tile(kt, carry):
        k0 = pl.multiple_of(kt * tk, tk)
        kb = kib_ref[pl.ds(k0, tk), :]
        acc = jnp.zeros((tq, tk), F32)
        for h in range(IDX_HEADS):
            d = lax.dot_general(qi[:, h * IDX_DIM:(h + 1) * IDX_DIM], kb,
                                (((1,), (1,)), ((), ())), preferred_element_type=F32)
            acc = acc + jnp.maximum(d, 0.0) * wq[:, IDX_DIM + h:IDX_DIM + h + 1]
        acc = acc + 0.0
        bits = pltpu.bitcast(acc, I32)
        key = jnp.where(bits < 0, bits ^ 0x7FFFFFFF, bits)
        kpos = k0 + lax.broadcasted_iota(I32, (tq, tk), 1)
        keys_ref[kt] = jnp.where(kpos < limit, key, INT_MIN)
        return carry

    lax.fori_loop(0, n_kt, score_tile, 0)

    def bit_step(bi, lo):
        cand = lo + lax.shift_left(jnp.int32(1), 31 - bi)

        def count_tile(kt, cnt):
            ge = (keys_ref[kt] >= cand).astype(I32)
            for c in range(tk // LANES):
                cnt = cnt + ge[:, c * LANES:(c + 1) * LANES]
            return cnt

        cnt = lax.fori_loop(0, n_kt, count_tile, jnp.zeros((tq, LANES), I32))
        total = jnp.sum(cnt, axis=1, keepdims=True)
        return jnp.where(total >= topk, cand, lo)

    lo = lax.fori_loop(0, 32, bit_step, jnp.full((tq, 1), INT_MIN, I32))
    tau = jnp.maximum(lo, INT_MIN + 1)

    scale = A_HEAD_DIM ** -0.5
    for j in range(A_KV_HEADS):
        q = jnp.concatenate(
            [qa_ref[:, (j * grp + g) * A_HEAD_DIM:(j * grp + g + 1) * A_HEAD_DIM]
             for g in range(grp)], axis=0)

        def attn_tile(kt, carry, j=j, q=q):
            m, l, acc = carry
            k0 = pl.multiple_of(kt * tk, tk)
            kk = ka_ref[pl.ds(k0, tk), j * A_HEAD_DIM:(j + 1) * A_HEAD_DIM]
            vv = va_ref[pl.ds(k0, tk), j * A_HEAD_DIM:(j + 1) * A_HEAD_DIM]
            s = lax.dot_general(q, kk, (((1,), (1,)), ((), ())),
                                preferred_element_type=F32) * scale
            bias = jnp.where(keys_ref[kt] >= tau, 0.0, NEG_BIG)
            s = (s.reshape(grp, tq, tk) + bias[None]).reshape(grp * tq, tk)
            m_new = jnp.maximum(m, jnp.max(s, axis=1, keepdims=True))
            alpha = jnp.exp(m - m_new)
            p = jnp.exp(s - m_new)
            l = alpha * l + jnp.sum(p, axis=1, keepdims=True)
            acc = alpha * acc + jnp.dot(p.astype(BF16), vv, preferred_element_type=F32)
            return m_new, l, acc

        m0 = jnp.full((grp * tq, 1), NEG_BIG, F32)
        l0 = jnp.zeros((grp * tq, 1), F32)
        a0 = jnp.zeros((grp * tq, A_HEAD_DIM), F32)
        _, l, acc = lax.fori_loop(0, n_kt, attn_tile, (m0, l0, a0))
        out = acc / l
        for g in range(grp):
            h = j * grp + g
            o_ref[:, h * A_HEAD_DIM:(h + 1) * A_HEAD_DIM] = out[g * tq:(g + 1) * tq].astype(o_ref.dtype)


def _dsa_mixer(z_a, z_p, z_i, z_k, bsz, seq):
    t = bsz * seq
    tq = _tile(seq, 128)
    tk = _tile(seq, 512)
    nq = seq // tq
    topk = min(TOPK_MAX, seq // 4)
    return pl.pallas_call(
        functools.partial(_dsa_kernel, tq=tq, tk=tk, topk=topk),
        grid=(bsz, nq),
        in_specs=[
            pl.BlockSpec((tq, IDX_Q_W), lambda b, i: (b * nq + i, 0)),
            pl.BlockSpec((tq, LANES), lambda b, i: (b * nq + i, 0)),
            pl.BlockSpec((seq, LANES), lambda b, i: (b, 0)),
            pl.BlockSpec((tq, A_Q_W), lambda b, i: (b * nq + i, 0)),
            pl.BlockSpec((seq, A_KV_W), lambda b, i: (b, A_Q_W // A_KV_W)),
            pl.BlockSpec((seq, A_KV_W), lambda b, i: (b, 0)),
        ],
        out_specs=pl.BlockSpec((tq, A_Q_W), lambda b, i: (b * nq + i, 0)),
        out_shape=jax.ShapeDtypeStruct((t, A_Q_W), BF16),
        scratch_shapes=[pltpu.VMEM((seq // tk, tq, tk), I32),
                        pltpu.VMEM((seq, IDX_DIM), BF16)],
        compiler_params=_cparams(("arbitrary", "arbitrary")),
        name="dsa_mixer",
    )(z_i, z_k, z_k, z_a, z_a, z_p)


def _stick_kernel(q_ref, k_ref, v_ref, o_ref, *, tq):
    i = pl.program_id(2)
    tk = tq
    q = q_ref[...]
    scale = B_HEAD_DIM ** -0.5
    r_io = lax.broadcasted_iota(I32, (tk, tk), 0)
    c_io = lax.broadcasted_iota(I32, (tk, tk), 1)
    upper = jnp.where(r_io > c_io, 1.0, 0.0).astype(BF16)
    tpos = i * tq + lax.broadcasted_iota(I32, (tq, tk), 0)

    def tile(n, carry):
        run, acc = carry
        kt = i - n
        k0 = pl.multiple_of(kt * tk, tk)
        kk = k_ref[pl.ds(k0, tk), :]
        vv = v_ref[pl.ds(k0, tk), :]
        z = lax.dot_general(q, kk, (((1,), (1,)), ((), ())),
                            preferred_element_type=F32) * scale
        causal = (k0 + lax.broadcasted_iota(I32, (tq, tk), 1)) < tpos
        soft = jnp.log1p(jnp.exp(-jnp.abs(z)))
        log_beta = jnp.minimum(z, 0.0) - soft
        log_keep = jnp.where(causal, log_beta - z, 0.0)
        hi = log_keep.astype(BF16)
        lo = (log_keep - hi.astype(F32)).astype(BF16)
        gap = (jnp.dot(hi, upper, preferred_element_type=F32)
               + jnp.dot(lo, upper, preferred_element_type=F32))
        attn = jnp.where(causal, jnp.exp(log_beta + gap + run), 0.0)
        acc = acc + jnp.dot(attn.astype(BF16), vv, preferred_element_type=F32)
        run = run + jnp.sum(log_keep, axis=1, keepdims=True)
        return run, acc

    _, acc = lax.fori_loop(0, i + 1, tile,
                           (jnp.zeros((tq, 1), F32), jnp.zeros((tq, B_HEAD_DIM), F32)))
    o_ref[...] = acc.astype(o_ref.dtype)


def _stick_mixer(z_p, bsz, seq):
    t = bsz * seq
    tq = _tile(seq, 256)
    nq = seq // tq
    q_off = A_KV_W // B_HEAD_DIM
    k_off = q_off + B_HEADS
    v_off = k_off + B_HEADS
    return pl.pallas_call(
        functools.partial(_stick_kernel, tq=tq),
        grid=(bsz, B_HEADS, nq),
        in_specs=[
            pl.BlockSpec((tq, B_HEAD_DIM), lambda b, h, i: (b * nq + i, q_off + h)),
            pl.BlockSpec((seq, B_HEAD_DIM), lambda b, h, i: (b, k_off + h)),
            pl.BlockSpec((seq, B_HEAD_DIM), lambda b, h, i: (b, v_off + h)),
        ],
        out_specs=pl.BlockSpec((tq, B_HEAD_DIM), lambda b, h, i: (b * nq + i, h)),
        out_shape=jax.ShapeDtypeStruct((t, B_W), BF16),
        compiler_params=_cparams(("arbitrary", "arbitrary", "arbitrary")),
        name="stick_mixer",
    )(z_p, z_p, z_p)


def _merge_kernel(oa_ref, ob_ref, g_ref, x_ref, wa_ref, wb_ref, wo_ref, gpost_ref, ga_ref, o_ref):
    d = x_ref.shape[1]
    a = jnp.dot(oa_ref[...], wa_ref[...], preferred_element_type=F32)
    b = jnp.dot(ob_ref[...], wb_ref[...], preferred_element_type=F32)
    merged = g_ref[:, :d].astype(F32) * a + g_ref[:, d:].astype(F32) * b
    mix = jnp.dot(merged.astype(BF16), wo_ref[...], preferred_element_type=F32)
    y = mix * lax.rsqrt(jnp.mean(mix * mix, axis=-1, keepdims=True) + RMS_EPS)
    o_ref[...] = x_ref[...] + ga_ref[0] * (y * gpost_ref[...])


def _merge(o_a, o_b, z_g, x2, w_a, w_b, w_o, g_post, gate, seq):
    t, d = x2.shape
    bsz = gate.shape[0]
    tm = _tile(seq, 256)
    per_b = seq // tm
    const = lambda i: (0, 0)
    return pl.pallas_call(
        _merge_kernel,
        grid=(t // tm,),
        in_specs=[
            pl.BlockSpec((tm, A_Q_W), lambda i: (i, 0)),
            pl.BlockSpec((tm, B_W), lambda i: (i, 0)),
            pl.BlockSpec((tm, 2 * d), lambda i: (i, 0)),
            pl.BlockSpec((tm, d), lambda i: (i, 0)),
            pl.BlockSpec((A_Q_W, d), const),
            pl.BlockSpec((B_W, d), const),
            pl.BlockSpec((d, d), const),
            pl.BlockSpec((1, d), const),
            pl.BlockSpec((1, 1, d), lambda i: (i // per_b, 0, 0)),
        ],
        out_specs=pl.BlockSpec((tm, d), lambda i: (i, 0)),
        out_shape=jax.ShapeDtypeStruct((t, d), F32),
        compiler_params=_cparams(("arbitrary",)),
        name="merge",
    )(o_a, o_b, z_g, x2, w_a, w_b, w_o, g_post.reshape(1, d), gate.reshape(bsz, 1, d))


def _router_kernel(h_ref, wr_ref, br_ref, e_ref, w_ref, r_ref, cnt_ref, base_ref):
    i = pl.program_id(0)
    tm = h_ref.shape[0]

    @pl.when(i == 0)
    def _():
        base_ref[...] = jnp.zeros_like(base_ref)

    logits = jnp.dot(h_ref[...], wr_ref[...], preferred_element_type=F32,
                     precision=lax.Precision.HIGHEST) + br_ref[...]
    lane = lax.broadcasted_iota(I32, (tm, LANES), 1)
    work = logits
    picks, vals = [], []
    for _ in range(TOP_K_EXPERTS):
        m = jnp.max(work, axis=1, keepdims=True)
        idx = jnp.min(jnp.where(work == m, lane, LANES), axis=1, keepdims=True)
        picks.append(idx)
        vals.append(m)
        work = jnp.where(lane == idx, -jnp.inf, work)
    exps = [jnp.exp(v - vals[0]) for v in vals]
    den = exps[0]
    for e in exps[1:]:
        den = den + e

    onehot = jnp.zeros((tm, LANES), F32)
    for idx in picks:
        onehot = onehot + jnp.where(lane == idx, 1.0, 0.0)
    r_io = lax.broadcasted_iota(I32, (tm, tm), 0)
    c_io = lax.broadcasted_iota(I32, (tm, tm), 1)
    lower = jnp.where(c_io < r_io, 1.0, 0.0).astype(BF16)
    prefix = jnp.dot(lower, onehot.astype(BF16), preferred_element_type=F32)
    total = prefix + base_ref[0:1, :]

    e_out = jnp.zeros((tm, LANES), I32)
    w_out = jnp.zeros((tm, LANES), F32)
    r_out = jnp.zeros((tm, LANES), I32)
    for k in range(TOP_K_EXPERTS):
        rank = jnp.sum(jnp.where(lane == picks[k], total, 0.0), axis=1, keepdims=True)
        e_out = jnp.where(lane == k, picks[k], e_out)
        w_out = jnp.where(lane == k, exps[k] / den, w_out)
        r_out = jnp.where(lane == k, rank.astype(I32), r_out)
    e_ref[...] = e_out
    w_ref[...] = w_out
    r_ref[...] = r_out
    new_base = base_ref[0:1, :] + jnp.sum(onehot, axis=0, keepdims=True)
    base_ref[...] = jnp.broadcast_to(new_base, base_ref.shape)
    cnt_ref[...] = jnp.broadcast_to(new_base, cnt_ref.shape)


def _route(h2, w_router, b_router):
    t, d = h2.shape
    tm = _tile(t, 512)
    wr = jnp.zeros((d, LANES), F32).at[:, :N_EXPERTS].set(w_router)
    br = jnp.full((1, LANES), NEG_BIG, F32).at[0, :N_EXPERTS].set(b_router)
    row = pl.BlockSpec((tm, LANES), lambda i: (i, 0))
    return pl.pallas_call(
        _router_kernel,
        grid=(t // tm,),
        in_specs=[pl.BlockSpec((tm, d), lambda i: (i, 0)),
                  pl.BlockSpec((d, LANES), lambda i: (0, 0)),
                  pl.BlockSpec((1, LANES), lambda i: (0, 0))],
        out_specs=[row, row, row, pl.BlockSpec((8, LANES), lambda i: (0, 0))],
        out_shape=[jax.ShapeDtypeStruct((t, LANES), I32),
                   jax.ShapeDtypeStruct((t, LANES), F32),
                   jax.ShapeDtypeStruct((t, LANES), I32),
                   jax.ShapeDtypeStruct((8, LANES), F32)],
        scratch_shapes=[pltpu.VMEM((8, LANES), F32)],
        compiler_params=_cparams(("arbitrary",)),
        name="router",
    )(h2, wr, br)


def _dispatch_kernel(dest_ref, h_ref, init_ref, xs_ref, sem):
    del init_ref
    tm = h_ref.shape[0]

    def copy(t, k):
        return pltpu.make_async_copy(h_ref.at[pl.ds(t, 1)],
                                     xs_ref.at[pl.ds(dest_ref[0, 0, t * TOP_K_EXPERTS + k], 1)], sem)

    def issue(t, c):
        for k in range(TOP_K_EXPERTS):
            copy(t, k).start()
        return c

    def drain(t, c):
        for k in range(TOP_K_EXPERTS):
            copy(t, k).wait()
        return c

    lax.fori_loop(0, tm, issue, 0)
    lax.fori_loop(0, tm, drain, 0)


def _dispatch(h2, dest, n_rows):
    t, d = h2.shape
    tm = _tile(t, 256)
    nt = t // tm
    dest3 = dest.reshape(nt, 1, tm * TOP_K_EXPERTS)
    init = jnp.zeros((n_rows, d), h2.dtype)
    return pl.pallas_call(
        _dispatch_kernel,
        grid=(nt,),
        in_specs=[pl.BlockSpec((1, 1, tm * TOP_K_EXPERTS), lambda i: (i, 0, 0), memory_space=pltpu.SMEM),
                  pl.BlockSpec((tm, d), lambda i: (i, 0)),
                  pl.BlockSpec(memory_space=pl.ANY)],
        out_specs=pl.BlockSpec(memory_space=pl.ANY),
        out_shape=jax.ShapeDtypeStruct((n_rows, d), h2.dtype),
        scratch_shapes=[pltpu.SemaphoreType.DMA(())],
        input_output_aliases={2: 0},
        compiler_params=pltpu.CompilerParams(dimension_semantics=("arbitrary",),
                                             vmem_limit_bytes=VMEM_LIMIT, has_side_effects=True),
        name="moe_dispatch",
    )(dest3, h2, init)


def _deinterleave_kernel(w_ref, p_ref, g_ref, l_ref):
    p = p_ref[...]
    width = p.shape[0]
    half = width // 2
    for c in range(w_ref.shape[2] // width):
        w = w_ref[0, :, c * width:(c + 1) * width].astype(BF16)
        r = jnp.dot(w, p, preferred_element_type=F32)
        g_ref[0, :, c * half:(c + 1) * half] = r[:, :half].astype(BF16)
        l_ref[0, :, c * half:(c + 1) * half] = r[:, half:].astype(BF16)


def _deinterleave(w1):
    n_e, d, f2 = w1.shape
    width = 2 * LANES
    src = jnp.arange(width)
    dst = jnp.where(src % 2 == 0, src // 2, LANES + src // 2)
    perm = (dst[:, None] == jnp.arange(width)[None, :]).astype(BF16)
    tr = _tile(d, 256)
    out = pl.BlockSpec((1, tr, f2 // 2), lambda e, r: (e, r, 0))
    return pl.pallas_call(
        _deinterleave_kernel,
        grid=(n_e, d // tr),
        in_specs=[pl.BlockSpec((1, tr, f2), lambda e, r: (e, r, 0)),
                  pl.BlockSpec((width, width), lambda e, r: (0, 0))],
        out_specs=[out, out],
        out_shape=[jax.ShapeDtypeStruct((n_e, d, f2 // 2), BF16)] * 2,
        compiler_params=_cparams(("arbitrary", "arbitrary")),
        name="w1_deinterleave",
    )(w1, perm)


def _expert_kernel(be_ref, nu_ref, x_ref, w1g_ref, w1l_ref, b1g_ref, b1l_ref, w2_ref, b2_ref,
                   o_ref, xb_ref, acc_ref):
    i = pl.program_id(0)
    j = pl.program_id(1)
    nf = pl.num_programs(1)
    valid = i < nu_ref[0]

    @pl.when(jnp.logical_and(valid, j == 0))
    def _():
        xb_ref[...] = x_ref[...].astype(BF16)
        acc_ref[...] = jnp.zeros_like(acc_ref)

    @pl.when(valid)
    def _():
        xb = xb_ref[...]
        glu = jnp.dot(xb, w1g_ref[0], preferred_element_type=F32) + b1g_ref[0]
        lin = jnp.dot(xb, w1l_ref[0], preferred_element_type=F32) + b1l_ref[0]
        glu = jnp.minimum(glu, SWIGLU_LIMIT)
        lin = jnp.clip(lin, -SWIGLU_LIMIT, SWIGLU_LIMIT)
        act = glu * jax.nn.sigmoid(SWIGLU_ALPHA * glu) * (lin + 1.0)
        acc_ref[...] += jnp.dot(act.astype(BF16), w2_ref[0], preferred_element_type=F32)

    @pl.when(jnp.logical_and(valid, j == nf - 1))
    def _():
        o_ref[...] = acc_ref[...] + b2_ref[0]


def _experts(xs, block_e, n_used, w1g, w1l, b1g, b1l, w2, b2):
    n_rows, d = xs.shape
    n_e, _, f = w1g.shape
    n_blocks = n_rows // MOE_ROWS
    tf = _tile(f, 512)
    nf = f // tf

    def blk(i, nu):
        return jnp.minimum(i, nu[0] - 1)

    def ftile(i, j, nu):
        return jnp.where(i < nu[0], j, nf - 1)

    grid_spec = pltpu.PrefetchScalarGridSpec(
        num_scalar_prefetch=2,
        grid=(n_blocks, nf),
        in_specs=[
            pl.BlockSpec((MOE_ROWS, d), lambda i, j, be, nu: (blk(i, nu), 0)),
            pl.BlockSpec((1, d, tf), lambda i, j, be, nu: (be[blk(i, nu)], 0, ftile(i, j, nu))),
            pl.BlockSpec((1, d, tf), lambda i, j, be, nu: (be[blk(i, nu)], 0, ftile(i, j, nu))),
            pl.BlockSpec((1, 1, tf), lambda i, j, be, nu: (be[blk(i, nu)], 0, ftile(i, j, nu))),
            pl.BlockSpec((1, 1, tf), lambda i, j, be, nu: (be[blk(i, nu)], 0, ftile(i, j, nu))),
            pl.BlockSpec((1, tf, d), lambda i, j, be, nu: (be[blk(i, nu)], ftile(i, j, nu), 0)),
            pl.BlockSpec((1, 1, d), lambda i, j, be, nu: (be[blk(i, nu)], 0, 0)),
        ],
        out_specs=pl.BlockSpec((MOE_ROWS, d), lambda i, j, be, nu: (blk(i, nu), 0)),
        scratch_shapes=[pltpu.VMEM((MOE_ROWS, d), BF16), pltpu.VMEM((MOE_ROWS, d), F32)],
    )
    return pl.pallas_call(
        _expert_kernel,
        grid_spec=grid_spec,
        out_shape=jax.ShapeDtypeStruct((n_rows, d), F32),
        compiler_params=_cparams(("arbitrary", "arbitrary")),
        name="moe_experts",
    )(block_e, n_used, xs, w1g, w1l, b1g, b1l, w2, b2)


def _combine_kernel(dest_ref, tw_ref, x_ref, gpost_ref, ga_ref, ys_ref, o_ref, buf_ref, sem):
    tm = x_ref.shape[0]

    def copy(t, k):
        return pltpu.make_async_copy(ys_ref.at[pl.ds(dest_ref[0, 0, t * TOP_K_EXPERTS + k], 1)],
                                     buf_ref.at[k, pl.ds(t, 1)], sem)

    def issue(t, c):
        for k in range(TOP_K_EXPERTS):
            copy(t, k).start()
        return c

    def drain(t, c):
        for k in range(TOP_K_EXPERTS):
            copy(t, k).wait()
        return c

    lax.fori_loop(0, tm, issue, 0)
    lax.fori_loop(0, tm, drain, 0)

    tw = tw_ref[...]
    f = tw[:, 0:1] * buf_ref[0]
    for k in range(1, TOP_K_EXPERTS):
        f = f + tw[:, k:k + 1] * buf_ref[k]
    y = f * lax.rsqrt(jnp.mean(f * f, axis=-1, keepdims=True) + RMS_EPS)
    o_ref[...] = x_ref[...] + ga_ref[0] * (y * gpost_ref[...])


def _combine(ys, dest, top_w, x1, g_post, gate, seq):
    t, d = x1.shape
    bsz = gate.shape[0]
    tm = _tile(seq, 256)
    nt = t // tm
    per_b = seq // tm
    dest3 = dest.reshape(nt, 1, tm * TOP_K_EXPERTS)
    return pl.pallas_call(
        _combine_kernel,
        grid=(nt,),
        in_specs=[pl.BlockSpec((1, 1, tm * TOP_K_EXPERTS), lambda i: (i, 0, 0), memory_space=pltpu.SMEM),
                  pl.BlockSpec((tm, LANES), lambda i: (i, 0)),
                  pl.BlockSpec((tm, d), lambda i: (i, 0)),
                  pl.BlockSpec((1, d), lambda i: (0, 0)),
                  pl.BlockSpec((1, 1, d), lambda i: (i // per_b, 0, 0)),
                  pl.BlockSpec(memory_space=pl.ANY)],
        out_specs=pl.BlockSpec((tm, d), lambda i: (i, 0)),
        out_shape=jax.ShapeDtypeStruct((t, d), F32),
        scratch_shapes=[pltpu.VMEM((TOP_K_EXPERTS, tm, d), F32), pltpu.SemaphoreType.DMA(())],
        compiler_params=_cparams(("arbitrary",)),
        name="moe_combine",
    )(dest3, top_w, x1, g_post.reshape(1, d), gate.reshape(bsz, 1, d), ys)


def _moe(h2, x1, w_router, b_router, w1, b1, w2, b2, g_post, gate, seq):
    t, d = h2.shape
    top_e, top_w, rank, counts = _route(h2, w_router, b_router)
    top_e = top_e[:, :TOP_K_EXPERTS]
    rank = rank[:, :TOP_K_EXPERTS]
    counts = counts[0, :N_EXPERTS].astype(I32)

    n_blocks = (t * TOP_K_EXPERTS) // MOE_ROWS + N_EXPERTS
    padded = (counts + MOE_ROWS - 1) // MOE_ROWS * MOE_ROWS
    pad_end = jnp.cumsum(padded)
    pad_start = pad_end - padded
    dest = (pad_start[top_e] + rank).astype(I32)
    block_e = jnp.minimum(jnp.searchsorted(pad_end, jnp.arange(n_blocks, dtype=I32) * MOE_ROWS,
                                           side="right"), N_EXPERTS - 1).astype(I32)
    n_used = (pad_end[-1:] // MOE_ROWS).astype(I32)

    xs = _dispatch(h2, dest, n_blocks * MOE_ROWS)
    w1g, w1l = _deinterleave(w1)
    n_e, f2 = b1.shape
    b1g = b1[:, 0::2].reshape(n_e, 1, f2 // 2)
    b1l = b1[:, 1::2].reshape(n_e, 1, f2 // 2)
    ys = _experts(xs, block_e, n_used, w1g, w1l, b1g, b1l, w2.astype(BF16), b2.reshape(n_e, 1, d))
    return _combine(ys, dest, top_w, x1, g_post, gate, seq)


def _layer(x2, c, tables, bsz, seq, w_ada, b_ada, g_pre_mix, g_post_mix, w_in, w_branch_a,
           w_branch_b, w_out, g_pre_ffn, g_post_ffn, w_router, b_router, w1, b1, w2, b2):
    d = x2.shape[1]
    mod = _modulation(c, w_ada, b_ada)
    sh1, sc1, ga1, sh2, sc2, ga2 = [mod[:, k * d:(k + 1) * d] for k in range(6)]
    cos_a, sin_a, cos_i, sin_i = tables

    h = _prenorm(x2, g_pre_mix, sc1, sh1, seq, BF16)
    wb = w_in.astype(BF16)
    o = 0
    w_qka = wb[:, o:o + A_Q_W + A_KV_W]
    o += A_Q_W + A_KV_W
    w_plain = wb[:, o:o + A_KV_W + 3 * B_W]
    o += A_KV_W + 3 * B_W
    w_qi = wb[:, o:o + IDX_Q_W]
    o += IDX_Q_W
    w_kw = jnp.zeros((d, LANES), BF16).at[:, :IDX_DIM + IDX_HEADS].set(wb[:, o:o + IDX_DIM + IDX_HEADS])
    o += IDX_DIM + IDX_HEADS
    w_gate = wb[:, o:o + 2 * d]

    z_a = _project(h, w_qka, "rope_a", BF16, (cos_a, sin_a), tn_pref=640)
    z_p = _project(h, w_plain, "plain", BF16, tn_pref=1664)
    z_i = _project(h, w_qi, "rope_idx", BF16, (cos_i, sin_i))
    z_k = _project(h, w_kw, "rope_key_idx", F32, (cos_i, sin_i))
    z_g = _project(h, w_gate, "sigmoid", BF16, tn_pref=1024)

    o_a = _dsa_mixer(z_a, z_p, z_i, z_k, bsz, seq)
    o_b = _stick_mixer(z_p, bsz, seq)
    x1 = _merge(o_a, o_b, z_g, x2, w_branch_a.astype(BF16), w_branch_b.astype(BF16),
                w_out.astype(BF16), g_post_mix, ga1, seq)

    h2 = _prenorm(x1, g_pre_ffn, sc2, sh2, seq, F32)
    return _moe(h2, x1, w_router, b_router, w1, b1, w2, b2, g_post_ffn, ga2, seq)


def kernel(x, c, positions, w_ada, b_ada, g_pre_mix, g_post_mix, w_in, w_branch_a, w_branch_b, w_out, g_pre_ffn, g_post_ffn, w_router, b_router, w1, b1, w2, b2):
    bsz, seq, d = x.shape
    x2 = x.reshape(bsz * seq, d)
    tables = _rope_tables(positions)
    for l in range(w_ada.shape[0]):
        x2 = _layer(x2, c, tables, bsz, seq, w_ada[l], b_ada[l], g_pre_mix[l], g_post_mix[l], w_in[l],
                    w_branch_a[l], w_branch_b[l], w_out[l], g_pre_ffn[l], g_post_ffn[l],
                    w_router[l], b_router[l], w1[l], b1[l], w2[l], b2[l])
    return x2.reshape(bsz, seq, d)
```

```python
import functools

import jax
import jax.numpy as jnp
from jax import lax
from jax.experimental import pallas as pl
from jax.experimental.pallas import tpu as pltpu

F32 = jnp.float32
BF16 = jnp.bfloat16
I32 = jnp.int32

CHUNK = 64
RMS_EPS = 1e-6
ROPE_THETA = 500000.0
A_HEADS = 8
A_KV_HEADS = 2
A_HEAD_DIM = 128
A_ROT_DIM = A_HEAD_DIM // 4
IDX_HEADS = 16
IDX_DIM = 64
IDX_ROT_DIM = IDX_DIM // 4
TOPK_MAX = 256
B_HEADS = 8
B_HEAD_DIM = 128
N_EXPERTS = 32
TOP_K_EXPERTS = 4
SWIGLU_LIMIT = 7.0
SWIGLU_ALPHA = 1.702

A_Q_W = A_HEADS * A_HEAD_DIM
A_KV_W = A_KV_HEADS * A_HEAD_DIM
B_W = B_HEADS * B_HEAD_DIM
IDX_Q_W = IDX_HEADS * IDX_DIM

LANES = 128
VMEM_LIMIT = 56 * 1024 * 1024
NEG_BIG = -1e30
INT_MIN = -(2 ** 31)
MOE_ROWS = 512


def _cparams(sem):
    return pltpu.CompilerParams(dimension_semantics=sem, vmem_limit_bytes=VMEM_LIMIT)


def _tile(n, pref):
    t = min(n, pref)
    assert n % t == 0, (n, t)
    return t


def _mod_kernel(c_ref, w_ref, b_ref, o_ref):
    c = c_ref[...]
    s = c * jax.nn.sigmoid(c)
    o_ref[...] = jnp.dot(s, w_ref[...], preferred_element_type=F32,
                         precision=lax.Precision.HIGHEST) + b_ref[...]


def _modulation(c, w_ada, b_ada):
    bsz, d = c.shape
    n = w_ada.shape[1]
    rows = 8
    c_pad = jnp.zeros((rows, d), F32).at[:bsz].set(c)
    tn = _tile(n, 1024)
    out = pl.pallas_call(
        _mod_kernel,
        grid=(n // tn,),
        in_specs=[pl.BlockSpec((rows, d), lambda j: (0, 0)),
                  pl.BlockSpec((d, tn), lambda j: (0, j)),
                  pl.BlockSpec((1, tn), lambda j: (0, j))],
        out_specs=pl.BlockSpec((rows, tn), lambda j: (0, j)),
        out_shape=jax.ShapeDtypeStruct((rows, n), F32),
        compiler_params=_cparams(("arbitrary",)),
        name="modulation",
    )(c_pad, w_ada, b_ada.reshape(1, n))
    return out[:bsz]


def _rope_table_kernel(pos_ref, fa_ref, sa_ref, fi_ref, si_ref, ca_o, sa_o, ci_o, si_o):
    pos = pos_ref[...].astype(F32)
    ang_a = pos * fa_ref[...]
    ca_o[...] = jnp.cos(ang_a)
    sa_o[...] = jnp.sin(ang_a) * sa_ref[...]
    ang_i = pos * fi_ref[...]
    ci_o[...] = jnp.cos(ang_i)
    si_o[...] = jnp.sin(ang_i) * si_ref[...]


def _rope_pattern(rot_dim, head_dim):
    half = rot_dim // 2
    inv_freq = ROPE_THETA ** (-jnp.arange(half, dtype=F32) / half)
    freq = jnp.concatenate([inv_freq, inv_freq, jnp.zeros((head_dim - rot_dim,), F32)])
    sign = jnp.concatenate([-jnp.ones((half,), F32), jnp.ones((half,), F32),
                            jnp.zeros((head_dim - rot_dim,), F32)])
    reps = LANES // head_dim
    return jnp.tile(freq, reps).reshape(1, LANES), jnp.tile(sign, reps).reshape(1, LANES)


def _rope_tables(positions):
    t = positions.size
    pos = positions.reshape(t, 1).astype(I32)
    fa, sa = _rope_pattern(A_ROT_DIM, A_HEAD_DIM)
    fi, si = _rope_pattern(IDX_ROT_DIM, IDX_DIM)
    tm = _tile(t, 1024)
    pat = pl.BlockSpec((1, LANES), lambda i: (0, 0))
    tab = pl.BlockSpec((tm, LANES), lambda i: (i, 0))
    return pl.pallas_call(
        _rope_table_kernel,
        grid=(t // tm,),
        in_specs=[pl.BlockSpec((tm, 1), lambda i: (i, 0)), pat, pat, pat, pat],
        out_specs=[tab, tab, tab, tab],
        out_shape=[jax.ShapeDtypeStruct((t, LANES), F32)] * 4,
        compiler_params=_cparams(("arbitrary",)),
        name="rope_tables",
    )(pos, fa, sa, fi, si)


def _prenorm_kernel(x_ref, g_ref, sc_ref, sh_ref, o_ref):
    x = x_ref[...]
    y = x * lax.rsqrt(jnp.mean(x * x, axis=-1, keepdims=True) + RMS_EPS)
    h = (y * g_ref[...]) * (1.0 + sc_ref[0]) + sh_ref[0]
    o_ref[...] = h.astype(o_ref.dtype)


def _prenorm(x2, g, scale, shift, seq, out_dtype):
    t, d = x2.shape
    bsz = scale.shape[0]
    tm = _tile(seq, 512)
    per_b = seq // tm
    row = pl.BlockSpec((tm, d), lambda i: (i, 0))
    bvec = pl.BlockSpec((1, 1, d), lambda i: (i // per_b, 0, 0))
    return pl.pallas_call(
        _prenorm_kernel,
        grid=(t // tm,),
        in_specs=[row, pl.BlockSpec((1, d), lambda i: (0, 0)), bvec, bvec],
        out_specs=row,
        out_shape=jax.ShapeDtypeStruct((t, d), out_dtype),
        compiler_params=_cparams(("arbitrary",)),
        name="prenorm",
    )(x2, g.reshape(1, d), scale.reshape(bsz, 1, d), shift.reshape(bsz, 1, d))


def _rotate(x, cos, sin, period, half):
    lane = lax.broadcasted_iota(I32, x.shape, 1) & (period - 1)
    swapped = jnp.where(lane < half, pltpu.roll(x, LANES - half, 1), pltpu.roll(x, half, 1))
    return x * cos + swapped * sin


def _proj_kernel(*refs, epilogue):
    if epilogue in ("plain", "sigmoid"):
        h_ref, w_ref, o_ref = refs
    else:
        h_ref, w_ref, cos_ref, sin_ref, o_ref = refs
    acc = jnp.dot(h_ref[...], w_ref[...], preferred_element_type=F32)
    if epilogue == "plain":
        o_ref[...] = acc.astype(o_ref.dtype)
    elif epilogue == "sigmoid":
        o_ref[...] = jax.nn.sigmoid(acc).astype(o_ref.dtype)
    else:
        cos = cos_ref[...]
        sin = sin_ref[...]
        if epilogue == "rope_a":
            period, half = A_HEAD_DIM, A_ROT_DIM // 2
        else:
            period, half = IDX_DIM, IDX_ROT_DIM // 2
        if epilogue == "rope_key_idx":
            lane = lax.broadcasted_iota(I32, cos.shape, 1)
            cos = jnp.where(lane < IDX_DIM, cos, 1.0)
            sin = jnp.where(lane < IDX_DIM, sin, 0.0)
        for c in range(acc.shape[1] // LANES):
            sl = slice(c * LANES, (c + 1) * LANES)
            o_ref[:, sl] = _rotate(acc[:, sl], cos, sin, period, half).astype(o_ref.dtype)


def _project(h, w, epilogue, out_dtype, tables=None, tn_pref=512):
    t, d = h.shape
    n = w.shape[1]
    tm = _tile(t, 1024)
    tn = n
    for cand in (tn_pref, 256, 128):
        if n % cand == 0:
            tn = cand
            break
    in_specs = [pl.BlockSpec((tm, d), lambda i, j: (i, 0)),
                pl.BlockSpec((d, tn), lambda i, j: (0, j))]
    args = [h, w]
    if tables is not None:
        tab = pl.BlockSpec((tm, LANES), lambda i, j: (i, 0))
        in_specs += [tab, tab]
        args += list(tables)
    return pl.pallas_call(
        functools.partial(_proj_kernel, epilogue=epilogue),
        grid=(t // tm, n // tn),
        in_specs=in_specs,
        out_specs=pl.BlockSpec((tm, tn), lambda i, j: (i, j)),
        out_shape=jax.ShapeDtypeStruct((t, n), out_dtype),
        compiler_params=_cparams(("arbitrary", "arbitrary")),
        name="proj_" + epilogue,
    )(*args)


def _dsa_kernel(qi_ref, kwq_ref, kwk_ref, qa_ref, ka_ref, va_ref, o_ref,
                keys_ref, kib_ref, *, tq, tk, topk):
    i = pl.program_id(1)
    t0 = i * tq
    n_kt = (t0 + tq + tk - 1) // tk
    grp = A_HEADS // A_KV_HEADS

    @pl.when(i == 0)
    def _():
        kib_ref[...] = kwk_ref[:, :IDX_DIM].astype(BF16)

    qi = qi_ref[...]
    wq = kwq_ref[...]
    row = t0 + lax.broadcasted_iota(I32, (tq, 1), 0)
    limit = (row // CHUNK + 1) * CHUNK

    def score_tile(kt, carry):
        k0 = pl.multiple_of(kt * tk, tk)
        kb = kib_ref[pl.ds(k0, tk), :]
        acc = jnp.zeros((tq, tk), F32)
        for h in range(IDX_HEADS):
            d = lax.dot_general(qi[:, h * IDX_DIM:(h + 1) * IDX_DIM], kb,
                                (((1,), (1,)), ((), ())), preferred_element_type=F32)
            acc = acc + jnp.maximum(d, 0.0) * wq[:, IDX_DIM + h:IDX_DIM + h + 1]
        acc = acc + 0.0
        bits = pltpu.bitcast(acc, I32)
        key = jnp.where(bits < 0, bits ^ 0x7FFFFFFF, bits)
        kpos = k0 + lax.broadcasted_iota(I32, (tq, tk), 1)
        keys_ref[kt] = jnp.where(kpos < limit, key, INT_MIN)
        return carry

    lax.fori_loop(0, n_kt, score_tile, 0)

    def bit_step(bi, lo):
        cand = lo + lax.shift_left(jnp.int32(1), 31 - bi)

        def count_tile(kt, cnt):
            ge = (keys_ref[kt] >= cand).astype(I32)
            for c in range(tk // LANES):
                cnt = cnt + ge[:, c * LANES:(c + 1) * LANES]
            return cnt

        cnt = lax.fori_loop(0, n_kt, count_tile, jnp.zeros((tq, LANES), I32))
        total = jnp.sum(cnt, axis=1, keepdims=True)
        return jnp.where(total >= topk, cand, lo)

    lo = lax.fori_loop(0, 32, bit_step, jnp.full((tq, 1), INT_MIN, I32))
    tau = jnp.maximum(lo, INT_MIN + 1)

    scale = A_HEAD_DIM ** -0.5
    for j in range(A_KV_HEADS):
        q = jnp.concatenate(
            [qa_ref[:, (j * grp + g) * A_HEAD_DIM:(j * grp + g + 1) * A_HEAD_DIM]
             for g in range(grp)], axis=0)

        def attn_tile(kt, carry, j=j, q=q):
            m, l, acc = carry
            k0 = pl.multiple_of(kt * tk, tk)
            kk = ka_ref[pl.ds(k0, tk), j * A_HEAD_DIM:(j + 1) * A_HEAD_DIM]
            vv = va_ref[pl.ds(k0, tk), j * A_HEAD_DIM:(j + 1) * A_HEAD_DIM]
            s = lax.dot_general(q, kk, (((1,), (1,)), ((), ())),
                                preferred_element_type=F32) * scale
            bias = jnp.where(keys_ref[kt] >= tau, 0.0, NEG_BIG)
            s = (s.reshape(grp, tq, tk) + bias[None]).reshape(grp * tq, tk)
            m_new = jnp.maximum(m, jnp.max(s, axis=1, keepdims=True))
            alpha = jnp.exp(m - m_new)
            p = jnp.exp(s - m_new)
            l = alpha * l + jnp.sum(p, axis=1, keepdims=True)
            acc = alpha * acc + jnp.dot(p.astype(BF16), vv, preferred_element_type=F32)
            return m_new, l, acc

        m0 = jnp.full((grp * tq, 1), NEG_BIG, F32)
        l0 = jnp.zeros((grp * tq, 1), F32)
        a0 = jnp.zeros((grp * tq, A_HEAD_DIM), F32)
        _, l, acc = lax.fori_loop(0, n_kt, attn_tile, (m0, l0, a0))
        out = acc / l
        for g in range(grp):
            h = j * grp + g
            o_ref[:, h * A_HEAD_DIM:(h + 1) * A_HEAD_DIM] = out[g * tq:(g + 1) * tq].astype(o_ref.dtype)


def _dsa_mixer(z_a, z_p, z_i, z_k, bsz, seq):
    t = bsz * seq
    tq = _tile(seq, 128)
    tk = _tile(seq, 512)
    nq = seq // tq
    topk = min(TOPK_MAX, seq // 4)
    return pl.pallas_call(
        functools.partial(_dsa_kernel, tq=tq, tk=tk, topk=topk),
        grid=(bsz, nq),
        in_specs=[
            pl.BlockSpec((tq, IDX_Q_W), lambda b, i: (b * nq + i, 0)),
            pl.BlockSpec((tq, LANES), lambda b, i: (b * nq + i, 0)),
            pl.BlockSpec((seq, LANES), lambda b, i: (b, 0)),
            pl.BlockSpec((tq, A_Q_W), lambda b, i: (b * nq + i, 0)),
            pl.BlockSpec((seq, A_KV_W), lambda b, i: (b, A_Q_W // A_KV_W)),
            pl.BlockSpec((seq, A_KV_W), lambda b, i: (b, 0)),
        ],
        out_specs=pl.BlockSpec((tq, A_Q_W), lambda b, i: (b * nq + i, 0)),
        out_shape=jax.ShapeDtypeStruct((t, A_Q_W), BF16),
        scratch_shapes=[pltpu.VMEM((seq // tk, tq, tk), I32),
                        pltpu.VMEM((seq, IDX_DIM), BF16)],
        compiler_params=_cparams(("arbitrary", "arbitrary")),
        name="dsa_mixer",
    )(z_i, z_k, z_k, z_a, z_a, z_p)


def _stick_kernel(q_ref, k_ref, v_ref, o_ref, *, tq, tk, heads):
    i = pl.program_id(2)
    d_idx = (i * tq) // tk
    hd = B_HEAD_DIM
    scale = hd ** -0.5
    qs = [(q_ref[:, h * hd:(h + 1) * hd].astype(F32) * scale).astype(BF16) for h in range(heads)]
    r_io = lax.broadcasted_iota(I32, (tk, tk), 0)
    c_io = lax.broadcasted_iota(I32, (tk, tk), 1)
    upper = jnp.where(r_io > c_io, 1.0, 0.0).astype(BF16)
    diagonal = (d_idx * tk + lax.broadcasted_iota(I32, (tq, tk), 1)
                < i * tq + lax.broadcasted_iota(I32, (tq, tk), 0))

    def scores(kt):
        k0 = pl.multiple_of(kt * tk, tk)
        return tuple(lax.dot_general(qs[h], k_ref[pl.ds(k0, tk), h * hd:(h + 1) * hd],
                                     (((1,), (1,)), ((), ())), preferred_element_type=F32)
                     for h in range(heads))

    def front(z, causal):
        soft = jnp.log(1.0 + jnp.exp(-jnp.abs(z)))
        log_beta = jnp.minimum(z, 0.0) - soft
        log_keep = log_beta - z
        if causal is not None:
            log_keep = jnp.where(causal, log_keep, 0.0)
        hi = log_keep.astype(BF16)
        lo = (log_keep - hi.astype(F32)).astype(BF16)
        gap = (jnp.dot(hi, upper, preferred_element_type=F32)
               + jnp.dot(lo, upper, preferred_element_type=F32))
        logit = log_beta + gap
        if causal is not None:
            logit = jnp.where(causal, logit, NEG_BIG)
        return logit, jnp.sum(log_keep, axis=1, keepdims=True)

    def back(kt, pend, carry):
        k0 = pl.multiple_of(kt * tk, tk)
        out = []
        for h in range(heads):
            logit, keep_sum = pend[2 * h], pend[2 * h + 1]
            run, acc = carry[2 * h], carry[2 * h + 1]
            attn = jnp.exp(logit + run).astype(BF16)
            vv = v_ref[pl.ds(k0, tk), h * hd:(h + 1) * hd]
            out += [run + keep_sum, acc + jnp.dot(attn, vv, preferred_element_type=F32)]
        return tuple(out)

    def fronts(zs, causal):
        out = []
        for h in range(heads):
            out.extend(front(zs[h], causal))
        return tuple(out)

    def step(n, state):
        zs, pend, carry = state
        kt = d_idx - 1 - n
        zs_next = scores(jnp.maximum(kt - 1, 0))
        pend_next = fronts(zs, None)
        return zs_next, pend_next, back(kt + 1, pend, carry)

    init = []
    for h in range(heads):
        init += [jnp.zeros((tq, 1), F32), jnp.zeros((tq, hd), F32)]
    pend = fronts(scores(d_idx), diagonal)
    zs = scores(jnp.maximum(d_idx - 1, 0))
    _, pend, carry = lax.fori_loop(0, d_idx, step, (zs, pend, tuple(init)))
    carry = back(0, pend, carry)
    for h in range(heads):
        o_ref[:, h * hd:(h + 1) * hd] = carry[2 * h + 1].astype(o_ref.dtype)


def _stick_mixer(z_p, bsz, seq):
    t = bsz * seq
    tq = _tile(seq, 256)
    tk = _tile(seq, 256)
    nq = seq // tq
    heads = 2
    width = heads * B_HEAD_DIM
    q_off = A_KV_W // width
    k_off = q_off + B_W // width
    v_off = k_off + B_W // width
    return pl.pallas_call(
        functools.partial(_stick_kernel, tq=tq, tk=tk, heads=heads),
        grid=(bsz, B_HEADS // heads, nq),
        in_specs=[
            pl.BlockSpec((tq, width), lambda b, h, i: (b * nq + i, q_off + h)),
            pl.BlockSpec((seq, width), lambda b, h, i: (b, k_off + h)),
            pl.BlockSpec((seq, width), lambda b, h, i: (b, v_off + h)),
        ],
        out_specs=pl.BlockSpec((tq, width), lambda b, h, i: (b * nq + i, h)),
        out_shape=jax.ShapeDtypeStruct((t, B_W), BF16),
        compiler_params=_cparams(("arbitrary", "arbitrary", "arbitrary")),
        name="stick_mixer",
    )(z_p, z_p, z_p)


def _merge_kernel(oa_ref, ob_ref, g_ref, x_ref, wa_ref, wb_ref, wo_ref, gpost_ref, ga_ref, o_ref):
    d = x_ref.shape[1]
    a = jnp.dot(oa_ref[...], wa_ref[...], preferred_element_type=F32)
    b = jnp.dot(ob_ref[...], wb_ref[...], preferred_element_type=F32)
    merged = g_ref[:, :d].astype(F32) * a + g_ref[:, d:].astype(F32) * b
    mix = jnp.dot(merged.astype(BF16), wo_ref[...], preferred_element_type=F32)
    y = mix * lax.rsqrt(jnp.mean(mix * mix, axis=-1, keepdims=True) + RMS_EPS)
    o_ref[...] = x_ref[...] + ga_ref[0] * (y * gpost_ref[...])


def _merge(o_a, o_b, z_g, x2, w_a, w_b, w_o, g_post, gate, seq):
    t, d = x2.shape
    bsz = gate.shape[0]
    tm = _tile(seq, 256)
    per_b = seq // tm
    const = lambda i: (0, 0)
    return pl.pallas_call(
        _merge_kernel,
        grid=(t // tm,),
        in_specs=[
            pl.BlockSpec((tm, A_Q_W), lambda i: (i, 0)),
            pl.BlockSpec((tm, B_W), lambda i: (i, 0)),
            pl.BlockSpec((tm, 2 * d), lambda i: (i, 0)),
            pl.BlockSpec((tm, d), lambda i: (i, 0)),
            pl.BlockSpec((A_Q_W, d), const),
            pl.BlockSpec((B_W, d), const),
            pl.BlockSpec((d, d), const),
            pl.BlockSpec((1, d), const),
            pl.BlockSpec((1, 1, d), lambda i: (i // per_b, 0, 0)),
        ],
        out_specs=pl.BlockSpec((tm, d), lambda i: (i, 0)),
        out_shape=jax.ShapeDtypeStruct((t, d), F32),
        compiler_params=_cparams(("arbitrary",)),
        name="merge",
    )(o_a, o_b, z_g, x2, w_a, w_b, w_o, g_post.reshape(1, d), gate.reshape(bsz, 1, d))


def _router_kernel(h_ref, wr_ref, br_ref, e_ref, w_ref, r_ref, cnt_ref, base_ref):
    i = pl.program_id(0)
    tm = h_ref.shape[0]

    @pl.when(i == 0)
    def _():
        base_ref[...] = jnp.zeros_like(base_ref)

    logits = jnp.dot(h_ref[...], wr_ref[...], preferred_element_type=F32,
                     precision=lax.Precision.HIGHEST) + br_ref[...]
    lane = lax.broadcasted_iota(I32, (tm, LANES), 1)
    work = logits
    picks, vals = [], []
    for _ in range(TOP_K_EXPERTS):
        m = jnp.max(work, axis=1, keepdims=True)
        idx = jnp.min(jnp.where(work == m, lane, LANES), axis=1, keepdims=True)
        picks.append(idx)
        vals.append(m)
        work = jnp.where(lane == idx, -jnp.inf, work)
    exps = [jnp.exp(v - vals[0]) for v in vals]
    den = exps[0]
    for e in exps[1:]:
        den = den + e

    onehot = jnp.zeros((tm, LANES), F32)
    for idx in picks:
        onehot = onehot + jnp.where(lane == idx, 1.0, 0.0)
    r_io = lax.broadcasted_iota(I32, (tm, tm), 0)
    c_io = lax.broadcasted_iota(I32, (tm, tm), 1)
    lower = jnp.where(c_io < r_io, 1.0, 0.0).astype(BF16)
    prefix = jnp.dot(lower, onehot.astype(BF16), preferred_element_type=F32)
    total = prefix + base_ref[0:1, :]

    e_out = jnp.zeros((tm, LANES), I32)
    w_out = jnp.zeros((tm, LANES), F32)
    r_out = jnp.zeros((tm, LANES), I32)
    for k in range(TOP_K_EXPERTS):
        rank = jnp.sum(jnp.where(lane == picks[k], total, 0.0), axis=1, keepdims=True)
        e_out = jnp.where(lane == k, picks[k], e_out)
        w_out = jnp.where(lane == k, exps[k] / den, w_out)
        r_out = jnp.where(lane == k, rank.astype(I32), r_out)
    e_ref[...] = e_out
    w_ref[...] = w_out
    r_ref[...] = r_out
    new_base = base_ref[0:1, :] + jnp.sum(onehot, axis=0, keepdims=True)
    base_ref[...] = jnp.broadcast_to(new_base, base_ref.shape)
    cnt_ref[...] = jnp.broadcast_to(new_base, cnt_ref.shape)


def _route(h2, w_router, b_router):
    t, d = h2.shape
    tm = _tile(t, 512)
    wr = jnp.zeros((d, LANES), F32).at[:, :N_EXPERTS].set(w_router)
    br = jnp.full((1, LANES), NEG_BIG, F32).at[0, :N_EXPERTS].set(b_router)
    row = pl.BlockSpec((tm, LANES), lambda i: (i, 0))
    return pl.pallas_call(
        _router_kernel,
        grid=(t // tm,),
        in_specs=[pl.BlockSpec((tm, d), lambda i: (i, 0)),
                  pl.BlockSpec((d, LANES), lambda i: (0, 0)),
                  pl.BlockSpec((1, LANES), lambda i: (0, 0))],
        out_specs=[row, row, row, pl.BlockSpec((8, LANES), lambda i: (0, 0))],
        out_shape=[jax.ShapeDtypeStruct((t, LANES), I32),
                   jax.ShapeDtypeStruct((t, LANES), F32),
                   jax.ShapeDtypeStruct((t, LANES), I32),
                   jax.ShapeDtypeStruct((8, LANES), F32)],
        scratch_shapes=[pltpu.VMEM((8, LANES), F32)],
        compiler_params=_cparams(("arbitrary",)),
        name="router",
    )(h2, wr, br)


def _dispatch_kernel(dest_ref, h_ref, init_ref, xs_ref, sem):
    del init_ref
    tm = h_ref.shape[0]

    def copy(t, k):
        return pltpu.make_async_copy(h_ref.at[pl.ds(t, 1)],
                                     xs_ref.at[pl.ds(dest_ref[0, 0, t * TOP_K_EXPERTS + k], 1)], sem)

    def issue(t, c):
        for k in range(TOP_K_EXPERTS):
            copy(t, k).start()
        return c

    def drain(t, c):
        for k in range(TOP_K_EXPERTS):
            copy(t, k).wait()
        return c

    lax.fori_loop(0, tm, issue, 0)
    lax.fori_loop(0, tm, drain, 0)


def _dispatch(h2, dest, n_rows):
    t, d = h2.shape
    tm = _tile(t, 256)
    nt = t // tm
    dest3 = dest.reshape(nt, 1, tm * TOP_K_EXPERTS)
    init = jnp.zeros((n_rows, d), h2.dtype)
    return pl.pallas_call(
        _dispatch_kernel,
        grid=(nt,),
        in_specs=[pl.BlockSpec((1, 1, tm * TOP_K_EXPERTS), lambda i: (i, 0, 0), memory_space=pltpu.SMEM),
                  pl.BlockSpec((tm, d), lambda i: (i, 0)),
                  pl.BlockSpec(memory_space=pl.ANY)],
        out_specs=pl.BlockSpec(memory_space=pl.ANY),
        out_shape=jax.ShapeDtypeStruct((n_rows, d), h2.dtype),
        scratch_shapes=[pltpu.SemaphoreType.DMA(())],
        input_output_aliases={2: 0},
        compiler_params=pltpu.CompilerParams(dimension_semantics=("arbitrary",),
                                             vmem_limit_bytes=VMEM_LIMIT, has_side_effects=True),
        name="moe_dispatch",
    )(dest3, h2, init)


def _deinterleave_kernel(w_ref, p_ref, g_ref, l_ref):
    p = p_ref[...]
    width = p.shape[0]
    half = width // 2
    for c in range(w_ref.shape[2] // width):
        w = w_ref[0, :, c * width:(c + 1) * width].astype(BF16)
        r = jnp.dot(w, p, preferred_element_type=F32)
        g_ref[0, :, c * half:(c + 1) * half] = r[:, :half].astype(BF16)
        l_ref[0, :, c * half:(c + 1) * half] = r[:, half:].astype(BF16)


def _deinterleave(w1):
    n_e, d, f2 = w1.shape
    width = 2 * LANES
    src = jnp.arange(width)
    dst = jnp.where(src % 2 == 0, src // 2, LANES + src // 2)
    perm = (dst[:, None] == jnp.arange(width)[None, :]).astype(BF16)
    tr = _tile(d, 256)
    out = pl.BlockSpec((1, tr, f2 // 2), lambda e, r: (e, r, 0))
    return pl.pallas_call(
        _deinterleave_kernel,
        grid=(n_e, d // tr),
        in_specs=[pl.BlockSpec((1, tr, f2), lambda e, r: (e, r, 0)),
                  pl.BlockSpec((width, width), lambda e, r: (0, 0))],
        out_specs=[out, out],
        out_shape=[jax.ShapeDtypeStruct((n_e, d, f2 // 2), BF16)] * 2,
        compiler_params=_cparams(("arbitrary", "arbitrary")),
        name="w1_deinterleave",
    )(w1, perm)


def _expert_kernel(be_ref, nu_ref, x_ref, w1g_ref, w1l_ref, b1g_ref, b1l_ref, w2_ref, b2_ref,
                   o_ref, xb_ref, acc_ref):
    i = pl.program_id(0)
    j = pl.program_id(1)
    nf = pl.num_programs(1)
    valid = i < nu_ref[0]

    @pl.when(jnp.logical_and(valid, j == 0))
    def _():
        xb_ref[...] = x_ref[...].astype(BF16)
        acc_ref[...] = jnp.zeros_like(acc_ref)

    @pl.when(valid)
    def _():
        xb = xb_ref[...]
        glu = jnp.dot(xb, w1g_ref[0], preferred_element_type=F32) + b1g_ref[0]
        lin = jnp.dot(xb, w1l_ref[0], preferred_element_type=F32) + b1l_ref[0]
        glu = jnp.minimum(glu, SWIGLU_LIMIT)
        lin = jnp.clip(lin, -SWIGLU_LIMIT, SWIGLU_LIMIT)
        act = glu * jax.nn.sigmoid(SWIGLU_ALPHA * glu) * (lin + 1.0)
        acc_ref[...] += jnp.dot(act.astype(BF16), w2_ref[0], preferred_element_type=F32)

    @pl.when(jnp.logical_and(valid, j == nf - 1))
    def _():
        o_ref[...] = acc_ref[...] + b2_ref[0]


def _experts(xs, block_e, n_used, w1g, w1l, b1g, b1l, w2, b2):
    n_rows, d = xs.shape
    n_e, _, f = w1g.shape
    n_blocks = n_rows // MOE_ROWS
    tf = _tile(f, 512)
    nf = f // tf

    def blk(i, nu):
        return jnp.minimum(i, nu[0] - 1)

    def ftile(i, j, nu):
        return jnp.where(i < nu[0], j, nf - 1)

    grid_spec = pltpu.PrefetchScalarGridSpec(
        num_scalar_prefetch=2,
        grid=(n_blocks, nf),
        in_specs=[
            pl.BlockSpec((MOE_ROWS, d), lambda i, j, be, nu: (blk(i, nu), 0)),
            pl.BlockSpec((1, d, tf), lambda i, j, be, nu: (be[blk(i, nu)], 0, ftile(i, j, nu))),
            pl.BlockSpec((1, d, tf), lambda i, j, be, nu: (be[blk(i, nu)], 0, ftile(i, j, nu))),
            pl.BlockSpec((1, 1, tf), lambda i, j, be, nu: (be[blk(i, nu)], 0, ftile(i, j, nu))),
            pl.BlockSpec((1, 1, tf), lambda i, j, be, nu: (be[blk(i, nu)], 0, ftile(i, j, nu))),
            pl.BlockSpec((1, tf, d), lambda i, j, be, nu: (be[blk(i, nu)], ftile(i, j, nu), 0)),
            pl.BlockSpec((1, 1, d), lambda i, j, be, nu: (be[blk(i, nu)], 0, 0)),
        ],
        out_specs=pl.BlockSpec((MOE_ROWS, d), lambda i, j, be, nu: (blk(i, nu), 0)),
        scratch_shapes=[pltpu.VMEM((MOE_ROWS, d), BF16), pltpu.VMEM((MOE_ROWS, d), F32)],
    )
    return pl.pallas_call(
        _expert_kernel,
        grid_spec=grid_spec,
        out_shape=jax.ShapeDtypeStruct((n_rows, d), F32),
        compiler_params=_cparams(("arbitrary", "arbitrary")),
        name="moe_experts",
    )(block_e, n_used, xs, w1g, w1l, b1g, b1l, w2, b2)


def _combine_kernel(dest_ref, tw_ref, x_ref, gpost_ref, ga_ref, ys_ref, o_ref, buf_ref, sem):
    tm = x_ref.shape[0]

    def copy(t, k):
        return pltpu.make_async_copy(ys_ref.at[pl.ds(dest_ref[0, 0, t * TOP_K_EXPERTS + k], 1)],
                                     buf_ref.at[k, pl.ds(t, 1)], sem)

    def issue(t, c):
        for k in range(TOP_K_EXPERTS):
            copy(t, k).start()
        return c

    def drain(t, c):
        for k in range(TOP_K_EXPERTS):
            copy(t, k).wait()
        return c

    lax.fori_loop(0, tm, issue, 0)
    lax.fori_loop(0, tm, drain, 0)

    tw = tw_ref[...]
    f = tw[:, 0:1] * buf_ref[0]
    for k in range(1, TOP_K_EXPERTS):
        f = f + tw[:, k:k + 1] * buf_ref[k]
    y = f * lax.rsqrt(jnp.mean(f * f, axis=-1, keepdims=True) + RMS_EPS)
    o_ref[...] = x_ref[...] + ga_ref[0] * (y * gpost_ref[...])


def _combine(ys, dest, top_w, x1, g_post, gate, seq):
    t, d = x1.shape
    bsz = gate.shape[0]
    tm = _tile(seq, 256)
    nt = t // tm
    per_b = seq // tm
    dest3 = dest.reshape(nt, 1, tm * TOP_K_EXPERTS)
    return pl.pallas_call(
        _combine_kernel,
        grid=(nt,),
        in_specs=[pl.BlockSpec((1, 1, tm * TOP_K_EXPERTS), lambda i: (i, 0, 0), memory_space=pltpu.SMEM),
                  pl.BlockSpec((tm, LANES), lambda i: (i, 0)),
                  pl.BlockSpec((tm, d), lambda i: (i, 0)),
                  pl.BlockSpec((1, d), lambda i: (0, 0)),
                  pl.BlockSpec((1, 1, d), lambda i: (i // per_b, 0, 0)),
                  pl.BlockSpec(memory_space=pl.ANY)],
        out_specs=pl.BlockSpec((tm, d), lambda i: (i, 0)),
        out_shape=jax.ShapeDtypeStruct((t, d), F32),
        scratch_shapes=[pltpu.VMEM((TOP_K_EXPERTS, tm, d), F32), pltpu.SemaphoreType.DMA(())],
        compiler_params=_cparams(("arbitrary",)),
        name="moe_combine",
    )(dest3, top_w, x1, g_post.reshape(1, d), gate.reshape(bsz, 1, d), ys)


def _moe(h2, x1, w_router, b_router, w1, b1, w2, b2, g_post, gate, seq):
    t, d = h2.shape
    top_e, top_w, rank, counts = _route(h2, w_router, b_router)
    top_e = top_e[:, :TOP_K_EXPERTS]
    rank = rank[:, :TOP_K_EXPERTS]
    counts = counts[0, :N_EXPERTS].astype(I32)

    n_blocks = (t * TOP_K_EXPERTS) // MOE_ROWS + N_EXPERTS
    padded = (counts + MOE_ROWS - 1) // MOE_ROWS * MOE_ROWS
    pad_end = jnp.cumsum(padded)
    pad_start = pad_end - padded
    dest = (pad_start[top_e] + rank).astype(I32)
    block_e = jnp.minimum(jnp.searchsorted(pad_end, jnp.arange(n_blocks, dtype=I32) * MOE_ROWS,
                                           side="right"), N_EXPERTS - 1).astype(I32)
    n_used = (pad_end[-1:] // MOE_ROWS).astype(I32)

    xs = _dispatch(h2, dest, n_blocks * MOE_ROWS)
    w1g, w1l = _deinterleave(w1)
    n_e, f2 = b1.shape
    b1g = b1[:, 0::2].reshape(n_e, 1, f2 // 2)
    b1l = b1[:, 1::2].reshape(n_e, 1, f2 // 2)
    ys = _experts(xs, block_e, n_used, w1g, w1l, b1g, b1l, w2.astype(BF16), b2.reshape(n_e, 1, d))
    return _combine(ys, dest, top_w, x1, g_post, gate, seq)


def _layer(x2, c, tables, bsz, seq, w_ada, b_ada, g_pre_mix, g_post_mix, w_in, w_branch_a,
           w_branch_b, w_out, g_pre_ffn, g_post_ffn, w_router, b_router, w1, b1, w2, b2):
    d = x2.shape[1]
    mod = _modulation(c, w_ada, b_ada)
    sh1, sc1, ga1, sh2, sc2, ga2 = [mod[:, k * d:(k + 1) * d] for k in range(6)]
    cos_a, sin_a, cos_i, sin_i = tables

    h = _prenorm(x2, g_pre_mix, sc1, sh1, seq, BF16)
    wb = w_in.astype(BF16)
    o = 0
    w_qka = wb[:, o:o + A_Q_W + A_KV_W]
    o += A_Q_W + A_KV_W
    w_plain = wb[:, o:o + A_KV_W + 3 * B_W]
    o += A_KV_W + 3 * B_W
    w_qi = wb[:, o:o + IDX_Q_W]
    o += IDX_Q_W
    w_kw = jnp.zeros((d, LANES), BF16).at[:, :IDX_DIM + IDX_HEADS].set(wb[:, o:o + IDX_DIM + IDX_HEADS])
    o += IDX_DIM + IDX_HEADS
    w_gate = wb[:, o:o + 2 * d]

    z_a = _project(h, w_qka, "rope_a", BF16, (cos_a, sin_a), tn_pref=640)
    z_p = _project(h, w_plain, "plain", BF16, tn_pref=1664)
    z_i = _project(h, w_qi, "rope_idx", BF16, (cos_i, sin_i))
    z_k = _project(h, w_kw, "rope_key_idx", F32, (cos_i, sin_i))
    z_g = _project(h, w_gate, "sigmoid", BF16, tn_pref=1024)

    o_a = _dsa_mixer(z_a, z_p, z_i, z_k, bsz, seq)
    o_b = _stick_mixer(z_p, bsz, seq)
    x1 = _merge(o_a, o_b, z_g, x2, w_branch_a.astype(BF16), w_branch_b.astype(BF16),
                w_out.astype(BF16), g_post_mix, ga1, seq)

    h2 = _prenorm(x1, g_pre_ffn, sc2, sh2, seq, F32)
    return _moe(h2, x1, w_router, b_router, w1, b1, w2, b2, g_post_ffn, ga2, seq)


def kernel(x, c, positions, w_ada, b_ada, g_pre_mix, g_post_mix, w_in, w_branch_a, w_branch_b, w_out, g_pre_ffn, g_post_ffn, w_router, b_router, w1, b1, w2, b2):
    bsz, seq, d = x.shape
    x2 = x.reshape(bsz * seq, d)
    tables = _rope_tables(positions)
    for l in range(w_ada.shape[0]):
        x2 = _layer(x2, c, tables, bsz, seq, w_ada[l], b_ada[l], g_pre_mix[l], g_post_mix[l], w_in[l],
                    w_branch_a[l], w_branch_b[l], w_out[l], g_pre_ffn[l], g_post_ffn[l],
                    w_router[l], b_router[l], w1[l], b1[l], w2[l], b2[l])
    return x2.reshape(bsz, seq, d)
```

```python
import functools

import jax
import jax.numpy as jnp
from jax import lax
from jax.experimental import pallas as pl
from jax.experimental.pallas import tpu as pltpu

F32 = jnp.float32
BF16 = jnp.bfloat16
I32 = jnp.int32

CHUNK = 64
RMS_EPS = 1e-6
ROPE_THETA = 500000.0
A_HEADS = 8
A_KV_HEADS = 2
A_HEAD_DIM = 128
A_ROT_DIM = A_HEAD_DIM // 4
IDX_HEADS = 16
IDX_DIM = 64
IDX_ROT_DIM = IDX_DIM // 4
TOPK_MAX = 256
B_HEADS = 8
B_HEAD_DIM = 128
N_EXPERTS = 32
TOP_K_EXPERTS = 4
SWIGLU_LIMIT = 7.0
SWIGLU_ALPHA = 1.702

A_Q_W = A_HEADS * A_HEAD_DIM
A_KV_W = A_KV_HEADS * A_HEAD_DIM
B_W = B_HEADS * B_HEAD_DIM
IDX_Q_W = IDX_HEADS * IDX_DIM

LANES = 128
VMEM_LIMIT = 56 * 1024 * 1024
NEG_BIG = -1e30
LOG2_E = 1.4426950408889634
F32_MAX = 3.4028234663852886e38
BISECT_CAP = 320
MOE_ROWS = 512


def _cparams(sem):
    return pltpu.CompilerParams(dimension_semantics=sem, vmem_limit_bytes=VMEM_LIMIT)


def _tile(n, pref):
    t = min(n, pref)
    assert n % t == 0, (n, t)
    return t


def _mod_kernel(c_ref, w_ref, b_ref, o_ref):
    c = c_ref[...]
    s = c * jax.nn.sigmoid(c)
    o_ref[...] = jnp.dot(s, w_ref[...], preferred_element_type=F32,
                         precision=lax.Precision.HIGHEST) + b_ref[...]


def _modulation(c, w_ada, b_ada):
    bsz, d = c.shape
    n = w_ada.shape[1]
    rows = 8
    c_pad = jnp.zeros((rows, d), F32).at[:bsz].set(c)
    tn = _tile(n, 1024)
    out = pl.pallas_call(
        _mod_kernel,
        grid=(n // tn,),
        in_specs=[pl.BlockSpec((rows, d), lambda j: (0, 0)),
                  pl.BlockSpec((d, tn), lambda j: (0, j)),
                  pl.BlockSpec((1, tn), lambda j: (0, j))],
        out_specs=pl.BlockSpec((rows, tn), lambda j: (0, j)),
        out_shape=jax.ShapeDtypeStruct((rows, n), F32),
        compiler_params=_cparams(("arbitrary",)),
        name="modulation",
    )(c_pad, w_ada, b_ada.reshape(1, n))
    return out[:bsz]


def _rope_table_kernel(pos_ref, fa_ref, sa_ref, fi_ref, si_ref, ca_o, sa_o, ci_o, si_o):
    pos = pos_ref[...].astype(F32)
    ang_a = pos * fa_ref[...]
    ca_o[...] = jnp.cos(ang_a)
    sa_o[...] = jnp.sin(ang_a) * sa_ref[...]
    ang_i = pos * fi_ref[...]
    ci_o[...] = jnp.cos(ang_i)
    si_o[...] = jnp.sin(ang_i) * si_ref[...]


def _rope_pattern(rot_dim, head_dim):
    half = rot_dim // 2
    inv_freq = ROPE_THETA ** (-jnp.arange(half, dtype=F32) / half)
    freq = jnp.concatenate([inv_freq, inv_freq, jnp.zeros((head_dim - rot_dim,), F32)])
    sign = jnp.concatenate([-jnp.ones((half,), F32), jnp.ones((half,), F32),
                            jnp.zeros((head_dim - rot_dim,), F32)])
    reps = LANES // head_dim
    return jnp.tile(freq, reps).reshape(1, LANES), jnp.tile(sign, reps).reshape(1, LANES)


def _rope_tables(positions):
    t = positions.size
    pos = positions.reshape(t, 1).astype(I32)
    fa, sa = _rope_pattern(A_ROT_DIM, A_HEAD_DIM)
    fi, si = _rope_pattern(IDX_ROT_DIM, IDX_DIM)
    tm = _tile(t, 1024)
    pat = pl.BlockSpec((1, LANES), lambda i: (0, 0))
    tab = pl.BlockSpec((tm, LANES), lambda i: (i, 0))
    return pl.pallas_call(
        _rope_table_kernel,
        grid=(t // tm,),
        in_specs=[pl.BlockSpec((tm, 1), lambda i: (i, 0)), pat, pat, pat, pat],
        out_specs=[tab, tab, tab, tab],
        out_shape=[jax.ShapeDtypeStruct((t, LANES), F32)] * 4,
        compiler_params=_cparams(("arbitrary",)),
        name="rope_tables",
    )(pos, fa, sa, fi, si)


def _prenorm_kernel(x_ref, g_ref, sc_ref, sh_ref, o_ref):
    x = x_ref[...]
    y = x * lax.rsqrt(jnp.mean(x * x, axis=-1, keepdims=True) + RMS_EPS)
    h = (y * g_ref[...]) * (1.0 + sc_ref[0]) + sh_ref[0]
    o_ref[...] = h.astype(o_ref.dtype)


def _prenorm(x2, g, scale, shift, seq, out_dtype):
    t, d = x2.shape
    bsz = scale.shape[0]
    tm = _tile(seq, 512)
    per_b = seq // tm
    row = pl.BlockSpec((tm, d), lambda i: (i, 0))
    bvec = pl.BlockSpec((1, 1, d), lambda i: (i // per_b, 0, 0))
    return pl.pallas_call(
        _prenorm_kernel,
        grid=(t // tm,),
        in_specs=[row, pl.BlockSpec((1, d), lambda i: (0, 0)), bvec, bvec],
        out_specs=row,
        out_shape=jax.ShapeDtypeStruct((t, d), out_dtype),
        compiler_params=_cparams(("arbitrary",)),
        name="prenorm",
    )(x2, g.reshape(1, d), scale.reshape(bsz, 1, d), shift.reshape(bsz, 1, d))


def _rotate(x, cos, sin, period, half):
    lane = lax.broadcasted_iota(I32, x.shape, 1) & (period - 1)
    swapped = jnp.where(lane < half, pltpu.roll(x, LANES - half, 1), pltpu.roll(x, half, 1))
    return x * cos + swapped * sin


def _proj_kernel(*refs, epilogue):
    if epilogue in ("plain", "sigmoid"):
        h_ref, w_ref, o_ref = refs
    else:
        h_ref, w_ref, cos_ref, sin_ref, o_ref = refs
    acc = jnp.dot(h_ref[...], w_ref[...], preferred_element_type=F32)
    if epilogue == "plain":
        o_ref[...] = acc.astype(o_ref.dtype)
    elif epilogue == "sigmoid":
        o_ref[...] = jax.nn.sigmoid(acc).astype(o_ref.dtype)
    else:
        cos = cos_ref[...]
        sin = sin_ref[...]
        if epilogue == "rope_a":
            period, half = A_HEAD_DIM, A_ROT_DIM // 2
        else:
            period, half = IDX_DIM, IDX_ROT_DIM // 2
        if epilogue == "rope_key_idx":
            lane = lax.broadcasted_iota(I32, cos.shape, 1)
            cos = jnp.where(lane < IDX_DIM, cos, 1.0)
            sin = jnp.where(lane < IDX_DIM, sin, 0.0)
        for c in range(acc.shape[1] // LANES):
            sl = slice(c * LANES, (c + 1) * LANES)
            o_ref[:, sl] = _rotate(acc[:, sl], cos, sin, period, half).astype(o_ref.dtype)


def _project(h, w, epilogue, out_dtype, tables=None, tn_pref=512):
    t, d = h.shape
    n = w.shape[1]
    tm = _tile(t, 1024)
    tn = n
    for cand in (tn_pref, 256, 128):
        if n % cand == 0:
            tn = cand
            break
    in_specs = [pl.BlockSpec((tm, d), lambda i, j: (i, 0)),
                pl.BlockSpec((d, tn), lambda i, j: (0, j))]
    args = [h, w]
    if tables is not None:
        tab = pl.BlockSpec((tm, LANES), lambda i, j: (i, 0))
        in_specs += [tab, tab]
        args += list(tables)
    return pl.pallas_call(
        functools.partial(_proj_kernel, epilogue=epilogue),
        grid=(t // tm, n // tn),
        in_specs=in_specs,
        out_specs=pl.BlockSpec((tm, tn), lambda i, j: (i, j)),
        out_shape=jax.ShapeDtypeStruct((t, n), out_dtype),
        compiler_params=_cparams(("arbitrary", "arbitrary")),
        name="proj_" + epilogue,
    )(*args)


def _dsa_kernel(qi_ref, kwq_ref, kwk_ref, qa_ref, ka_ref, va_ref, o_ref,
                keys_ref, kib_ref, *, tq, tk, topk):
    i = pl.program_id(1)
    t0 = i * tq
    n_kt = (t0 + tq + tk - 1) // tk
    grp = A_HEADS // A_KV_HEADS

    @pl.when(i == 0)
    def _():
        kib_ref[...] = kwk_ref[:, :IDX_DIM].astype(BF16)

    qi = qi_ref[...]
    wq = kwq_ref[...]
    row = t0 + lax.broadcasted_iota(I32, (tq, 1), 0)
    limit = (row // CHUNK + 1) * CHUNK

    def score_tile(kt, carry):
        hi_acc, lo_acc = carry
        k0 = pl.multiple_of(kt * tk, tk)
        kb = kib_ref[pl.ds(k0, tk), :]
        acc = jnp.zeros((tq, tk), F32)
        for h in range(IDX_HEADS):
            d = lax.dot_general(qi[:, h * IDX_DIM:(h + 1) * IDX_DIM], kb,
                                (((1,), (1,)), ((), ())), preferred_element_type=F32)
            acc = acc + jnp.maximum(d, 0.0) * wq[:, IDX_DIM + h:IDX_DIM + h + 1]
        adm = (k0 + lax.broadcasted_iota(I32, (tq, tk), 1)) < limit
        s_hi = jnp.where(adm, acc, -jnp.inf)
        s_lo = jnp.where(adm, acc, jnp.inf)
        keys_ref[kt] = s_hi
        for c in range(tk // LANES):
            hi_acc = jnp.maximum(hi_acc, s_hi[:, c * LANES:(c + 1) * LANES])
            lo_acc = jnp.minimum(lo_acc, s_lo[:, c * LANES:(c + 1) * LANES])
        return hi_acc, lo_acc

    hi_acc, lo_acc = lax.fori_loop(
        0, n_kt, score_tile,
        (jnp.full((tq, LANES), -jnp.inf, F32), jnp.full((tq, LANES), jnp.inf, F32)))
    row_max = jnp.max(hi_acc, axis=1, keepdims=True)
    row_min = jnp.min(lo_acc, axis=1, keepdims=True)

    def count_ge(cand):
        def count_tile(kt, cnt):
            ge = (keys_ref[kt] >= cand).astype(I32)
            for c in range(tk // LANES):
                cnt = cnt + ge[:, c * LANES:(c + 1) * LANES]
            return cnt

        cnt = lax.fori_loop(0, n_kt, count_tile, jnp.zeros((tq, LANES), I32))
        return jnp.sum(cnt, axis=1, keepdims=True)

    def bisect_cond(state):
        it, pending = state[0], state[1]
        return jnp.logical_and(it < BISECT_CAP, pending > 0)

    def bisect_step(state):
        it, _, lo, hi, tau, todo = state
        mid = 0.5 * lo + 0.5 * hi
        stuck = jnp.logical_or(mid <= lo, mid >= hi)
        cnt = count_ge(mid)
        fin = jnp.logical_or(stuck, cnt == topk)
        tau = jnp.where(jnp.logical_and(todo > 0, fin), jnp.where(stuck, lo, mid), tau)
        up = cnt >= topk
        lo = jnp.where(up, mid, lo)
        hi = jnp.where(up, hi, mid)
        todo = jnp.where(fin, 0, todo)
        return it + 1, jnp.max(todo), lo, hi, tau, todo

    todo0 = jnp.where(limit > topk, 1, 0).astype(I32)
    hi0 = row_max + jnp.maximum(jnp.abs(row_max) * 1e-6, 1e-30)
    state = (jnp.int32(0), jnp.max(todo0), row_min, hi0,
             jnp.full((tq, 1), -F32_MAX, F32), todo0)
    _, _, lo, _, tau, todo = lax.while_loop(bisect_cond, bisect_step, state)
    tau = jnp.where(todo > 0, lo, tau)

    qscale = A_HEAD_DIM ** -0.5 * LOG2_E
    qs = [jnp.concatenate(
        [(qa_ref[:, (j * grp + g) * A_HEAD_DIM:(j * grp + g + 1) * A_HEAD_DIM].astype(F32)
          * qscale).astype(BF16) for g in range(grp)], axis=0)
        for j in range(A_KV_HEADS)]

    def attn_tile(kt, carry):
        k0 = pl.multiple_of(kt * tk, tk)
        bias = jnp.where(keys_ref[kt] >= tau, 0.0, NEG_BIG)
        out = []
        for j in range(A_KV_HEADS):
            m, l, acc = carry[3 * j:3 * j + 3]
            kk = ka_ref[pl.ds(k0, tk), j * A_HEAD_DIM:(j + 1) * A_HEAD_DIM]
            vv = va_ref[pl.ds(k0, tk), j * A_HEAD_DIM:(j + 1) * A_HEAD_DIM]
            s = lax.dot_general(qs[j], kk, (((1,), (1,)), ((), ())),
                                preferred_element_type=F32)
            s = (s.reshape(grp, tq, tk) + bias[None]).reshape(grp * tq, tk)
            m_new = jnp.maximum(m, jnp.max(s, axis=1, keepdims=True))
            alpha = jnp.exp2(m - m_new)
            p = jnp.exp2(s - m_new)
            l = alpha * l + jnp.sum(p, axis=1, keepdims=True)
            acc = alpha * acc + jnp.dot(p.astype(BF16), vv, preferred_element_type=F32)
            out += [m_new, l, acc]
        return tuple(out)

    init = []
    for j in range(A_KV_HEADS):
        init += [jnp.full((grp * tq, 1), NEG_BIG, F32), jnp.zeros((grp * tq, 1), F32),
                 jnp.zeros((grp * tq, A_HEAD_DIM), F32)]
    fin = lax.fori_loop(0, n_kt, attn_tile, tuple(init))
    for j in range(A_KV_HEADS):
        out = fin[3 * j + 2] / fin[3 * j + 1]
        for g in range(grp):
            h = j * grp + g
            o_ref[:, h * A_HEAD_DIM:(h + 1) * A_HEAD_DIM] = out[g * tq:(g + 1) * tq].astype(o_ref.dtype)


def _dsa_mixer(z_a, z_p, z_i, z_k, bsz, seq):
    t = bsz * seq
    tq = _tile(seq, 128)
    tk = _tile(seq, 1024)
    nq = seq // tq
    topk = min(TOPK_MAX, seq // 4)
    return pl.pallas_call(
        functools.partial(_dsa_kernel, tq=tq, tk=tk, topk=topk),
        grid=(bsz, nq),
        in_specs=[
            pl.BlockSpec((tq, IDX_Q_W), lambda b, i: (b * nq + i, 0)),
            pl.BlockSpec((tq, LANES), lambda b, i: (b * nq + i, 0)),
            pl.BlockSpec((seq, LANES), lambda b, i: (b, 0)),
            pl.BlockSpec((tq, A_Q_W), lambda b, i: (b * nq + i, 0)),
            pl.BlockSpec((seq, A_KV_W), lambda b, i: (b, A_Q_W // A_KV_W)),
            pl.BlockSpec((seq, A_KV_W), lambda b, i: (b, 0)),
        ],
        out_specs=pl.BlockSpec((tq, A_Q_W), lambda b, i: (b * nq + i, 0)),
        out_shape=jax.ShapeDtypeStruct((t, A_Q_W), BF16),
        scratch_shapes=[pltpu.VMEM((seq // tk, tq, tk), F32),
                        pltpu.VMEM((seq, IDX_DIM), BF16)],
        compiler_params=_cparams(("arbitrary", "arbitrary")),
        name="dsa_mixer",
    )(z_i, z_k, z_k, z_a, z_a, z_p)


def _stick_kernel(q_ref, k_ref, v_ref, o_ref, *, tq, tk, heads):
    i = pl.program_id(2)
    d_idx = (i * tq) // tk
    hd = B_HEAD_DIM
    scale = hd ** -0.5
    qs = [(q_ref[:, h * hd:(h + 1) * hd].astype(F32) * scale).astype(BF16) for h in range(heads)]
    r_io = lax.broadcasted_iota(I32, (tk, tk), 0)
    c_io = lax.broadcasted_iota(I32, (tk, tk), 1)
    upper = jnp.where(r_io > c_io, 1.0, 0.0).astype(BF16)
    diagonal = (d_idx * tk + lax.broadcasted_iota(I32, (tq, tk), 1)
                < i * tq + lax.broadcasted_iota(I32, (tq, tk), 0))

    def scores(kt):
        k0 = pl.multiple_of(kt * tk, tk)
        return tuple(lax.dot_general(qs[h], k_ref[pl.ds(k0, tk), h * hd:(h + 1) * hd],
                                     (((1,), (1,)), ((), ())), preferred_element_type=F32)
                     for h in range(heads))

    def front(z, causal):
        soft = jnp.log(1.0 + jnp.exp(-jnp.abs(z)))
        log_beta = jnp.minimum(z, 0.0) - soft
        log_keep = log_beta - z
        if causal is not None:
            log_keep = jnp.where(causal, log_keep, 0.0)
        hi = log_keep.astype(BF16)
        lo = (log_keep - hi.astype(F32)).astype(BF16)
        gap = (jnp.dot(hi, upper, preferred_element_type=F32)
               + jnp.dot(lo, upper, preferred_element_type=F32))
        logit = log_beta + gap
        if causal is not None:
            logit = jnp.where(causal, logit, NEG_BIG)
        return logit, jnp.sum(log_keep, axis=1, keepdims=True)

    def back(kt, pend, carry):
        k0 = pl.multiple_of(kt * tk, tk)
        out = []
        for h in range(heads):
            logit, keep_sum = pend[2 * h], pend[2 * h + 1]
            run, acc = carry[2 * h], carry[2 * h + 1]
            attn = jnp.exp(logit + run).astype(BF16)
            vv = v_ref[pl.ds(k0, tk), h * hd:(h + 1) * hd]
            out += [run + keep_sum, acc + jnp.dot(attn, vv, preferred_element_type=F32)]
        return tuple(out)

    def fronts(zs, causal):
        out = []
        for h in range(heads):
            out.extend(front(zs[h], causal))
        return tuple(out)

    def step(n, state):
        zs, pend, carry = state
        kt = d_idx - 1 - n
        zs_next = scores(jnp.maximum(kt - 1, 0))
        pend_next = fronts(zs, None)
        return zs_next, pend_next, back(kt + 1, pend, carry)

    init = []
    for h in range(heads):
        init += [jnp.zeros((tq, 1), F32), jnp.zeros((tq, hd), F32)]
    pend = fronts(scores(d_idx), diagonal)
    zs = scores(jnp.maximum(d_idx - 1, 0))
    _, pend, carry = lax.fori_loop(0, d_idx, step, (zs, pend, tuple(init)))
    carry = back(0, pend, carry)
    for h in range(heads):
        o_ref[:, h * hd:(h + 1) * hd] = carry[2 * h + 1].astype(o_ref.dtype)


def _stick_mixer(z_p, bsz, seq):
    t = bsz * seq
    tq = _tile(seq, 256)
    tk = _tile(seq, 256)
    nq = seq // tq
    heads = 2
    width = heads * B_HEAD_DIM
    q_off = A_KV_W // width
    k_off = q_off + B_W // width
    v_off = k_off + B_W // width
    return pl.pallas_call(
        functools.partial(_stick_kernel, tq=tq, tk=tk, heads=heads),
        grid=(bsz, B_HEADS // heads, nq),
        in_specs=[
            pl.BlockSpec((tq, width), lambda b, h, i: (b * nq + i, q_off + h)),
            pl.BlockSpec((seq, width), lambda b, h, i: (b, k_off + h)),
            pl.BlockSpec((seq, width), lambda b, h, i: (b, v_off + h)),
        ],
        out_specs=pl.BlockSpec((tq, width), lambda b, h, i: (b * nq + i, h)),
        out_shape=jax.ShapeDtypeStruct((t, B_W), BF16),
        compiler_params=_cparams(("arbitrary", "arbitrary", "arbitrary")),
        name="stick_mixer",
    )(z_p, z_p, z_p)


def _merge_kernel(oa_ref, ob_ref, g_ref, x_ref, wa_ref, wb_ref, wo_ref, gpost_ref, ga_ref, o_ref):
    d = x_ref.shape[1]
    a = jnp.dot(oa_ref[...], wa_ref[...], preferred_element_type=F32)
    b = jnp.dot(ob_ref[...], wb_ref[...], preferred_element_type=F32)
    merged = g_ref[:, :d].astype(F32) * a + g_ref[:, d:].astype(F32) * b
    mix = jnp.dot(merged.astype(BF16), wo_ref[...], preferred_element_type=F32)
    y = mix * lax.rsqrt(jnp.mean(mix * mix, axis=-1, keepdims=True) + RMS_EPS)
    o_ref[...] = x_ref[...] + ga_ref[0] * (y * gpost_ref[...])


def _merge(o_a, o_b, z_g, x2, w_a, w_b, w_o, g_post, gate, seq):
    t, d = x2.shape
    bsz = gate.shape[0]
    tm = _tile(seq, 256)
    per_b = seq // tm
    const = lambda i: (0, 0)
    return pl.pallas_call(
        _merge_kernel,
        grid=(t // tm,),
        in_specs=[
            pl.BlockSpec((tm, A_Q_W), lambda i: (i, 0)),
            pl.BlockSpec((tm, B_W), lambda i: (i, 0)),
            pl.BlockSpec((tm, 2 * d), lambda i: (i, 0)),
            pl.BlockSpec((tm, d), lambda i: (i, 0)),
            pl.BlockSpec((A_Q_W, d), const),
            pl.BlockSpec((B_W, d), const),
            pl.BlockSpec((d, d), const),
            pl.BlockSpec((1, d), const),
            pl.BlockSpec((1, 1, d), lambda i: (i // per_b, 0, 0)),
        ],
        out_specs=pl.BlockSpec((tm, d), lambda i: (i, 0)),
        out_shape=jax.ShapeDtypeStruct((t, d), F32),
        compiler_params=_cparams(("arbitrary",)),
        name="merge",
    )(o_a, o_b, z_g, x2, w_a, w_b, w_o, g_post.reshape(1, d), gate.reshape(bsz, 1, d))


def _router_kernel(h_ref, wr_ref, br_ref, e_ref, w_ref, r_ref, cnt_ref, base_ref):
    i = pl.program_id(0)
    tm = h_ref.shape[0]

    @pl.when(i == 0)
    def _():
        base_ref[...] = jnp.zeros_like(base_ref)

    logits = jnp.dot(h_ref[...], wr_ref[...], preferred_element_type=F32,
                     precision=lax.Precision.HIGHEST) + br_ref[...]
    lane = lax.broadcasted_iota(I32, (tm, LANES), 1)
    work = logits
    picks, vals = [], []
    for _ in range(TOP_K_EXPERTS):
        m = jnp.max(work, axis=1, keepdims=True)
        idx = jnp.min(jnp.where(work == m, lane, LANES), axis=1, keepdims=True)
        picks.append(idx)
        vals.append(m)
        work = jnp.where(lane == idx, -jnp.inf, work)
    exps = [jnp.exp(v - vals[0]) for v in vals]
    den = exps[0]
    for e in exps[1:]:
        den = den + e

    onehot = jnp.zeros((tm, LANES), F32)
    for idx in picks:
        onehot = onehot + jnp.where(lane == idx, 1.0, 0.0)
    r_io = lax.broadcasted_iota(I32, (tm, tm), 0)
    c_io = lax.broadcasted_iota(I32, (tm, tm), 1)
    lower = jnp.where(c_io < r_io, 1.0, 0.0).astype(BF16)
    prefix = jnp.dot(lower, onehot.astype(BF16), preferred_element_type=F32)
    total = prefix + base_ref[0:1, :]

    e_out = jnp.zeros((tm, LANES), I32)
    w_out = jnp.zeros((tm, LANES), F32)
    r_out = jnp.zeros((tm, LANES), I32)
    for k in range(TOP_K_EXPERTS):
        rank = jnp.sum(jnp.where(lane == picks[k], total, 0.0), axis=1, keepdims=True)
        e_out = jnp.where(lane == k, picks[k], e_out)
        w_out = jnp.where(lane == k, exps[k] / den, w_out)
        r_out = jnp.where(lane == k, rank.astype(I32), r_out)
    e_ref[...] = e_out
    w_ref[...] = w_out
    r_ref[...] = r_out
    new_base = base_ref[0:1, :] + jnp.sum(onehot, axis=0, keepdims=True)
    base_ref[...] = jnp.broadcast_to(new_base, base_ref.shape)
    cnt_ref[...] = jnp.broadcast_to(new_base, cnt_ref.shape)


def _route(h2, w_router, b_router):
    t, d = h2.shape
    tm = _tile(t, 512)
    wr = jnp.zeros((d, LANES), F32).at[:, :N_EXPERTS].set(w_router)
    br = jnp.full((1, LANES), NEG_BIG, F32).at[0, :N_EXPERTS].set(b_router)
    row = pl.BlockSpec((tm, LANES), lambda i: (i, 0))
    return pl.pallas_call(
        _router_kernel,
        grid=(t // tm,),
        in_specs=[pl.BlockSpec((tm, d), lambda i: (i, 0)),
                  pl.BlockSpec((d, LANES), lambda i: (0, 0)),
                  pl.BlockSpec((1, LANES), lambda i: (0, 0))],
        out_specs=[row, row, row, pl.BlockSpec((8, LANES), lambda i: (0, 0))],
        out_shape=[jax.ShapeDtypeStruct((t, LANES), I32),
                   jax.ShapeDtypeStruct((t, LANES), F32),
                   jax.ShapeDtypeStruct((t, LANES), I32),
                   jax.ShapeDtypeStruct((8, LANES), F32)],
        scratch_shapes=[pltpu.VMEM((8, LANES), F32)],
        compiler_params=_cparams(("arbitrary",)),
        name="router",
    )(h2, wr, br)


def _dispatch_kernel(dest_ref, h_ref, init_ref, xs_ref, sem):
    del init_ref
    tm = h_ref.shape[0]

    def copy(t, k):
        return pltpu.make_async_copy(h_ref.at[pl.ds(t, 1)],
                                     xs_ref.at[pl.ds(dest_ref[0, 0, t * TOP_K_EXPERTS + k], 1)], sem)

    def issue(t, c):
        for k in range(TOP_K_EXPERTS):
            copy(t, k).start()
        return c

    def drain(t, c):
        for k in range(TOP_K_EXPERTS):
            copy(t, k).wait()
        return c

    lax.fori_loop(0, tm, issue, 0)
    lax.fori_loop(0, tm, drain, 0)


def _dispatch(h2, dest, n_rows):
    t, d = h2.shape
    tm = _tile(t, 256)
    nt = t // tm
    dest3 = dest.reshape(nt, 1, tm * TOP_K_EXPERTS)
    init = jnp.zeros((n_rows, d), h2.dtype)
    return pl.pallas_call(
        _dispatch_kernel,
        grid=(nt,),
        in_specs=[pl.BlockSpec((1, 1, tm * TOP_K_EXPERTS), lambda i: (i, 0, 0), memory_space=pltpu.SMEM),
                  pl.BlockSpec((tm, d), lambda i: (i, 0)),
                  pl.BlockSpec(memory_space=pl.ANY)],
        out_specs=pl.BlockSpec(memory_space=pl.ANY),
        out_shape=jax.ShapeDtypeStruct((n_rows, d), h2.dtype),
        scratch_shapes=[pltpu.SemaphoreType.DMA(())],
        input_output_aliases={2: 0},
        compiler_params=pltpu.CompilerParams(dimension_semantics=("arbitrary",),
                                             vmem_limit_bytes=VMEM_LIMIT, has_side_effects=True),
        name="moe_dispatch",
    )(dest3, h2, init)


def _deinterleave_kernel(w_ref, p_ref, g_ref, l_ref):
    p = p_ref[...]
    width = p.shape[0]
    half = width // 2
    for c in range(w_ref.shape[2] // width):
        w = w_ref[0, :, c * width:(c + 1) * width].astype(BF16)
        r = jnp.dot(w, p, preferred_element_type=F32)
        g_ref[0, :, c * half:(c + 1) * half] = r[:, :half].astype(BF16)
        l_ref[0, :, c * half:(c + 1) * half] = r[:, half:].astype(BF16)


def _deinterleave(w1):
    n_e, d, f2 = w1.shape
    width = 2 * LANES
    src = jnp.arange(width)
    dst = jnp.where(src % 2 == 0, src // 2, LANES + src // 2)
    perm = (dst[:, None] == jnp.arange(width)[None, :]).astype(BF16)
    tr = _tile(d, 256)
    out = pl.BlockSpec((1, tr, f2 // 2), lambda e, r: (e, r, 0))
    return pl.pallas_call(
        _deinterleave_kernel,
        grid=(n_e, d // tr),
        in_specs=[pl.BlockSpec((1, tr, f2), lambda e, r: (e, r, 0)),
                  pl.BlockSpec((width, width), lambda e, r: (0, 0))],
        out_specs=[out, out],
        out_shape=[jax.ShapeDtypeStruct((n_e, d, f2 // 2), BF16)] * 2,
        compiler_params=_cparams(("arbitrary", "arbitrary")),
        name="w1_deinterleave",
    )(w1, perm)


def _expert_kernel(be_ref, nu_ref, x_ref, w1g_ref, w1l_ref, b1g_ref, b1l_ref, w2_ref, b2_ref,
                   o_ref, xb_ref, acc_ref):
    i = pl.program_id(0)
    j = pl.program_id(1)
    nf = pl.num_programs(1)
    valid = i < nu_ref[0]

    @pl.when(jnp.logical_and(valid, j == 0))
    def _():
        xb_ref[...] = x_ref[...].astype(BF16)
        acc_ref[...] = jnp.zeros_like(acc_ref)

    @pl.when(valid)
    def _():
        xb = xb_ref[...]
        glu = jnp.dot(xb, w1g_ref[0], preferred_element_type=F32) + b1g_ref[0]
        lin = jnp.dot(xb, w1l_ref[0], preferred_element_type=F32) + b1l_ref[0]
        glu = jnp.minimum(glu, SWIGLU_LIMIT)
        lin = jnp.clip(lin, -SWIGLU_LIMIT, SWIGLU_LIMIT)
        act = glu * jax.nn.sigmoid(SWIGLU_ALPHA * glu) * (lin + 1.0)
        acc_ref[...] += jnp.dot(act.astype(BF16), w2_ref[0], preferred_element_type=F32)

    @pl.when(jnp.logical_and(valid, j == nf - 1))
    def _():
        o_ref[...] = acc_ref[...] + b2_ref[0]


def _experts(xs, block_e, n_used, w1g, w1l, b1g, b1l, w2, b2):
    n_rows, d = xs.shape
    n_e, _, f = w1g.shape
    n_blocks = n_rows // MOE_ROWS
    tf = _tile(f, 512)
    nf = f // tf

    def blk(i, nu):
        return jnp.minimum(i, nu[0] - 1)

    def ftile(i, j, nu):
        return jnp.where(i < nu[0], j, nf - 1)

    grid_spec = pltpu.PrefetchScalarGridSpec(
        num_scalar_prefetch=2,
        grid=(n_blocks, nf),
        in_specs=[
            pl.BlockSpec((MOE_ROWS, d), lambda i, j, be, nu: (blk(i, nu), 0)),
            pl.BlockSpec((1, d, tf), lambda i, j, be, nu: (be[blk(i, nu)], 0, ftile(i, j, nu))),
            pl.BlockSpec((1, d, tf), lambda i, j, be, nu: (be[blk(i, nu)], 0, ftile(i, j, nu))),
            pl.BlockSpec((1, 1, tf), lambda i, j, be, nu: (be[blk(i, nu)], 0, ftile(i, j, nu))),
            pl.BlockSpec((1, 1, tf), lambda i, j, be, nu: (be[blk(i, nu)], 0, ftile(i, j, nu))),
            pl.BlockSpec((1, tf, d), lambda i, j, be, nu: (be[blk(i, nu)], ftile(i, j, nu), 0)),
            pl.BlockSpec((1, 1, d), lambda i, j, be, nu: (be[blk(i, nu)], 0, 0)),
        ],
        out_specs=pl.BlockSpec((MOE_ROWS, d), lambda i, j, be, nu: (blk(i, nu), 0)),
        scratch_shapes=[pltpu.VMEM((MOE_ROWS, d), BF16), pltpu.VMEM((MOE_ROWS, d), F32)],
    )
    return pl.pallas_call(
        _expert_kernel,
        grid_spec=grid_spec,
        out_shape=jax.ShapeDtypeStruct((n_rows, d), F32),
        compiler_params=_cparams(("arbitrary", "arbitrary")),
        name="moe_experts",
    )(block_e, n_used, xs, w1g, w1l, b1g, b1l, w2, b2)


def _combine_kernel(dest_ref, tw_ref, x_ref, gpost_ref, ga_ref, ys_ref, o_ref, buf_ref, sem):
    tm = x_ref.shape[0]

    def copy(t, k):
        return pltpu.make_async_copy(ys_ref.at[pl.ds(dest_ref[0, 0, t * TOP_K_EXPERTS + k], 1)],
                                     buf_ref.at[k, pl.ds(t, 1)], sem)

    def issue(t, c):
        for k in range(TOP_K_EXPERTS):
            copy(t, k).start()
        return c

    def drain(t, c):
        for k in range(TOP_K_EXPERTS):
            copy(t, k).wait()
        return c

    lax.fori_loop(0, tm, issue, 0)
    lax.fori_loop(0, tm, drain, 0)

    tw = tw_ref[...]
    f = tw[:, 0:1] * buf_ref[0]
    for k in range(1, TOP_K_EXPERTS):
        f = f + tw[:, k:k + 1] * buf_ref[k]
    y = f * lax.rsqrt(jnp.mean(f * f, axis=-1, keepdims=True) + RMS_EPS)
    o_ref[...] = x_ref[...] + ga_ref[0] * (y * gpost_ref[...])


def _combine(ys, dest, top_w, x1, g_post, gate, seq):
    t, d = x1.shape
    bsz = gate.shape[0]
    tm = _tile(seq, 256)
    nt = t // tm
    per_b = seq // tm
    dest3 = dest.reshape(nt, 1, tm * TOP_K_EXPERTS)
    return pl.pallas_call(
        _combine_kernel,
        grid=(nt,),
        in_specs=[pl.BlockSpec((1, 1, tm * TOP_K_EXPERTS), lambda i: (i, 0, 0), memory_space=pltpu.SMEM),
                  pl.BlockSpec((tm, LANES), lambda i: (i, 0)),
                  pl.BlockSpec((tm, d), lambda i: (i, 0)),
                  pl.BlockSpec((1, d), lambda i: (0, 0)),
                  pl.BlockSpec((1, 1, d), lambda i: (i // per_b, 0, 0)),
                  pl.BlockSpec(memory_space=pl.ANY)],
        out_specs=pl.BlockSpec((tm, d), lambda i: (i, 0)),
        out_shape=jax.ShapeDtypeStruct((t, d), F32),
        scratch_shapes=[pltpu.VMEM((TOP_K_EXPERTS, tm, d), F32), pltpu.SemaphoreType.DMA(())],
        compiler_params=_cparams(("arbitrary",)),
        name="moe_combine",
    )(dest3, top_w, x1, g_post.reshape(1, d), gate.reshape(bsz, 1, d), ys)


def _moe(h2, x1, w_router, b_router, w1, b1, w2, b2, g_post, gate, seq):
    t, d = h2.shape
    top_e, top_w, rank, counts = _route(h2, w_router, b_router)
    top_e = top_e[:, :TOP_K_EXPERTS]
    rank = rank[:, :TOP_K_EXPERTS]
    counts = counts[0, :N_EXPERTS].astype(I32)

    n_blocks = (t * TOP_K_EXPERTS) // MOE_ROWS + N_EXPERTS
    padded = (counts + MOE_ROWS - 1) // MOE_ROWS * MOE_ROWS
    pad_end = jnp.cumsum(padded)
    pad_start = pad_end - padded
    dest = (pad_start[top_e] + rank).astype(I32)
    block_e = jnp.minimum(jnp.searchsorted(pad_end, jnp.arange(n_blocks, dtype=I32) * MOE_ROWS,
                                           side="right"), N_EXPERTS - 1).astype(I32)
    n_used = (pad_end[-1:] // MOE_ROWS).astype(I32)

    xs = _dispatch(h2, dest, n_blocks * MOE_ROWS)
    w1g, w1l = _deinterleave(w1)
    n_e, f2 = b1.shape
    b1g = b1[:, 0::2].reshape(n_e, 1, f2 // 2)
    b1l = b1[:, 1::2].reshape(n_e, 1, f2 // 2)
    ys = _experts(xs, block_e, n_used, w1g, w1l, b1g, b1l, w2.astype(BF16), b2.reshape(n_e, 1, d))
    return _combine(ys, dest, top_w, x1, g_post, gate, seq)


def _layer(x2, c, tables, bsz, seq, w_ada, b_ada, g_pre_mix, g_post_mix, w_in, w_branch_a,
           w_branch_b, w_out, g_pre_ffn, g_post_ffn, w_router, b_router, w1, b1, w2, b2):
    d = x2.shape[1]
    mod = _modulation(c, w_ada, b_ada)
    sh1, sc1, ga1, sh2, sc2, ga2 = [mod[:, k * d:(k + 1) * d] for k in range(6)]
    cos_a, sin_a, cos_i, sin_i = tables

    h = _prenorm(x2, g_pre_mix, sc1, sh1, seq, BF16)
    wb = w_in.astype(BF16)
    o = 0
    w_qka = wb[:, o:o + A_Q_W + A_KV_W]
    o += A_Q_W + A_KV_W
    w_plain = wb[:, o:o + A_KV_W + 3 * B_W]
    o += A_KV_W + 3 * B_W
    w_qi = wb[:, o:o + IDX_Q_W]
    o += IDX_Q_W
    w_kw = jnp.zeros((d, LANES), BF16).at[:, :IDX_DIM + IDX_HEADS].set(wb[:, o:o + IDX_DIM + IDX_HEADS])
    o += IDX_DIM + IDX_HEADS
    w_gate = wb[:, o:o + 2 * d]

    z_a = _project(h, w_qka, "rope_a", BF16, (cos_a, sin_a), tn_pref=640)
    z_p = _project(h, w_plain, "plain", BF16, tn_pref=1664)
    z_i = _project(h, w_qi, "rope_idx", BF16, (cos_i, sin_i))
    z_k = _project(h, w_kw, "rope_key_idx", F32, (cos_i, sin_i))
    z_g = _project(h, w_gate, "sigmoid", BF16, tn_pref=1024)

    o_a = _dsa_mixer(z_a, z_p, z_i, z_k, bsz, seq)
    o_b = _stick_mixer(z_p, bsz, seq)
    x1 = _merge(o_a, o_b, z_g, x2, w_branch_a.astype(BF16), w_branch_b.astype(BF16),
                w_out.astype(BF16), g_post_mix, ga1, seq)

    h2 = _prenorm(x1, g_pre_ffn, sc2, sh2, seq, F32)
    return _moe(h2, x1, w_router, b_router, w1, b1, w2, b2, g_post_ffn, ga2, seq)


def kernel(x, c, positions, w_ada, b_ada, g_pre_mix, g_post_mix, w_in, w_branch_a, w_branch_b, w_out, g_pre_ffn, g_post_ffn, w_router, b_router, w1, b1, w2, b2):
    bsz, seq, d = x.shape
    x2 = x.reshape(bsz * seq, d)
    tables = _rope_tables(positions)
    for l in range(w_ada.shape[0]):
        x2 = _layer(x2, c, tables, bsz, seq, w_ada[l], b_ada[l], g_pre_mix[l], g_post_mix[l], w_in[l],
                    w_branch_a[l], w_branch_b[l], w_out[l], g_pre_ffn[l], g_post_ffn[l],
                    w_router[l], b_router[l], w1[l], b1[l], w2[l], b2[l])
    return x2.reshape(bsz, seq, d)
```

```python
import functools

import jax
import jax.numpy as jnp
from jax import lax
from jax.experimental import pallas as pl
from jax.experimental.pallas import tpu as pltpu

F32 = jnp.float32
BF16 = jnp.bfloat16
I32 = jnp.int32

CHUNK = 64
RMS_EPS = 1e-6
ROPE_THETA = 500000.0
A_HEADS = 8
A_KV_HEADS = 2
A_HEAD_DIM = 128
A_ROT_DIM = A_HEAD_DIM // 4
IDX_HEADS = 16
IDX_DIM = 64
IDX_ROT_DIM = IDX_DIM // 4
TOPK_MAX = 256
B_HEADS = 8
B_HEAD_DIM = 128
N_EXPERTS = 32
TOP_K_EXPERTS = 4
SWIGLU_LIMIT = 7.0
SWIGLU_ALPHA = 1.702

A_Q_W = A_HEADS * A_HEAD_DIM
A_KV_W = A_KV_HEADS * A_HEAD_DIM
B_W = B_HEADS * B_HEAD_DIM
IDX_Q_W = IDX_HEADS * IDX_DIM

LANES = 128
VMEM_LIMIT = 56 * 1024 * 1024
NEG_BIG = -1e30
LOG2_E = 1.4426950408889634
F32_MAX = 3.4028234663852886e38
BISECT_CAP = 320
STICK_DEAD = -106.0
MOE_ROWS = 512


def _cparams(sem):
    return pltpu.CompilerParams(dimension_semantics=sem, vmem_limit_bytes=VMEM_LIMIT)


def _tile(n, pref):
    t = min(n, pref)
    assert n % t == 0, (n, t)
    return t


def _mod_kernel(c_ref, w_ref, b_ref, o_ref):
    c = c_ref[...]
    s = c * jax.nn.sigmoid(c)
    o_ref[...] = jnp.dot(s, w_ref[...], preferred_element_type=F32,
                         precision=lax.Precision.HIGHEST) + b_ref[...]


def _modulation(c, w_ada, b_ada):
    bsz, d = c.shape
    n = w_ada.shape[1]
    rows = 8
    c_pad = jnp.zeros((rows, d), F32).at[:bsz].set(c)
    tn = _tile(n, 1024)
    out = pl.pallas_call(
        _mod_kernel,
        grid=(n // tn,),
        in_specs=[pl.BlockSpec((rows, d), lambda j: (0, 0)),
                  pl.BlockSpec((d, tn), lambda j: (0, j)),
                  pl.BlockSpec((1, tn), lambda j: (0, j))],
        out_specs=pl.BlockSpec((rows, tn), lambda j: (0, j)),
        out_shape=jax.ShapeDtypeStruct((rows, n), F32),
        compiler_params=_cparams(("arbitrary",)),
        name="modulation",
    )(c_pad, w_ada, b_ada.reshape(1, n))
    return out[:bsz]


def _rope_table_kernel(pos_ref, fa_ref, sa_ref, fi_ref, si_ref, ca_o, sa_o, ci_o, si_o):
    pos = pos_ref[...].astype(F32)
    ang_a = pos * fa_ref[...]
    ca_o[...] = jnp.cos(ang_a)
    sa_o[...] = jnp.sin(ang_a) * sa_ref[...]
    ang_i = pos * fi_ref[...]
    ci_o[...] = jnp.cos(ang_i)
    si_o[...] = jnp.sin(ang_i) * si_ref[...]


def _rope_pattern(rot_dim, head_dim):
    half = rot_dim // 2
    inv_freq = ROPE_THETA ** (-jnp.arange(half, dtype=F32) / half)
    freq = jnp.concatenate([inv_freq, inv_freq, jnp.zeros((head_dim - rot_dim,), F32)])
    sign = jnp.concatenate([-jnp.ones((half,), F32), jnp.ones((half,), F32),
                            jnp.zeros((head_dim - rot_dim,), F32)])
    reps = LANES // head_dim
    return jnp.tile(freq, reps).reshape(1, LANES), jnp.tile(sign, reps).reshape(1, LANES)


def _rope_tables(positions):
    t = positions.size
    pos = positions.reshape(t, 1).astype(I32)
    fa, sa = _rope_pattern(A_ROT_DIM, A_HEAD_DIM)
    fi, si = _rope_pattern(IDX_ROT_DIM, IDX_DIM)
    tm = _tile(t, 1024)
    pat = pl.BlockSpec((1, LANES), lambda i: (0, 0))
    tab = pl.BlockSpec((tm, LANES), lambda i: (i, 0))
    return pl.pallas_call(
        _rope_table_kernel,
        grid=(t // tm,),
        in_specs=[pl.BlockSpec((tm, 1), lambda i: (i, 0)), pat, pat, pat, pat],
        out_specs=[tab, tab, tab, tab],
        out_shape=[jax.ShapeDtypeStruct((t, LANES), F32)] * 4,
        compiler_params=_cparams(("arbitrary",)),
        name="rope_tables",
    )(pos, fa, sa, fi, si)


def _prenorm_kernel(x_ref, g_ref, sc_ref, sh_ref, o_ref):
    x = x_ref[...]
    y = x * lax.rsqrt(jnp.mean(x * x, axis=-1, keepdims=True) + RMS_EPS)
    h = (y * g_ref[...]) * (1.0 + sc_ref[0]) + sh_ref[0]
    o_ref[...] = h.astype(o_ref.dtype)


def _prenorm(x2, g, scale, shift, seq, out_dtype):
    t, d = x2.shape
    bsz = scale.shape[0]
    tm = _tile(seq, 512)
    per_b = seq // tm
    row = pl.BlockSpec((tm, d), lambda i: (i, 0))
    bvec = pl.BlockSpec((1, 1, d), lambda i: (i // per_b, 0, 0))
    return pl.pallas_call(
        _prenorm_kernel,
        grid=(t // tm,),
        in_specs=[row, pl.BlockSpec((1, d), lambda i: (0, 0)), bvec, bvec],
        out_specs=row,
        out_shape=jax.ShapeDtypeStruct((t, d), out_dtype),
        compiler_params=_cparams(("arbitrary",)),
        name="prenorm",
    )(x2, g.reshape(1, d), scale.reshape(bsz, 1, d), shift.reshape(bsz, 1, d))


def _rotate(x, cos, sin, period, half):
    lane = lax.broadcasted_iota(I32, x.shape, 1) & (period - 1)
    swapped = jnp.where(lane < half, pltpu.roll(x, LANES - half, 1), pltpu.roll(x, half, 1))
    return x * cos + swapped * sin


def _proj_kernel(*refs, epilogue):
    if epilogue in ("plain", "sigmoid"):
        h_ref, w_ref, o_ref = refs
    else:
        h_ref, w_ref, cos_ref, sin_ref, o_ref = refs
    acc = jnp.dot(h_ref[...], w_ref[...], preferred_element_type=F32)
    if epilogue == "plain":
        o_ref[...] = acc.astype(o_ref.dtype)
    elif epilogue == "sigmoid":
        o_ref[...] = jax.nn.sigmoid(acc).astype(o_ref.dtype)
    else:
        cos = cos_ref[...]
        sin = sin_ref[...]
        if epilogue == "rope_a":
            period, half = A_HEAD_DIM, A_ROT_DIM // 2
        else:
            period, half = IDX_DIM, IDX_ROT_DIM // 2
        if epilogue == "rope_key_idx":
            lane = lax.broadcasted_iota(I32, cos.shape, 1)
            cos = jnp.where(lane < IDX_DIM, cos, 1.0)
            sin = jnp.where(lane < IDX_DIM, sin, 0.0)
        for c in range(acc.shape[1] // LANES):
            sl = slice(c * LANES, (c + 1) * LANES)
            o_ref[:, sl] = _rotate(acc[:, sl], cos, sin, period, half).astype(o_ref.dtype)


def _project(h, w, epilogue, out_dtype, tables=None, tn_pref=512):
    t, d = h.shape
    n = w.shape[1]
    tm = _tile(t, 1024)
    tn = n
    for cand in (tn_pref, 256, 128):
        if n % cand == 0:
            tn = cand
            break
    in_specs = [pl.BlockSpec((tm, d), lambda i, j: (i, 0)),
                pl.BlockSpec((d, tn), lambda i, j: (0, j))]
    args = [h, w]
    if tables is not None:
        tab = pl.BlockSpec((tm, LANES), lambda i, j: (i, 0))
        in_specs += [tab, tab]
        args += list(tables)
    return pl.pallas_call(
        functools.partial(_proj_kernel, epilogue=epilogue),
        grid=(t // tm, n // tn),
        in_specs=in_specs,
        out_specs=pl.BlockSpec((tm, tn), lambda i, j: (i, j)),
        out_shape=jax.ShapeDtypeStruct((t, n), out_dtype),
        compiler_params=_cparams(("arbitrary", "arbitrary")),
        name="proj_" + epilogue,
    )(*args)


def _dsa_kernel(qi_ref, kwq_ref, kwk_ref, qa_ref, ka_ref, va_ref, o_ref,
                keys_ref, kib_ref, *, tq, tk, topk):
    i = pl.program_id(1)
    t0 = i * tq
    n_kt = (t0 + tq + tk - 1) // tk
    grp = A_HEADS // A_KV_HEADS

    @pl.when(i == 0)
    def _():
        kib_ref[...] = kwk_ref[:, :IDX_DIM].astype(BF16)

    qi = qi_ref[...]
    wq = kwq_ref[...]
    row = t0 + lax.broadcasted_iota(I32, (tq, 1), 0)
    limit = (row // CHUNK + 1) * CHUNK

    def score_tile(kt, carry):
        hi_acc, lo_acc = carry
        k0 = pl.multiple_of(kt * tk, tk)
        kb = kib_ref[pl.ds(k0, tk), :]
        acc = jnp.zeros((tq, tk), F32)
        for h in range(IDX_HEADS):
            d = lax.dot_general(qi[:, h * IDX_DIM:(h + 1) * IDX_DIM], kb,
                                (((1,), (1,)), ((), ())), preferred_element_type=F32)
            acc = acc + jnp.maximum(d, 0.0) * wq[:, IDX_DIM + h:IDX_DIM + h + 1]
        adm = (k0 + lax.broadcasted_iota(I32, (tq, tk), 1)) < limit
        s_hi = jnp.where(adm, acc, -jnp.inf)
        s_lo = jnp.where(adm, acc, jnp.inf)
        keys_ref[kt] = s_hi
        for c in range(tk // LANES):
            hi_acc = jnp.maximum(hi_acc, s_hi[:, c * LANES:(c + 1) * LANES])
            lo_acc = jnp.minimum(lo_acc, s_lo[:, c * LANES:(c + 1) * LANES])
        return hi_acc, lo_acc

    hi_acc, lo_acc = lax.fori_loop(
        0, n_kt, score_tile,
        (jnp.full((tq, LANES), -jnp.inf, F32), jnp.full((tq, LANES), jnp.inf, F32)))
    row_max = jnp.max(hi_acc, axis=1, keepdims=True)
    row_min = jnp.min(lo_acc, axis=1, keepdims=True)

    def count_ge(cand):
        def count_tile(kt, cnt):
            ge = (keys_ref[kt] >= cand).astype(I32)
            for c in range(tk // LANES):
                cnt = cnt + ge[:, c * LANES:(c + 1) * LANES]
            return cnt

        cnt = lax.fori_loop(0, n_kt, count_tile, jnp.zeros((tq, LANES), I32))
        return jnp.sum(cnt, axis=1, keepdims=True)

    def bisect_cond(state):
        it, pending = state[0], state[1]
        return jnp.logical_and(it < BISECT_CAP, pending > 0)

    def bisect_step(state):
        it, _, lo, hi, tau, todo = state
        mid = 0.5 * lo + 0.5 * hi
        stuck = jnp.logical_or(mid <= lo, mid >= hi)
        cnt = count_ge(mid)
        fin = jnp.logical_or(stuck, cnt == topk)
        tau = jnp.where(jnp.logical_and(todo > 0, fin), jnp.where(stuck, lo, mid), tau)
        up = cnt >= topk
        lo = jnp.where(up, mid, lo)
        hi = jnp.where(up, hi, mid)
        todo = jnp.where(fin, 0, todo)
        return it + 1, jnp.max(todo), lo, hi, tau, todo

    todo0 = jnp.where(limit > topk, 1, 0).astype(I32)
    hi0 = row_max + jnp.maximum(jnp.abs(row_max) * 1e-6, 1e-30)
    state = (jnp.int32(0), jnp.max(todo0), row_min, hi0,
             jnp.full((tq, 1), -F32_MAX, F32), todo0)
    _, _, lo, _, tau, todo = lax.while_loop(bisect_cond, bisect_step, state)
    tau = jnp.where(todo > 0, lo, tau)

    qscale = A_HEAD_DIM ** -0.5 * LOG2_E
    qs = [jnp.concatenate(
        [(qa_ref[:, (j * grp + g) * A_HEAD_DIM:(j * grp + g + 1) * A_HEAD_DIM].astype(F32)
          * qscale).astype(BF16) for g in range(grp)], axis=0)
        for j in range(A_KV_HEADS)]

    def attn_tile(kt, carry):
        k0 = pl.multiple_of(kt * tk, tk)
        bias = jnp.where(keys_ref[kt] >= tau, 0.0, NEG_BIG)
        out = []
        for j in range(A_KV_HEADS):
            m, l, acc = carry[3 * j:3 * j + 3]
            kk = ka_ref[pl.ds(k0, tk), j * A_HEAD_DIM:(j + 1) * A_HEAD_DIM]
            vv = va_ref[pl.ds(k0, tk), j * A_HEAD_DIM:(j + 1) * A_HEAD_DIM]
            s = lax.dot_general(qs[j], kk, (((1,), (1,)), ((), ())),
                                preferred_element_type=F32)
            s = (s.reshape(grp, tq, tk) + bias[None]).reshape(grp * tq, tk)
            m_new = jnp.maximum(m, jnp.max(s, axis=1, keepdims=True))
            alpha = jnp.exp2(m - m_new)
            p = jnp.exp2(s - m_new)
            l = alpha * l + jnp.sum(p, axis=1, keepdims=True)
            acc = alpha * acc + jnp.dot(p.astype(BF16), vv, preferred_element_type=F32)
            out += [m_new, l, acc]
        return tuple(out)

    init = []
    for j in range(A_KV_HEADS):
        init += [jnp.full((grp * tq, 1), NEG_BIG, F32), jnp.zeros((grp * tq, 1), F32),
                 jnp.zeros((grp * tq, A_HEAD_DIM), F32)]
    fin = lax.fori_loop(0, n_kt, attn_tile, tuple(init))
    for j in range(A_KV_HEADS):
        out = fin[3 * j + 2] / fin[3 * j + 1]
        for g in range(grp):
            h = j * grp + g
            o_ref[:, h * A_HEAD_DIM:(h + 1) * A_HEAD_DIM] = out[g * tq:(g + 1) * tq].astype(o_ref.dtype)


def _dsa_mixer(z_a, z_p, z_i, z_k, bsz, seq):
    t = bsz * seq
    tq = _tile(seq, 128)
    tk = _tile(seq, 1024)
    nq = seq // tq
    topk = min(TOPK_MAX, seq // 4)
    return pl.pallas_call(
        functools.partial(_dsa_kernel, tq=tq, tk=tk, topk=topk),
        grid=(bsz, nq),
        in_specs=[
            pl.BlockSpec((tq, IDX_Q_W), lambda b, i: (b * nq + i, 0)),
            pl.BlockSpec((tq, LANES), lambda b, i: (b * nq + i, 0)),
            pl.BlockSpec((seq, LANES), lambda b, i: (b, 0)),
            pl.BlockSpec((tq, A_Q_W), lambda b, i: (b * nq + i, 0)),
            pl.BlockSpec((seq, A_KV_W), lambda b, i: (b, A_Q_W // A_KV_W)),
            pl.BlockSpec((seq, A_KV_W), lambda b, i: (b, 0)),
        ],
        out_specs=pl.BlockSpec((tq, A_Q_W), lambda b, i: (b * nq + i, 0)),
        out_shape=jax.ShapeDtypeStruct((t, A_Q_W), BF16),
        scratch_shapes=[pltpu.VMEM((seq // tk, tq, tk), F32),
                        pltpu.VMEM((seq, IDX_DIM), BF16)],
        compiler_params=_cparams(("arbitrary", "arbitrary")),
        name="dsa_mixer",
    )(z_i, z_k, z_k, z_a, z_a, z_p)


def _stick_kernel(q_ref, k_ref, v_ref, o_ref, *, tq, tk, heads):
    i = pl.program_id(2)
    d_idx = (i * tq) // tk
    hd = B_HEAD_DIM
    scale = hd ** -0.5
    qs = [(q_ref[:, h * hd:(h + 1) * hd].astype(F32) * scale).astype(BF16) for h in range(heads)]
    r_io = lax.broadcasted_iota(I32, (tk, tk), 0)
    c_io = lax.broadcasted_iota(I32, (tk, tk), 1)
    upper = jnp.where(r_io > c_io, 1.0, 0.0).astype(BF16)
    diagonal = (d_idx * tk + lax.broadcasted_iota(I32, (tq, tk), 1)
                < i * tq + lax.broadcasted_iota(I32, (tq, tk), 0))

    def scores(kt):
        k0 = pl.multiple_of(kt * tk, tk)
        return tuple(lax.dot_general(qs[h], k_ref[pl.ds(k0, tk), h * hd:(h + 1) * hd],
                                     (((1,), (1,)), ((), ())), preferred_element_type=F32)
                     for h in range(heads))

    def front(z, causal):
        soft = jnp.log(1.0 + jnp.exp(-jnp.abs(z)))
        log_beta = jnp.minimum(z, 0.0) - soft
        log_keep = log_beta - z
        if causal is not None:
            log_keep = jnp.where(causal, log_keep, 0.0)
        hi = log_keep.astype(BF16)
        lo = (log_keep - hi.astype(F32)).astype(BF16)
        gap = (jnp.dot(hi, upper, preferred_element_type=F32)
               + jnp.dot(lo, upper, preferred_element_type=F32))
        logit = log_beta + gap
        if causal is not None:
            logit = jnp.where(causal, logit, NEG_BIG)
        return logit, jnp.sum(log_keep, axis=1, keepdims=True)

    def back(kt, pend, carry):
        k0 = pl.multiple_of(kt * tk, tk)
        out = []
        for h in range(heads):
            logit, keep_sum = pend[2 * h], pend[2 * h + 1]
            run, acc = carry[2 * h], carry[2 * h + 1]
            attn = jnp.exp(logit + run).astype(BF16)
            vv = v_ref[pl.ds(k0, tk), h * hd:(h + 1) * hd]
            out += [run + keep_sum, acc + jnp.dot(attn, vv, preferred_element_type=F32)]
        return tuple(out)

    def fronts(zs, causal):
        out = []
        for h in range(heads):
            out.extend(front(zs[h], causal))
        return tuple(out)

    def step(state):
        n, _, zs, pend, carry = state
        kt = d_idx - 1 - n
        zs_next = scores(jnp.maximum(kt - 1, 0))
        pend_next = fronts(zs, None)
        carry = back(kt + 1, pend, carry)
        top = carry[0]
        for h in range(1, heads):
            top = jnp.maximum(top, carry[2 * h])
        return n + 1, jnp.max(top), zs_next, pend_next, carry

    def alive(state):
        return jnp.logical_and(state[0] < d_idx, state[1] > STICK_DEAD)

    init = []
    for h in range(heads):
        init += [jnp.zeros((tq, 1), F32), jnp.zeros((tq, hd), F32)]
    pend = fronts(scores(d_idx), diagonal)
    zs = scores(jnp.maximum(d_idx - 1, 0))
    n_done, _, _, pend, carry = lax.while_loop(
        alive, step, (jnp.int32(0), jnp.float32(0.0), zs, pend, tuple(init)))
    carry = back(d_idx - n_done, pend, carry)
    for h in range(heads):
        o_ref[:, h * hd:(h + 1) * hd] = carry[2 * h + 1].astype(o_ref.dtype)


def _stick_mixer(z_p, bsz, seq):
    t = bsz * seq
    tq = _tile(seq, 256)
    tk = _tile(seq, 256)
    nq = seq // tq
    heads = 2
    width = heads * B_HEAD_DIM
    assert A_KV_W % width == 0 and B_W % width == 0
    q_off = A_KV_W // width
    k_off = q_off + B_W // width
    v_off = k_off + B_W // width
    return pl.pallas_call(
        functools.partial(_stick_kernel, tq=tq, tk=tk, heads=heads),
        grid=(bsz, B_HEADS // heads, nq),
        in_specs=[
            pl.BlockSpec((tq, width), lambda b, h, i: (b * nq + i, q_off + h)),
            pl.BlockSpec((seq, width), lambda b, h, i: (b, k_off + h)),
            pl.BlockSpec((seq, width), lambda b, h, i: (b, v_off + h)),
        ],
        out_specs=pl.BlockSpec((tq, width), lambda b, h, i: (b * nq + i, h)),
        out_shape=jax.ShapeDtypeStruct((t, B_W), BF16),
        compiler_params=_cparams(("arbitrary", "arbitrary", "arbitrary")),
        name="stick_mixer",
    )(z_p, z_p, z_p)


def _merge_kernel(oa_ref, ob_ref, g_ref, x_ref, wa_ref, wb_ref, wo_ref, gpost_ref, ga_ref, o_ref):
    d = x_ref.shape[1]
    a = jnp.dot(oa_ref[...], wa_ref[...], preferred_element_type=F32)
    b = jnp.dot(ob_ref[...], wb_ref[...], preferred_element_type=F32)
    merged = g_ref[:, :d].astype(F32) * a + g_ref[:, d:].astype(F32) * b
    mix = jnp.dot(merged.astype(BF16), wo_ref[...], preferred_element_type=F32)
    y = mix * lax.rsqrt(jnp.mean(mix * mix, axis=-1, keepdims=True) + RMS_EPS)
    o_ref[...] = x_ref[...] + ga_ref[0] * (y * gpost_ref[...])


def _merge(o_a, o_b, z_g, x2, w_a, w_b, w_o, g_post, gate, seq):
    t, d = x2.shape
    bsz = gate.shape[0]
    tm = _tile(seq, 256)
    per_b = seq // tm
    const = lambda i: (0, 0)
    return pl.pallas_call(
        _merge_kernel,
        grid=(t // tm,),
        in_specs=[
            pl.BlockSpec((tm, A_Q_W), lambda i: (i, 0)),
            pl.BlockSpec((tm, B_W), lambda i: (i, 0)),
            pl.BlockSpec((tm, 2 * d), lambda i: (i, 0)),
            pl.BlockSpec((tm, d), lambda i: (i, 0)),
            pl.BlockSpec((A_Q_W, d), const),
            pl.BlockSpec((B_W, d), const),
            pl.BlockSpec((d, d), const),
            pl.BlockSpec((1, d), const),
            pl.BlockSpec((1, 1, d), lambda i: (i // per_b, 0, 0)),
        ],
        out_specs=pl.BlockSpec((tm, d), lambda i: (i, 0)),
        out_shape=jax.ShapeDtypeStruct((t, d), F32),
        compiler_params=_cparams(("arbitrary",)),
        name="merge",
    )(o_a, o_b, z_g, x2, w_a, w_b, w_o, g_post.reshape(1, d), gate.reshape(bsz, 1, d))


def _router_kernel(h_ref, wr_ref, br_ref, e_ref, w_ref, r_ref, cnt_ref, base_ref):
    i = pl.program_id(0)
    tm = h_ref.shape[0]

    @pl.when(i == 0)
    def _():
        base_ref[...] = jnp.zeros_like(base_ref)

    logits = jnp.dot(h_ref[...], wr_ref[...], preferred_element_type=F32,
                     precision=lax.Precision.HIGHEST) + br_ref[...]
    lane = lax.broadcasted_iota(I32, (tm, LANES), 1)
    work = logits
    picks, vals = [], []
    for _ in range(TOP_K_EXPERTS):
        m = jnp.max(work, axis=1, keepdims=True)
        idx = jnp.min(jnp.where(work == m, lane, LANES), axis=1, keepdims=True)
        picks.append(idx)
        vals.append(m)
        work = jnp.where(lane == idx, -jnp.inf, work)
    exps = [jnp.exp(v - vals[0]) for v in vals]
    den = exps[0]
    for e in exps[1:]:
        den = den + e

    onehot = jnp.zeros((tm, LANES), F32)
    for idx in picks:
        onehot = onehot + jnp.where(lane == idx, 1.0, 0.0)
    r_io = lax.broadcasted_iota(I32, (tm, tm), 0)
    c_io = lax.broadcasted_iota(I32, (tm, tm), 1)
    lower = jnp.where(c_io < r_io, 1.0, 0.0).astype(BF16)
    prefix = jnp.dot(lower, onehot.astype(BF16), preferred_element_type=F32)
    total = prefix + base_ref[0:1, :]

    e_out = jnp.zeros((tm, LANES), I32)
    w_out = jnp.zeros((tm, LANES), F32)
    r_out = jnp.zeros((tm, LANES), I32)
    for k in range(TOP_K_EXPERTS):
        rank = jnp.sum(jnp.where(lane == picks[k], total, 0.0), axis=1, keepdims=True)
        e_out = jnp.where(lane == k, picks[k], e_out)
        w_out = jnp.where(lane == k, exps[k] / den, w_out)
        r_out = jnp.where(lane == k, rank.astype(I32), r_out)
    e_ref[...] = e_out
    w_ref[...] = w_out
    r_ref[...] = r_out
    new_base = base_ref[0:1, :] + jnp.sum(onehot, axis=0, keepdims=True)
    base_ref[...] = jnp.broadcast_to(new_base, base_ref.shape)
    cnt_ref[...] = jnp.broadcast_to(new_base, cnt_ref.shape)


def _route(h2, w_router, b_router):
    t, d = h2.shape
    tm = _tile(t, 512)
    wr = jnp.zeros((d, LANES), F32).at[:, :N_EXPERTS].set(w_router)
    br = jnp.full((1, LANES), NEG_BIG, F32).at[0, :N_EXPERTS].set(b_router)
    row = pl.BlockSpec((tm, LANES), lambda i: (i, 0))
    return pl.pallas_call(
        _router_kernel,
        grid=(t // tm,),
        in_specs=[pl.BlockSpec((tm, d), lambda i: (i, 0)),
                  pl.BlockSpec((d, LANES), lambda i: (0, 0)),
                  pl.BlockSpec((1, LANES), lambda i: (0, 0))],
        out_specs=[row, row, row, pl.BlockSpec((8, LANES), lambda i: (0, 0))],
        out_shape=[jax.ShapeDtypeStruct((t, LANES), I32),
                   jax.ShapeDtypeStruct((t, LANES), F32),
                   jax.ShapeDtypeStruct((t, LANES), I32),
                   jax.ShapeDtypeStruct((8, LANES), F32)],
        scratch_shapes=[pltpu.VMEM((8, LANES), F32)],
        compiler_params=_cparams(("arbitrary",)),
        name="router",
    )(h2, wr, br)


def _dispatch_kernel(dest_ref, h_ref, init_ref, xs_ref, sem):
    del init_ref
    tm = h_ref.shape[0]

    def copy(t, k):
        return pltpu.make_async_copy(h_ref.at[pl.ds(t, 1)],
                                     xs_ref.at[pl.ds(dest_ref[0, 0, t * TOP_K_EXPERTS + k], 1)], sem)

    def issue(t, c):
        for k in range(TOP_K_EXPERTS):
            copy(t, k).start()
        return c

    def drain(t, c):
        for k in range(TOP_K_EXPERTS):
            copy(t, k).wait()
        return c

    lax.fori_loop(0, tm, issue, 0)
    lax.fori_loop(0, tm, drain, 0)


def _dispatch(h2, dest, n_rows):
    t, d = h2.shape
    tm = _tile(t, 256)
    nt = t // tm
    dest3 = dest.reshape(nt, 1, tm * TOP_K_EXPERTS)
    init = jnp.zeros((n_rows, d), h2.dtype)
    return pl.pallas_call(
        _dispatch_kernel,
        grid=(nt,),
        in_specs=[pl.BlockSpec((1, 1, tm * TOP_K_EXPERTS), lambda i: (i, 0, 0), memory_space=pltpu.SMEM),
                  pl.BlockSpec((tm, d), lambda i: (i, 0)),
                  pl.BlockSpec(memory_space=pl.ANY)],
        out_specs=pl.BlockSpec(memory_space=pl.ANY),
        out_shape=jax.ShapeDtypeStruct((n_rows, d), h2.dtype),
        scratch_shapes=[pltpu.SemaphoreType.DMA(())],
        input_output_aliases={2: 0},
        compiler_params=pltpu.CompilerParams(dimension_semantics=("arbitrary",),
                                             vmem_limit_bytes=VMEM_LIMIT, has_side_effects=True),
        name="moe_dispatch",
    )(dest3, h2, init)


def _deinterleave_kernel(w_ref, p_ref, g_ref, l_ref):
    p = p_ref[...]
    width = p.shape[0]
    half = width // 2
    for c in range(w_ref.shape[2] // width):
        w = w_ref[0, :, c * width:(c + 1) * width].astype(BF16)
        r = jnp.dot(w, p, preferred_element_type=F32)
        g_ref[0, :, c * half:(c + 1) * half] = r[:, :half].astype(BF16)
        l_ref[0, :, c * half:(c + 1) * half] = r[:, half:].astype(BF16)


def _deinterleave(w1):
    n_e, d, f2 = w1.shape
    width = 2 * LANES
    src = jnp.arange(width)
    dst = jnp.where(src % 2 == 0, src // 2, LANES + src // 2)
    perm = (dst[:, None] == jnp.arange(width)[None, :]).astype(BF16)
    tr = _tile(d, 256)
    out = pl.BlockSpec((1, tr, f2 // 2), lambda e, r: (e, r, 0))
    return pl.pallas_call(
        _deinterleave_kernel,
        grid=(n_e, d // tr),
        in_specs=[pl.BlockSpec((1, tr, f2), lambda e, r: (e, r, 0)),
                  pl.BlockSpec((width, width), lambda e, r: (0, 0))],
        out_specs=[out, out],
        out_shape=[jax.ShapeDtypeStruct((n_e, d, f2 // 2), BF16)] * 2,
        compiler_params=_cparams(("arbitrary", "arbitrary")),
        name="w1_deinterleave",
    )(w1, perm)


def _expert_kernel(be_ref, nu_ref, x_ref, w1g_ref, w1l_ref, b1g_ref, b1l_ref, w2_ref, b2_ref,
                   o_ref, xb_ref, acc_ref):
    i = pl.program_id(0)
    j = pl.program_id(1)
    nf = pl.num_programs(1)
    valid = i < nu_ref[0]

    @pl.when(jnp.logical_and(valid, j == 0))
    def _():
        xb_ref[...] = x_ref[...].astype(BF16)
        acc_ref[...] = jnp.zeros_like(acc_ref)

    @pl.when(valid)
    def _():
        xb = xb_ref[...]
        glu = jnp.dot(xb, w1g_ref[0], preferred_element_type=F32) + b1g_ref[0]
        lin = jnp.dot(xb, w1l_ref[0], preferred_element_type=F32) + b1l_ref[0]
        glu = jnp.minimum(glu, SWIGLU_LIMIT)
        lin = jnp.clip(lin, -SWIGLU_LIMIT, SWIGLU_LIMIT)
        act = glu * jax.nn.sigmoid(SWIGLU_ALPHA * glu) * (lin + 1.0)
        acc_ref[...] += jnp.dot(act.astype(BF16), w2_ref[0], preferred_element_type=F32)

    @pl.when(jnp.logical_and(valid, j == nf - 1))
    def _():
        o_ref[...] = acc_ref[...] + b2_ref[0]


def _experts(xs, block_e, n_used, w1g, w1l, b1g, b1l, w2, b2):
    n_rows, d = xs.shape
    n_e, _, f = w1g.shape
    n_blocks = n_rows // MOE_ROWS
    tf = _tile(f, 512)
    nf = f // tf

    def blk(i, nu):
        return jnp.minimum(i, nu[0] - 1)

    def ftile(i, j, nu):
        return jnp.where(i < nu[0], j, nf - 1)

    grid_spec = pltpu.PrefetchScalarGridSpec(
        num_scalar_prefetch=2,
        grid=(n_blocks, nf),
        in_specs=[
            pl.BlockSpec((MOE_ROWS, d), lambda i, j, be, nu: (blk(i, nu), 0)),
            pl.BlockSpec((1, d, tf), lambda i, j, be, nu: (be[blk(i, nu)], 0, ftile(i, j, nu))),
            pl.BlockSpec((1, d, tf), lambda i, j, be, nu: (be[blk(i, nu)], 0, ftile(i, j, nu))),
            pl.BlockSpec((1, 1, tf), lambda i, j, be, nu: (be[blk(i, nu)], 0, ftile(i, j, nu))),
            pl.BlockSpec((1, 1, tf), lambda i, j, be, nu: (be[blk(i, nu)], 0, ftile(i, j, nu))),
            pl.BlockSpec((1, tf, d), lambda i, j, be, nu: (be[blk(i, nu)], ftile(i, j, nu), 0)),
            pl.BlockSpec((1, 1, d), lambda i, j, be, nu: (be[blk(i, nu)], 0, 0)),
        ],
        out_specs=pl.BlockSpec((MOE_ROWS, d), lambda i, j, be, nu: (blk(i, nu), 0)),
        scratch_shapes=[pltpu.VMEM((MOE_ROWS, d), BF16), pltpu.VMEM((MOE_ROWS, d), F32)],
    )
    return pl.pallas_call(
        _expert_kernel,
        grid_spec=grid_spec,
        out_shape=jax.ShapeDtypeStruct((n_rows, d), F32),
        compiler_params=_cparams(("arbitrary", "arbitrary")),
        name="moe_experts",
    )(block_e, n_used, xs, w1g, w1l, b1g, b1l, w2, b2)


def _combine_kernel(dest_ref, tw_ref, x_ref, gpost_ref, ga_ref, ys_ref, o_ref, buf_ref, sem):
    tm = x_ref.shape[0]

    def copy(t, k):
        return pltpu.make_async_copy(ys_ref.at[pl.ds(dest_ref[0, 0, t * TOP_K_EXPERTS + k], 1)],
                                     buf_ref.at[k, pl.ds(t, 1)], sem)

    def issue(t, c):
        for k in range(TOP_K_EXPERTS):
            copy(t, k).start()
        return c

    def drain(t, c):
        for k in range(TOP_K_EXPERTS):
            copy(t, k).wait()
        return c

    lax.fori_loop(0, tm, issue, 0)
    lax.fori_loop(0, tm, drain, 0)

    tw = tw_ref[...]
    f = tw[:, 0:1] * buf_ref[0]
    for k in range(1, TOP_K_EXPERTS):
        f = f + tw[:, k:k + 1] * buf_ref[k]
    y = f * lax.rsqrt(jnp.mean(f * f, axis=-1, keepdims=True) + RMS_EPS)
    o_ref[...] = x_ref[...] + ga_ref[0] * (y * gpost_ref[...])


def _combine(ys, dest, top_w, x1, g_post, gate, seq):
    t, d = x1.shape
    bsz = gate.shape[0]
    tm = _tile(seq, 256)
    nt = t // tm
    per_b = seq // tm
    dest3 = dest.reshape(nt, 1, tm * TOP_K_EXPERTS)
    return pl.pallas_call(
        _combine_kernel,
        grid=(nt,),
        in_specs=[pl.BlockSpec((1, 1, tm * TOP_K_EXPERTS), lambda i: (i, 0, 0), memory_space=pltpu.SMEM),
                  pl.BlockSpec((tm, LANES), lambda i: (i, 0)),
                  pl.BlockSpec((tm, d), lambda i: (i, 0)),
                  pl.BlockSpec((1, d), lambda i: (0, 0)),
                  pl.BlockSpec((1, 1, d), lambda i: (i // per_b, 0, 0)),
                  pl.BlockSpec(memory_space=pl.ANY)],
        out_specs=pl.BlockSpec((tm, d), lambda i: (i, 0)),
        out_shape=jax.ShapeDtypeStruct((t, d), F32),
        scratch_shapes=[pltpu.VMEM((TOP_K_EXPERTS, tm, d), F32), pltpu.SemaphoreType.DMA(())],
        compiler_params=_cparams(("arbitrary",)),
        name="moe_combine",
    )(dest3, top_w, x1, g_post.reshape(1, d), gate.reshape(bsz, 1, d), ys)


def _moe(h2, x1, w_router, b_router, w1, b1, w2, b2, g_post, gate, seq):
    t, d = h2.shape
    top_e, top_w, rank, counts = _route(h2, w_router, b_router)
    top_e = top_e[:, :TOP_K_EXPERTS]
    rank = rank[:, :TOP_K_EXPERTS]
    counts = counts[0, :N_EXPERTS].astype(I32)

    n_blocks = (t * TOP_K_EXPERTS) // MOE_ROWS + N_EXPERTS
    padded = (counts + MOE_ROWS - 1) // MOE_ROWS * MOE_ROWS
    pad_end = jnp.cumsum(padded)
    pad_start = pad_end - padded
    dest = (pad_start[top_e] + rank).astype(I32)
    block_row = jnp.arange(n_blocks, dtype=I32) * MOE_ROWS
    block_e = jnp.minimum(jnp.sum((pad_end[None, :] <= block_row[:, None]).astype(I32), axis=1),
                          N_EXPERTS - 1).astype(I32)
    n_used = (pad_end[-1:] // MOE_ROWS).astype(I32)

    xs = _dispatch(h2, dest, n_blocks * MOE_ROWS)
    w1g, w1l = _deinterleave(w1)
    n_e, f2 = b1.shape
    b1g = b1[:, 0::2].reshape(n_e, 1, f2 // 2)
    b1l = b1[:, 1::2].reshape(n_e, 1, f2 // 2)
    ys = _experts(xs, block_e, n_used, w1g, w1l, b1g, b1l, w2.astype(BF16), b2.reshape(n_e, 1, d))
    return _combine(ys, dest, top_w, x1, g_post, gate, seq)


def _layer(x2, c, tables, bsz, seq, w_ada, b_ada, g_pre_mix, g_post_mix, w_in, w_branch_a,
           w_branch_b, w_out, g_pre_ffn, g_post_ffn, w_router, b_router, w1, b1, w2, b2):
    d = x2.shape[1]
    mod = _modulation(c, w_ada, b_ada)
    sh1, sc1, ga1, sh2, sc2, ga2 = [mod[:, k * d:(k + 1) * d] for k in range(6)]
    cos_a, sin_a, cos_i, sin_i = tables

    h = _prenorm(x2, g_pre_mix, sc1, sh1, seq, BF16)
    wb = w_in.astype(BF16)
    o = 0
    w_qka = wb[:, o:o + A_Q_W + A_KV_W]
    o += A_Q_W + A_KV_W
    w_plain = wb[:, o:o + A_KV_W + 3 * B_W]
    o += A_KV_W + 3 * B_W
    w_qi = wb[:, o:o + IDX_Q_W]
    o += IDX_Q_W
    w_kw = jnp.zeros((d, LANES), BF16).at[:, :IDX_DIM + IDX_HEADS].set(wb[:, o:o + IDX_DIM + IDX_HEADS])
    o += IDX_DIM + IDX_HEADS
    w_gate = wb[:, o:o + 2 * d]

    z_a = _project(h, w_qka, "rope_a", BF16, (cos_a, sin_a), tn_pref=640)
    z_p = _project(h, w_plain, "plain", BF16, tn_pref=1664)
    z_i = _project(h, w_qi, "rope_idx", BF16, (cos_i, sin_i))
    z_k = _project(h, w_kw, "rope_key_idx", F32, (cos_i, sin_i))
    z_g = _project(h, w_gate, "sigmoid", BF16, tn_pref=1024)

    o_a = _dsa_mixer(z_a, z_p, z_i, z_k, bsz, seq)
    o_b = _stick_mixer(z_p, bsz, seq)
    x1 = _merge(o_a, o_b, z_g, x2, w_branch_a.astype(BF16), w_branch_b.astype(BF16),
                w_out.astype(BF16), g_post_mix, ga1, seq)

    h2 = _prenorm(x1, g_pre_ffn, sc2, sh2, seq, F32)
    return _moe(h2, x1, w_router, b_router, w1, b1, w2, b2, g_post_ffn, ga2, seq)


def kernel(x, c, positions, w_ada, b_ada, g_pre_mix, g_post_mix, w_in, w_branch_a, w_branch_b, w_out, g_pre_ffn, g_post_ffn, w_router, b_router, w1, b1, w2, b2):
    bsz, seq, d = x.shape
    x2 = x.reshape(bsz * seq, d)
    tables = _rope_tables(positions)
    for l in range(w_ada.shape[0]):
        x2 = _layer(x2, c, tables, bsz, seq, w_ada[l], b_ada[l], g_pre_mix[l], g_post_mix[l], w_in[l],
                    w_branch_a[l], w_branch_b[l], w_out[l], g_pre_ffn[l], g_post_ffn[l],
                    w_router[l], b_router[l], w1[l], b1[l], w2[l], b2[l])
    return x2.reshape(bsz, seq, d)
```

```python
import functools

import jax
import jax.numpy as jnp
from jax import lax
from jax.experimental import pallas as pl
from jax.experimental.pallas import tpu as pltpu

F32 = jnp.float32
BF16 = jnp.bfloat16
I32 = jnp.int32

CHUNK = 64
RMS_EPS = 1e-6
ROPE_THETA = 500000.0
A_HEADS = 8
A_KV_HEADS = 2
A_HEAD_DIM = 128
A_ROT_DIM = A_HEAD_DIM // 4
IDX_HEADS = 16
IDX_DIM = 64
IDX_ROT_DIM = IDX_DIM // 4
TOPK_MAX = 256
B_HEADS = 8
B_HEAD_DIM = 128
N_EXPERTS = 32
TOP_K_EXPERTS = 4
SWIGLU_LIMIT = 7.0
SWIGLU_ALPHA = 1.702

A_Q_W = A_HEADS * A_HEAD_DIM
A_KV_W = A_KV_HEADS * A_HEAD_DIM
B_W = B_HEADS * B_HEAD_DIM
IDX_Q_W = IDX_HEADS * IDX_DIM

LANES = 128
SUBLANES = 8
VMEM_LIMIT = 56 * 1024 * 1024
NEG_BIG = -1e30
LOG2_E = 1.4426950408889634
F32_MAX = 3.4028234663852886e38
BISECT_CAP = 320
STICK_DEAD = -106.0
MOE_ROWS = 512


def _cparams(sem):
    return pltpu.CompilerParams(dimension_semantics=sem, vmem_limit_bytes=VMEM_LIMIT)


def _tile(n, pref):
    t = min(n, pref)
    assert n % t == 0, (n, t)
    return t


def _mod_kernel(c_ref, w_ref, b_ref, o_ref):
    c = c_ref[...]
    s = c * jax.nn.sigmoid(c)
    o_ref[...] = jnp.dot(s, w_ref[...], preferred_element_type=F32,
                         precision=lax.Precision.HIGHEST) + b_ref[...]


def _modulation(c, w_ada, b_ada):
    bsz, d = c.shape
    n = w_ada.shape[1]
    rows = 8
    c_pad = jnp.zeros((rows, d), F32).at[:bsz].set(c)
    tn = _tile(n, 1024)
    out = pl.pallas_call(
        _mod_kernel,
        grid=(n // tn,),
        in_specs=[pl.BlockSpec((rows, d), lambda j: (0, 0)),
                  pl.BlockSpec((d, tn), lambda j: (0, j)),
                  pl.BlockSpec((1, tn), lambda j: (0, j))],
        out_specs=pl.BlockSpec((rows, tn), lambda j: (0, j)),
        out_shape=jax.ShapeDtypeStruct((rows, n), F32),
        compiler_params=_cparams(("arbitrary",)),
        name="modulation",
    )(c_pad, w_ada, b_ada.reshape(1, n))
    return out[:bsz]


def _rope_table_kernel(pos_ref, fa_ref, sa_ref, fi_ref, si_ref, ca_o, sa_o, ci_o, si_o):
    pos = pos_ref[...].astype(F32)
    ang_a = pos * fa_ref[...]
    ca_o[...] = jnp.cos(ang_a)
    sa_o[...] = jnp.sin(ang_a) * sa_ref[...]
    ang_i = pos * fi_ref[...]
    ci_o[...] = jnp.cos(ang_i)
    si_o[...] = jnp.sin(ang_i) * si_ref[...]


def _rope_pattern(rot_dim, head_dim):
    half = rot_dim // 2
    inv_freq = ROPE_THETA ** (-jnp.arange(half, dtype=F32) / half)
    freq = jnp.concatenate([inv_freq, inv_freq, jnp.zeros((head_dim - rot_dim,), F32)])
    sign = jnp.concatenate([-jnp.ones((half,), F32), jnp.ones((half,), F32),
                            jnp.zeros((head_dim - rot_dim,), F32)])
    reps = LANES // head_dim
    return jnp.tile(freq, reps).reshape(1, LANES), jnp.tile(sign, reps).reshape(1, LANES)


def _rope_tables(positions):
    t = positions.size
    pos = positions.reshape(t, 1).astype(I32)
    fa, sa = _rope_pattern(A_ROT_DIM, A_HEAD_DIM)
    fi, si = _rope_pattern(IDX_ROT_DIM, IDX_DIM)
    tm = _tile(t, 1024)
    pat = pl.BlockSpec((1, LANES), lambda i: (0, 0))
    tab = pl.BlockSpec((tm, LANES), lambda i: (i, 0))
    return pl.pallas_call(
        _rope_table_kernel,
        grid=(t // tm,),
        in_specs=[pl.BlockSpec((tm, 1), lambda i: (i, 0)), pat, pat, pat, pat],
        out_specs=[tab, tab, tab, tab],
        out_shape=[jax.ShapeDtypeStruct((t, LANES), F32)] * 4,
        compiler_params=_cparams(("arbitrary",)),
        name="rope_tables",
    )(pos, fa, sa, fi, si)


def _prenorm_kernel(x_ref, g_ref, sc_ref, sh_ref, o_ref):
    x = x_ref[...]
    y = x * lax.rsqrt(jnp.mean(x * x, axis=-1, keepdims=True) + RMS_EPS)
    h = (y * g_ref[...]) * (1.0 + sc_ref[0]) + sh_ref[0]
    o_ref[...] = h.astype(o_ref.dtype)


def _prenorm(x2, g, scale, shift, seq, out_dtype):
    t, d = x2.shape
    bsz = scale.shape[0]
    tm = _tile(seq, 512)
    per_b = seq // tm
    row = pl.BlockSpec((tm, d), lambda i: (i, 0))
    bvec = pl.BlockSpec((1, 1, d), lambda i: (i // per_b, 0, 0))
    return pl.pallas_call(
        _prenorm_kernel,
        grid=(t // tm,),
        in_specs=[row, pl.BlockSpec((1, d), lambda i: (0, 0)), bvec, bvec],
        out_specs=row,
        out_shape=jax.ShapeDtypeStruct((t, d), out_dtype),
        compiler_params=_cparams(("arbitrary",)),
        name="prenorm",
    )(x2, g.reshape(1, d), scale.reshape(bsz, 1, d), shift.reshape(bsz, 1, d))


def _rotate(x, cos, sin, period, half):
    lane = lax.broadcasted_iota(I32, x.shape, 1) & (period - 1)
    swapped = jnp.where(lane < half, pltpu.roll(x, LANES - half, 1), pltpu.roll(x, half, 1))
    return x * cos + swapped * sin


def _proj_kernel(*refs, epilogue):
    if epilogue in ("plain", "sigmoid"):
        h_ref, w_ref, o_ref = refs
    else:
        h_ref, w_ref, cos_ref, sin_ref, o_ref = refs
    acc = jnp.dot(h_ref[...], w_ref[...], preferred_element_type=F32)
    if epilogue == "plain":
        o_ref[...] = acc.astype(o_ref.dtype)
    elif epilogue == "sigmoid":
        o_ref[...] = jax.nn.sigmoid(acc).astype(o_ref.dtype)
    else:
        cos = cos_ref[...]
        sin = sin_ref[...]
        if epilogue == "rope_a":
            period, half = A_HEAD_DIM, A_ROT_DIM // 2
        else:
            period, half = IDX_DIM, IDX_ROT_DIM // 2
        if epilogue == "rope_key_idx":
            lane = lax.broadcasted_iota(I32, cos.shape, 1)
            cos = jnp.where(lane < IDX_DIM, cos, 1.0)
            sin = jnp.where(lane < IDX_DIM, sin, 0.0)
        for c in range(acc.shape[1] // LANES):
            sl = slice(c * LANES, (c + 1) * LANES)
            o_ref[:, sl] = _rotate(acc[:, sl], cos, sin, period, half).astype(o_ref.dtype)


def _project(h, w, epilogue, out_dtype, tables=None, tn_pref=512):
    t, d = h.shape
    n = w.shape[1]
    tm = _tile(t, 1024)
    tn = n
    for cand in (tn_pref, 256, 128):
        if n % cand == 0:
            tn = cand
            break
    in_specs = [pl.BlockSpec((tm, d), lambda i, j: (i, 0)),
                pl.BlockSpec((d, tn), lambda i, j: (0, j))]
    args = [h, w]
    if tables is not None:
        tab = pl.BlockSpec((tm, LANES), lambda i, j: (i, 0))
        in_specs += [tab, tab]
        args += list(tables)
    return pl.pallas_call(
        functools.partial(_proj_kernel, epilogue=epilogue),
        grid=(t // tm, n // tn),
        in_specs=in_specs,
        out_specs=pl.BlockSpec((tm, tn), lambda i, j: (i, j)),
        out_shape=jax.ShapeDtypeStruct((t, n), out_dtype),
        compiler_params=_cparams(("arbitrary", "arbitrary")),
        name="proj_" + epilogue,
    )(*args)


def _dsa_kernel(qi_ref, kwq_ref, kwk_ref, qa_ref, ka_ref, va_ref, o_ref,
                keys_ref, kib_ref, *, tq, tk, topk):
    i = pl.program_id(1)
    t0 = i * tq
    n_kt = (t0 + tq + tk - 1) // tk
    grp = A_HEADS // A_KV_HEADS

    @pl.when(i == 0)
    def _():
        kib_ref[...] = kwk_ref[:, :IDX_DIM].astype(BF16)

    qi = qi_ref[...]
    wq = kwq_ref[...]
    row = t0 + lax.broadcasted_iota(I32, (tq, 1), 0)
    limit = (row // CHUNK + 1) * CHUNK

    def score_tile(kt, carry):
        hi_acc, lo_acc = carry
        k0 = pl.multiple_of(kt * tk, tk)
        kb = kib_ref[pl.ds(k0, tk), :]
        acc = jnp.zeros((tq, tk), F32)
        for h in range(IDX_HEADS):
            d = lax.dot_general(qi[:, h * IDX_DIM:(h + 1) * IDX_DIM], kb,
                                (((1,), (1,)), ((), ())), preferred_element_type=F32)
            acc = acc + jnp.maximum(d, 0.0) * wq[:, IDX_DIM + h:IDX_DIM + h + 1]
        adm = (k0 + lax.broadcasted_iota(I32, (tq, tk), 1)) < limit
        s_hi = jnp.where(adm, acc, -jnp.inf)
        s_lo = jnp.where(adm, acc, jnp.inf)
        keys_ref[kt] = s_hi
        for c in range(tk // LANES):
            hi_acc = jnp.maximum(hi_acc, s_hi[:, c * LANES:(c + 1) * LANES])
            lo_acc = jnp.minimum(lo_acc, s_lo[:, c * LANES:(c + 1) * LANES])
        return hi_acc, lo_acc

    hi_acc, lo_acc = lax.fori_loop(
        0, n_kt, score_tile,
        (jnp.full((tq, LANES), -jnp.inf, F32), jnp.full((tq, LANES), jnp.inf, F32)))
    row_max = jnp.max(hi_acc, axis=1, keepdims=True)
    row_min = jnp.min(lo_acc, axis=1, keepdims=True)

    def count_ge(cand):
        def count_tile(kt, cnt):
            ge = (keys_ref[kt] >= cand).astype(I32)
            for c in range(tk // LANES):
                cnt = cnt + ge[:, c * LANES:(c + 1) * LANES]
            return cnt

        cnt = lax.fori_loop(0, n_kt, count_tile, jnp.zeros((tq, LANES), I32))
        return jnp.sum(cnt, axis=1, keepdims=True)

    def bisect_cond(state):
        it, pending = state[0], state[1]
        return jnp.logical_and(it < BISECT_CAP, pending > 0)

    def bisect_step(state):
        it, _, lo, hi, tau, todo = state
        mid = 0.5 * lo + 0.5 * hi
        stuck = jnp.logical_or(mid <= lo, mid >= hi)
        cnt = count_ge(mid)
        fin = jnp.logical_or(stuck, cnt == topk)
        tau = jnp.where(jnp.logical_and(todo > 0, fin), jnp.where(stuck, lo, mid), tau)
        up = cnt >= topk
        lo = jnp.where(up, mid, lo)
        hi = jnp.where(up, hi, mid)
        todo = jnp.where(fin, 0, todo)
        return it + 1, jnp.max(todo), lo, hi, tau, todo

    todo0 = jnp.where(limit > topk, 1, 0).astype(I32)
    hi0 = row_max + jnp.maximum(jnp.abs(row_max) * 1e-6, 1e-30)
    state = (jnp.int32(0), jnp.max(todo0), row_min, hi0,
             jnp.full((tq, 1), -F32_MAX, F32), todo0)
    _, _, lo, _, tau, todo = lax.while_loop(bisect_cond, bisect_step, state)
    tau = jnp.where(todo > 0, lo, tau)

    qscale = A_HEAD_DIM ** -0.5 * LOG2_E
    qs = [jnp.concatenate(
        [(qa_ref[:, (j * grp + g) * A_HEAD_DIM:(j * grp + g + 1) * A_HEAD_DIM].astype(F32)
          * qscale).astype(BF16) for g in range(grp)], axis=0)
        for j in range(A_KV_HEADS)]

    def attn_tile(kt, carry):
        k0 = pl.multiple_of(kt * tk, tk)
        bias = jnp.where(keys_ref[kt] >= tau, 0.0, NEG_BIG)
        out = []
        for j in range(A_KV_HEADS):
            m, l, acc = carry[3 * j:3 * j + 3]
            kk = ka_ref[pl.ds(k0, tk), j * A_HEAD_DIM:(j + 1) * A_HEAD_DIM]
            vv = va_ref[pl.ds(k0, tk), j * A_HEAD_DIM:(j + 1) * A_HEAD_DIM]
            s = lax.dot_general(qs[j], kk, (((1,), (1,)), ((), ())),
                                preferred_element_type=F32)
            s = (s.reshape(grp, tq, tk) + bias[None]).reshape(grp * tq, tk)
            m_new = jnp.maximum(m, jnp.max(s, axis=1, keepdims=True))
            alpha = jnp.exp2(m - m_new)
            p = jnp.exp2(s - m_new)
            l = alpha * l + jnp.sum(p, axis=1, keepdims=True)
            acc = alpha * acc + jnp.dot(p.astype(BF16), vv, preferred_element_type=F32)
            out += [m_new, l, acc]
        return tuple(out)

    init = []
    for j in range(A_KV_HEADS):
        init += [jnp.full((grp * tq, 1), NEG_BIG, F32), jnp.zeros((grp * tq, 1), F32),
                 jnp.zeros((grp * tq, A_HEAD_DIM), F32)]
    fin = lax.fori_loop(0, n_kt, attn_tile, tuple(init))
    for j in range(A_KV_HEADS):
        out = fin[3 * j + 2] / fin[3 * j + 1]
        for g in range(grp):
            h = j * grp + g
            o_ref[:, h * A_HEAD_DIM:(h + 1) * A_HEAD_DIM] = out[g * tq:(g + 1) * tq].astype(o_ref.dtype)


def _dsa_mixer(z_a, z_p, z_i, z_k, bsz, seq):
    t = bsz * seq
    tq = _tile(seq, 256)
    tk = _tile(seq, 1024)
    nq = seq // tq
    topk = min(TOPK_MAX, seq // 4)
    return pl.pallas_call(
        functools.partial(_dsa_kernel, tq=tq, tk=tk, topk=topk),
        grid=(bsz, nq),
        in_specs=[
            pl.BlockSpec((tq, IDX_Q_W), lambda b, i: (b * nq + i, 0)),
            pl.BlockSpec((tq, LANES), lambda b, i: (b * nq + i, 0)),
            pl.BlockSpec((seq, LANES), lambda b, i: (b, 0)),
            pl.BlockSpec((tq, A_Q_W), lambda b, i: (b * nq + i, 0)),
            pl.BlockSpec((seq, A_KV_W), lambda b, i: (b, A_Q_W // A_KV_W)),
            pl.BlockSpec((seq, A_KV_W), lambda b, i: (b, 0)),
        ],
        out_specs=pl.BlockSpec((tq, A_Q_W), lambda b, i: (b * nq + i, 0)),
        out_shape=jax.ShapeDtypeStruct((t, A_Q_W), BF16),
        scratch_shapes=[pltpu.VMEM((seq // tk, tq, tk), F32),
                        pltpu.VMEM((seq, IDX_DIM), BF16)],
        compiler_params=_cparams(("arbitrary", "arbitrary")),
        name="dsa_mixer",
    )(z_i, z_k, z_k, z_a, z_a, z_p)


def _stick_kernel(q_ref, k_ref, v_ref, o_ref, *, tq, tk, heads):
    i = pl.program_id(2)
    d_idx = (i * tq) // tk
    hd = B_HEAD_DIM
    scale = hd ** -0.5
    qs = [(q_ref[:, h * hd:(h + 1) * hd].astype(F32) * scale).astype(BF16) for h in range(heads)]
    r_io = lax.broadcasted_iota(I32, (tk, tk), 0)
    c_io = lax.broadcasted_iota(I32, (tk, tk), 1)
    upper = jnp.where(r_io > c_io, 1.0, 0.0).astype(BF16)
    diagonal = (d_idx * tk + lax.broadcasted_iota(I32, (tq, tk), 1)
                < i * tq + lax.broadcasted_iota(I32, (tq, tk), 0))

    def scores(kt):
        k0 = pl.multiple_of(kt * tk, tk)
        return tuple(lax.dot_general(qs[h], k_ref[pl.ds(k0, tk), h * hd:(h + 1) * hd],
                                     (((1,), (1,)), ((), ())), preferred_element_type=F32)
                     for h in range(heads))

    def front(z, causal):
        soft = jnp.log(1.0 + jnp.exp(-jnp.abs(z)))
        log_beta = jnp.minimum(z, 0.0) - soft
        log_keep = log_beta - z
        if causal is not None:
            log_keep = jnp.where(causal, log_keep, 0.0)
        hi = log_keep.astype(BF16)
        lo = (log_keep - hi.astype(F32)).astype(BF16)
        gap = (jnp.dot(hi, upper, preferred_element_type=F32)
               + jnp.dot(lo, upper, preferred_element_type=F32))
        logit = log_beta + gap
        if causal is not None:
            logit = jnp.where(causal, logit, NEG_BIG)
        return logit, jnp.sum(log_keep, axis=1, keepdims=True)

    def back(kt, pend, carry):
        k0 = pl.multiple_of(kt * tk, tk)
        out = []
        for h in range(heads):
            logit, keep_sum = pend[2 * h], pend[2 * h + 1]
            run, acc = carry[2 * h], carry[2 * h + 1]
            attn = jnp.exp(logit + run).astype(BF16)
            vv = v_ref[pl.ds(k0, tk), h * hd:(h + 1) * hd]
            out += [run + keep_sum, acc + jnp.dot(attn, vv, preferred_element_type=F32)]
        return tuple(out)

    def fronts(zs, causal):
        out = []
        for h in range(heads):
            out.extend(front(zs[h], causal))
        return tuple(out)

    def step(state):
        n, _, zs, pend, carry = state
        kt = d_idx - 1 - n
        zs_next = scores(jnp.maximum(kt - 1, 0))
        pend_next = fronts(zs, None)
        carry = back(kt + 1, pend, carry)
        top = carry[0]
        for h in range(1, heads):
            top = jnp.maximum(top, carry[2 * h])
        return n + 1, jnp.max(top), zs_next, pend_next, carry

    def alive(state):
        return jnp.logical_and(state[0] < d_idx, state[1] > STICK_DEAD)

    init = []
    for h in range(heads):
        init += [jnp.zeros((tq, 1), F32), jnp.zeros((tq, hd), F32)]
    pend = fronts(scores(d_idx), diagonal)
    zs = scores(jnp.maximum(d_idx - 1, 0))
    n_done, _, _, pend, carry = lax.while_loop(
        alive, step, (jnp.int32(0), jnp.float32(0.0), zs, pend, tuple(init)))
    carry = back(d_idx - n_done, pend, carry)
    for h in range(heads):
        o_ref[:, h * hd:(h + 1) * hd] = carry[2 * h + 1].astype(o_ref.dtype)


def _stick_mixer(z_p, bsz, seq):
    t = bsz * seq
    tq = _tile(seq, 256)
    tk = _tile(seq, 256)
    nq = seq // tq
    heads = 2
    width = heads * B_HEAD_DIM
    assert A_KV_W % width == 0 and B_W % width == 0
    q_off = A_KV_W // width
    k_off = q_off + B_W // width
    v_off = k_off + B_W // width
    return pl.pallas_call(
        functools.partial(_stick_kernel, tq=tq, tk=tk, heads=heads),
        grid=(bsz, B_HEADS // heads, nq),
        in_specs=[
            pl.BlockSpec((tq, width), lambda b, h, i: (b * nq + i, q_off + h)),
            pl.BlockSpec((seq, width), lambda b, h, i: (b, k_off + h)),
            pl.BlockSpec((seq, width), lambda b, h, i: (b, v_off + h)),
        ],
        out_specs=pl.BlockSpec((tq, width), lambda b, h, i: (b * nq + i, h)),
        out_shape=jax.ShapeDtypeStruct((t, B_W), BF16),
        compiler_params=_cparams(("arbitrary", "arbitrary", "arbitrary")),
        name="stick_mixer",
    )(z_p, z_p, z_p)


def _merge_kernel(oa_ref, ob_ref, g_ref, x_ref, wa_ref, wb_ref, wo_ref, gpost_ref, ga_ref,
                  gpre_ref, sc_ref, sh_ref, o_ref, h_ref):
    d = x_ref.shape[1]
    a = jnp.dot(oa_ref[...], wa_ref[...], preferred_element_type=F32)
    b = jnp.dot(ob_ref[...], wb_ref[...], preferred_element_type=F32)
    merged = g_ref[:, :d].astype(F32) * a + g_ref[:, d:].astype(F32) * b
    mix = jnp.dot(merged.astype(BF16), wo_ref[...], preferred_element_type=F32)
    y = mix * lax.rsqrt(jnp.mean(mix * mix, axis=-1, keepdims=True) + RMS_EPS)
    x1 = x_ref[...] + ga_ref[0] * (y * gpost_ref[...])
    o_ref[...] = x1
    y2 = x1 * lax.rsqrt(jnp.mean(x1 * x1, axis=-1, keepdims=True) + RMS_EPS)
    h_ref[...] = (y2 * gpre_ref[...]) * (1.0 + sc_ref[0]) + sh_ref[0]


def _merge(o_a, o_b, z_g, x2, w_a, w_b, w_o, g_post, gate, g_pre, scale, shift, seq):
    t, d = x2.shape
    bsz = gate.shape[0]
    tm = _tile(seq, 256)
    per_b = seq // tm
    const = lambda i: (0, 0)
    row = pl.BlockSpec((tm, d), lambda i: (i, 0))
    bvec = pl.BlockSpec((1, 1, d), lambda i: (i // per_b, 0, 0))
    return pl.pallas_call(
        _merge_kernel,
        grid=(t // tm,),
        in_specs=[
            pl.BlockSpec((tm, A_Q_W), lambda i: (i, 0)),
            pl.BlockSpec((tm, B_W), lambda i: (i, 0)),
            pl.BlockSpec((tm, 2 * d), lambda i: (i, 0)),
            row,
            pl.BlockSpec((A_Q_W, d), const),
            pl.BlockSpec((B_W, d), const),
            pl.BlockSpec((d, d), const),
            pl.BlockSpec((1, d), const),
            bvec,
            pl.BlockSpec((1, d), const),
            bvec,
            bvec,
        ],
        out_specs=[row, row],
        out_shape=[jax.ShapeDtypeStruct((t, d), F32)] * 2,
        compiler_params=_cparams(("arbitrary",)),
        name="merge",
    )(o_a, o_b, z_g, x2, w_a, w_b, w_o, g_post.reshape(1, d), gate.reshape(bsz, 1, d),
      g_pre.reshape(1, d), scale.reshape(bsz, 1, d), shift.reshape(bsz, 1, d))


def _router_kernel(h_ref, wr_ref, br_ref, e_ref, w_ref, r_ref, cnt_ref, base_ref):
    i = pl.program_id(0)
    tm = h_ref.shape[0]

    @pl.when(i == 0)
    def _():
        base_ref[...] = jnp.zeros_like(base_ref)

    logits = jnp.dot(h_ref[...], wr_ref[...], preferred_element_type=F32,
                     precision=lax.Precision.HIGHEST) + br_ref[...]
    lane = lax.broadcasted_iota(I32, (tm, LANES), 1)
    work = logits
    picks, vals = [], []
    for _ in range(TOP_K_EXPERTS):
        m = jnp.max(work, axis=1, keepdims=True)
        idx = jnp.min(jnp.where(work == m, lane, LANES), axis=1, keepdims=True)
        picks.append(idx)
        vals.append(m)
        work = jnp.where(lane == idx, -jnp.inf, work)
    exps = [jnp.exp(v - vals[0]) for v in vals]
    den = exps[0]
    for e in exps[1:]:
        den = den + e

    onehot = jnp.zeros((tm, LANES), F32)
    for idx in picks:
        onehot = onehot + jnp.where(lane == idx, 1.0, 0.0)
    r_io = lax.broadcasted_iota(I32, (tm, tm), 0)
    c_io = lax.broadcasted_iota(I32, (tm, tm), 1)
    lower = jnp.where(c_io < r_io, 1.0, 0.0).astype(BF16)
    prefix = jnp.dot(lower, onehot.astype(BF16), preferred_element_type=F32)
    total = prefix + base_ref[0:1, :]

    e_out = jnp.zeros((tm, LANES), I32)
    w_out = jnp.zeros((tm, LANES), F32)
    r_out = jnp.zeros((tm, LANES), I32)
    for k in range(TOP_K_EXPERTS):
        rank = jnp.sum(jnp.where(lane == picks[k], total, 0.0), axis=1, keepdims=True)
        e_out = jnp.where(lane == k, picks[k], e_out)
        w_out = jnp.where(lane == k, exps[k] / den, w_out)
        r_out = jnp.where(lane == k, rank.astype(I32), r_out)
    e_ref[...] = e_out
    w_ref[...] = w_out
    r_ref[...] = r_out
    new_base = base_ref[0:1, :] + jnp.sum(onehot, axis=0, keepdims=True)
    base_ref[...] = jnp.broadcast_to(new_base, base_ref.shape)
    cnt_ref[...] = jnp.broadcast_to(new_base, cnt_ref.shape)


def _route(h2, w_router, b_router):
    t, d = h2.shape
    tm = _tile(t, 512)
    wr = jnp.zeros((d, LANES), F32).at[:, :N_EXPERTS].set(w_router)
    br = jnp.full((1, LANES), NEG_BIG, F32).at[0, :N_EXPERTS].set(b_router)
    row = pl.BlockSpec((tm, LANES), lambda i: (i, 0))
    return pl.pallas_call(
        _router_kernel,
        grid=(t // tm,),
        in_specs=[pl.BlockSpec((tm, d), lambda i: (i, 0)),
                  pl.BlockSpec((d, LANES), lambda i: (0, 0)),
                  pl.BlockSpec((1, LANES), lambda i: (0, 0))],
        out_specs=[row, row, row, pl.BlockSpec((8, LANES), lambda i: (0, 0))],
        out_shape=[jax.ShapeDtypeStruct((t, LANES), I32),
                   jax.ShapeDtypeStruct((t, LANES), F32),
                   jax.ShapeDtypeStruct((t, LANES), I32),
                   jax.ShapeDtypeStruct((8, LANES), F32)],
        scratch_shapes=[pltpu.VMEM((8, LANES), F32)],
        compiler_params=_cparams(("arbitrary",)),
        name="router",
    )(h2, wr, br)


def _dispatch_kernel(dest_ref, padlo_ref, padlen_ref, h_ref, xs_ref, zero_ref, sem, zsem):
    tm = h_ref.shape[0]

    def pad_copies(e, act):
        first = padlo_ref[e]
        left = padlen_ref[e]
        end = first + left
        bit = MOE_ROWS // 2
        while bit >= SUBLANES:
            take = left & bit
            off = pl.multiple_of(end - bit, bit)

            @pl.when(take != 0)
            def _(off=off, bit=bit):
                act(pltpu.make_async_copy(zero_ref.at[pl.ds(0, bit)], xs_ref.at[pl.ds(off, bit)], zsem))

            end = end - take
            bit //= 2
        for r in range(SUBLANES - 1):
            @pl.when(r < (left & (SUBLANES - 1)))
            def _(r=r):
                act(pltpu.make_async_copy(zero_ref.at[pl.ds(0, 1)], xs_ref.at[pl.ds(first + r, 1)], zsem))

    @pl.when(pl.program_id(0) == 0)
    def _():
        zero_ref[...] = jnp.zeros_like(zero_ref)

        def start(e, c):
            pad_copies(e, lambda cp: cp.start())
            return c

        lax.fori_loop(0, N_EXPERTS, start, 0)

    def copy(t, k):
        return pltpu.make_async_copy(h_ref.at[pl.ds(t, 1)],
                                     xs_ref.at[pl.ds(dest_ref[0, 0, t * TOP_K_EXPERTS + k], 1)], sem)

    def issue(t, c):
        for k in range(TOP_K_EXPERTS):
            copy(t, k).start()
        return c

    def drain(t, c):
        for k in range(TOP_K_EXPERTS):
            copy(t, k).wait()
        return c

    lax.fori_loop(0, tm, issue, 0)
    lax.fori_loop(0, tm, drain, 0)

    @pl.when(pl.program_id(0) == 0)
    def _():
        def finish(e, c):
            pad_copies(e, lambda cp: cp.wait())
            return c

        lax.fori_loop(0, N_EXPERTS, finish, 0)


def _dispatch(h2, dest, pad_lo, pad_len, n_rows):
    t, d = h2.shape
    tm = _tile(t, 256)
    nt = t // tm
    dest3 = dest.reshape(nt, 1, tm * TOP_K_EXPERTS)
    smem = pl.BlockSpec(memory_space=pltpu.SMEM)
    return pl.pallas_call(
        _dispatch_kernel,
        grid=(nt,),
        in_specs=[pl.BlockSpec((1, 1, tm * TOP_K_EXPERTS), lambda i: (i, 0, 0), memory_space=pltpu.SMEM),
                  smem, smem,
                  pl.BlockSpec((tm, d), lambda i: (i, 0))],
        out_specs=pl.BlockSpec(memory_space=pl.ANY),
        out_shape=jax.ShapeDtypeStruct((n_rows, d), h2.dtype),
        scratch_shapes=[pltpu.VMEM((MOE_ROWS // 2, d), h2.dtype),
                        pltpu.SemaphoreType.DMA(()), pltpu.SemaphoreType.DMA(())],
        compiler_params=pltpu.CompilerParams(dimension_semantics=("arbitrary",),
                                             vmem_limit_bytes=VMEM_LIMIT, has_side_effects=True),
        name="moe_dispatch",
    )(dest3, pad_lo, pad_len, h2)


def _deinterleave_kernel(w_ref, p_ref, g_ref, l_ref):
    p = p_ref[...]
    width = p.shape[0]
    half = width // 2
    for c in range(w_ref.shape[2] // width):
        w = w_ref[0, :, c * width:(c + 1) * width].astype(BF16)
        r = jnp.dot(w, p, preferred_element_type=F32)
        g_ref[0, :, c * half:(c + 1) * half] = r[:, :half].astype(BF16)
        l_ref[0, :, c * half:(c + 1) * half] = r[:, half:].astype(BF16)


def _deinterleave(w1):
    n_e, d, f2 = w1.shape
    width = 2 * LANES
    src = jnp.arange(width)
    dst = jnp.where(src % 2 == 0, src // 2, LANES + src // 2)
    perm = (dst[:, None] == jnp.arange(width)[None, :]).astype(BF16)
    tr = _tile(d, 256)
    out = pl.BlockSpec((1, tr, f2 // 2), lambda e, r: (e, r, 0))
    return pl.pallas_call(
        _deinterleave_kernel,
        grid=(n_e, d // tr),
        in_specs=[pl.BlockSpec((1, tr, f2), lambda e, r: (e, r, 0)),
                  pl.BlockSpec((width, width), lambda e, r: (0, 0))],
        out_specs=[out, out],
        out_shape=[jax.ShapeDtypeStruct((n_e, d, f2 // 2), BF16)] * 2,
        compiler_params=_cparams(("arbitrary", "arbitrary")),
        name="w1_deinterleave",
    )(w1, perm)


def _expert_kernel(be_ref, nu_ref, x_ref, w1g_ref, w1l_ref, b1g_ref, b1l_ref, w2_ref, b2_ref,
                   o_ref, xb_ref, acc_ref):
    i = pl.program_id(0)
    j = pl.program_id(1)
    nf = pl.num_programs(1)
    valid = i < nu_ref[0]

    @pl.when(jnp.logical_and(valid, j == 0))
    def _():
        xb_ref[...] = x_ref[...].astype(BF16)
        acc_ref[...] = jnp.zeros_like(acc_ref)

    @pl.when(valid)
    def _():
        xb = xb_ref[...]
        glu = jnp.dot(xb, w1g_ref[0], preferred_element_type=F32) + b1g_ref[0]
        lin = jnp.dot(xb, w1l_ref[0], preferred_element_type=F32) + b1l_ref[0]
        glu = jnp.minimum(glu, SWIGLU_LIMIT)
        lin = jnp.clip(lin, -SWIGLU_LIMIT, SWIGLU_LIMIT)
        act = glu * jax.nn.sigmoid(SWIGLU_ALPHA * glu) * (lin + 1.0)
        acc_ref[...] += jnp.dot(act.astype(BF16), w2_ref[0], preferred_element_type=F32)

    @pl.when(jnp.logical_and(valid, j == nf - 1))
    def _():
        o_ref[...] = acc_ref[...] + b2_ref[0]


def _experts(xs, block_e, n_used, w1g, w1l, b1g, b1l, w2, b2):
    n_rows, d = xs.shape
    n_e, _, f = w1g.shape
    n_blocks = n_rows // MOE_ROWS
    tf = _tile(f, 512)
    nf = f // tf

    def blk(i, nu):
        return jnp.minimum(i, nu[0] - 1)

    def ftile(i, j, nu):
        return jnp.where(i < nu[0], j, nf - 1)

    grid_spec = pltpu.PrefetchScalarGridSpec(
        num_scalar_prefetch=2,
        grid=(n_blocks, nf),
        in_specs=[
            pl.BlockSpec((MOE_ROWS, d), lambda i, j, be, nu: (blk(i, nu), 0)),
            pl.BlockSpec((1, d, tf), lambda i, j, be, nu: (be[blk(i, nu)], 0, ftile(i, j, nu))),
            pl.BlockSpec((1, d, tf), lambda i, j, be, nu: (be[blk(i, nu)], 0, ftile(i, j, nu))),
            pl.BlockSpec((1, 1, tf), lambda i, j, be, nu: (be[blk(i, nu)], 0, ftile(i, j, nu))),
            pl.BlockSpec((1, 1, tf), lambda i, j, be, nu: (be[blk(i, nu)], 0, ftile(i, j, nu))),
            pl.BlockSpec((1, tf, d), lambda i, j, be, nu: (be[blk(i, nu)], ftile(i, j, nu), 0)),
            pl.BlockSpec((1, 1, d), lambda i, j, be, nu: (be[blk(i, nu)], 0, 0)),
        ],
        out_specs=pl.BlockSpec((MOE_ROWS, d), lambda i, j, be, nu: (blk(i, nu), 0)),
        scratch_shapes=[pltpu.VMEM((MOE_ROWS, d), BF16), pltpu.VMEM((MOE_ROWS, d), F32)],
    )
    return pl.pallas_call(
        _expert_kernel,
        grid_spec=grid_spec,
        out_shape=jax.ShapeDtypeStruct((n_rows, d), F32),
        compiler_params=_cparams(("arbitrary", "arbitrary")),
        name="moe_experts",
    )(block_e, n_used, xs, w1g, w1l, b1g, b1l, w2, b2)


def _combine_kernel(dest_ref, tw_ref, x_ref, gpost_ref, ga_ref, ys_ref, o_ref, buf_ref, sem):
    tm = x_ref.shape[0]

    def copy(t, k):
        return pltpu.make_async_copy(ys_ref.at[pl.ds(dest_ref[0, 0, t * TOP_K_EXPERTS + k], 1)],
                                     buf_ref.at[k, pl.ds(t, 1)], sem)

    def issue(t, c):
        for k in range(TOP_K_EXPERTS):
            copy(t, k).start()
        return c

    def drain(t, c):
        for k in range(TOP_K_EXPERTS):
            copy(t, k).wait()
        return c

    lax.fori_loop(0, tm, issue, 0)
    lax.fori_loop(0, tm, drain, 0)

    tw = tw_ref[...]
    f = tw[:, 0:1] * buf_ref[0]
    for k in range(1, TOP_K_EXPERTS):
        f = f + tw[:, k:k + 1] * buf_ref[k]
    y = f * lax.rsqrt(jnp.mean(f * f, axis=-1, keepdims=True) + RMS_EPS)
    o_ref[...] = x_ref[...] + ga_ref[0] * (y * gpost_ref[...])


def _combine(ys, dest, top_w, x1, g_post, gate, seq):
    t, d = x1.shape
    bsz = gate.shape[0]
    tm = _tile(seq, 256)
    nt = t // tm
    per_b = seq // tm
    dest3 = dest.reshape(nt, 1, tm * TOP_K_EXPERTS)
    return pl.pallas_call(
        _combine_kernel,
        grid=(nt,),
        in_specs=[pl.BlockSpec((1, 1, tm * TOP_K_EXPERTS), lambda i: (i, 0, 0), memory_space=pltpu.SMEM),
                  pl.BlockSpec((tm, LANES), lambda i: (i, 0)),
                  pl.BlockSpec((tm, d), lambda i: (i, 0)),
                  pl.BlockSpec((1, d), lambda i: (0, 0)),
                  pl.BlockSpec((1, 1, d), lambda i: (i // per_b, 0, 0)),
                  pl.BlockSpec(memory_space=pl.ANY)],
        out_specs=pl.BlockSpec((tm, d), lambda i: (i, 0)),
        out_shape=jax.ShapeDtypeStruct((t, d), F32),
        scratch_shapes=[pltpu.VMEM((TOP_K_EXPERTS, tm, d), F32), pltpu.SemaphoreType.DMA(())],
        compiler_params=_cparams(("arbitrary",)),
        name="moe_combine",
    )(dest3, top_w, x1, g_post.reshape(1, d), gate.reshape(bsz, 1, d), ys)


def _moe(h2, x1, w_router, b_router, w1, b1, w2, b2, g_post, gate, seq):
    t, d = h2.shape
    top_e, top_w, rank, counts = _route(h2, w_router, b_router)
    top_e = top_e[:, :TOP_K_EXPERTS]
    rank = rank[:, :TOP_K_EXPERTS]
    counts = counts[0, :N_EXPERTS].astype(I32)

    n_blocks = (t * TOP_K_EXPERTS) // MOE_ROWS + N_EXPERTS
    padded = (counts + MOE_ROWS - 1) // MOE_ROWS * MOE_ROWS
    pad_end = jnp.cumsum(padded)
    pad_start = pad_end - padded
    dest = (pad_start[top_e] + rank).astype(I32)
    block_row = jnp.arange(n_blocks, dtype=I32) * MOE_ROWS
    block_e = jnp.minimum(jnp.sum((pad_end[None, :] <= block_row[:, None]).astype(I32), axis=1),
                          N_EXPERTS - 1).astype(I32)
    n_used = (pad_end[-1:] // MOE_ROWS).astype(I32)

    xs = _dispatch(h2, dest, (pad_start + counts).astype(I32), (padded - counts).astype(I32),
                   n_blocks * MOE_ROWS)
    w1g, w1l = _deinterleave(w1)
    n_e, f2 = b1.shape
    b1g = b1[:, 0::2].reshape(n_e, 1, f2 // 2)
    b1l = b1[:, 1::2].reshape(n_e, 1, f2 // 2)
    ys = _experts(xs, block_e, n_used, w1g, w1l, b1g, b1l, w2.astype(BF16), b2.reshape(n_e, 1, d))
    return _combine(ys, dest, top_w, x1, g_post, gate, seq)


def _layer(x2, c, tables, bsz, seq, w_ada, b_ada, g_pre_mix, g_post_mix, w_in, w_branch_a,
           w_branch_b, w_out, g_pre_ffn, g_post_ffn, w_router, b_router, w1, b1, w2, b2):
    d = x2.shape[1]
    mod = _modulation(c, w_ada, b_ada)
    sh1, sc1, ga1, sh2, sc2, ga2 = [mod[:, k * d:(k + 1) * d] for k in range(6)]
    cos_a, sin_a, cos_i, sin_i = tables

    h = _prenorm(x2, g_pre_mix, sc1, sh1, seq, BF16)
    wb = w_in.astype(BF16)
    o = 0
    w_qka = wb[:, o:o + A_Q_W + A_KV_W]
    o += A_Q_W + A_KV_W
    w_plain = wb[:, o:o + A_KV_W + 3 * B_W]
    o += A_KV_W + 3 * B_W
    w_qi = wb[:, o:o + IDX_Q_W]
    o += IDX_Q_W
    w_kw = jnp.zeros((d, LANES), BF16).at[:, :IDX_DIM + IDX_HEADS].set(wb[:, o:o + IDX_DIM + IDX_HEADS])
    o += IDX_DIM + IDX_HEADS
    w_gate = wb[:, o:o + 2 * d]

    z_a = _project(h, w_qka, "rope_a", BF16, (cos_a, sin_a), tn_pref=640)
    z_p = _project(h, w_plain, "plain", BF16, tn_pref=1664)
    z_i = _project(h, w_qi, "rope_idx", BF16, (cos_i, sin_i))
    z_k = _project(h, w_kw, "rope_key_idx", F32, (cos_i, sin_i))
    z_g = _project(h, w_gate, "sigmoid", BF16, tn_pref=1024)

    o_a = _dsa_mixer(z_a, z_p, z_i, z_k, bsz, seq)
    o_b = _stick_mixer(z_p, bsz, seq)
    x1, h2 = _merge(o_a, o_b, z_g, x2, w_branch_a.astype(BF16), w_branch_b.astype(BF16),
                    w_out.astype(BF16), g_post_mix, ga1, g_pre_ffn, sc2, sh2, seq)

    return _moe(h2, x1, w_router, b_router, w1, b1, w2, b2, g_post_ffn, ga2, seq)


def kernel(x, c, positions, w_ada, b_ada, g_pre_mix, g_post_mix, w_in, w_branch_a, w_branch_b, w_out, g_pre_ffn, g_post_ffn, w_router, b_router, w1, b1, w2, b2):
    bsz, seq, d = x.shape
    x2 = x.reshape(bsz * seq, d)
    tables = _rope_tables(positions)
    for l in range(w_ada.shape[0]):
        x2 = _layer(x2, c, tables, bsz, seq, w_ada[l], b_ada[l], g_pre_mix[l], g_post_mix[l], w_in[l],
                    w_branch_a[l], w_branch_b[l], w_out[l], g_pre_ffn[l], g_post_ffn[l],
                    w_router[l], b_router[l], w1[l], b1[l], w2[l], b2[l])
    return x2.reshape(bsz, seq, d)
```

```python
import functools

import jax
import jax.numpy as jnp
from jax import lax
from jax.experimental import pallas as pl
from jax.experimental.pallas import tpu as pltpu

F32 = jnp.float32
BF16 = jnp.bfloat16
I32 = jnp.int32

CHUNK = 64
RMS_EPS = 1e-6
ROPE_THETA = 500000.0
A_HEADS = 8
A_KV_HEADS = 2
A_HEAD_DIM = 128
A_ROT_DIM = A_HEAD_DIM // 4
IDX_HEADS = 16
IDX_DIM = 64
IDX_ROT_DIM = IDX_DIM // 4
TOPK_MAX = 256
B_HEADS = 8
B_HEAD_DIM = 128
N_EXPERTS = 32
TOP_K_EXPERTS = 4
SWIGLU_LIMIT = 7.0
SWIGLU_ALPHA = 1.702

A_Q_W = A_HEADS * A_HEAD_DIM
A_KV_W = A_KV_HEADS * A_HEAD_DIM
B_W = B_HEADS * B_HEAD_DIM
IDX_Q_W = IDX_HEADS * IDX_DIM

LANES = 128
SUBLANES = 8
VMEM_LIMIT = 56 * 1024 * 1024
NEG_BIG = -1e30
LOG2_E = 1.4426950408889634
F32_MAX = 3.4028234663852886e38
BISECT_CAP = 320
STICK_DEAD = -106.0
MOE_ROWS = 512


def _cparams(sem):
    return pltpu.CompilerParams(dimension_semantics=sem, vmem_limit_bytes=VMEM_LIMIT)


def _tile(n, pref):
    t = min(n, pref)
    assert n % t == 0, (n, t)
    return t


def _mod_kernel(c_ref, w_ref, b_ref, o_ref):
    c = c_ref[...]
    s = c * jax.nn.sigmoid(c)
    o_ref[...] = jnp.dot(s, w_ref[...], preferred_element_type=F32,
                         precision=lax.Precision.HIGHEST) + b_ref[...]


def _modulation(c, w_ada, b_ada):
    bsz, d = c.shape
    n = w_ada.shape[1]
    rows = 8
    c_pad = jnp.zeros((rows, d), F32).at[:bsz].set(c)
    tn = _tile(n, 1024)
    out = pl.pallas_call(
        _mod_kernel,
        grid=(n // tn,),
        in_specs=[pl.BlockSpec((rows, d), lambda j: (0, 0)),
                  pl.BlockSpec((d, tn), lambda j: (0, j)),
                  pl.BlockSpec((1, tn), lambda j: (0, j))],
        out_specs=pl.BlockSpec((rows, tn), lambda j: (0, j)),
        out_shape=jax.ShapeDtypeStruct((rows, n), F32),
        compiler_params=_cparams(("arbitrary",)),
        name="modulation",
    )(c_pad, w_ada, b_ada.reshape(1, n))
    return out[:bsz]


def _rope_table_kernel(pos_ref, fa_ref, sa_ref, fi_ref, si_ref, ca_o, sa_o, ci_o, si_o):
    pos = pos_ref[...].astype(F32)
    ang_a = pos * fa_ref[...]
    ca_o[...] = jnp.cos(ang_a)
    sa_o[...] = jnp.sin(ang_a) * sa_ref[...]
    ang_i = pos * fi_ref[...]
    ci_o[...] = jnp.cos(ang_i)
    si_o[...] = jnp.sin(ang_i) * si_ref[...]


def _rope_pattern(rot_dim, head_dim):
    half = rot_dim // 2
    inv_freq = ROPE_THETA ** (-jnp.arange(half, dtype=F32) / half)
    freq = jnp.concatenate([inv_freq, inv_freq, jnp.zeros((head_dim - rot_dim,), F32)])
    sign = jnp.concatenate([-jnp.ones((half,), F32), jnp.ones((half,), F32),
                            jnp.zeros((head_dim - rot_dim,), F32)])
    reps = LANES // head_dim
    return jnp.tile(freq, reps).reshape(1, LANES), jnp.tile(sign, reps).reshape(1, LANES)


def _rope_tables(positions):
    t = positions.size
    pos = positions.reshape(t, 1).astype(I32)
    fa, sa = _rope_pattern(A_ROT_DIM, A_HEAD_DIM)
    fi, si = _rope_pattern(IDX_ROT_DIM, IDX_DIM)
    tm = _tile(t, 1024)
    pat = pl.BlockSpec((1, LANES), lambda i: (0, 0))
    tab = pl.BlockSpec((tm, LANES), lambda i: (i, 0))
    return pl.pallas_call(
        _rope_table_kernel,
        grid=(t // tm,),
        in_specs=[pl.BlockSpec((tm, 1), lambda i: (i, 0)), pat, pat, pat, pat],
        out_specs=[tab, tab, tab, tab],
        out_shape=[jax.ShapeDtypeStruct((t, LANES), F32)] * 4,
        compiler_params=_cparams(("arbitrary",)),
        name="rope_tables",
    )(pos, fa, sa, fi, si)


def _prenorm_kernel(x_ref, g_ref, sc_ref, sh_ref, o_ref):
    x = x_ref[...]
    y = x * lax.rsqrt(jnp.mean(x * x, axis=-1, keepdims=True) + RMS_EPS)
    h = (y * g_ref[...]) * (1.0 + sc_ref[0]) + sh_ref[0]
    o_ref[...] = h.astype(o_ref.dtype)


def _prenorm(x2, g, scale, shift, seq, out_dtype):
    t, d = x2.shape
    bsz = scale.shape[0]
    tm = _tile(seq, 512)
    per_b = seq // tm
    row = pl.BlockSpec((tm, d), lambda i: (i, 0))
    bvec = pl.BlockSpec((1, 1, d), lambda i: (i // per_b, 0, 0))
    return pl.pallas_call(
        _prenorm_kernel,
        grid=(t // tm,),
        in_specs=[row, pl.BlockSpec((1, d), lambda i: (0, 0)), bvec, bvec],
        out_specs=row,
        out_shape=jax.ShapeDtypeStruct((t, d), out_dtype),
        compiler_params=_cparams(("arbitrary",)),
        name="prenorm",
    )(x2, g.reshape(1, d), scale.reshape(bsz, 1, d), shift.reshape(bsz, 1, d))


def _rotate(x, cos, sin, period, half):
    lane = lax.broadcasted_iota(I32, x.shape, 1) & (period - 1)
    swapped = jnp.where(lane < half, pltpu.roll(x, LANES - half, 1), pltpu.roll(x, half, 1))
    return x * cos + swapped * sin


def _proj_kernel(*refs, epilogue):
    if epilogue in ("plain", "sigmoid"):
        h_ref, w_ref, o_ref = refs
    else:
        h_ref, w_ref, cos_ref, sin_ref, o_ref = refs
    acc = jnp.dot(h_ref[...], w_ref[...], preferred_element_type=F32)
    if epilogue == "plain":
        o_ref[...] = acc.astype(o_ref.dtype)
    elif epilogue == "sigmoid":
        o_ref[...] = jax.nn.sigmoid(acc).astype(o_ref.dtype)
    else:
        cos = cos_ref[...]
        sin = sin_ref[...]
        if epilogue == "rope_a":
            period, half = A_HEAD_DIM, A_ROT_DIM // 2
        else:
            period, half = IDX_DIM, IDX_ROT_DIM // 2
        if epilogue == "rope_key_idx":
            lane = lax.broadcasted_iota(I32, cos.shape, 1)
            cos = jnp.where(lane < IDX_DIM, cos, 1.0)
            sin = jnp.where(lane < IDX_DIM, sin, 0.0)
        for c in range(acc.shape[1] // LANES):
            sl = slice(c * LANES, (c + 1) * LANES)
            o_ref[:, sl] = _rotate(acc[:, sl], cos, sin, period, half).astype(o_ref.dtype)


def _project(h, w, epilogue, out_dtype, tables=None, tn_pref=512):
    t, d = h.shape
    n = w.shape[1]
    tm = _tile(t, 1024)
    tn = n
    for cand in (tn_pref, 256, 128):
        if n % cand == 0:
            tn = cand
            break
    in_specs = [pl.BlockSpec((tm, d), lambda i, j: (i, 0)),
                pl.BlockSpec((d, tn), lambda i, j: (0, j))]
    args = [h, w]
    if tables is not None:
        tab = pl.BlockSpec((tm, LANES), lambda i, j: (i, 0))
        in_specs += [tab, tab]
        args += list(tables)
    return pl.pallas_call(
        functools.partial(_proj_kernel, epilogue=epilogue),
        grid=(t // tm, n // tn),
        in_specs=in_specs,
        out_specs=pl.BlockSpec((tm, tn), lambda i, j: (i, j)),
        out_shape=jax.ShapeDtypeStruct((t, n), out_dtype),
        compiler_params=_cparams(("arbitrary", "arbitrary")),
        name="proj_" + epilogue,
    )(*args)


def _dsa_kernel(qi_ref, kwq_ref, kwk_ref, qa_ref, ka_ref, va_ref, o_ref,
                keys_ref, kib_ref, *, tq, tk, topk):
    i = pl.program_id(1)
    t0 = i * tq
    n_kt = (t0 + tq + tk - 1) // tk
    grp = A_HEADS // A_KV_HEADS

    @pl.when(i == 0)
    def _():
        kib_ref[...] = kwk_ref[:, :IDX_DIM].astype(BF16)

    qi = qi_ref[...]
    wq = kwq_ref[...]
    row = t0 + lax.broadcasted_iota(I32, (tq, 1), 0)
    limit = (row // CHUNK + 1) * CHUNK

    def score_tile(kt, carry):
        hi_acc, lo_acc = carry
        k0 = pl.multiple_of(kt * tk, tk)
        kb = kib_ref[pl.ds(k0, tk), :]
        acc = jnp.zeros((tq, tk), F32)
        for h in range(IDX_HEADS):
            d = lax.dot_general(qi[:, h * IDX_DIM:(h + 1) * IDX_DIM], kb,
                                (((1,), (1,)), ((), ())), preferred_element_type=F32)
            acc = acc + jnp.maximum(d, 0.0) * wq[:, IDX_DIM + h:IDX_DIM + h + 1]
        adm = (k0 + lax.broadcasted_iota(I32, (tq, tk), 1)) < limit
        s_hi = jnp.where(adm, acc, -jnp.inf)
        s_lo = jnp.where(adm, acc, jnp.inf)
        keys_ref[kt] = s_hi
        for c in range(tk // LANES):
            hi_acc = jnp.maximum(hi_acc, s_hi[:, c * LANES:(c + 1) * LANES])
            lo_acc = jnp.minimum(lo_acc, s_lo[:, c * LANES:(c + 1) * LANES])
        return hi_acc, lo_acc

    hi_acc, lo_acc = lax.fori_loop(
        0, n_kt, score_tile,
        (jnp.full((tq, LANES), -jnp.inf, F32), jnp.full((tq, LANES), jnp.inf, F32)))
    row_max = jnp.max(hi_acc, axis=1, keepdims=True)
    row_min = jnp.min(lo_acc, axis=1, keepdims=True)

    def count_ge(cand):
        def count_tile(kt, cnt):
            ge = (keys_ref[kt] >= cand).astype(I32)
            for c in range(tk // LANES):
                cnt = cnt + ge[:, c * LANES:(c + 1) * LANES]
            return cnt

        cnt = lax.fori_loop(0, n_kt, count_tile, jnp.zeros((tq, LANES), I32))
        return jnp.sum(cnt, axis=1, keepdims=True)

    def bisect_cond(state):
        it, pending = state[0], state[1]
        return jnp.logical_and(it < BISECT_CAP, pending > 0)

    def bisect_step(state):
        it, _, lo, hi, tau, todo = state
        mid = 0.5 * lo + 0.5 * hi
        stuck = jnp.logical_or(mid <= lo, mid >= hi)
        cnt = count_ge(mid)
        fin = jnp.logical_or(stuck, cnt == topk)
        tau = jnp.where(jnp.logical_and(todo > 0, fin), jnp.where(stuck, lo, mid), tau)
        up = cnt >= topk
        lo = jnp.where(up, mid, lo)
        hi = jnp.where(up, hi, mid)
        todo = jnp.where(fin, 0, todo)
        return it + 1, jnp.max(todo), lo, hi, tau, todo

    todo0 = jnp.where(limit > topk, 1, 0).astype(I32)
    hi0 = row_max + jnp.maximum(jnp.abs(row_max) * 1e-6, 1e-30)
    state = (jnp.int32(0), jnp.max(todo0), row_min, hi0,
             jnp.full((tq, 1), -F32_MAX, F32), todo0)
    _, _, lo, _, tau, todo = lax.while_loop(bisect_cond, bisect_step, state)
    tau = jnp.where(todo > 0, lo, tau)

    qscale = A_HEAD_DIM ** -0.5 * LOG2_E
    qs = [jnp.concatenate(
        [(qa_ref[:, (j * grp + g) * A_HEAD_DIM:(j * grp + g + 1) * A_HEAD_DIM].astype(F32)
          * qscale).astype(BF16) for g in range(grp)], axis=0)
        for j in range(A_KV_HEADS)]

    def attn_tile(kt, carry):
        k0 = pl.multiple_of(kt * tk, tk)
        bias = jnp.where(keys_ref[kt] >= tau, 0.0, NEG_BIG)
        out = []
        for j in range(A_KV_HEADS):
            m, l, acc = carry[3 * j:3 * j + 3]
            kk = ka_ref[pl.ds(k0, tk), j * A_HEAD_DIM:(j + 1) * A_HEAD_DIM]
            vv = va_ref[pl.ds(k0, tk), j * A_HEAD_DIM:(j + 1) * A_HEAD_DIM]
            s = lax.dot_general(qs[j], kk, (((1,), (1,)), ((), ())),
                                preferred_element_type=F32)
            s = (s.reshape(grp, tq, tk) + bias[None]).reshape(grp * tq, tk)
            m_new = jnp.maximum(m, jnp.max(s, axis=1, keepdims=True))
            alpha = jnp.exp2(m - m_new)
            p = jnp.exp2(s - m_new)
            l = alpha * l + jnp.sum(p, axis=1, keepdims=True)
            acc = alpha * acc + jnp.dot(p.astype(BF16), vv, preferred_element_type=F32)
            out += [m_new, l, acc]
        return tuple(out)

    init = []
    for j in range(A_KV_HEADS):
        init += [jnp.full((grp * tq, 1), NEG_BIG, F32), jnp.zeros((grp * tq, 1), F32),
                 jnp.zeros((grp * tq, A_HEAD_DIM), F32)]
    fin = lax.fori_loop(0, n_kt, attn_tile, tuple(init))
    for j in range(A_KV_HEADS):
        out = fin[3 * j + 2] / fin[3 * j + 1]
        for g in range(grp):
            h = j * grp + g
            o_ref[:, h * A_HEAD_DIM:(h + 1) * A_HEAD_DIM] = out[g * tq:(g + 1) * tq].astype(o_ref.dtype)


def _dsa_mixer(z_a, z_p, z_i, z_k, bsz, seq):
    t = bsz * seq
    tq = _tile(seq, 256)
    tk = _tile(seq, 1024)
    nq = seq // tq
    topk = min(TOPK_MAX, seq // 4)
    return pl.pallas_call(
        functools.partial(_dsa_kernel, tq=tq, tk=tk, topk=topk),
        grid=(bsz, nq),
        in_specs=[
            pl.BlockSpec((tq, IDX_Q_W), lambda b, i: (b * nq + i, 0)),
            pl.BlockSpec((tq, LANES), lambda b, i: (b * nq + i, 0)),
            pl.BlockSpec((seq, LANES), lambda b, i: (b, 0)),
            pl.BlockSpec((tq, A_Q_W), lambda b, i: (b * nq + i, 0)),
            pl.BlockSpec((seq, A_KV_W), lambda b, i: (b, A_Q_W // A_KV_W)),
            pl.BlockSpec((seq, A_KV_W), lambda b, i: (b, 0)),
        ],
        out_specs=pl.BlockSpec((tq, A_Q_W), lambda b, i: (b * nq + i, 0)),
        out_shape=jax.ShapeDtypeStruct((t, A_Q_W), BF16),
        scratch_shapes=[pltpu.VMEM((seq // tk, tq, tk), F32),
                        pltpu.VMEM((seq, IDX_DIM), BF16)],
        compiler_params=_cparams(("arbitrary", "arbitrary")),
        name="dsa_mixer",
    )(z_i, z_k, z_k, z_a, z_a, z_p)


def _stick_kernel(q_ref, k_ref, v_ref, o_ref, *, tq, tk, heads):
    i = pl.program_id(2)
    d_idx = (i * tq) // tk
    hd = B_HEAD_DIM
    scale = hd ** -0.5
    qs = [(q_ref[:, h * hd:(h + 1) * hd].astype(F32) * scale).astype(BF16) for h in range(heads)]
    r_io = lax.broadcasted_iota(I32, (tk, tk), 0)
    c_io = lax.broadcasted_iota(I32, (tk, tk), 1)
    upper = jnp.where(r_io > c_io, 1.0, 0.0).astype(BF16)
    diagonal = (d_idx * tk + lax.broadcasted_iota(I32, (tq, tk), 1)
                < i * tq + lax.broadcasted_iota(I32, (tq, tk), 0))

    def scores(kt):
        k0 = pl.multiple_of(kt * tk, tk)
        return tuple(lax.dot_general(qs[h], k_ref[pl.ds(k0, tk), h * hd:(h + 1) * hd],
                                     (((1,), (1,)), ((), ())), preferred_element_type=F32)
                     for h in range(heads))

    def front(z, causal):
        soft = jnp.log(1.0 + jnp.exp(-jnp.abs(z)))
        log_beta = jnp.minimum(z, 0.0) - soft
        log_keep = log_beta - z
        if causal is not None:
            log_keep = jnp.where(causal, log_keep, 0.0)
        hi = log_keep.astype(BF16)
        lo = (log_keep - hi.astype(F32)).astype(BF16)
        gap = (jnp.dot(hi, upper, preferred_element_type=F32)
               + jnp.dot(lo, upper, preferred_element_type=F32))
        logit = log_beta + gap
        if causal is not None:
            logit = jnp.where(causal, logit, NEG_BIG)
        return logit, jnp.sum(log_keep, axis=1, keepdims=True)

    def back(kt, pend, carry):
        k0 = pl.multiple_of(kt * tk, tk)
        out = []
        for h in range(heads):
            logit, keep_sum = pend[2 * h], pend[2 * h + 1]
            run, acc = carry[2 * h], carry[2 * h + 1]
            attn = jnp.exp(logit + run).astype(BF16)
            vv = v_ref[pl.ds(k0, tk), h * hd:(h + 1) * hd]
            out += [run + keep_sum, acc + jnp.dot(attn, vv, preferred_element_type=F32)]
        return tuple(out)

    def fronts(zs, causal):
        out = []
        for h in range(heads):
            out.extend(front(zs[h], causal))
        return tuple(out)

    def step(state):
        n, _, zs, pend, carry = state
        kt = d_idx - 1 - n
        zs_next = scores(jnp.maximum(kt - 1, 0))
        pend_next = fronts(zs, None)
        carry = back(kt + 1, pend, carry)
        return n + 1, run_below(carry, pend_next), zs_next, pend_next, carry

    def run_below(carry, pend):
        top = carry[0] + pend[1]
        for h in range(1, heads):
            top = jnp.maximum(top, carry[2 * h] + pend[2 * h + 1])
        return jnp.max(top)

    def alive(state):
        return jnp.logical_and(state[0] < d_idx, state[1] > STICK_DEAD)

    init = []
    for h in range(heads):
        init += [jnp.zeros((tq, 1), F32), jnp.zeros((tq, hd), F32)]
    init = tuple(init)
    pend = fronts(scores(d_idx), diagonal)
    zs = scores(jnp.maximum(d_idx - 1, 0))
    n_done, _, _, pend, carry = lax.while_loop(
        alive, step, (jnp.int32(0), run_below(init, pend), zs, pend, init))
    carry = back(d_idx - n_done, pend, carry)
    for h in range(heads):
        o_ref[:, h * hd:(h + 1) * hd] = carry[2 * h + 1].astype(o_ref.dtype)


def _stick_mixer(z_p, bsz, seq):
    t = bsz * seq
    tq = _tile(seq, 256)
    tk = _tile(seq, 256)
    nq = seq // tq
    heads = 2
    width = heads * B_HEAD_DIM
    assert A_KV_W % width == 0 and B_W % width == 0
    q_off = A_KV_W // width
    k_off = q_off + B_W // width
    v_off = k_off + B_W // width
    return pl.pallas_call(
        functools.partial(_stick_kernel, tq=tq, tk=tk, heads=heads),
        grid=(bsz, B_HEADS // heads, nq),
        in_specs=[
            pl.BlockSpec((tq, width), lambda b, h, i: (b * nq + i, q_off + h)),
            pl.BlockSpec((seq, width), lambda b, h, i: (b, k_off + h)),
            pl.BlockSpec((seq, width), lambda b, h, i: (b, v_off + h)),
        ],
        out_specs=pl.BlockSpec((tq, width), lambda b, h, i: (b * nq + i, h)),
        out_shape=jax.ShapeDtypeStruct((t, B_W), BF16),
        compiler_params=_cparams(("arbitrary", "arbitrary", "arbitrary")),
        name="stick_mixer",
    )(z_p, z_p, z_p)


def _merge_kernel(oa_ref, ob_ref, g_ref, x_ref, wa_ref, wb_ref, wo_ref, gpost_ref, ga_ref,
                  gpre_ref, sc_ref, sh_ref, o_ref, h_ref):
    d = x_ref.shape[1]
    a = jnp.dot(oa_ref[...], wa_ref[...], preferred_element_type=F32)
    b = jnp.dot(ob_ref[...], wb_ref[...], preferred_element_type=F32)
    merged = g_ref[:, :d].astype(F32) * a + g_ref[:, d:].astype(F32) * b
    mix = jnp.dot(merged.astype(BF16), wo_ref[...], preferred_element_type=F32)
    y = mix * lax.rsqrt(jnp.mean(mix * mix, axis=-1, keepdims=True) + RMS_EPS)
    x1 = x_ref[...] + ga_ref[0] * (y * gpost_ref[...])
    o_ref[...] = x1
    y2 = x1 * lax.rsqrt(jnp.mean(x1 * x1, axis=-1, keepdims=True) + RMS_EPS)
    h_ref[...] = (y2 * gpre_ref[...]) * (1.0 + sc_ref[0]) + sh_ref[0]


def _merge(o_a, o_b, z_g, x2, w_a, w_b, w_o, g_post, gate, g_pre, scale, shift, seq):
    t, d = x2.shape
    bsz = gate.shape[0]
    tm = _tile(seq, 256)
    per_b = seq // tm
    const = lambda i: (0, 0)
    row = pl.BlockSpec((tm, d), lambda i: (i, 0))
    bvec = pl.BlockSpec((1, 1, d), lambda i: (i // per_b, 0, 0))
    return pl.pallas_call(
        _merge_kernel,
        grid=(t // tm,),
        in_specs=[
            pl.BlockSpec((tm, A_Q_W), lambda i: (i, 0)),
            pl.BlockSpec((tm, B_W), lambda i: (i, 0)),
            pl.BlockSpec((tm, 2 * d), lambda i: (i, 0)),
            row,
            pl.BlockSpec((A_Q_W, d), const),
            pl.BlockSpec((B_W, d), const),
            pl.BlockSpec((d, d), const),
            pl.BlockSpec((1, d), const),
            bvec,
            pl.BlockSpec((1, d), const),
            bvec,
            bvec,
        ],
        out_specs=[row, row],
        out_shape=[jax.ShapeDtypeStruct((t, d), F32)] * 2,
        compiler_params=_cparams(("arbitrary",)),
        name="merge",
    )(o_a, o_b, z_g, x2, w_a, w_b, w_o, g_post.reshape(1, d), gate.reshape(bsz, 1, d),
      g_pre.reshape(1, d), scale.reshape(bsz, 1, d), shift.reshape(bsz, 1, d))


def _router_kernel(h_ref, wr_ref, br_ref, e_ref, w_ref, r_ref, cnt_ref, base_ref):
    i = pl.program_id(0)
    tm = h_ref.shape[0]

    @pl.when(i == 0)
    def _():
        base_ref[...] = jnp.zeros_like(base_ref)

    logits = jnp.dot(h_ref[...], wr_ref[...], preferred_element_type=F32,
                     precision=lax.Precision.HIGHEST) + br_ref[...]
    lane = lax.broadcasted_iota(I32, (tm, LANES), 1)
    work = logits
    picks, vals = [], []
    for _ in range(TOP_K_EXPERTS):
        m = jnp.max(work, axis=1, keepdims=True)
        idx = jnp.min(jnp.where(work == m, lane, LANES), axis=1, keepdims=True)
        picks.append(idx)
        vals.append(m)
        work = jnp.where(lane == idx, -jnp.inf, work)
    exps = [jnp.exp(v - vals[0]) for v in vals]
    den = exps[0]
    for e in exps[1:]:
        den = den + e

    onehot = jnp.zeros((tm, LANES), F32)
    for idx in picks:
        onehot = onehot + jnp.where(lane == idx, 1.0, 0.0)
    r_io = lax.broadcasted_iota(I32, (tm, tm), 0)
    c_io = lax.broadcasted_iota(I32, (tm, tm), 1)
    lower = jnp.where(c_io < r_io, 1.0, 0.0).astype(BF16)
    prefix = jnp.dot(lower, onehot.astype(BF16), preferred_element_type=F32)
    total = prefix + base_ref[0:1, :]

    e_out = jnp.zeros((tm, LANES), I32)
    w_out = jnp.zeros((tm, LANES), F32)
    r_out = jnp.zeros((tm, LANES), I32)
    for k in range(TOP_K_EXPERTS):
        rank = jnp.sum(jnp.where(lane == picks[k], total, 0.0), axis=1, keepdims=True)
        e_out = jnp.where(lane == k, picks[k], e_out)
        w_out = jnp.where(lane == k, exps[k] / den, w_out)
        r_out = jnp.where(lane == k, rank.astype(I32), r_out)
    e_ref[...] = e_out
    w_ref[...] = w_out
    r_ref[...] = r_out
    new_base = base_ref[0:1, :] + jnp.sum(onehot, axis=0, keepdims=True)
    base_ref[...] = jnp.broadcast_to(new_base, base_ref.shape)
    cnt_ref[...] = jnp.broadcast_to(new_base, cnt_ref.shape)


def _route(h2, w_router, b_router):
    t, d = h2.shape
    tm = _tile(t, 512)
    wr = jnp.zeros((d, LANES), F32).at[:, :N_EXPERTS].set(w_router)
    br = jnp.full((1, LANES), NEG_BIG, F32).at[0, :N_EXPERTS].set(b_router)
    row = pl.BlockSpec((tm, LANES), lambda i: (i, 0))
    return pl.pallas_call(
        _router_kernel,
        grid=(t // tm,),
        in_specs=[pl.BlockSpec((tm, d), lambda i: (i, 0)),
                  pl.BlockSpec((d, LANES), lambda i: (0, 0)),
                  pl.BlockSpec((1, LANES), lambda i: (0, 0))],
        out_specs=[row, row, row, pl.BlockSpec((8, LANES), lambda i: (0, 0))],
        out_shape=[jax.ShapeDtypeStruct((t, LANES), I32),
                   jax.ShapeDtypeStruct((t, LANES), F32),
                   jax.ShapeDtypeStruct((t, LANES), I32),
                   jax.ShapeDtypeStruct((8, LANES), F32)],
        scratch_shapes=[pltpu.VMEM((8, LANES), F32)],
        compiler_params=_cparams(("arbitrary",)),
        name="router",
    )(h2, wr, br)


def _dispatch_kernel(dest_ref, padlo_ref, padlen_ref, h_ref, xs_ref, zero_ref, sem, zsem):
    tm = h_ref.shape[0]

    def pad_copies(e, act):
        first = padlo_ref[e]
        left = padlen_ref[e]
        end = first + left
        bit = MOE_ROWS // 2
        while bit >= SUBLANES:
            take = left & bit
            off = pl.multiple_of(end - bit, bit)

            @pl.when(take != 0)
            def _(off=off, bit=bit):
                act(pltpu.make_async_copy(zero_ref.at[pl.ds(0, bit)], xs_ref.at[pl.ds(off, bit)], zsem))

            end = end - take
            bit //= 2
        for r in range(SUBLANES - 1):
            @pl.when(r < (left & (SUBLANES - 1)))
            def _(r=r):
                act(pltpu.make_async_copy(zero_ref.at[pl.ds(0, 1)], xs_ref.at[pl.ds(first + r, 1)], zsem))

    @pl.when(pl.program_id(0) == 0)
    def _():
        zero_ref[...] = jnp.zeros_like(zero_ref)

        def start(e, c):
            pad_copies(e, lambda cp: cp.start())
            return c

        lax.fori_loop(0, N_EXPERTS, start, 0)

    def copy(t, k):
        return pltpu.make_async_copy(h_ref.at[pl.ds(t, 1)],
                                     xs_ref.at[pl.ds(dest_ref[0, 0, t * TOP_K_EXPERTS + k], 1)], sem)

    def issue(t, c):
        for k in range(TOP_K_EXPERTS):
            copy(t, k).start()
        return c

    def drain(t, c):
        for k in range(TOP_K_EXPERTS):
            copy(t, k).wait()
        return c

    lax.fori_loop(0, tm, issue, 0)
    lax.fori_loop(0, tm, drain, 0)

    @pl.when(pl.program_id(0) == 0)
    def _():
        def finish(e, c):
            pad_copies(e, lambda cp: cp.wait())
            return c

        lax.fori_loop(0, N_EXPERTS, finish, 0)


def _dispatch(h2, dest, pad_lo, pad_len, n_rows):
    t, d = h2.shape
    tm = _tile(t, 256)
    nt = t // tm
    dest3 = dest.reshape(nt, 1, tm * TOP_K_EXPERTS)
    smem = pl.BlockSpec(memory_space=pltpu.SMEM)
    return pl.pallas_call(
        _dispatch_kernel,
        grid=(nt,),
        in_specs=[pl.BlockSpec((1, 1, tm * TOP_K_EXPERTS), lambda i: (i, 0, 0), memory_space=pltpu.SMEM),
                  smem, smem,
                  pl.BlockSpec((tm, d), lambda i: (i, 0))],
        out_specs=pl.BlockSpec(memory_space=pl.ANY),
        out_shape=jax.ShapeDtypeStruct((n_rows, d), h2.dtype),
        scratch_shapes=[pltpu.VMEM((MOE_ROWS // 2, d), h2.dtype),
                        pltpu.SemaphoreType.DMA(()), pltpu.SemaphoreType.DMA(())],
        compiler_params=pltpu.CompilerParams(dimension_semantics=("arbitrary",),
                                             vmem_limit_bytes=VMEM_LIMIT, has_side_effects=True),
        name="moe_dispatch",
    )(dest3, pad_lo, pad_len, h2)


def _deinterleave_kernel(w_ref, p_ref, g_ref, l_ref):
    p = p_ref[...]
    width = p.shape[0]
    half = width // 2
    for c in range(w_ref.shape[2] // width):
        w = w_ref[0, :, c * width:(c + 1) * width].astype(BF16)
        r = jnp.dot(w, p, preferred_element_type=F32)
        g_ref[0, :, c * half:(c + 1) * half] = r[:, :half].astype(BF16)
        l_ref[0, :, c * half:(c + 1) * half] = r[:, half:].astype(BF16)


def _deinterleave(w1):
    n_e, d, f2 = w1.shape
    width = 2 * LANES
    src = jnp.arange(width)
    dst = jnp.where(src % 2 == 0, src // 2, LANES + src // 2)
    perm = (dst[:, None] == jnp.arange(width)[None, :]).astype(BF16)
    tr = _tile(d, 256)
    out = pl.BlockSpec((1, tr, f2 // 2), lambda e, r: (e, r, 0))
    return pl.pallas_call(
        _deinterleave_kernel,
        grid=(n_e, d // tr),
        in_specs=[pl.BlockSpec((1, tr, f2), lambda e, r: (e, r, 0)),
                  pl.BlockSpec((width, width), lambda e, r: (0, 0))],
        out_specs=[out, out],
        out_shape=[jax.ShapeDtypeStruct((n_e, d, f2 // 2), BF16)] * 2,
        compiler_params=_cparams(("arbitrary", "arbitrary")),
        name="w1_deinterleave",
    )(w1, perm)


def _expert_kernel(be_ref, nu_ref, x_ref, w1g_ref, w1l_ref, b1g_ref, b1l_ref, w2_ref, b2_ref,
                   o_ref, xb_ref, acc_ref):
    i = pl.program_id(0)
    j = pl.program_id(1)
    nf = pl.num_programs(1)
    valid = i < nu_ref[0]

    @pl.when(jnp.logical_and(valid, j == 0))
    def _():
        xb_ref[...] = x_ref[...].astype(BF16)
        acc_ref[...] = jnp.zeros_like(acc_ref)

    @pl.when(valid)
    def _():
        xb = xb_ref[...]
        glu = jnp.dot(xb, w1g_ref[0], preferred_element_type=F32) + b1g_ref[0]
        lin = jnp.dot(xb, w1l_ref[0], preferred_element_type=F32) + b1l_ref[0]
        glu = jnp.minimum(glu, SWIGLU_LIMIT)
        lin = jnp.clip(lin, -SWIGLU_LIMIT, SWIGLU_LIMIT)
        act = glu * jax.nn.sigmoid(SWIGLU_ALPHA * glu) * (lin + 1.0)
        acc_ref[...] += jnp.dot(act.astype(BF16), w2_ref[0], preferred_element_type=F32)

    @pl.when(jnp.logical_and(valid, j == nf - 1))
    def _():
        o_ref[...] = acc_ref[...] + b2_ref[0]


def _experts(xs, block_e, n_used, w1g, w1l, b1g, b1l, w2, b2):
    n_rows, d = xs.shape
    n_e, _, f = w1g.shape
    n_blocks = n_rows // MOE_ROWS
    tf = _tile(f, 1024)
    nf = f // tf

    def blk(i, nu):
        return jnp.minimum(i, nu[0] - 1)

    def ftile(i, j, nu):
        return jnp.where(i < nu[0], j, nf - 1)

    grid_spec = pltpu.PrefetchScalarGridSpec(
        num_scalar_prefetch=2,
        grid=(n_blocks, nf),
        in_specs=[
            pl.BlockSpec((MOE_ROWS, d), lambda i, j, be, nu: (blk(i, nu), 0)),
            pl.BlockSpec((1, d, tf), lambda i, j, be, nu: (be[blk(i, nu)], 0, ftile(i, j, nu))),
            pl.BlockSpec((1, d, tf), lambda i, j, be, nu: (be[blk(i, nu)], 0, ftile(i, j, nu))),
            pl.BlockSpec((1, 1, tf), lambda i, j, be, nu: (be[blk(i, nu)], 0, ftile(i, j, nu))),
            pl.BlockSpec((1, 1, tf), lambda i, j, be, nu: (be[blk(i, nu)], 0, ftile(i, j, nu))),
            pl.BlockSpec((1, tf, d), lambda i, j, be, nu: (be[blk(i, nu)], ftile(i, j, nu), 0)),
            pl.BlockSpec((1, 1, d), lambda i, j, be, nu: (be[blk(i, nu)], 0, 0)),
        ],
        out_specs=pl.BlockSpec((MOE_ROWS, d), lambda i, j, be, nu: (blk(i, nu), 0)),
        scratch_shapes=[pltpu.VMEM((MOE_ROWS, d), BF16), pltpu.VMEM((MOE_ROWS, d), F32)],
    )
    return pl.pallas_call(
        _expert_kernel,
        grid_spec=grid_spec,
        out_shape=jax.ShapeDtypeStruct((n_rows, d), F32),
        compiler_params=_cparams(("arbitrary", "arbitrary")),
        name="moe_experts",
    )(block_e, n_used, xs, w1g, w1l, b1g, b1l, w2, b2)


def _combine_kernel(dest_ref, tw_ref, x_ref, gpost_ref, ga_ref, ys_ref, o_ref, buf_ref, sem):
    tm = x_ref.shape[0]

    def copy(t, k):
        return pltpu.make_async_copy(ys_ref.at[pl.ds(dest_ref[0, 0, t * TOP_K_EXPERTS + k], 1)],
                                     buf_ref.at[k, pl.ds(t, 1)], sem)

    def issue(t, c):
        for k in range(TOP_K_EXPERTS):
            copy(t, k).start()
        return c

    def drain(t, c):
        for k in range(TOP_K_EXPERTS):
            copy(t, k).wait()
        return c

    lax.fori_loop(0, tm, issue, 0)
    lax.fori_loop(0, tm, drain, 0)

    tw = tw_ref[...]
    f = tw[:, 0:1] * buf_ref[0]
    for k in range(1, TOP_K_EXPERTS):
        f = f + tw[:, k:k + 1] * buf_ref[k]
    y = f * lax.rsqrt(jnp.mean(f * f, axis=-1, keepdims=True) + RMS_EPS)
    o_ref[...] = x_ref[...] + ga_ref[0] * (y * gpost_ref[...])


def _combine(ys, dest, top_w, x1, g_post, gate, seq):
    t, d = x1.shape
    bsz = gate.shape[0]
    tm = _tile(seq, 256)
    nt = t // tm
    per_b = seq // tm
    dest3 = dest.reshape(nt, 1, tm * TOP_K_EXPERTS)
    return pl.pallas_call(
        _combine_kernel,
        grid=(nt,),
        in_specs=[pl.BlockSpec((1, 1, tm * TOP_K_EXPERTS), lambda i: (i, 0, 0), memory_space=pltpu.SMEM),
                  pl.BlockSpec((tm, LANES), lambda i: (i, 0)),
                  pl.BlockSpec((tm, d), lambda i: (i, 0)),
                  pl.BlockSpec((1, d), lambda i: (0, 0)),
                  pl.BlockSpec((1, 1, d), lambda i: (i // per_b, 0, 0)),
                  pl.BlockSpec(memory_space=pl.ANY)],
        out_specs=pl.BlockSpec((tm, d), lambda i: (i, 0)),
        out_shape=jax.ShapeDtypeStruct((t, d), F32),
        scratch_shapes=[pltpu.VMEM((TOP_K_EXPERTS, tm, d), F32), pltpu.SemaphoreType.DMA(())],
        compiler_params=_cparams(("arbitrary",)),
        name="moe_combine",
    )(dest3, top_w, x1, g_post.reshape(1, d), gate.reshape(bsz, 1, d), ys)


def _moe(h2, x1, w_router, b_router, w1, b1, w2, b2, g_post, gate, seq):
    t, d = h2.shape
    top_e, top_w, rank, counts = _route(h2, w_router, b_router)
    top_e = top_e[:, :TOP_K_EXPERTS]
    rank = rank[:, :TOP_K_EXPERTS]
    counts = counts[0, :N_EXPERTS].astype(I32)

    n_blocks = (t * TOP_K_EXPERTS) // MOE_ROWS + N_EXPERTS
    padded = (counts + MOE_ROWS - 1) // MOE_ROWS * MOE_ROWS
    pad_end = jnp.cumsum(padded)
    pad_start = pad_end - padded
    dest = (pad_start[top_e] + rank).astype(I32)
    block_row = jnp.arange(n_blocks, dtype=I32) * MOE_ROWS
    block_e = jnp.minimum(jnp.sum((pad_end[None, :] <= block_row[:, None]).astype(I32), axis=1),
                          N_EXPERTS - 1).astype(I32)
    n_used = (pad_end[-1:] // MOE_ROWS).astype(I32)

    xs = _dispatch(h2, dest, (pad_start + counts).astype(I32), (padded - counts).astype(I32),
                   n_blocks * MOE_ROWS)
    w1g, w1l = _deinterleave(w1)
    n_e, f2 = b1.shape
    b1g = b1[:, 0::2].reshape(n_e, 1, f2 // 2)
    b1l = b1[:, 1::2].reshape(n_e, 1, f2 // 2)
    ys = _experts(xs, block_e, n_used, w1g, w1l, b1g, b1l, w2.astype(BF16), b2.reshape(n_e, 1, d))
    return _combine(ys, dest, top_w, x1, g_post, gate, seq)


def _layer(x2, c, tables, bsz, seq, w_ada, b_ada, g_pre_mix, g_post_mix, w_in, w_branch_a,
           w_branch_b, w_out, g_pre_ffn, g_post_ffn, w_router, b_router, w1, b1, w2, b2):
    d = x2.shape[1]
    mod = _modulation(c, w_ada, b_ada)
    sh1, sc1, ga1, sh2, sc2, ga2 = [mod[:, k * d:(k + 1) * d] for k in range(6)]
    cos_a, sin_a, cos_i, sin_i = tables

    h = _prenorm(x2, g_pre_mix, sc1, sh1, seq, BF16)
    wb = w_in.astype(BF16)
    o = 0
    w_qka = wb[:, o:o + A_Q_W + A_KV_W]
    o += A_Q_W + A_KV_W
    w_plain = wb[:, o:o + A_KV_W + 3 * B_W]
    o += A_KV_W + 3 * B_W
    w_qi = wb[:, o:o + IDX_Q_W]
    o += IDX_Q_W
    w_kw = jnp.zeros((d, LANES), BF16).at[:, :IDX_DIM + IDX_HEADS].set(wb[:, o:o + IDX_DIM + IDX_HEADS])
    o += IDX_DIM + IDX_HEADS
    w_gate = wb[:, o:o + 2 * d]

    z_a = _project(h, w_qka, "rope_a", BF16, (cos_a, sin_a), tn_pref=640)
    z_p = _project(h, w_plain, "plain", BF16, tn_pref=1664)
    z_i = _project(h, w_qi, "rope_idx", BF16, (cos_i, sin_i))
    z_k = _project(h, w_kw, "rope_key_idx", F32, (cos_i, sin_i))
    z_g = _project(h, w_gate, "sigmoid", BF16, tn_pref=1024)

    o_a = _dsa_mixer(z_a, z_p, z_i, z_k, bsz, seq)
    o_b = _stick_mixer(z_p, bsz, seq)
    x1, h2 = _merge(o_a, o_b, z_g, x2, w_branch_a.astype(BF16), w_branch_b.astype(BF16),
                    w_out.astype(BF16), g_post_mix, ga1, g_pre_ffn, sc2, sh2, seq)

    return _moe(h2, x1, w_router, b_router, w1, b1, w2, b2, g_post_ffn, ga2, seq)


def kernel(x, c, positions, w_ada, b_ada, g_pre_mix, g_post_mix, w_in, w_branch_a, w_branch_b, w_out, g_pre_ffn, g_post_ffn, w_router, b_router, w1, b1, w2, b2):
    bsz, seq, d = x.shape
    x2 = x.reshape(bsz * seq, d)
    tables = _rope_tables(positions)
    for l in range(w_ada.shape[0]):
        x2 = _layer(x2, c, tables, bsz, seq, w_ada[l], b_ada[l], g_pre_mix[l], g_post_mix[l], w_in[l],
                    w_branch_a[l], w_branch_b[l], w_out[l], g_pre_ffn[l], g_post_ffn[l],
                    w_router[l], b_router[l], w1[l], b1[l], w2[l], b2[l])
    return x2.reshape(bsz, seq, d)
```

```python
import functools

import jax
import jax.numpy as jnp
from jax import lax
from jax.experimental import pallas as pl
from jax.experimental.pallas import tpu as pltpu

F32 = jnp.float32
BF16 = jnp.bfloat16
I32 = jnp.int32

CHUNK = 64
RMS_EPS = 1e-6
ROPE_THETA = 500000.0
A_HEADS = 8
A_KV_HEADS = 2
A_HEAD_DIM = 128
A_ROT_DIM = A_HEAD_DIM // 4
IDX_HEADS = 16
IDX_DIM = 64
IDX_ROT_DIM = IDX_DIM // 4
TOPK_MAX = 256
B_HEADS = 8
B_HEAD_DIM = 128
N_EXPERTS = 32
TOP_K_EXPERTS = 4
SWIGLU_LIMIT = 7.0
SWIGLU_ALPHA = 1.702

A_Q_W = A_HEADS * A_HEAD_DIM
A_KV_W = A_KV_HEADS * A_HEAD_DIM
B_W = B_HEADS * B_HEAD_DIM
IDX_Q_W = IDX_HEADS * IDX_DIM

LANES = 128
SUBLANES = 8
VMEM_LIMIT = 56 * 1024 * 1024
NEG_BIG = -1e30
INT_MAX = 2 ** 31 - 1
LOG2_E = 1.4426950408889634
F32_MAX = 3.4028234663852886e38
BISECT_CAP = 320
STICK_DEAD = -106.0
MOE_ROWS = 512


def _cparams(sem):
    return pltpu.CompilerParams(dimension_semantics=sem, vmem_limit_bytes=VMEM_LIMIT)


def _tile(n, pref):
    t = min(n, pref)
    assert n % t == 0, (n, t)
    return t


def _mod_kernel(c_ref, w_ref, b_ref, o_ref):
    c = c_ref[...]
    s = c * jax.nn.sigmoid(c)
    o_ref[...] = jnp.dot(s, w_ref[...], preferred_element_type=F32,
                         precision=lax.Precision.HIGHEST) + b_ref[...]


def _modulation(c, w_ada, b_ada):
    bsz, d = c.shape
    n = w_ada.shape[1]
    rows = 8
    c_pad = jnp.zeros((rows, d), F32).at[:bsz].set(c)
    tn = _tile(n, 1024)
    out = pl.pallas_call(
        _mod_kernel,
        grid=(n // tn,),
        in_specs=[pl.BlockSpec((rows, d), lambda j: (0, 0)),
                  pl.BlockSpec((d, tn), lambda j: (0, j)),
                  pl.BlockSpec((1, tn), lambda j: (0, j))],
        out_specs=pl.BlockSpec((rows, tn), lambda j: (0, j)),
        out_shape=jax.ShapeDtypeStruct((rows, n), F32),
        compiler_params=_cparams(("arbitrary",)),
        name="modulation",
    )(c_pad, w_ada, b_ada.reshape(1, n))
    return out[:bsz]


def _rope_table_kernel(pos_ref, fa_ref, sa_ref, fi_ref, si_ref, ca_o, sa_o, ci_o, si_o):
    pos = pos_ref[...].astype(F32)
    ang_a = pos * fa_ref[...]
    ca_o[...] = jnp.cos(ang_a)
    sa_o[...] = jnp.sin(ang_a) * sa_ref[...]
    ang_i = pos * fi_ref[...]
    ci_o[...] = jnp.cos(ang_i)
    si_o[...] = jnp.sin(ang_i) * si_ref[...]


def _rope_pattern(rot_dim, head_dim):
    half = rot_dim // 2
    inv_freq = ROPE_THETA ** (-jnp.arange(half, dtype=F32) / half)
    freq = jnp.concatenate([inv_freq, inv_freq, jnp.zeros((head_dim - rot_dim,), F32)])
    sign = jnp.concatenate([-jnp.ones((half,), F32), jnp.ones((half,), F32),
                            jnp.zeros((head_dim - rot_dim,), F32)])
    reps = LANES // head_dim
    return jnp.tile(freq, reps).reshape(1, LANES), jnp.tile(sign, reps).reshape(1, LANES)


def _rope_tables(positions):
    t = positions.size
    pos = positions.reshape(t, 1).astype(I32)
    fa, sa = _rope_pattern(A_ROT_DIM, A_HEAD_DIM)
    fi, si = _rope_pattern(IDX_ROT_DIM, IDX_DIM)
    tm = _tile(t, 1024)
    pat = pl.BlockSpec((1, LANES), lambda i: (0, 0))
    tab = pl.BlockSpec((tm, LANES), lambda i: (i, 0))
    return pl.pallas_call(
        _rope_table_kernel,
        grid=(t // tm,),
        in_specs=[pl.BlockSpec((tm, 1), lambda i: (i, 0)), pat, pat, pat, pat],
        out_specs=[tab, tab, tab, tab],
        out_shape=[jax.ShapeDtypeStruct((t, LANES), F32)] * 4,
        compiler_params=_cparams(("arbitrary",)),
        name="rope_tables",
    )(pos, fa, sa, fi, si)


def _prenorm_kernel(x_ref, g_ref, sc_ref, sh_ref, o_ref):
    x = x_ref[...]
    y = x * lax.rsqrt(jnp.mean(x * x, axis=-1, keepdims=True) + RMS_EPS)
    h = (y * g_ref[...]) * (1.0 + sc_ref[0]) + sh_ref[0]
    o_ref[...] = h.astype(o_ref.dtype)


def _prenorm(x2, g, scale, shift, seq, out_dtype):
    t, d = x2.shape
    bsz = scale.shape[0]
    tm = _tile(seq, 512)
    per_b = seq // tm
    row = pl.BlockSpec((tm, d), lambda i: (i, 0))
    bvec = pl.BlockSpec((1, 1, d), lambda i: (i // per_b, 0, 0))
    return pl.pallas_call(
        _prenorm_kernel,
        grid=(t // tm,),
        in_specs=[row, pl.BlockSpec((1, d), lambda i: (0, 0)), bvec, bvec],
        out_specs=row,
        out_shape=jax.ShapeDtypeStruct((t, d), out_dtype),
        compiler_params=_cparams(("arbitrary",)),
        name="prenorm",
    )(x2, g.reshape(1, d), scale.reshape(bsz, 1, d), shift.reshape(bsz, 1, d))


def _rotate(x, cos, sin, period, half):
    lane = lax.broadcasted_iota(I32, x.shape, 1) & (period - 1)
    swapped = jnp.where(lane < half, pltpu.roll(x, LANES - half, 1), pltpu.roll(x, half, 1))
    return x * cos + swapped * sin


def _proj_kernel(*refs, epilogue):
    if epilogue in ("plain", "sigmoid"):
        h_ref, w_ref, o_ref = refs
    else:
        h_ref, w_ref, cos_ref, sin_ref, o_ref = refs
    acc = jnp.dot(h_ref[...], w_ref[...], preferred_element_type=F32)
    if epilogue == "plain":
        o_ref[...] = acc.astype(o_ref.dtype)
    elif epilogue == "sigmoid":
        o_ref[...] = jax.nn.sigmoid(acc).astype(o_ref.dtype)
    else:
        cos = cos_ref[...]
        sin = sin_ref[...]
        if epilogue == "rope_a":
            period, half = A_HEAD_DIM, A_ROT_DIM // 2
        else:
            period, half = IDX_DIM, IDX_ROT_DIM // 2
        if epilogue == "rope_key_idx":
            lane = lax.broadcasted_iota(I32, cos.shape, 1)
            cos = jnp.where(lane < IDX_DIM, cos, 1.0)
            sin = jnp.where(lane < IDX_DIM, sin, 0.0)
        for c in range(acc.shape[1] // LANES):
            sl = slice(c * LANES, (c + 1) * LANES)
            o_ref[:, sl] = _rotate(acc[:, sl], cos, sin, period, half).astype(o_ref.dtype)


def _project(h, w, epilogue, out_dtype, tables=None, tn_pref=512):
    t, d = h.shape
    n = w.shape[1]
    tm = _tile(t, 1024)
    tn = n
    for cand in (tn_pref, 256, 128):
        if n % cand == 0:
            tn = cand
            break
    in_specs = [pl.BlockSpec((tm, d), lambda i, j: (i, 0)),
                pl.BlockSpec((d, tn), lambda i, j: (0, j))]
    args = [h, w]
    if tables is not None:
        tab = pl.BlockSpec((tm, LANES), lambda i, j: (i, 0))
        in_specs += [tab, tab]
        args += list(tables)
    return pl.pallas_call(
        functools.partial(_proj_kernel, epilogue=epilogue),
        grid=(t // tm, n // tn),
        in_specs=in_specs,
        out_specs=pl.BlockSpec((tm, tn), lambda i, j: (i, j)),
        out_shape=jax.ShapeDtypeStruct((t, n), out_dtype),
        compiler_params=_cparams(("arbitrary", "arbitrary")),
        name="proj_" + epilogue,
    )(*args)


def _dsa_kernel(qi_ref, kwq_ref, kwk_ref, qa_ref, ka_ref, va_ref, o_ref,
                keys_ref, kib_ref, *, tq, tk, topk):
    i = pl.program_id(1)
    t0 = i * tq
    n_kt = (t0 + tq + tk - 1) // tk
    grp = A_HEADS // A_KV_HEADS
    n_cut_steps = (kwk_ref.shape[0] - 1).bit_length() + 1

    @pl.when(i == 0)
    def _():
        kib_ref[...] = kwk_ref[:, :IDX_DIM].astype(BF16)

    qi = qi_ref[...]
    wq = kwq_ref[...]
    row = t0 + lax.broadcasted_iota(I32, (tq, 1), 0)
    limit = (row // CHUNK + 1) * CHUNK

    def score_tile(kt, carry):
        hi_acc, lo_acc = carry
        k0 = pl.multiple_of(kt * tk, tk)
        kb = kib_ref[pl.ds(k0, tk), :]
        acc = jnp.zeros((tq, tk), F32)
        for h in range(IDX_HEADS):
            d = lax.dot_general(qi[:, h * IDX_DIM:(h + 1) * IDX_DIM], kb,
                                (((1,), (1,)), ((), ())), preferred_element_type=F32)
            acc = acc + jnp.maximum(d, 0.0) * wq[:, IDX_DIM + h:IDX_DIM + h + 1]
        adm = (k0 + lax.broadcasted_iota(I32, (tq, tk), 1)) < limit
        s_hi = jnp.where(adm, acc, -jnp.inf)
        s_lo = jnp.where(adm, acc, jnp.inf)
        keys_ref[kt] = s_hi
        for c in range(tk // LANES):
            hi_acc = jnp.maximum(hi_acc, s_hi[:, c * LANES:(c + 1) * LANES])
            lo_acc = jnp.minimum(lo_acc, s_lo[:, c * LANES:(c + 1) * LANES])
        return hi_acc, lo_acc

    hi_acc, lo_acc = lax.fori_loop(
        0, n_kt, score_tile,
        (jnp.full((tq, LANES), -jnp.inf, F32), jnp.full((tq, LANES), jnp.inf, F32)))
    row_max = jnp.max(hi_acc, axis=1, keepdims=True)
    row_min = jnp.min(lo_acc, axis=1, keepdims=True)

    def count_where(pred):
        def count_tile(kt, cnt):
            hit = pred(keys_ref[kt], kt).astype(I32)
            for c in range(tk // LANES):
                cnt = cnt + hit[:, c * LANES:(c + 1) * LANES]
            return cnt

        cnt = lax.fori_loop(0, n_kt, count_tile, jnp.zeros((tq, LANES), I32))
        return jnp.sum(cnt, axis=1, keepdims=True)

    def bisect_cond(state):
        it, pending = state[0], state[1]
        return jnp.logical_and(it < BISECT_CAP, pending > 0)

    def bisect_step(state):
        it, _, lo, hi, c_lo, tau, surplus, todo = state
        mid = 0.5 * lo + 0.5 * hi
        stuck = jnp.logical_or(mid <= lo, mid >= hi)
        cnt = count_where(lambda s, kt: s >= mid)
        fin = jnp.logical_or(stuck, cnt == topk)
        done_now = jnp.logical_and(todo > 0, fin)
        tau = jnp.where(done_now, jnp.where(stuck, lo, mid), tau)
        surplus = jnp.where(done_now, jnp.where(stuck, c_lo - topk, 0), surplus)
        up = cnt >= topk
        lo = jnp.where(up, mid, lo)
        c_lo = jnp.where(up, cnt, c_lo)
        hi = jnp.where(up, hi, mid)
        todo = jnp.where(fin, 0, todo)
        return it + 1, jnp.max(todo), lo, hi, c_lo, tau, surplus, todo

    todo0 = jnp.where(limit > topk, 1, 0).astype(I32)
    hi0 = row_max + jnp.maximum(jnp.abs(row_max) * 1e-6, 1e-30)
    state = (jnp.int32(0), jnp.max(todo0), row_min, hi0, limit,
             jnp.full((tq, 1), -F32_MAX, F32), jnp.zeros((tq, 1), I32), todo0)
    _, _, lo, _, c_lo, tau, surplus, todo = lax.while_loop(bisect_cond, bisect_step, state)
    tau = jnp.where(todo > 0, lo, tau)
    surplus = jnp.where(todo > 0, c_lo - topk, surplus)

    any_ties = jnp.max(surplus) > 0

    def key_pos(kt):
        return kt * tk + lax.broadcasted_iota(I32, (tq, tk), 1)

    def tie_cutoff():
        need = topk - count_where(lambda s, kt: s > tau)

        def cut_step(_, st):
            lo_c, hi_c = st
            mid = lax.shift_right_logical(lo_c + hi_c, 1)
            kept = count_where(lambda s, kt: jnp.logical_and(s == tau, key_pos(kt) < mid))
            ok = kept >= need
            return jnp.where(ok, lo_c, mid), jnp.where(ok, mid, hi_c)

        _, hi_c = lax.fori_loop(0, n_cut_steps, cut_step, (jnp.zeros((tq, 1), I32), limit))
        return jnp.where(surplus > 0, hi_c, INT_MAX)

    cutoff = lax.cond(any_ties, tie_cutoff, lambda: jnp.full((tq, 1), INT_MAX, I32))

    def select_plain(kt):
        return jnp.where(keys_ref[kt] >= tau, 0.0, NEG_BIG)

    def select_ties(kt):
        s = keys_ref[kt]
        keep = jnp.logical_or(s > tau, jnp.logical_and(s == tau, key_pos(kt) < cutoff))
        return jnp.where(keep, 0.0, NEG_BIG)

    qscale = A_HEAD_DIM ** -0.5 * LOG2_E
    qs = [jnp.concatenate(
        [(qa_ref[:, (j * grp + g) * A_HEAD_DIM:(j * grp + g + 1) * A_HEAD_DIM].astype(F32)
          * qscale).astype(BF16) for g in range(grp)], axis=0)
        for j in range(A_KV_HEADS)]

    def attn_tile(kt, carry):
        k0 = pl.multiple_of(kt * tk, tk)
        bias = lax.cond(any_ties, select_ties, select_plain, kt)
        out = []
        for j in range(A_KV_HEADS):
            m, l, acc = carry[3 * j:3 * j + 3]
            kk = ka_ref[pl.ds(k0, tk), j * A_HEAD_DIM:(j + 1) * A_HEAD_DIM]
            vv = va_ref[pl.ds(k0, tk), j * A_HEAD_DIM:(j + 1) * A_HEAD_DIM]
            s = lax.dot_general(qs[j], kk, (((1,), (1,)), ((), ())),
                                preferred_element_type=F32)
            s = (s.reshape(grp, tq, tk) + bias[None]).reshape(grp * tq, tk)
            m_new = jnp.maximum(m, jnp.max(s, axis=1, keepdims=True))
            alpha = jnp.exp2(m - m_new)
            p = jnp.exp2(s - m_new)
            l = alpha * l + jnp.sum(p, axis=1, keepdims=True)
            acc = alpha * acc + jnp.dot(p.astype(BF16), vv, preferred_element_type=F32)
            out += [m_new, l, acc]
        return tuple(out)

    init = []
    for j in range(A_KV_HEADS):
        init += [jnp.full((grp * tq, 1), NEG_BIG, F32), jnp.zeros((grp * tq, 1), F32),
                 jnp.zeros((grp * tq, A_HEAD_DIM), F32)]
    fin = lax.fori_loop(0, n_kt, attn_tile, tuple(init))
    for j in range(A_KV_HEADS):
        out = fin[3 * j + 2] / fin[3 * j + 1]
        for g in range(grp):
            h = j * grp + g
            o_ref[:, h * A_HEAD_DIM:(h + 1) * A_HEAD_DIM] = out[g * tq:(g + 1) * tq].astype(o_ref.dtype)


def _dsa_mixer(z_a, z_p, z_i, z_k, bsz, seq):
    t = bsz * seq
    tq = _tile(seq, 256)
    tk = _tile(seq, 1024)
    nq = seq // tq
    topk = min(TOPK_MAX, seq // 4)
    return pl.pallas_call(
        functools.partial(_dsa_kernel, tq=tq, tk=tk, topk=topk),
        grid=(bsz, nq),
        in_specs=[
            pl.BlockSpec((tq, IDX_Q_W), lambda b, i: (b * nq + i, 0)),
            pl.BlockSpec((tq, LANES), lambda b, i: (b * nq + i, 0)),
            pl.BlockSpec((seq, LANES), lambda b, i: (b, 0)),
            pl.BlockSpec((tq, A_Q_W), lambda b, i: (b * nq + i, 0)),
            pl.BlockSpec((seq, A_KV_W), lambda b, i: (b, A_Q_W // A_KV_W)),
            pl.BlockSpec((seq, A_KV_W), lambda b, i: (b, 0)),
        ],
        out_specs=pl.BlockSpec((tq, A_Q_W), lambda b, i: (b * nq + i, 0)),
        out_shape=jax.ShapeDtypeStruct((t, A_Q_W), BF16),
        scratch_shapes=[pltpu.VMEM((seq // tk, tq, tk), F32),
                        pltpu.VMEM((seq, IDX_DIM), BF16)],
        compiler_params=_cparams(("arbitrary", "arbitrary")),
        name="dsa_mixer",
    )(z_i, z_k, z_k, z_a, z_a, z_p)


def _stick_kernel(q_ref, k_ref, v_ref, o_ref, *, tq, tk, heads):
    i = pl.program_id(2)
    d_idx = (i * tq) // tk
    hd = B_HEAD_DIM
    scale = hd ** -0.5
    qs = [(q_ref[:, h * hd:(h + 1) * hd].astype(F32) * scale).astype(BF16) for h in range(heads)]
    r_io = lax.broadcasted_iota(I32, (tk, tk), 0)
    c_io = lax.broadcasted_iota(I32, (tk, tk), 1)
    upper = jnp.where(r_io > c_io, 1.0, 0.0).astype(BF16)
    diagonal = (d_idx * tk + lax.broadcasted_iota(I32, (tq, tk), 1)
                < i * tq + lax.broadcasted_iota(I32, (tq, tk), 0))

    def scores(kt):
        k0 = pl.multiple_of(kt * tk, tk)
        return tuple(lax.dot_general(qs[h], k_ref[pl.ds(k0, tk), h * hd:(h + 1) * hd],
                                     (((1,), (1,)), ((), ())), preferred_element_type=F32)
                     for h in range(heads))

    def front(z, causal):
        soft = jnp.log(1.0 + jnp.exp(-jnp.abs(z)))
        log_beta = jnp.minimum(z, 0.0) - soft
        log_keep = log_beta - z
        if causal is not None:
            log_keep = jnp.where(causal, log_keep, 0.0)
        hi = log_keep.astype(BF16)
        lo = (log_keep - hi.astype(F32)).astype(BF16)
        gap = (jnp.dot(hi, upper, preferred_element_type=F32)
               + jnp.dot(lo, upper, preferred_element_type=F32))
        logit = log_beta + gap
        if causal is not None:
            logit = jnp.where(causal, logit, NEG_BIG)
        return logit, jnp.sum(log_keep, axis=1, keepdims=True)

    def back(kt, pend, carry):
        k0 = pl.multiple_of(kt * tk, tk)
        out = []
        for h in range(heads):
            logit, keep_sum = pend[2 * h], pend[2 * h + 1]
            run, acc = carry[2 * h], carry[2 * h + 1]
            attn = jnp.exp(logit + run).astype(BF16)
            vv = v_ref[pl.ds(k0, tk), h * hd:(h + 1) * hd]
            out += [run + keep_sum, acc + jnp.dot(attn, vv, preferred_element_type=F32)]
        return tuple(out)

    def fronts(zs, causal):
        out = []
        for h in range(heads):
            out.extend(front(zs[h], causal))
        return tuple(out)

    def step(state):
        n, _, zs, pend, carry = state
        kt = d_idx - 1 - n
        zs_next = scores(jnp.maximum(kt - 1, 0))
        pend_next = fronts(zs, None)
        carry = back(kt + 1, pend, carry)
        return n + 1, run_below(carry, pend_next), zs_next, pend_next, carry

    def run_below(carry, pend):
        top = carry[0] + pend[1]
        for h in range(1, heads):
            top = jnp.maximum(top, carry[2 * h] + pend[2 * h + 1])
        return jnp.max(top)

    def alive(state):
        return jnp.logical_and(state[0] < d_idx, state[1] > STICK_DEAD)

    init = []
    for h in range(heads):
        init += [jnp.zeros((tq, 1), F32), jnp.zeros((tq, hd), F32)]
    init = tuple(init)
    pend = fronts(scores(d_idx), diagonal)
    zs = scores(jnp.maximum(d_idx - 1, 0))
    n_done, _, _, pend, carry = lax.while_loop(
        alive, step, (jnp.int32(0), run_below(init, pend), zs, pend, init))
    carry = back(d_idx - n_done, pend, carry)
    for h in range(heads):
        o_ref[:, h * hd:(h + 1) * hd] = carry[2 * h + 1].astype(o_ref.dtype)


def _stick_mixer(z_p, bsz, seq):
    t = bsz * seq
    tq = _tile(seq, 256)
    tk = _tile(seq, 256)
    nq = seq // tq
    heads = 2
    width = heads * B_HEAD_DIM
    assert A_KV_W % width == 0 and B_W % width == 0
    q_off = A_KV_W // width
    k_off = q_off + B_W // width
    v_off = k_off + B_W // width
    return pl.pallas_call(
        functools.partial(_stick_kernel, tq=tq, tk=tk, heads=heads),
        grid=(bsz, B_HEADS // heads, nq),
        in_specs=[
            pl.BlockSpec((tq, width), lambda b, h, i: (b * nq + i, q_off + h)),
            pl.BlockSpec((seq, width), lambda b, h, i: (b, k_off + h)),
            pl.BlockSpec((seq, width), lambda b, h, i: (b, v_off + h)),
        ],
        out_specs=pl.BlockSpec((tq, width), lambda b, h, i: (b * nq + i, h)),
        out_shape=jax.ShapeDtypeStruct((t, B_W), BF16),
        compiler_params=_cparams(("arbitrary", "arbitrary", "arbitrary")),
        name="stick_mixer",
    )(z_p, z_p, z_p)


def _merge_kernel(oa_ref, ob_ref, g_ref, x_ref, wa_ref, wb_ref, wo_ref, gpost_ref, ga_ref,
                  gpre_ref, sc_ref, sh_ref, o_ref, h_ref):
    d = x_ref.shape[1]
    a = jnp.dot(oa_ref[...], wa_ref[...], preferred_element_type=F32)
    b = jnp.dot(ob_ref[...], wb_ref[...], preferred_element_type=F32)
    merged = g_ref[:, :d].astype(F32) * a + g_ref[:, d:].astype(F32) * b
    mix = jnp.dot(merged.astype(BF16), wo_ref[...], preferred_element_type=F32)
    y = mix * lax.rsqrt(jnp.mean(mix * mix, axis=-1, keepdims=True) + RMS_EPS)
    x1 = x_ref[...] + ga_ref[0] * (y * gpost_ref[...])
    o_ref[...] = x1
    y2 = x1 * lax.rsqrt(jnp.mean(x1 * x1, axis=-1, keepdims=True) + RMS_EPS)
    h_ref[...] = (y2 * gpre_ref[...]) * (1.0 + sc_ref[0]) + sh_ref[0]


def _merge(o_a, o_b, z_g, x2, w_a, w_b, w_o, g_post, gate, g_pre, scale, shift, seq):
    t, d = x2.shape
    bsz = gate.shape[0]
    tm = _tile(seq, 256)
    per_b = seq // tm
    const = lambda i: (0, 0)
    row = pl.BlockSpec((tm, d), lambda i: (i, 0))
    bvec = pl.BlockSpec((1, 1, d), lambda i: (i // per_b, 0, 0))
    return pl.pallas_call(
        _merge_kernel,
        grid=(t // tm,),
        in_specs=[
            pl.BlockSpec((tm, A_Q_W), lambda i: (i, 0)),
            pl.BlockSpec((tm, B_W), lambda i: (i, 0)),
            pl.BlockSpec((tm, 2 * d), lambda i: (i, 0)),
            row,
            pl.BlockSpec((A_Q_W, d), const),
            pl.BlockSpec((B_W, d), const),
            pl.BlockSpec((d, d), const),
            pl.BlockSpec((1, d), const),
            bvec,
            pl.BlockSpec((1, d), const),
            bvec,
            bvec,
        ],
        out_specs=[row, row],
        out_shape=[jax.ShapeDtypeStruct((t, d), F32)] * 2,
        compiler_params=_cparams(("arbitrary",)),
        name="merge",
    )(o_a, o_b, z_g, x2, w_a, w_b, w_o, g_post.reshape(1, d), gate.reshape(bsz, 1, d),
      g_pre.reshape(1, d), scale.reshape(bsz, 1, d), shift.reshape(bsz, 1, d))


def _router_kernel(h_ref, wr_ref, br_ref, e_ref, w_ref, r_ref, cnt_ref, base_ref):
    i = pl.program_id(0)
    tm = h_ref.shape[0]

    @pl.when(i == 0)
    def _():
        base_ref[...] = jnp.zeros_like(base_ref)

    logits = jnp.dot(h_ref[...], wr_ref[...], preferred_element_type=F32,
                     precision=lax.Precision.HIGHEST) + br_ref[...]
    lane = lax.broadcasted_iota(I32, (tm, LANES), 1)
    work = logits
    picks, vals = [], []
    for _ in range(TOP_K_EXPERTS):
        m = jnp.max(work, axis=1, keepdims=True)
        idx = jnp.min(jnp.where(work == m, lane, LANES), axis=1, keepdims=True)
        picks.append(idx)
        vals.append(m)
        work = jnp.where(lane == idx, -jnp.inf, work)
    exps = [jnp.exp(v - vals[0]) for v in vals]
    den = exps[0]
    for e in exps[1:]:
        den = den + e

    onehot = jnp.zeros((tm, LANES), F32)
    for idx in picks:
        onehot = onehot + jnp.where(lane == idx, 1.0, 0.0)
    r_io = lax.broadcasted_iota(I32, (tm, tm), 0)
    c_io = lax.broadcasted_iota(I32, (tm, tm), 1)
    lower = jnp.where(c_io < r_io, 1.0, 0.0).astype(BF16)
    prefix = jnp.dot(lower, onehot.astype(BF16), preferred_element_type=F32)
    total = prefix + base_ref[0:1, :]

    e_out = jnp.zeros((tm, LANES), I32)
    w_out = jnp.zeros((tm, LANES), F32)
    r_out = jnp.zeros((tm, LANES), I32)
    for k in range(TOP_K_EXPERTS):
        rank = jnp.sum(jnp.where(lane == picks[k], total, 0.0), axis=1, keepdims=True)
        e_out = jnp.where(lane == k, picks[k], e_out)
        w_out = jnp.where(lane == k, exps[k] / den, w_out)
        r_out = jnp.where(lane == k, rank.astype(I32), r_out)
    e_ref[...] = e_out
    w_ref[...] = w_out
    r_ref[...] = r_out
    new_base = base_ref[0:1, :] + jnp.sum(onehot, axis=0, keepdims=True)
    base_ref[...] = jnp.broadcast_to(new_base, base_ref.shape)
    cnt_ref[...] = jnp.broadcast_to(new_base, cnt_ref.shape)


def _route(h2, w_router, b_router):
    t, d = h2.shape
    tm = _tile(t, 512)
    wr = jnp.zeros((d, LANES), F32).at[:, :N_EXPERTS].set(w_router)
    br = jnp.full((1, LANES), NEG_BIG, F32).at[0, :N_EXPERTS].set(b_router)
    row = pl.BlockSpec((tm, LANES), lambda i: (i, 0))
    return pl.pallas_call(
        _router_kernel,
        grid=(t // tm,),
        in_specs=[pl.BlockSpec((tm, d), lambda i: (i, 0)),
                  pl.BlockSpec((d, LANES), lambda i: (0, 0)),
                  pl.BlockSpec((1, LANES), lambda i: (0, 0))],
        out_specs=[row, row, row, pl.BlockSpec((8, LANES), lambda i: (0, 0))],
        out_shape=[jax.ShapeDtypeStruct((t, LANES), I32),
                   jax.ShapeDtypeStruct((t, LANES), F32),
                   jax.ShapeDtypeStruct((t, LANES), I32),
                   jax.ShapeDtypeStruct((8, LANES), F32)],
        scratch_shapes=[pltpu.VMEM((8, LANES), F32)],
        compiler_params=_cparams(("arbitrary",)),
        name="router",
    )(h2, wr, br)


def _dispatch_kernel(dest_ref, padlo_ref, padlen_ref, h_ref, xs_ref, zero_ref, sem, zsem):
    tm = h_ref.shape[0]

    def pad_copies(e, act):
        first = padlo_ref[e]
        left = padlen_ref[e]
        end = first + left
        bit = MOE_ROWS // 2
        while bit >= SUBLANES:
            take = left & bit
            off = pl.multiple_of(end - bit, bit)

            @pl.when(take != 0)
            def _(off=off, bit=bit):
                act(pltpu.make_async_copy(zero_ref.at[pl.ds(0, bit)], xs_ref.at[pl.ds(off, bit)], zsem))

            end = end - take
            bit //= 2
        for r in range(SUBLANES - 1):
            @pl.when(r < (left & (SUBLANES - 1)))
            def _(r=r):
                act(pltpu.make_async_copy(zero_ref.at[pl.ds(0, 1)], xs_ref.at[pl.ds(first + r, 1)], zsem))

    @pl.when(pl.program_id(0) == 0)
    def _():
        zero_ref[...] = jnp.zeros_like(zero_ref)

        def start(e, c):
            pad_copies(e, lambda cp: cp.start())
            return c

        lax.fori_loop(0, N_EXPERTS, start, 0)

    def copy(t, k):
        return pltpu.make_async_copy(h_ref.at[pl.ds(t, 1)],
                                     xs_ref.at[pl.ds(dest_ref[0, 0, t * TOP_K_EXPERTS + k], 1)], sem)

    def issue(t, c):
        for k in range(TOP_K_EXPERTS):
            copy(t, k).start()
        return c

    def drain(t, c):
        for k in range(TOP_K_EXPERTS):
            copy(t, k).wait()
        return c

    lax.fori_loop(0, tm, issue, 0)
    lax.fori_loop(0, tm, drain, 0)

    @pl.when(pl.program_id(0) == 0)
    def _():
        def finish(e, c):
            pad_copies(e, lambda cp: cp.wait())
            return c

        lax.fori_loop(0, N_EXPERTS, finish, 0)


def _dispatch(h2, dest, pad_lo, pad_len, n_rows):
    t, d = h2.shape
    tm = _tile(t, 256)
    nt = t // tm
    dest3 = dest.reshape(nt, 1, tm * TOP_K_EXPERTS)
    smem = pl.BlockSpec(memory_space=pltpu.SMEM)
    return pl.pallas_call(
        _dispatch_kernel,
        grid=(nt,),
        in_specs=[pl.BlockSpec((1, 1, tm * TOP_K_EXPERTS), lambda i: (i, 0, 0), memory_space=pltpu.SMEM),
                  smem, smem,
                  pl.BlockSpec((tm, d), lambda i: (i, 0))],
        out_specs=pl.BlockSpec(memory_space=pl.ANY),
        out_shape=jax.ShapeDtypeStruct((n_rows, d), h2.dtype),
        scratch_shapes=[pltpu.VMEM((MOE_ROWS // 2, d), h2.dtype),
                        pltpu.SemaphoreType.DMA(()), pltpu.SemaphoreType.DMA(())],
        compiler_params=pltpu.CompilerParams(dimension_semantics=("arbitrary",),
                                             vmem_limit_bytes=VMEM_LIMIT, has_side_effects=True),
        name="moe_dispatch",
    )(dest3, pad_lo, pad_len, h2)


def _deinterleave_kernel(w_ref, p_ref, g_ref, l_ref):
    p = p_ref[...]
    width = p.shape[0]
    half = width // 2
    for c in range(w_ref.shape[2] // width):
        w = w_ref[0, :, c * width:(c + 1) * width].astype(BF16)
        r = jnp.dot(w, p, preferred_element_type=F32)
        g_ref[0, :, c * half:(c + 1) * half] = r[:, :half].astype(BF16)
        l_ref[0, :, c * half:(c + 1) * half] = r[:, half:].astype(BF16)


def _deinterleave(w1):
    n_e, d, f2 = w1.shape
    width = 2 * LANES
    src = jnp.arange(width)
    dst = jnp.where(src % 2 == 0, src // 2, LANES + src // 2)
    perm = (dst[:, None] == jnp.arange(width)[None, :]).astype(BF16)
    tr = _tile(d, 256)
    out = pl.BlockSpec((1, tr, f2 // 2), lambda e, r: (e, r, 0))
    return pl.pallas_call(
        _deinterleave_kernel,
        grid=(n_e, d // tr),
        in_specs=[pl.BlockSpec((1, tr, f2), lambda e, r: (e, r, 0)),
                  pl.BlockSpec((width, width), lambda e, r: (0, 0))],
        out_specs=[out, out],
        out_shape=[jax.ShapeDtypeStruct((n_e, d, f2 // 2), BF16)] * 2,
        compiler_params=_cparams(("arbitrary", "arbitrary")),
        name="w1_deinterleave",
    )(w1, perm)


def _expert_kernel(be_ref, nu_ref, x_ref, w1g_ref, w1l_ref, b1g_ref, b1l_ref, w2_ref, b2_ref,
                   o_ref, xb_ref, acc_ref):
    i = pl.program_id(0)
    j = pl.program_id(1)
    nf = pl.num_programs(1)
    valid = i < nu_ref[0]

    @pl.when(jnp.logical_and(valid, j == 0))
    def _():
        xb_ref[...] = x_ref[...].astype(BF16)
        acc_ref[...] = jnp.zeros_like(acc_ref)

    @pl.when(valid)
    def _():
        xb = xb_ref[...]
        glu = jnp.dot(xb, w1g_ref[0], preferred_element_type=F32) + b1g_ref[0]
        lin = jnp.dot(xb, w1l_ref[0], preferred_element_type=F32) + b1l_ref[0]
        glu = jnp.minimum(glu, SWIGLU_LIMIT)
        lin = jnp.clip(lin, -SWIGLU_LIMIT, SWIGLU_LIMIT)
        act = glu * jax.nn.sigmoid(SWIGLU_ALPHA * glu) * (lin + 1.0)
        acc_ref[...] += jnp.dot(act.astype(BF16), w2_ref[0], preferred_element_type=F32)

    @pl.when(jnp.logical_and(valid, j == nf - 1))
    def _():
        o_ref[...] = acc_ref[...] + b2_ref[0]


def _experts(xs, block_e, n_used, w1g, w1l, b1g, b1l, w2, b2):
    n_rows, d = xs.shape
    n_e, _, f = w1g.shape
    n_blocks = n_rows // MOE_ROWS
    tf = _tile(f, 1024)
    nf = f // tf

    def blk(i, nu):
        return jnp.minimum(i, nu[0] - 1)

    def ftile(i, j, nu):
        return jnp.where(i < nu[0], j, nf - 1)

    grid_spec = pltpu.PrefetchScalarGridSpec(
        num_scalar_prefetch=2,
        grid=(n_blocks, nf),
        in_specs=[
            pl.BlockSpec((MOE_ROWS, d), lambda i, j, be, nu: (blk(i, nu), 0)),
            pl.BlockSpec((1, d, tf), lambda i, j, be, nu: (be[blk(i, nu)], 0, ftile(i, j, nu))),
            pl.BlockSpec((1, d, tf), lambda i, j, be, nu: (be[blk(i, nu)], 0, ftile(i, j, nu))),
            pl.BlockSpec((1, 1, tf), lambda i, j, be, nu: (be[blk(i, nu)], 0, ftile(i, j, nu))),
            pl.BlockSpec((1, 1, tf), lambda i, j, be, nu: (be[blk(i, nu)], 0, ftile(i, j, nu))),
            pl.BlockSpec((1, tf, d), lambda i, j, be, nu: (be[blk(i, nu)], ftile(i, j, nu), 0)),
            pl.BlockSpec((1, 1, d), lambda i, j, be, nu: (be[blk(i, nu)], 0, 0)),
        ],
        out_specs=pl.BlockSpec((MOE_ROWS, d), lambda i, j, be, nu: (blk(i, nu), 0)),
        scratch_shapes=[pltpu.VMEM((MOE_ROWS, d), BF16), pltpu.VMEM((MOE_ROWS, d), F32)],
    )
    return pl.pallas_call(
        _expert_kernel,
        grid_spec=grid_spec,
        out_shape=jax.ShapeDtypeStruct((n_rows, d), F32),
        compiler_params=_cparams(("arbitrary", "arbitrary")),
        name="moe_experts",
    )(block_e, n_used, xs, w1g, w1l, b1g, b1l, w2, b2)


def _combine_kernel(dest_ref, tw_ref, x_ref, gpost_ref, ga_ref, ys_ref, o_ref, buf_ref, sem):
    tm = x_ref.shape[0]

    def copy(t, k):
        return pltpu.make_async_copy(ys_ref.at[pl.ds(dest_ref[0, 0, t * TOP_K_EXPERTS + k], 1)],
                                     buf_ref.at[k, pl.ds(t, 1)], sem)

    def issue(t, c):
        for k in range(TOP_K_EXPERTS):
            copy(t, k).start()
        return c

    def drain(t, c):
        for k in range(TOP_K_EXPERTS):
            copy(t, k).wait()
        return c

    lax.fori_loop(0, tm, issue, 0)
    lax.fori_loop(0, tm, drain, 0)

    tw = tw_ref[...]
    f = tw[:, 0:1] * buf_ref[0]
    for k in range(1, TOP_K_EXPERTS):
        f = f + tw[:, k:k + 1] * buf_ref[k]
    y = f * lax.rsqrt(jnp.mean(f * f, axis=-1, keepdims=True) + RMS_EPS)
    o_ref[...] = x_ref[...] + ga_ref[0] * (y * gpost_ref[...])


def _combine(ys, dest, top_w, x1, g_post, gate, seq):
    t, d = x1.shape
    bsz = gate.shape[0]
    tm = _tile(seq, 256)
    nt = t // tm
    per_b = seq // tm
    dest3 = dest.reshape(nt, 1, tm * TOP_K_EXPERTS)
    return pl.pallas_call(
        _combine_kernel,
        grid=(nt,),
        in_specs=[pl.BlockSpec((1, 1, tm * TOP_K_EXPERTS), lambda i: (i, 0, 0), memory_space=pltpu.SMEM),
                  pl.BlockSpec((tm, LANES), lambda i: (i, 0)),
                  pl.BlockSpec((tm, d), lambda i: (i, 0)),
                  pl.BlockSpec((1, d), lambda i: (0, 0)),
                  pl.BlockSpec((1, 1, d), lambda i: (i // per_b, 0, 0)),
                  pl.BlockSpec(memory_space=pl.ANY)],
        out_specs=pl.BlockSpec((tm, d), lambda i: (i, 0)),
        out_shape=jax.ShapeDtypeStruct((t, d), F32),
        scratch_shapes=[pltpu.VMEM((TOP_K_EXPERTS, tm, d), F32), pltpu.SemaphoreType.DMA(())],
        compiler_params=_cparams(("arbitrary",)),
        name="moe_combine",
    )(dest3, top_w, x1, g_post.reshape(1, d), gate.reshape(bsz, 1, d), ys)


def _moe(h2, x1, w_router, b_router, w1, b1, w2, b2, g_post, gate, seq):
    t, d = h2.shape
    top_e, top_w, rank, counts = _route(h2, w_router, b_router)
    top_e = top_e[:, :TOP_K_EXPERTS]
    rank = rank[:, :TOP_K_EXPERTS]
    counts = counts[0, :N_EXPERTS].astype(I32)

    n_blocks = (t * TOP_K_EXPERTS) // MOE_ROWS + N_EXPERTS
    padded = (counts + MOE_ROWS - 1) // MOE_ROWS * MOE_ROWS
    pad_end = jnp.cumsum(padded)
    pad_start = pad_end - padded
    dest = (pad_start[top_e] + rank).astype(I32)
    block_row = jnp.arange(n_blocks, dtype=I32) * MOE_ROWS
    block_e = jnp.minimum(jnp.sum((pad_end[None, :] <= block_row[:, None]).astype(I32), axis=1),
                          N_EXPERTS - 1).astype(I32)
    n_used = (pad_end[-1:] // MOE_ROWS).astype(I32)

    xs = _dispatch(h2, dest, (pad_start + counts).astype(I32), (padded - counts).astype(I32),
                   n_blocks * MOE_ROWS)
    w1g, w1l = _deinterleave(w1)
    n_e, f2 = b1.shape
    b1g = b1[:, 0::2].reshape(n_e, 1, f2 // 2)
    b1l = b1[:, 1::2].reshape(n_e, 1, f2 // 2)
    ys = _experts(xs, block_e, n_used, w1g, w1l, b1g, b1l, w2.astype(BF16), b2.reshape(n_e, 1, d))
    return _combine(ys, dest, top_w, x1, g_post, gate, seq)


def _layer(x2, c, tables, bsz, seq, w_ada, b_ada, g_pre_mix, g_post_mix, w_in, w_branch_a,
           w_branch_b, w_out, g_pre_ffn, g_post_ffn, w_router, b_router, w1, b1, w2, b2):
    d = x2.shape[1]
    mod = _modulation(c, w_ada, b_ada)
    sh1, sc1, ga1, sh2, sc2, ga2 = [mod[:, k * d:(k + 1) * d] for k in range(6)]
    cos_a, sin_a, cos_i, sin_i = tables

    h = _prenorm(x2, g_pre_mix, sc1, sh1, seq, BF16)
    wb = w_in.astype(BF16)
    o = 0
    w_qka = wb[:, o:o + A_Q_W + A_KV_W]
    o += A_Q_W + A_KV_W
    w_plain = wb[:, o:o + A_KV_W + 3 * B_W]
    o += A_KV_W + 3 * B_W
    w_qi = wb[:, o:o + IDX_Q_W]
    o += IDX_Q_W
    w_kw = jnp.zeros((d, LANES), BF16).at[:, :IDX_DIM + IDX_HEADS].set(wb[:, o:o + IDX_DIM + IDX_HEADS])
    o += IDX_DIM + IDX_HEADS
    w_gate = wb[:, o:o + 2 * d]

    z_a = _project(h, w_qka, "rope_a", BF16, (cos_a, sin_a), tn_pref=640)
    z_p = _project(h, w_plain, "plain", BF16, tn_pref=1664)
    z_i = _project(h, w_qi, "rope_idx", BF16, (cos_i, sin_i))
    z_k = _project(h, w_kw, "rope_key_idx", F32, (cos_i, sin_i))
    z_g = _project(h, w_gate, "sigmoid", BF16, tn_pref=1024)

    o_a = _dsa_mixer(z_a, z_p, z_i, z_k, bsz, seq)
    o_b = _stick_mixer(z_p, bsz, seq)
    x1, h2 = _merge(o_a, o_b, z_g, x2, w_branch_a.astype(BF16), w_branch_b.astype(BF16),
                    w_out.astype(BF16), g_post_mix, ga1, g_pre_ffn, sc2, sh2, seq)

    return _moe(h2, x1, w_router, b_router, w1, b1, w2, b2, g_post_ffn, ga2, seq)


def kernel(x, c, positions, w_ada, b_ada, g_pre_mix, g_post_mix, w_in, w_branch_a, w_branch_b, w_out, g_pre_ffn, g_post_ffn, w_router, b_router, w1, b1, w2, b2):
    bsz, seq, d = x.shape
    x2 = x.reshape(bsz * seq, d)
    tables = _rope_tables(positions)
    for l in range(w_ada.shape[0]):
        x2 = _layer(x2, c, tables, bsz, seq, w_ada[l], b_ada[l], g_pre_mix[l], g_post_mix[l], w_in[l],
                    w_branch_a[l], w_branch_b[l], w_out[l], g_pre_ffn[l], g_post_ffn[l],
                    w_router[l], b_router[l], w1[l], b1[l], w2[l], b2[l])
    return x2.reshape(bsz, seq, d)
```

```python
import functools

import jax
import jax.numpy as jnp
from jax import lax
from jax.experimental import pallas as pl
from jax.experimental.pallas import tpu as pltpu

F32 = jnp.float32
BF16 = jnp.bfloat16
I32 = jnp.int32

CHUNK = 64
RMS_EPS = 1e-6
ROPE_THETA = 500000.0
A_HEADS = 8
A_KV_HEADS = 2
A_HEAD_DIM = 128
A_ROT_DIM = A_HEAD_DIM // 4
IDX_HEADS = 16
IDX_DIM = 64
IDX_ROT_DIM = IDX_DIM // 4
TOPK_MAX = 256
B_HEADS = 8
B_HEAD_DIM = 128
N_EXPERTS = 32
TOP_K_EXPERTS = 4
SWIGLU_LIMIT = 7.0
SWIGLU_ALPHA = 1.702

A_Q_W = A_HEADS * A_HEAD_DIM
A_KV_W = A_KV_HEADS * A_HEAD_DIM
B_W = B_HEADS * B_HEAD_DIM
IDX_Q_W = IDX_HEADS * IDX_DIM

LANES = 128
SUBLANES = 8
VMEM_LIMIT = 56 * 1024 * 1024
NEG_BIG = -1e30
INT_MAX = 2 ** 31 - 1
LOG2_E = 1.4426950408889634
F32_MAX = 3.4028234663852886e38
BISECT_CAP = 320
STICK_DEAD = -106.0
MOE_ROWS = 512


def _cparams(sem):
    return pltpu.CompilerParams(dimension_semantics=sem, vmem_limit_bytes=VMEM_LIMIT)


def _tile(n, pref):
    t = min(n, pref)
    assert n % t == 0, (n, t)
    return t


def _mod_kernel(c_ref, w_ref, b_ref, o_ref):
    c = c_ref[...]
    s = c * jax.nn.sigmoid(c)
    o_ref[...] = jnp.dot(s, w_ref[...], preferred_element_type=F32,
                         precision=lax.Precision.HIGHEST) + b_ref[...]


def _modulation(c, w_ada, b_ada):
    bsz, d = c.shape
    n = w_ada.shape[1]
    rows = 8
    c_pad = jnp.zeros((rows, d), F32).at[:bsz].set(c)
    tn = _tile(n, 1024)
    out = pl.pallas_call(
        _mod_kernel,
        grid=(n // tn,),
        in_specs=[pl.BlockSpec((rows, d), lambda j: (0, 0)),
                  pl.BlockSpec((d, tn), lambda j: (0, j)),
                  pl.BlockSpec((1, tn), lambda j: (0, j))],
        out_specs=pl.BlockSpec((rows, tn), lambda j: (0, j)),
        out_shape=jax.ShapeDtypeStruct((rows, n), F32),
        compiler_params=_cparams(("arbitrary",)),
        name="modulation",
    )(c_pad, w_ada, b_ada.reshape(1, n))
    return out[:bsz]


def _rope_table_kernel(pos_ref, fa_ref, sa_ref, fi_ref, si_ref, ca_o, sa_o, ci_o, si_o):
    pos = pos_ref[...].astype(F32)
    ang_a = pos * fa_ref[...]
    ca_o[...] = jnp.cos(ang_a)
    sa_o[...] = jnp.sin(ang_a) * sa_ref[...]
    ang_i = pos * fi_ref[...]
    ci_o[...] = jnp.cos(ang_i)
    si_o[...] = jnp.sin(ang_i) * si_ref[...]


def _rope_pattern(rot_dim, head_dim):
    half = rot_dim // 2
    inv_freq = ROPE_THETA ** (-jnp.arange(half, dtype=F32) / half)
    freq = jnp.concatenate([inv_freq, inv_freq, jnp.zeros((head_dim - rot_dim,), F32)])
    sign = jnp.concatenate([-jnp.ones((half,), F32), jnp.ones((half,), F32),
                            jnp.zeros((head_dim - rot_dim,), F32)])
    reps = LANES // head_dim
    return jnp.tile(freq, reps).reshape(1, LANES), jnp.tile(sign, reps).reshape(1, LANES)


def _rope_tables(positions):
    t = positions.size
    pos = positions.reshape(t, 1).astype(I32)
    fa, sa = _rope_pattern(A_ROT_DIM, A_HEAD_DIM)
    fi, si = _rope_pattern(IDX_ROT_DIM, IDX_DIM)
    tm = _tile(t, 1024)
    pat = pl.BlockSpec((1, LANES), lambda i: (0, 0))
    tab = pl.BlockSpec((tm, LANES), lambda i: (i, 0))
    return pl.pallas_call(
        _rope_table_kernel,
        grid=(t // tm,),
        in_specs=[pl.BlockSpec((tm, 1), lambda i: (i, 0)), pat, pat, pat, pat],
        out_specs=[tab, tab, tab, tab],
        out_shape=[jax.ShapeDtypeStruct((t, LANES), F32)] * 4,
        compiler_params=_cparams(("arbitrary",)),
        name="rope_tables",
    )(pos, fa, sa, fi, si)


def _prenorm_kernel(x_ref, g_ref, sc_ref, sh_ref, o_ref):
    x = x_ref[...]
    y = x * lax.rsqrt(jnp.mean(x * x, axis=-1, keepdims=True) + RMS_EPS)
    h = (y * g_ref[...]) * (1.0 + sc_ref[0]) + sh_ref[0]
    o_ref[...] = h.astype(o_ref.dtype)


def _prenorm(x2, g, scale, shift, seq, out_dtype):
    t, d = x2.shape
    bsz = scale.shape[0]
    tm = _tile(seq, 512)
    per_b = seq // tm
    row = pl.BlockSpec((tm, d), lambda i: (i, 0))
    bvec = pl.BlockSpec((1, 1, d), lambda i: (i // per_b, 0, 0))
    return pl.pallas_call(
        _prenorm_kernel,
        grid=(t // tm,),
        in_specs=[row, pl.BlockSpec((1, d), lambda i: (0, 0)), bvec, bvec],
        out_specs=row,
        out_shape=jax.ShapeDtypeStruct((t, d), out_dtype),
        compiler_params=_cparams(("arbitrary",)),
        name="prenorm",
    )(x2, g.reshape(1, d), scale.reshape(bsz, 1, d), shift.reshape(bsz, 1, d))


def _rotate(x, cos, sin, period, half):
    lane = lax.broadcasted_iota(I32, x.shape, 1) & (period - 1)
    swapped = jnp.where(lane < half, pltpu.roll(x, LANES - half, 1), pltpu.roll(x, half, 1))
    return x * cos + swapped * sin


def _proj_kernel(*refs, epilogue):
    if epilogue in ("plain", "sigmoid"):
        h_ref, w_ref, o_ref = refs
    else:
        h_ref, w_ref, cos_ref, sin_ref, o_ref = refs
    acc = jnp.dot(h_ref[...], w_ref[...], preferred_element_type=F32)
    if epilogue == "plain":
        o_ref[...] = acc.astype(o_ref.dtype)
    elif epilogue == "sigmoid":
        o_ref[...] = jax.nn.sigmoid(acc).astype(o_ref.dtype)
    else:
        cos = cos_ref[...]
        sin = sin_ref[...]
        if epilogue == "rope_a":
            period, half = A_HEAD_DIM, A_ROT_DIM // 2
        else:
            period, half = IDX_DIM, IDX_ROT_DIM // 2
        if epilogue == "rope_key_idx":
            lane = lax.broadcasted_iota(I32, cos.shape, 1)
            cos = jnp.where(lane < IDX_DIM, cos, 1.0)
            sin = jnp.where(lane < IDX_DIM, sin, 0.0)
        for c in range(acc.shape[1] // LANES):
            sl = slice(c * LANES, (c + 1) * LANES)
            o_ref[:, sl] = _rotate(acc[:, sl], cos, sin, period, half).astype(o_ref.dtype)


def _project(h, w, epilogue, out_dtype, tables=None, tn_pref=512):
    t, d = h.shape
    n = w.shape[1]
    tm = _tile(t, 1024)
    tn = n
    for cand in (tn_pref, 256, 128):
        if n % cand == 0:
            tn = cand
            break
    in_specs = [pl.BlockSpec((tm, d), lambda i, j: (i, 0)),
                pl.BlockSpec((d, tn), lambda i, j: (0, j))]
    args = [h, w]
    if tables is not None:
        tab = pl.BlockSpec((tm, LANES), lambda i, j: (i, 0))
        in_specs += [tab, tab]
        args += list(tables)
    return pl.pallas_call(
        functools.partial(_proj_kernel, epilogue=epilogue),
        grid=(t // tm, n // tn),
        in_specs=in_specs,
        out_specs=pl.BlockSpec((tm, tn), lambda i, j: (i, j)),
        out_shape=jax.ShapeDtypeStruct((t, n), out_dtype),
        compiler_params=_cparams(("arbitrary", "arbitrary")),
        name="proj_" + epilogue,
    )(*args)


def _dsa_kernel(qi_ref, kwq_ref, kwk_ref, qa_ref, ka_ref, va_ref, o_ref,
                keys_ref, kib_ref, *, tq, tk, topk):
    i = pl.program_id(1)
    t0 = i * tq
    n_kt = (t0 + tq + tk - 1) // tk
    grp = A_HEADS // A_KV_HEADS
    n_cut_steps = (kwk_ref.shape[0] - 1).bit_length() + 1

    @pl.when(i == 0)
    def _():
        kib_ref[...] = kwk_ref[:, :IDX_DIM].astype(BF16)

    qi = qi_ref[...]
    wq = kwq_ref[...]
    row = t0 + lax.broadcasted_iota(I32, (tq, 1), 0)
    limit = (row // CHUNK + 1) * CHUNK

    def score_tile(kt, carry):
        hi_acc, lo_acc = carry
        k0 = pl.multiple_of(kt * tk, tk)
        kb = kib_ref[pl.ds(k0, tk), :]
        acc = jnp.zeros((tq, tk), F32)
        for h in range(IDX_HEADS):
            d = lax.dot_general(qi[:, h * IDX_DIM:(h + 1) * IDX_DIM], kb,
                                (((1,), (1,)), ((), ())), preferred_element_type=F32)
            acc = acc + jnp.maximum(d, 0.0) * wq[:, IDX_DIM + h:IDX_DIM + h + 1]
        adm = (k0 + lax.broadcasted_iota(I32, (tq, tk), 1)) < limit
        s_hi = jnp.where(adm, acc, -jnp.inf)
        s_lo = jnp.where(adm, acc, jnp.inf)
        keys_ref[kt] = s_hi
        for c in range(tk // LANES):
            hi_acc = jnp.maximum(hi_acc, s_hi[:, c * LANES:(c + 1) * LANES])
            lo_acc = jnp.minimum(lo_acc, s_lo[:, c * LANES:(c + 1) * LANES])
        return hi_acc, lo_acc

    hi_acc, lo_acc = lax.fori_loop(
        0, n_kt, score_tile,
        (jnp.full((tq, LANES), -jnp.inf, F32), jnp.full((tq, LANES), jnp.inf, F32)))
    row_max = jnp.max(hi_acc, axis=1, keepdims=True)
    row_min = jnp.min(lo_acc, axis=1, keepdims=True)

    def count_where(pred):
        def count_tile(kt, cnt):
            hit = pred(keys_ref[kt], kt).astype(I32)
            for c in range(tk // LANES):
                cnt = cnt + hit[:, c * LANES:(c + 1) * LANES]
            return cnt

        cnt = lax.fori_loop(0, n_kt, count_tile, jnp.zeros((tq, LANES), I32))
        return jnp.sum(cnt, axis=1, keepdims=True)

    def bisect_cond(state):
        it, pending = state[0], state[1]
        return jnp.logical_and(it < BISECT_CAP, pending > 0)

    def bisect_step(state):
        it, _, lo, hi, c_lo, tau, surplus, todo = state
        mid = 0.5 * lo + 0.5 * hi
        stuck = jnp.logical_or(mid <= lo, mid >= hi)
        cnt = count_where(lambda s, kt: s >= mid)
        fin = jnp.logical_or(stuck, cnt == topk)
        done_now = jnp.logical_and(todo > 0, fin)
        tau = jnp.where(done_now, jnp.where(stuck, lo, mid), tau)
        surplus = jnp.where(done_now, jnp.where(stuck, c_lo - topk, 0), surplus)
        up = cnt >= topk
        lo = jnp.where(up, mid, lo)
        c_lo = jnp.where(up, cnt, c_lo)
        hi = jnp.where(up, hi, mid)
        todo = jnp.where(fin, 0, todo)
        return it + 1, jnp.max(todo), lo, hi, c_lo, tau, surplus, todo

    todo0 = jnp.where(limit > topk, 1, 0).astype(I32)
    hi0 = row_max + jnp.maximum(jnp.abs(row_max) * 1e-6, 1e-30)
    state = (jnp.int32(0), jnp.max(todo0), row_min, hi0, limit,
             jnp.full((tq, 1), -F32_MAX, F32), jnp.zeros((tq, 1), I32), todo0)
    _, _, lo, _, c_lo, tau, surplus, todo = lax.while_loop(bisect_cond, bisect_step, state)
    tau = jnp.where(todo > 0, lo, tau)
    surplus = jnp.where(todo > 0, c_lo - topk, surplus)

    def key_pos(kt):
        return kt * tk + lax.broadcasted_iota(I32, (tq, tk), 1)

    @pl.when(jnp.max(surplus) > 0)
    def _():
        need = topk - count_where(lambda s, kt: s > tau)

        def cut_step(_, st):
            lo_c, hi_c = st
            mid = lax.shift_right_logical(lo_c + hi_c, 1)
            kept = count_where(lambda s, kt: jnp.logical_and(s == tau, key_pos(kt) < mid))
            ok = kept >= need
            return jnp.where(ok, lo_c, mid), jnp.where(ok, mid, hi_c)

        _, hi_c = lax.fori_loop(0, n_cut_steps, cut_step, (jnp.zeros((tq, 1), I32), limit))
        cutoff = jnp.where(surplus > 0, hi_c, INT_MAX)

        def strike(kt, c):
            s = keys_ref[kt]
            drop = jnp.logical_and(s == tau, key_pos(kt) >= cutoff)
            keys_ref[kt] = jnp.where(drop, -jnp.inf, s)
            return c

        lax.fori_loop(0, n_kt, strike, 0)

    qscale = A_HEAD_DIM ** -0.5 * LOG2_E
    qs = [jnp.concatenate(
        [(qa_ref[:, (j * grp + g) * A_HEAD_DIM:(j * grp + g + 1) * A_HEAD_DIM].astype(F32)
          * qscale).astype(BF16) for g in range(grp)], axis=0)
        for j in range(A_KV_HEADS)]

    def attn_tile(kt, carry):
        k0 = pl.multiple_of(kt * tk, tk)
        bias = jnp.where(keys_ref[kt] >= tau, 0.0, NEG_BIG)
        out = []
        for j in range(A_KV_HEADS):
            m, l, acc = carry[3 * j:3 * j + 3]
            kk = ka_ref[pl.ds(k0, tk), j * A_HEAD_DIM:(j + 1) * A_HEAD_DIM]
            vv = va_ref[pl.ds(k0, tk), j * A_HEAD_DIM:(j + 1) * A_HEAD_DIM]
            s = lax.dot_general(qs[j], kk, (((1,), (1,)), ((), ())),
                                preferred_element_type=F32)
            s = (s.reshape(grp, tq, tk) + bias[None]).reshape(grp * tq, tk)
            m_new = jnp.maximum(m, jnp.max(s, axis=1, keepdims=True))
            alpha = jnp.exp2(m - m_new)
            p = jnp.exp2(s - m_new)
            l = alpha * l + jnp.sum(p, axis=1, keepdims=True)
            acc = alpha * acc + jnp.dot(p.astype(BF16), vv, preferred_element_type=F32)
            out += [m_new, l, acc]
        return tuple(out)

    init = []
    for j in range(A_KV_HEADS):
        init += [jnp.full((grp * tq, 1), NEG_BIG, F32), jnp.zeros((grp * tq, 1), F32),
                 jnp.zeros((grp * tq, A_HEAD_DIM), F32)]
    fin = lax.fori_loop(0, n_kt, attn_tile, tuple(init))
    for j in range(A_KV_HEADS):
        out = fin[3 * j + 2] / fin[3 * j + 1]
        for g in range(grp):
            h = j * grp + g
            o_ref[:, h * A_HEAD_DIM:(h + 1) * A_HEAD_DIM] = out[g * tq:(g + 1) * tq].astype(o_ref.dtype)


def _dsa_mixer(z_a, z_p, z_i, z_k, bsz, seq):
    t = bsz * seq
    tq = _tile(seq, 256)
    tk = _tile(seq, 1024)
    nq = seq // tq
    topk = min(TOPK_MAX, seq // 4)
    return pl.pallas_call(
        functools.partial(_dsa_kernel, tq=tq, tk=tk, topk=topk),
        grid=(bsz, nq),
        in_specs=[
            pl.BlockSpec((tq, IDX_Q_W), lambda b, i: (b * nq + i, 0)),
            pl.BlockSpec((tq, LANES), lambda b, i: (b * nq + i, 0)),
            pl.BlockSpec((seq, LANES), lambda b, i: (b, 0)),
            pl.BlockSpec((tq, A_Q_W), lambda b, i: (b * nq + i, 0)),
            pl.BlockSpec((seq, A_KV_W), lambda b, i: (b, A_Q_W // A_KV_W)),
            pl.BlockSpec((seq, A_KV_W), lambda b, i: (b, 0)),
        ],
        out_specs=pl.BlockSpec((tq, A_Q_W), lambda b, i: (b * nq + i, 0)),
        out_shape=jax.ShapeDtypeStruct((t, A_Q_W), BF16),
        scratch_shapes=[pltpu.VMEM((seq // tk, tq, tk), F32),
                        pltpu.VMEM((seq, IDX_DIM), BF16)],
        compiler_params=_cparams(("arbitrary", "arbitrary")),
        name="dsa_mixer",
    )(z_i, z_k, z_k, z_a, z_a, z_p)


def _stick_kernel(q_ref, k_ref, v_ref, o_ref, *, tq, tk, heads):
    i = pl.program_id(2)
    d_idx = (i * tq) // tk
    hd = B_HEAD_DIM
    scale = hd ** -0.5
    qs = [(q_ref[:, h * hd:(h + 1) * hd].astype(F32) * scale).astype(BF16) for h in range(heads)]
    r_io = lax.broadcasted_iota(I32, (tk, tk), 0)
    c_io = lax.broadcasted_iota(I32, (tk, tk), 1)
    upper = jnp.where(r_io > c_io, 1.0, 0.0).astype(BF16)
    diagonal = (d_idx * tk + lax.broadcasted_iota(I32, (tq, tk), 1)
                < i * tq + lax.broadcasted_iota(I32, (tq, tk), 0))

    def scores(kt):
        k0 = pl.multiple_of(kt * tk, tk)
        return tuple(lax.dot_general(qs[h], k_ref[pl.ds(k0, tk), h * hd:(h + 1) * hd],
                                     (((1,), (1,)), ((), ())), preferred_element_type=F32)
                     for h in range(heads))

    def front(z, causal):
        soft = jnp.log(1.0 + jnp.exp(-jnp.abs(z)))
        log_beta = jnp.minimum(z, 0.0) - soft
        log_keep = log_beta - z
        if causal is not None:
            log_keep = jnp.where(causal, log_keep, 0.0)
        hi = log_keep.astype(BF16)
        lo = (log_keep - hi.astype(F32)).astype(BF16)
        gap = (jnp.dot(hi, upper, preferred_element_type=F32)
               + jnp.dot(lo, upper, preferred_element_type=F32))
        logit = log_beta + gap
        if causal is not None:
            logit = jnp.where(causal, logit, NEG_BIG)
        return logit, jnp.sum(log_keep, axis=1, keepdims=True)

    def back(kt, pend, carry):
        k0 = pl.multiple_of(kt * tk, tk)
        out = []
        for h in range(heads):
            logit, keep_sum = pend[2 * h], pend[2 * h + 1]
            run, acc = carry[2 * h], carry[2 * h + 1]
            attn = jnp.exp(logit + run).astype(BF16)
            vv = v_ref[pl.ds(k0, tk), h * hd:(h + 1) * hd]
            out += [run + keep_sum, acc + jnp.dot(attn, vv, preferred_element_type=F32)]
        return tuple(out)

    def fronts(zs, causal):
        out = []
        for h in range(heads):
            out.extend(front(zs[h], causal))
        return tuple(out)

    def step(state):
        n, _, zs, pend, carry = state
        kt = d_idx - 1 - n
        zs_next = scores(jnp.maximum(kt - 1, 0))
        pend_next = fronts(zs, None)
        carry = back(kt + 1, pend, carry)
        return n + 1, run_below(carry, pend_next), zs_next, pend_next, carry

    def run_below(carry, pend):
        top = carry[0] + pend[1]
        for h in range(1, heads):
            top = jnp.maximum(top, carry[2 * h] + pend[2 * h + 1])
        return jnp.max(top)

    def alive(state):
        return jnp.logical_and(state[0] < d_idx, state[1] > STICK_DEAD)

    init = []
    for h in range(heads):
        init += [jnp.zeros((tq, 1), F32), jnp.zeros((tq, hd), F32)]
    init = tuple(init)
    pend = fronts(scores(d_idx), diagonal)
    zs = scores(jnp.maximum(d_idx - 1, 0))
    n_done, _, _, pend, carry = lax.while_loop(
        alive, step, (jnp.int32(0), run_below(init, pend), zs, pend, init))
    carry = back(d_idx - n_done, pend, carry)
    for h in range(heads):
        o_ref[:, h * hd:(h + 1) * hd] = carry[2 * h + 1].astype(o_ref.dtype)


def _stick_mixer(z_p, bsz, seq):
    t = bsz * seq
    tq = _tile(seq, 256)
    tk = _tile(seq, 256)
    nq = seq // tq
    heads = 2
    width = heads * B_HEAD_DIM
    assert A_KV_W % width == 0 and B_W % width == 0
    q_off = A_KV_W // width
    k_off = q_off + B_W // width
    v_off = k_off + B_W // width
    return pl.pallas_call(
        functools.partial(_stick_kernel, tq=tq, tk=tk, heads=heads),
        grid=(bsz, B_HEADS // heads, nq),
        in_specs=[
            pl.BlockSpec((tq, width), lambda b, h, i: (b * nq + i, q_off + h)),
            pl.BlockSpec((seq, width), lambda b, h, i: (b, k_off + h)),
            pl.BlockSpec((seq, width), lambda b, h, i: (b, v_off + h)),
        ],
        out_specs=pl.BlockSpec((tq, width), lambda b, h, i: (b * nq + i, h)),
        out_shape=jax.ShapeDtypeStruct((t, B_W), BF16),
        compiler_params=_cparams(("arbitrary", "arbitrary", "arbitrary")),
        name="stick_mixer",
    )(z_p, z_p, z_p)


def _merge_kernel(oa_ref, ob_ref, g_ref, x_ref, wa_ref, wb_ref, wo_ref, gpost_ref, ga_ref,
                  gpre_ref, sc_ref, sh_ref, o_ref, h_ref):
    d = x_ref.shape[1]
    a = jnp.dot(oa_ref[...], wa_ref[...], preferred_element_type=F32)
    b = jnp.dot(ob_ref[...], wb_ref[...], preferred_element_type=F32)
    merged = g_ref[:, :d].astype(F32) * a + g_ref[:, d:].astype(F32) * b
    mix = jnp.dot(merged.astype(BF16), wo_ref[...], preferred_element_type=F32)
    y = mix * lax.rsqrt(jnp.mean(mix * mix, axis=-1, keepdims=True) + RMS_EPS)
    x1 = x_ref[...] + ga_ref[0] * (y * gpost_ref[...])
    o_ref[...] = x1
    y2 = x1 * lax.rsqrt(jnp.mean(x1 * x1, axis=-1, keepdims=True) + RMS_EPS)
    h_ref[...] = (y2 * gpre_ref[...]) * (1.0 + sc_ref[0]) + sh_ref[0]


def _merge(o_a, o_b, z_g, x2, w_a, w_b, w_o, g_post, gate, g_pre, scale, shift, seq):
    t, d = x2.shape
    bsz = gate.shape[0]
    tm = _tile(seq, 256)
    per_b = seq // tm
    const = lambda i: (0, 0)
    row = pl.BlockSpec((tm, d), lambda i: (i, 0))
    bvec = pl.BlockSpec((1, 1, d), lambda i: (i // per_b, 0, 0))
    return pl.pallas_call(
        _merge_kernel,
        grid=(t // tm,),
        in_specs=[
            pl.BlockSpec((tm, A_Q_W), lambda i: (i, 0)),
            pl.BlockSpec((tm, B_W), lambda i: (i, 0)),
            pl.BlockSpec((tm, 2 * d), lambda i: (i, 0)),
            row,
            pl.BlockSpec((A_Q_W, d), const),
            pl.BlockSpec((B_W, d), const),
            pl.BlockSpec((d, d), const),
            pl.BlockSpec((1, d), const),
            bvec,
            pl.BlockSpec((1, d), const),
            bvec,
            bvec,
        ],
        out_specs=[row, row],
        out_shape=[jax.ShapeDtypeStruct((t, d), F32)] * 2,
        compiler_params=_cparams(("arbitrary",)),
        name="merge",
    )(o_a, o_b, z_g, x2, w_a, w_b, w_o, g_post.reshape(1, d), gate.reshape(bsz, 1, d),
      g_pre.reshape(1, d), scale.reshape(bsz, 1, d), shift.reshape(bsz, 1, d))


def _router_kernel(h_ref, wr_ref, br_ref, e_ref, w_ref, r_ref, cnt_ref, base_ref):
    i = pl.program_id(0)
    tm = h_ref.shape[0]

    @pl.when(i == 0)
    def _():
        base_ref[...] = jnp.zeros_like(base_ref)

    logits = jnp.dot(h_ref[...], wr_ref[...], preferred_element_type=F32,
                     precision=lax.Precision.HIGHEST) + br_ref[...]
    lane = lax.broadcasted_iota(I32, (tm, LANES), 1)
    work = logits
    picks, vals = [], []
    for _ in range(TOP_K_EXPERTS):
        m = jnp.max(work, axis=1, keepdims=True)
        idx = jnp.min(jnp.where(work == m, lane, LANES), axis=1, keepdims=True)
        picks.append(idx)
        vals.append(m)
        work = jnp.where(lane == idx, -jnp.inf, work)
    exps = [jnp.exp(v - vals[0]) for v in vals]
    den = exps[0]
    for e in exps[1:]:
        den = den + e

    onehot = jnp.zeros((tm, LANES), F32)
    for idx in picks:
        onehot = onehot + jnp.where(lane == idx, 1.0, 0.0)
    r_io = lax.broadcasted_iota(I32, (tm, tm), 0)
    c_io = lax.broadcasted_iota(I32, (tm, tm), 1)
    lower = jnp.where(c_io < r_io, 1.0, 0.0).astype(BF16)
    prefix = jnp.dot(lower, onehot.astype(BF16), preferred_element_type=F32)
    total = prefix + base_ref[0:1, :]

    e_out = jnp.zeros((tm, LANES), I32)
    w_out = jnp.zeros((tm, LANES), F32)
    r_out = jnp.zeros((tm, LANES), I32)
    for k in range(TOP_K_EXPERTS):
        rank = jnp.sum(jnp.where(lane == picks[k], total, 0.0), axis=1, keepdims=True)
        e_out = jnp.where(lane == k, picks[k], e_out)
        w_out = jnp.where(lane == k, exps[k] / den, w_out)
        r_out = jnp.where(lane == k, rank.astype(I32), r_out)
    e_ref[...] = e_out
    w_ref[...] = w_out
    r_ref[...] = r_out
    new_base = base_ref[0:1, :] + jnp.sum(onehot, axis=0, keepdims=True)
    base_ref[...] = jnp.broadcast_to(new_base, base_ref.shape)
    cnt_ref[...] = jnp.broadcast_to(new_base, cnt_ref.shape)


def _route(h2, w_router, b_router):
    t, d = h2.shape
    tm = _tile(t, 512)
    wr = jnp.zeros((d, LANES), F32).at[:, :N_EXPERTS].set(w_router)
    br = jnp.full((1, LANES), NEG_BIG, F32).at[0, :N_EXPERTS].set(b_router)
    row = pl.BlockSpec((tm, LANES), lambda i: (i, 0))
    return pl.pallas_call(
        _router_kernel,
        grid=(t // tm,),
        in_specs=[pl.BlockSpec((tm, d), lambda i: (i, 0)),
                  pl.BlockSpec((d, LANES), lambda i: (0, 0)),
                  pl.BlockSpec((1, LANES), lambda i: (0, 0))],
        out_specs=[row, row, row, pl.BlockSpec((8, LANES), lambda i: (0, 0))],
        out_shape=[jax.ShapeDtypeStruct((t, LANES), I32),
                   jax.ShapeDtypeStruct((t, LANES), F32),
                   jax.ShapeDtypeStruct((t, LANES), I32),
                   jax.ShapeDtypeStruct((8, LANES), F32)],
        scratch_shapes=[pltpu.VMEM((8, LANES), F32)],
        compiler_params=_cparams(("arbitrary",)),
        name="router",
    )(h2, wr, br)


def _dispatch_kernel(dest_ref, padlo_ref, padlen_ref, h_ref, xs_ref, zero_ref, sem, zsem):
    tm = h_ref.shape[0]

    def pad_copies(e, act):
        first = padlo_ref[e]
        left = padlen_ref[e]
        end = first + left
        bit = MOE_ROWS // 2
        while bit >= SUBLANES:
            take = left & bit
            off = pl.multiple_of(end - bit, bit)

            @pl.when(take != 0)
            def _(off=off, bit=bit):
                act(pltpu.make_async_copy(zero_ref.at[pl.ds(0, bit)], xs_ref.at[pl.ds(off, bit)], zsem))

            end = end - take
            bit //= 2
        for r in range(SUBLANES - 1):
            @pl.when(r < (left & (SUBLANES - 1)))
            def _(r=r):
                act(pltpu.make_async_copy(zero_ref.at[pl.ds(0, 1)], xs_ref.at[pl.ds(first + r, 1)], zsem))

    @pl.when(pl.program_id(0) == 0)
    def _():
        zero_ref[...] = jnp.zeros_like(zero_ref)

        def start(e, c):
            pad_copies(e, lambda cp: cp.start())
            return c

        lax.fori_loop(0, N_EXPERTS, start, 0)

    def copy(t, k):
        return pltpu.make_async_copy(h_ref.at[pl.ds(t, 1)],
                                     xs_ref.at[pl.ds(dest_ref[0, 0, t * TOP_K_EXPERTS + k], 1)], sem)

    def issue(t, c):
        for k in range(TOP_K_EXPERTS):
            copy(t, k).start()
        return c

    def drain(t, c):
        for k in range(TOP_K_EXPERTS):
            copy(t, k).wait()
        return c

    lax.fori_loop(0, tm, issue, 0)
    lax.fori_loop(0, tm, drain, 0)

    @pl.when(pl.program_id(0) == 0)
    def _():
        def finish(e, c):
            pad_copies(e, lambda cp: cp.wait())
            return c

        lax.fori_loop(0, N_EXPERTS, finish, 0)


def _dispatch(h2, dest, pad_lo, pad_len, n_rows):
    t, d = h2.shape
    tm = _tile(t, 256)
    nt = t // tm
    dest3 = dest.reshape(nt, 1, tm * TOP_K_EXPERTS)
    smem = pl.BlockSpec(memory_space=pltpu.SMEM)
    return pl.pallas_call(
        _dispatch_kernel,
        grid=(nt,),
        in_specs=[pl.BlockSpec((1, 1, tm * TOP_K_EXPERTS), lambda i: (i, 0, 0), memory_space=pltpu.SMEM),
                  smem, smem,
                  pl.BlockSpec((tm, d), lambda i: (i, 0))],
        out_specs=pl.BlockSpec(memory_space=pl.ANY),
        out_shape=jax.ShapeDtypeStruct((n_rows, d), h2.dtype),
        scratch_shapes=[pltpu.VMEM((MOE_ROWS // 2, d), h2.dtype),
                        pltpu.SemaphoreType.DMA(()), pltpu.SemaphoreType.DMA(())],
        compiler_params=pltpu.CompilerParams(dimension_semantics=("arbitrary",),
                                             vmem_limit_bytes=VMEM_LIMIT, has_side_effects=True),
        name="moe_dispatch",
    )(dest3, pad_lo, pad_len, h2)


def _deinterleave_kernel(w_ref, p_ref, g_ref, l_ref):
    p = p_ref[...]
    width = p.shape[0]
    half = width // 2
    for c in range(w_ref.shape[2] // width):
        w = w_ref[0, :, c * width:(c + 1) * width].astype(BF16)
        r = jnp.dot(w, p, preferred_element_type=F32)
        g_ref[0, :, c * half:(c + 1) * half] = r[:, :half].astype(BF16)
        l_ref[0, :, c * half:(c + 1) * half] = r[:, half:].astype(BF16)


def _deinterleave(w1):
    n_e, d, f2 = w1.shape
    width = 2 * LANES
    src = jnp.arange(width)
    dst = jnp.where(src % 2 == 0, src // 2, LANES + src // 2)
    perm = (dst[:, None] == jnp.arange(width)[None, :]).astype(BF16)
    tr = _tile(d, 256)
    out = pl.BlockSpec((1, tr, f2 // 2), lambda e, r: (e, r, 0))
    return pl.pallas_call(
        _deinterleave_kernel,
        grid=(n_e, d // tr),
        in_specs=[pl.BlockSpec((1, tr, f2), lambda e, r: (e, r, 0)),
                  pl.BlockSpec((width, width), lambda e, r: (0, 0))],
        out_specs=[out, out],
        out_shape=[jax.ShapeDtypeStruct((n_e, d, f2 // 2), BF16)] * 2,
        compiler_params=_cparams(("arbitrary", "arbitrary")),
        name="w1_deinterleave",
    )(w1, perm)


def _expert_kernel(be_ref, nu_ref, x_ref, w1g_ref, w1l_ref, b1g_ref, b1l_ref, w2_ref, b2_ref,
                   o_ref, xb_ref, acc_ref):
    i = pl.program_id(0)
    j = pl.program_id(1)
    nf = pl.num_programs(1)
    valid = i < nu_ref[0]

    @pl.when(jnp.logical_and(valid, j == 0))
    def _():
        xb_ref[...] = x_ref[...].astype(BF16)
        acc_ref[...] = jnp.zeros_like(acc_ref)

    @pl.when(valid)
    def _():
        xb = xb_ref[...]
        glu = jnp.dot(xb, w1g_ref[0], preferred_element_type=F32) + b1g_ref[0]
        lin = jnp.dot(xb, w1l_ref[0], preferred_element_type=F32) + b1l_ref[0]
        glu = jnp.minimum(glu, SWIGLU_LIMIT)
        lin = jnp.clip(lin, -SWIGLU_LIMIT, SWIGLU_LIMIT)
        act = glu * jax.nn.sigmoid(SWIGLU_ALPHA * glu) * (lin + 1.0)
        acc_ref[...] += jnp.dot(act.astype(BF16), w2_ref[0], preferred_element_type=F32)

    @pl.when(jnp.logical_and(valid, j == nf - 1))
    def _():
        o_ref[...] = acc_ref[...] + b2_ref[0]


def _experts(xs, block_e, n_used, w1g, w1l, b1g, b1l, w2, b2):
    n_rows, d = xs.shape
    n_e, _, f = w1g.shape
    n_blocks = n_rows // MOE_ROWS
    tf = _tile(f, 1024)
    nf = f // tf

    def blk(i, nu):
        return jnp.minimum(i, nu[0] - 1)

    def ftile(i, j, nu):
        return jnp.where(i < nu[0], j, nf - 1)

    grid_spec = pltpu.PrefetchScalarGridSpec(
        num_scalar_prefetch=2,
        grid=(n_blocks, nf),
        in_specs=[
            pl.BlockSpec((MOE_ROWS, d), lambda i, j, be, nu: (blk(i, nu), 0)),
            pl.BlockSpec((1, d, tf), lambda i, j, be, nu: (be[blk(i, nu)], 0, ftile(i, j, nu))),
            pl.BlockSpec((1, d, tf), lambda i, j, be, nu: (be[blk(i, nu)], 0, ftile(i, j, nu))),
            pl.BlockSpec((1, 1, tf), lambda i, j, be, nu: (be[blk(i, nu)], 0, ftile(i, j, nu))),
            pl.BlockSpec((1, 1, tf), lambda i, j, be, nu: (be[blk(i, nu)], 0, ftile(i, j, nu))),
            pl.BlockSpec((1, tf, d), lambda i, j, be, nu: (be[blk(i, nu)], ftile(i, j, nu), 0)),
            pl.BlockSpec((1, 1, d), lambda i, j, be, nu: (be[blk(i, nu)], 0, 0)),
        ],
        out_specs=pl.BlockSpec((MOE_ROWS, d), lambda i, j, be, nu: (blk(i, nu), 0)),
        scratch_shapes=[pltpu.VMEM((MOE_ROWS, d), BF16), pltpu.VMEM((MOE_ROWS, d), F32)],
    )
    return pl.pallas_call(
        _expert_kernel,
        grid_spec=grid_spec,
        out_shape=jax.ShapeDtypeStruct((n_rows, d), F32),
        compiler_params=_cparams(("arbitrary", "arbitrary")),
        name="moe_experts",
    )(block_e, n_used, xs, w1g, w1l, b1g, b1l, w2, b2)


def _combine_kernel(dest_ref, tw_ref, x_ref, gpost_ref, ga_ref, ys_ref, o_ref, buf_ref, sem):
    tm = x_ref.shape[0]

    def copy(t, k):
        return pltpu.make_async_copy(ys_ref.at[pl.ds(dest_ref[0, 0, t * TOP_K_EXPERTS + k], 1)],
                                     buf_ref.at[k, pl.ds(t, 1)], sem)

    def issue(t, c):
        for k in range(TOP_K_EXPERTS):
            copy(t, k).start()
        return c

    def drain(t, c):
        for k in range(TOP_K_EXPERTS):
            copy(t, k).wait()
        return c

    lax.fori_loop(0, tm, issue, 0)
    lax.fori_loop(0, tm, drain, 0)

    tw = tw_ref[...]
    f = tw[:, 0:1] * buf_ref[0]
    for k in range(1, TOP_K_EXPERTS):
        f = f + tw[:, k:k + 1] * buf_ref[k]
    y = f * lax.rsqrt(jnp.mean(f * f, axis=-1, keepdims=True) + RMS_EPS)
    o_ref[...] = x_ref[...] + ga_ref[0] * (y * gpost_ref[...])


def _combine(ys, dest, top_w, x1, g_post, gate, seq):
    t, d = x1.shape
    bsz = gate.shape[0]
    tm = _tile(seq, 256)
    nt = t // tm
    per_b = seq // tm
    dest3 = dest.reshape(nt, 1, tm * TOP_K_EXPERTS)
    return pl.pallas_call(
        _combine_kernel,
        grid=(nt,),
        in_specs=[pl.BlockSpec((1, 1, tm * TOP_K_EXPERTS), lambda i: (i, 0, 0), memory_space=pltpu.SMEM),
                  pl.BlockSpec((tm, LANES), lambda i: (i, 0)),
                  pl.BlockSpec((tm, d), lambda i: (i, 0)),
                  pl.BlockSpec((1, d), lambda i: (0, 0)),
                  pl.BlockSpec((1, 1, d), lambda i: (i // per_b, 0, 0)),
                  pl.BlockSpec(memory_space=pl.ANY)],
        out_specs=pl.BlockSpec((tm, d), lambda i: (i, 0)),
        out_shape=jax.ShapeDtypeStruct((t, d), F32),
        scratch_shapes=[pltpu.VMEM((TOP_K_EXPERTS, tm, d), F32), pltpu.SemaphoreType.DMA(())],
        compiler_params=_cparams(("arbitrary",)),
        name="moe_combine",
    )(dest3, top_w, x1, g_post.reshape(1, d), gate.reshape(bsz, 1, d), ys)


def _moe(h2, x1, w_router, b_router, w1, b1, w2, b2, g_post, gate, seq):
    t, d = h2.shape
    top_e, top_w, rank, counts = _route(h2, w_router, b_router)
    top_e = top_e[:, :TOP_K_EXPERTS]
    rank = rank[:, :TOP_K_EXPERTS]
    counts = counts[0, :N_EXPERTS].astype(I32)

    n_blocks = (t * TOP_K_EXPERTS) // MOE_ROWS + N_EXPERTS
    padded = (counts + MOE_ROWS - 1) // MOE_ROWS * MOE_ROWS
    pad_end = jnp.cumsum(padded)
    pad_start = pad_end - padded
    dest = (pad_start[top_e] + rank).astype(I32)
    block_row = jnp.arange(n_blocks, dtype=I32) * MOE_ROWS
    block_e = jnp.minimum(jnp.sum((pad_end[None, :] <= block_row[:, None]).astype(I32), axis=1),
                          N_EXPERTS - 1).astype(I32)
    n_used = (pad_end[-1:] // MOE_ROWS).astype(I32)

    xs = _dispatch(h2, dest, (pad_start + counts).astype(I32), (padded - counts).astype(I32),
                   n_blocks * MOE_ROWS)
    w1g, w1l = _deinterleave(w1)
    n_e, f2 = b1.shape
    b1g = b1[:, 0::2].reshape(n_e, 1, f2 // 2)
    b1l = b1[:, 1::2].reshape(n_e, 1, f2 // 2)
    ys = _experts(xs, block_e, n_used, w1g, w1l, b1g, b1l, w2.astype(BF16), b2.reshape(n_e, 1, d))
    return _combine(ys, dest, top_w, x1, g_post, gate, seq)


def _layer(x2, c, tables, bsz, seq, w_ada, b_ada, g_pre_mix, g_post_mix, w_in, w_branch_a,
           w_branch_b, w_out, g_pre_ffn, g_post_ffn, w_router, b_router, w1, b1, w2, b2):
    d = x2.shape[1]
    mod = _modulation(c, w_ada, b_ada)
    sh1, sc1, ga1, sh2, sc2, ga2 = [mod[:, k * d:(k + 1) * d] for k in range(6)]
    cos_a, sin_a, cos_i, sin_i = tables

    h = _prenorm(x2, g_pre_mix, sc1, sh1, seq, BF16)
    wb = w_in.astype(BF16)
    o = 0
    w_qka = wb[:, o:o + A_Q_W + A_KV_W]
    o += A_Q_W + A_KV_W
    w_plain = wb[:, o:o + A_KV_W + 3 * B_W]
    o += A_KV_W + 3 * B_W
    w_qi = wb[:, o:o + IDX_Q_W]
    o += IDX_Q_W
    w_kw = jnp.zeros((d, LANES), BF16).at[:, :IDX_DIM + IDX_HEADS].set(wb[:, o:o + IDX_DIM + IDX_HEADS])
    o += IDX_DIM + IDX_HEADS
    w_gate = wb[:, o:o + 2 * d]

    z_a = _project(h, w_qka, "rope_a", BF16, (cos_a, sin_a), tn_pref=640)
    z_p = _project(h, w_plain, "plain", BF16, tn_pref=1664)
    z_i = _project(h, w_qi, "rope_idx", BF16, (cos_i, sin_i))
    z_k = _project(h, w_kw, "rope_key_idx", F32, (cos_i, sin_i))
    z_g = _project(h, w_gate, "sigmoid", BF16, tn_pref=1024)

    o_a = _dsa_mixer(z_a, z_p, z_i, z_k, bsz, seq)
    o_b = _stick_mixer(z_p, bsz, seq)
    x1, h2 = _merge(o_a, o_b, z_g, x2, w_branch_a.astype(BF16), w_branch_b.astype(BF16),
                    w_out.astype(BF16), g_post_mix, ga1, g_pre_ffn, sc2, sh2, seq)

    return _moe(h2, x1, w_router, b_router, w1, b1, w2, b2, g_post_ffn, ga2, seq)


def kernel(x, c, positions, w_ada, b_ada, g_pre_mix, g_post_mix, w_in, w_branch_a, w_branch_b, w_out, g_pre_ffn, g_post_ffn, w_router, b_router, w1, b1, w2, b2):
    bsz, seq, d = x.shape
    x2 = x.reshape(bsz * seq, d)
    tables = _rope_tables(positions)
    for l in range(w_ada.shape[0]):
        x2 = _layer(x2, c, tables, bsz, seq, w_ada[l], b_ada[l], g_pre_mix[l], g_post_mix[l], w_in[l],
                    w_branch_a[l], w_branch_b[l], w_out[l], g_pre_ffn[l], g_post_ffn[l],
                    w_router[l], b_router[l], w1[l], b1[l], w2[l], b2[l])
    return x2.reshape(bsz, seq, d)
```

```python
import functools

import jax
import jax.numpy as jnp
from jax import lax
from jax.experimental import pallas as pl
from jax.experimental.pallas import tpu as pltpu

F32 = jnp.float32
BF16 = jnp.bfloat16
I32 = jnp.int32

CHUNK = 64
RMS_EPS = 1e-6
ROPE_THETA = 500000.0
A_HEADS = 8
A_KV_HEADS = 2
A_HEAD_DIM = 128
A_ROT_DIM = A_HEAD_DIM // 4
IDX_HEADS = 16
IDX_DIM = 64
IDX_ROT_DIM = IDX_DIM // 4
TOPK_MAX = 256
B_HEADS = 8
B_HEAD_DIM = 128
N_EXPERTS = 32
TOP_K_EXPERTS = 4
SWIGLU_LIMIT = 7.0
SWIGLU_ALPHA = 1.702

A_Q_W = A_HEADS * A_HEAD_DIM
A_KV_W = A_KV_HEADS * A_HEAD_DIM
B_W = B_HEADS * B_HEAD_DIM
IDX_Q_W = IDX_HEADS * IDX_DIM

LANES = 128
SUBLANES = 8
VMEM_LIMIT = 56 * 1024 * 1024
NEG_BIG = -1e30
INT_MAX = 2 ** 31 - 1
LOG2_E = 1.4426950408889634
F32_MAX = 3.4028234663852886e38
BISECT_CAP = 320
STICK_DEAD = -106.0
MOE_ROWS = 512


def _cparams(sem):
    return pltpu.CompilerParams(dimension_semantics=sem, vmem_limit_bytes=VMEM_LIMIT)


def _tile(n, pref):
    t = min(n, pref)
    assert n % t == 0, (n, t)
    return t


def _mod_kernel(c_ref, w_ref, b_ref, o_ref):
    c = c_ref[...]
    s = c * jax.nn.sigmoid(c)
    o_ref[...] = jnp.dot(s, w_ref[...], preferred_element_type=F32,
                         precision=lax.Precision.HIGHEST) + b_ref[...]


def _modulation(c, w_ada, b_ada):
    bsz, d = c.shape
    n = w_ada.shape[1]
    rows = 8
    c_pad = jnp.zeros((rows, d), F32).at[:bsz].set(c)
    tn = _tile(n, 1024)
    out = pl.pallas_call(
        _mod_kernel,
        grid=(n // tn,),
        in_specs=[pl.BlockSpec((rows, d), lambda j: (0, 0)),
                  pl.BlockSpec((d, tn), lambda j: (0, j)),
                  pl.BlockSpec((1, tn), lambda j: (0, j))],
        out_specs=pl.BlockSpec((rows, tn), lambda j: (0, j)),
        out_shape=jax.ShapeDtypeStruct((rows, n), F32),
        compiler_params=_cparams(("arbitrary",)),
        name="modulation",
    )(c_pad, w_ada, b_ada.reshape(1, n))
    return out[:bsz]


def _rope_table_kernel(pos_ref, fa_ref, sa_ref, fi_ref, si_ref, ca_o, sa_o, ci_o, si_o):
    pos = pos_ref[...].astype(F32)
    ang_a = pos * fa_ref[...]
    ca_o[...] = jnp.cos(ang_a)
    sa_o[...] = jnp.sin(ang_a) * sa_ref[...]
    ang_i = pos * fi_ref[...]
    ci_o[...] = jnp.cos(ang_i)
    si_o[...] = jnp.sin(ang_i) * si_ref[...]


def _rope_pattern(rot_dim, head_dim):
    half = rot_dim // 2
    inv_freq = ROPE_THETA ** (-jnp.arange(half, dtype=F32) / half)
    freq = jnp.concatenate([inv_freq, inv_freq, jnp.zeros((head_dim - rot_dim,), F32)])
    sign = jnp.concatenate([-jnp.ones((half,), F32), jnp.ones((half,), F32),
                            jnp.zeros((head_dim - rot_dim,), F32)])
    reps = LANES // head_dim
    return jnp.tile(freq, reps).reshape(1, LANES), jnp.tile(sign, reps).reshape(1, LANES)


def _rope_tables(positions):
    t = positions.size
    pos = positions.reshape(t, 1).astype(I32)
    fa, sa = _rope_pattern(A_ROT_DIM, A_HEAD_DIM)
    fi, si = _rope_pattern(IDX_ROT_DIM, IDX_DIM)
    tm = _tile(t, 1024)
    pat = pl.BlockSpec((1, LANES), lambda i: (0, 0))
    tab = pl.BlockSpec((tm, LANES), lambda i: (i, 0))
    return pl.pallas_call(
        _rope_table_kernel,
        grid=(t // tm,),
        in_specs=[pl.BlockSpec((tm, 1), lambda i: (i, 0)), pat, pat, pat, pat],
        out_specs=[tab, tab, tab, tab],
        out_shape=[jax.ShapeDtypeStruct((t, LANES), F32)] * 4,
        compiler_params=_cparams(("arbitrary",)),
        name="rope_tables",
    )(pos, fa, sa, fi, si)


def _prenorm_kernel(x_ref, g_ref, sc_ref, sh_ref, o_ref):
    x = x_ref[...]
    y = x * lax.rsqrt(jnp.mean(x * x, axis=-1, keepdims=True) + RMS_EPS)
    h = (y * g_ref[...]) * (1.0 + sc_ref[0]) + sh_ref[0]
    o_ref[...] = h.astype(o_ref.dtype)


def _prenorm(x2, g, scale, shift, seq, out_dtype):
    t, d = x2.shape
    bsz = scale.shape[0]
    tm = _tile(seq, 512)
    per_b = seq // tm
    row = pl.BlockSpec((tm, d), lambda i: (i, 0))
    bvec = pl.BlockSpec((1, 1, d), lambda i: (i // per_b, 0, 0))
    return pl.pallas_call(
        _prenorm_kernel,
        grid=(t // tm,),
        in_specs=[row, pl.BlockSpec((1, d), lambda i: (0, 0)), bvec, bvec],
        out_specs=row,
        out_shape=jax.ShapeDtypeStruct((t, d), out_dtype),
        compiler_params=_cparams(("arbitrary",)),
        name="prenorm",
    )(x2, g.reshape(1, d), scale.reshape(bsz, 1, d), shift.reshape(bsz, 1, d))


def _rotate(x, cos, sin, period, half):
    lane = lax.broadcasted_iota(I32, x.shape, 1) & (period - 1)
    swapped = jnp.where(lane < half, pltpu.roll(x, LANES - half, 1), pltpu.roll(x, half, 1))
    return x * cos + swapped * sin


def _proj_kernel(*refs, epilogue):
    if epilogue in ("plain", "sigmoid"):
        h_ref, w_ref, o_ref = refs
    else:
        h_ref, w_ref, cos_ref, sin_ref, o_ref = refs
    acc = jnp.dot(h_ref[...], w_ref[...], preferred_element_type=F32)
    if epilogue == "plain":
        o_ref[...] = acc.astype(o_ref.dtype)
    elif epilogue == "sigmoid":
        o_ref[...] = jax.nn.sigmoid(acc).astype(o_ref.dtype)
    else:
        cos = cos_ref[...]
        sin = sin_ref[...]
        if epilogue == "rope_a":
            period, half = A_HEAD_DIM, A_ROT_DIM // 2
        else:
            period, half = IDX_DIM, IDX_ROT_DIM // 2
        if epilogue == "rope_key_idx":
            lane = lax.broadcasted_iota(I32, cos.shape, 1)
            cos = jnp.where(lane < IDX_DIM, cos, 1.0)
            sin = jnp.where(lane < IDX_DIM, sin, 0.0)
        for c in range(acc.shape[1] // LANES):
            sl = slice(c * LANES, (c + 1) * LANES)
            o_ref[:, sl] = _rotate(acc[:, sl], cos, sin, period, half).astype(o_ref.dtype)


def _project(h, w, epilogue, out_dtype, tables=None, tn_pref=512):
    t, d = h.shape
    n = w.shape[1]
    tm = _tile(t, 1024)
    tn = n
    for cand in (tn_pref, 256, 128):
        if n % cand == 0:
            tn = cand
            break
    in_specs = [pl.BlockSpec((tm, d), lambda i, j: (i, 0)),
                pl.BlockSpec((d, tn), lambda i, j: (0, j))]
    args = [h, w]
    if tables is not None:
        tab = pl.BlockSpec((tm, LANES), lambda i, j: (i, 0))
        in_specs += [tab, tab]
        args += list(tables)
    return pl.pallas_call(
        functools.partial(_proj_kernel, epilogue=epilogue),
        grid=(t // tm, n // tn),
        in_specs=in_specs,
        out_specs=pl.BlockSpec((tm, tn), lambda i, j: (i, j)),
        out_shape=jax.ShapeDtypeStruct((t, n), out_dtype),
        compiler_params=_cparams(("arbitrary", "arbitrary")),
        name="proj_" + epilogue,
    )(*args)


def _dsa_kernel(qi_ref, kwq_ref, kwk_ref, qa_ref, ka_ref, va_ref, o_ref,
                keys_ref, kib_ref, *, tq, tk, topk):
    i = pl.program_id(1)
    t0 = i * tq
    n_kt = (t0 + tq + tk - 1) // tk
    grp = A_HEADS // A_KV_HEADS
    n_cut_steps = (kwk_ref.shape[0] - 1).bit_length() + 1

    @pl.when(i == 0)
    def _():
        kib_ref[...] = kwk_ref[:, :IDX_DIM].astype(BF16)

    qi = qi_ref[...]
    wq = kwq_ref[...]
    row = t0 + lax.broadcasted_iota(I32, (tq, 1), 0)
    limit = (row // CHUNK + 1) * CHUNK

    def score_tile(kt, carry):
        hi_acc, lo_acc = carry
        k0 = pl.multiple_of(kt * tk, tk)
        kb = kib_ref[pl.ds(k0, tk), :]
        acc = jnp.zeros((tq, tk), F32)
        for h in range(IDX_HEADS):
            d = lax.dot_general(qi[:, h * IDX_DIM:(h + 1) * IDX_DIM], kb,
                                (((1,), (1,)), ((), ())), preferred_element_type=F32)
            acc = acc + jnp.maximum(d, 0.0) * wq[:, IDX_DIM + h:IDX_DIM + h + 1]
        adm = (k0 + lax.broadcasted_iota(I32, (tq, tk), 1)) < limit
        s_hi = jnp.where(adm, acc, -jnp.inf)
        s_lo = jnp.where(adm, acc, jnp.inf)
        keys_ref[kt] = s_hi
        for c in range(tk // LANES):
            hi_acc = jnp.maximum(hi_acc, s_hi[:, c * LANES:(c + 1) * LANES])
            lo_acc = jnp.minimum(lo_acc, s_lo[:, c * LANES:(c + 1) * LANES])
        return hi_acc, lo_acc

    hi_acc, lo_acc = lax.fori_loop(
        0, n_kt, score_tile,
        (jnp.full((tq, LANES), -jnp.inf, F32), jnp.full((tq, LANES), jnp.inf, F32)))
    row_max = jnp.max(hi_acc, axis=1, keepdims=True)
    row_min = jnp.min(lo_acc, axis=1, keepdims=True)

    def count_where(pred):
        def count_tile(kt, cnt):
            hit = pred(keys_ref[kt], kt).astype(I32)
            for c in range(tk // LANES):
                cnt = cnt + hit[:, c * LANES:(c + 1) * LANES]
            return cnt

        cnt = lax.fori_loop(0, n_kt, count_tile, jnp.zeros((tq, LANES), I32))
        return jnp.sum(cnt, axis=1, keepdims=True)

    def bisect_cond(state):
        it, pending = state[0], state[1]
        return jnp.logical_and(it < BISECT_CAP, pending > 0)

    def bisect_step(state):
        it, _, lo, hi, c_lo, tau, surplus, todo = state
        mid = 0.5 * lo + 0.5 * hi
        stuck = jnp.logical_or(mid <= lo, mid >= hi)
        cnt = count_where(lambda s, kt: s >= mid)
        fin = jnp.logical_or(stuck, cnt == topk)
        done_now = jnp.logical_and(todo > 0, fin)
        tau = jnp.where(done_now, jnp.where(stuck, lo, mid), tau)
        surplus = jnp.where(done_now, jnp.where(stuck, c_lo - topk, 0), surplus)
        up = cnt >= topk
        lo = jnp.where(up, mid, lo)
        c_lo = jnp.where(up, cnt, c_lo)
        hi = jnp.where(up, hi, mid)
        todo = jnp.where(fin, 0, todo)
        return it + 1, jnp.max(todo), lo, hi, c_lo, tau, surplus, todo

    todo0 = jnp.where(limit > topk, 1, 0).astype(I32)
    hi0 = row_max + jnp.maximum(jnp.abs(row_max) * 1e-6, 1e-30)
    state = (jnp.int32(0), jnp.max(todo0), row_min, hi0, limit,
             jnp.full((tq, 1), -F32_MAX, F32), jnp.zeros((tq, 1), I32), todo0)
    _, _, lo, _, c_lo, tau, surplus, todo = lax.while_loop(bisect_cond, bisect_step, state)
    tau = jnp.where(todo > 0, lo, tau)
    surplus = jnp.where(todo > 0, c_lo - topk, surplus)

    def key_pos(kt):
        return kt * tk + lax.broadcasted_iota(I32, (tq, tk), 1)

    @pl.when(jnp.max(surplus) > 0)
    def _():
        need = topk - count_where(lambda s, kt: s > tau)

        def cut_step(_, st):
            lo_c, hi_c = st
            mid = lax.shift_right_logical(lo_c + hi_c, 1)
            kept = count_where(lambda s, kt: jnp.logical_and(s == tau, key_pos(kt) < mid))
            ok = kept >= need
            return jnp.where(ok, lo_c, mid), jnp.where(ok, mid, hi_c)

        _, hi_c = lax.fori_loop(0, n_cut_steps, cut_step, (jnp.zeros((tq, 1), I32), limit))
        cutoff = jnp.where(surplus > 0, hi_c, INT_MAX)

        def strike(kt, c):
            s = keys_ref[kt]
            drop = jnp.logical_and(s == tau, key_pos(kt) >= cutoff)
            keys_ref[kt] = jnp.where(drop, -jnp.inf, s)
            return c

        lax.fori_loop(0, n_kt, strike, 0)

    qscale = A_HEAD_DIM ** -0.5 * LOG2_E
    qs = [jnp.concatenate(
        [(qa_ref[:, (j * grp + g) * A_HEAD_DIM:(j * grp + g + 1) * A_HEAD_DIM].astype(F32)
          * qscale).astype(BF16) for g in range(grp)], axis=0)
        for j in range(A_KV_HEADS)]

    def attn_tile(kt, carry):
        k0 = pl.multiple_of(kt * tk, tk)
        bias = jnp.where(keys_ref[kt] >= tau, 0.0, NEG_BIG)
        out = []
        for j in range(A_KV_HEADS):
            m, l, acc = carry[3 * j:3 * j + 3]
            kk = ka_ref[pl.ds(k0, tk), j * A_HEAD_DIM:(j + 1) * A_HEAD_DIM]
            vv = va_ref[pl.ds(k0, tk), j * A_HEAD_DIM:(j + 1) * A_HEAD_DIM]
            s = lax.dot_general(qs[j], kk, (((1,), (1,)), ((), ())),
                                preferred_element_type=F32)
            s = (s.reshape(grp, tq, tk) + bias[None]).reshape(grp * tq, tk)
            m_new = jnp.maximum(m, jnp.max(s, axis=1, keepdims=True))
            alpha = jnp.exp2(m - m_new)
            p = jnp.exp2(s - m_new)
            l = alpha * l + jnp.sum(p, axis=1, keepdims=True)
            acc = alpha * acc + jnp.dot(p.astype(BF16), vv, preferred_element_type=F32)
            out += [m_new, l, acc]
        return tuple(out)

    init = []
    for j in range(A_KV_HEADS):
        init += [jnp.full((grp * tq, 1), NEG_BIG, F32), jnp.zeros((grp * tq, 1), F32),
                 jnp.zeros((grp * tq, A_HEAD_DIM), F32)]
    fin = lax.fori_loop(0, n_kt, attn_tile, tuple(init))
    for j in range(A_KV_HEADS):
        out = fin[3 * j + 2] / fin[3 * j + 1]
        for g in range(grp):
            h = j * grp + g
            o_ref[:, h * A_HEAD_DIM:(h + 1) * A_HEAD_DIM] = out[g * tq:(g + 1) * tq].astype(o_ref.dtype)


def _dsa_mixer(z_a, z_p, z_i, z_k, bsz, seq):
    t = bsz * seq
    tq = _tile(seq, 256)
    tk = _tile(seq, 1024)
    nq = seq // tq
    topk = min(TOPK_MAX, seq // 4)
    return pl.pallas_call(
        functools.partial(_dsa_kernel, tq=tq, tk=tk, topk=topk),
        grid=(bsz, nq),
        in_specs=[
            pl.BlockSpec((tq, IDX_Q_W), lambda b, i: (b * nq + i, 0)),
            pl.BlockSpec((tq, LANES), lambda b, i: (b * nq + i, 0)),
            pl.BlockSpec((seq, LANES), lambda b, i: (b, 0)),
            pl.BlockSpec((tq, A_Q_W), lambda b, i: (b * nq + i, 0)),
            pl.BlockSpec((seq, A_KV_W), lambda b, i: (b, A_Q_W // A_KV_W)),
            pl.BlockSpec((seq, A_KV_W), lambda b, i: (b, 0)),
        ],
        out_specs=pl.BlockSpec((tq, A_Q_W), lambda b, i: (b * nq + i, 0)),
        out_shape=jax.ShapeDtypeStruct((t, A_Q_W), BF16),
        scratch_shapes=[pltpu.VMEM((seq // tk, tq, tk), F32),
                        pltpu.VMEM((seq, IDX_DIM), BF16)],
        compiler_params=_cparams(("arbitrary", "arbitrary")),
        name="dsa_mixer",
    )(z_i, z_k, z_k, z_a, z_a, z_p)


def _stick_kernel(q_ref, k_ref, v_ref, o_ref, *, tq, tk, heads):
    i = pl.program_id(2)
    d_idx = (i * tq) // tk
    hd = B_HEAD_DIM
    scale = hd ** -0.5
    qs = [(q_ref[:, h * hd:(h + 1) * hd].astype(F32) * scale).astype(BF16) for h in range(heads)]
    r_io = lax.broadcasted_iota(I32, (tk, tk), 0)
    c_io = lax.broadcasted_iota(I32, (tk, tk), 1)
    upper = jnp.where(r_io > c_io, 1.0, 0.0).astype(BF16)
    diagonal = (d_idx * tk + lax.broadcasted_iota(I32, (tq, tk), 1)
                < i * tq + lax.broadcasted_iota(I32, (tq, tk), 0))

    def scores(kt):
        k0 = pl.multiple_of(kt * tk, tk)
        return tuple(lax.dot_general(qs[h], k_ref[pl.ds(k0, tk), h * hd:(h + 1) * hd],
                                     (((1,), (1,)), ((), ())), preferred_element_type=F32)
                     for h in range(heads))

    def front(z, causal):
        soft = jnp.log(1.0 + jnp.exp(-jnp.abs(z)))
        log_beta = jnp.minimum(z, 0.0) - soft
        log_keep = log_beta - z
        if causal is not None:
            log_keep = jnp.where(causal, log_keep, 0.0)
        hi = log_keep.astype(BF16)
        lo = (log_keep - hi.astype(F32)).astype(BF16)
        gap = (jnp.dot(hi, upper, preferred_element_type=F32)
               + jnp.dot(lo, upper, preferred_element_type=F32))
        logit = log_beta + gap
        if causal is not None:
            logit = jnp.where(causal, logit, NEG_BIG)
        return logit, jnp.sum(log_keep, axis=1, keepdims=True)

    def back(kt, pend, carry):
        k0 = pl.multiple_of(kt * tk, tk)
        out = []
        for h in range(heads):
            logit, keep_sum = pend[2 * h], pend[2 * h + 1]
            run, acc = carry[2 * h], carry[2 * h + 1]
            attn = jnp.exp(logit + run).astype(BF16)
            vv = v_ref[pl.ds(k0, tk), h * hd:(h + 1) * hd]
            out += [run + keep_sum, acc + jnp.dot(attn, vv, preferred_element_type=F32)]
        return tuple(out)

    def fronts(zs, causal):
        out = []
        for h in range(heads):
            out.extend(front(zs[h], causal))
        return tuple(out)

    def step(state):
        n, _, zs, pend, carry = state
        kt = d_idx - 1 - n
        zs_next = scores(jnp.maximum(kt - 1, 0))
        pend_next = fronts(zs, None)
        carry = back(kt + 1, pend, carry)
        return n + 1, run_below(carry, pend_next), zs_next, pend_next, carry

    def run_below(carry, pend):
        top = carry[0] + pend[1]
        for h in range(1, heads):
            top = jnp.maximum(top, carry[2 * h] + pend[2 * h + 1])
        return jnp.max(top)

    def alive(state):
        return jnp.logical_and(state[0] < d_idx, state[1] > STICK_DEAD)

    init = []
    for h in range(heads):
        init += [jnp.zeros((tq, 1), F32), jnp.zeros((tq, hd), F32)]
    init = tuple(init)
    pend = fronts(scores(d_idx), diagonal)
    zs = scores(jnp.maximum(d_idx - 1, 0))
    n_done, _, _, pend, carry = lax.while_loop(
        alive, step, (jnp.int32(0), run_below(init, pend), zs, pend, init))
    carry = back(d_idx - n_done, pend, carry)
    for h in range(heads):
        o_ref[:, h * hd:(h + 1) * hd] = carry[2 * h + 1].astype(o_ref.dtype)


def _stick_mixer(z_p, bsz, seq):
    t = bsz * seq
    tq = _tile(seq, 256)
    tk = _tile(seq, 256)
    nq = seq // tq
    heads = 2
    width = heads * B_HEAD_DIM
    assert A_KV_W % width == 0 and B_W % width == 0
    q_off = A_KV_W // width
    k_off = q_off + B_W // width
    v_off = k_off + B_W // width
    return pl.pallas_call(
        functools.partial(_stick_kernel, tq=tq, tk=tk, heads=heads),
        grid=(bsz, B_HEADS // heads, nq),
        in_specs=[
            pl.BlockSpec((tq, width), lambda b, h, i: (b * nq + i, q_off + h)),
            pl.BlockSpec((seq, width), lambda b, h, i: (b, k_off + h)),
            pl.BlockSpec((seq, width), lambda b, h, i: (b, v_off + h)),
        ],
        out_specs=pl.BlockSpec((tq, width), lambda b, h, i: (b * nq + i, h)),
        out_shape=jax.ShapeDtypeStruct((t, B_W), BF16),
        compiler_params=_cparams(("arbitrary", "arbitrary", "arbitrary")),
        name="stick_mixer",
    )(z_p, z_p, z_p)


def _merge_kernel(oa_ref, ob_ref, g_ref, x_ref, wa_ref, wb_ref, wo_ref, gpost_ref, ga_ref,
                  gpre_ref, sc_ref, sh_ref, o_ref, h_ref):
    d = x_ref.shape[1]
    a = jnp.dot(oa_ref[...], wa_ref[...], preferred_element_type=F32)
    b = jnp.dot(ob_ref[...], wb_ref[...], preferred_element_type=F32)
    merged = g_ref[:, :d].astype(F32) * a + g_ref[:, d:].astype(F32) * b
    mix = jnp.dot(merged.astype(BF16), wo_ref[...], preferred_element_type=F32)
    y = mix * lax.rsqrt(jnp.mean(mix * mix, axis=-1, keepdims=True) + RMS_EPS)
    x1 = x_ref[...] + ga_ref[0] * (y * gpost_ref[...])
    o_ref[...] = x1
    y2 = x1 * lax.rsqrt(jnp.mean(x1 * x1, axis=-1, keepdims=True) + RMS_EPS)
    h_ref[...] = (y2 * gpre_ref[...]) * (1.0 + sc_ref[0]) + sh_ref[0]


def _merge(o_a, o_b, z_g, x2, w_a, w_b, w_o, g_post, gate, g_pre, scale, shift, seq):
    t, d = x2.shape
    bsz = gate.shape[0]
    tm = _tile(seq, 256)
    per_b = seq // tm
    const = lambda i: (0, 0)
    row = pl.BlockSpec((tm, d), lambda i: (i, 0))
    bvec = pl.BlockSpec((1, 1, d), lambda i: (i // per_b, 0, 0))
    return pl.pallas_call(
        _merge_kernel,
        grid=(t // tm,),
        in_specs=[
            pl.BlockSpec((tm, A_Q_W), lambda i: (i, 0)),
            pl.BlockSpec((tm, B_W), lambda i: (i, 0)),
            pl.BlockSpec((tm, 2 * d), lambda i: (i, 0)),
            row,
            pl.BlockSpec((A_Q_W, d), const),
            pl.BlockSpec((B_W, d), const),
            pl.BlockSpec((d, d), const),
            pl.BlockSpec((1, d), const),
            bvec,
            pl.BlockSpec((1, d), const),
            bvec,
            bvec,
        ],
        out_specs=[row, row],
        out_shape=[jax.ShapeDtypeStruct((t, d), F32)] * 2,
        compiler_params=_cparams(("arbitrary",)),
        name="merge",
    )(o_a, o_b, z_g, x2, w_a, w_b, w_o, g_post.reshape(1, d), gate.reshape(bsz, 1, d),
      g_pre.reshape(1, d), scale.reshape(bsz, 1, d), shift.reshape(bsz, 1, d))


def _router_kernel(h_ref, wr_ref, br_ref, e_ref, w_ref, r_ref, cnt_ref, base_ref):
    i = pl.program_id(0)
    tm = h_ref.shape[0]

    @pl.when(i == 0)
    def _():
        base_ref[...] = jnp.zeros_like(base_ref)

    logits = jnp.dot(h_ref[...], wr_ref[...], preferred_element_type=F32,
                     precision=lax.Precision.HIGHEST) + br_ref[...]
    lane = lax.broadcasted_iota(I32, (tm, LANES), 1)
    work = logits
    picks, vals = [], []
    for _ in range(TOP_K_EXPERTS):
        m = jnp.max(work, axis=1, keepdims=True)
        idx = jnp.min(jnp.where(work == m, lane, LANES), axis=1, keepdims=True)
        picks.append(idx)
        vals.append(m)
        work = jnp.where(lane == idx, -jnp.inf, work)
    exps = [jnp.exp(v - vals[0]) for v in vals]
    den = exps[0]
    for e in exps[1:]:
        den = den + e

    onehot = jnp.zeros((tm, LANES), F32)
    for idx in picks:
        onehot = onehot + jnp.where(lane == idx, 1.0, 0.0)
    r_io = lax.broadcasted_iota(I32, (tm, tm), 0)
    c_io = lax.broadcasted_iota(I32, (tm, tm), 1)
    lower = jnp.where(c_io < r_io, 1.0, 0.0).astype(BF16)
    prefix = jnp.dot(lower, onehot.astype(BF16), preferred_element_type=F32)
    total = prefix + base_ref[0:1, :]

    e_out = jnp.zeros((tm, LANES), I32)
    w_out = jnp.zeros((tm, LANES), F32)
    r_out = jnp.zeros((tm, LANES), I32)
    for k in range(TOP_K_EXPERTS):
        rank = jnp.sum(jnp.where(lane == picks[k], total, 0.0), axis=1, keepdims=True)
        e_out = jnp.where(lane == k, picks[k], e_out)
        w_out = jnp.where(lane == k, exps[k] / den, w_out)
        r_out = jnp.where(lane == k, rank.astype(I32), r_out)
    e_ref[...] = e_out
    w_ref[...] = w_out
    r_ref[...] = r_out
    new_base = base_ref[0:1, :] + jnp.sum(onehot, axis=0, keepdims=True)
    base_ref[...] = jnp.broadcast_to(new_base, base_ref.shape)
    cnt_ref[...] = jnp.broadcast_to(new_base, cnt_ref.shape)


def _route(h2, w_router, b_router):
    t, d = h2.shape
    tm = _tile(t, 512)
    wr = jnp.zeros((d, LANES), F32).at[:, :N_EXPERTS].set(w_router)
    br = jnp.full((1, LANES), NEG_BIG, F32).at[0, :N_EXPERTS].set(b_router)
    row = pl.BlockSpec((tm, LANES), lambda i: (i, 0))
    return pl.pallas_call(
        _router_kernel,
        grid=(t // tm,),
        in_specs=[pl.BlockSpec((tm, d), lambda i: (i, 0)),
                  pl.BlockSpec((d, LANES), lambda i: (0, 0)),
                  pl.BlockSpec((1, LANES), lambda i: (0, 0))],
        out_specs=[row, row, row, pl.BlockSpec((8, LANES), lambda i: (0, 0))],
        out_shape=[jax.ShapeDtypeStruct((t, LANES), I32),
                   jax.ShapeDtypeStruct((t, LANES), F32),
                   jax.ShapeDtypeStruct((t, LANES), I32),
                   jax.ShapeDtypeStruct((8, LANES), F32)],
        scratch_shapes=[pltpu.VMEM((8, LANES), F32)],
        compiler_params=_cparams(("arbitrary",)),
        name="router",
    )(h2, wr, br)


def _dispatch_kernel(dest_ref, padlo_ref, padlen_ref, h_ref, xs_ref, zero_ref, sem, zsem):
    tm = h_ref.shape[0]

    def pad_copies(e, act):
        first = padlo_ref[e]
        left = padlen_ref[e]
        end = first + left
        bit = MOE_ROWS // 2
        while bit >= SUBLANES:
            take = left & bit
            off = pl.multiple_of(end - bit, bit)

            @pl.when(take != 0)
            def _(off=off, bit=bit):
                act(pltpu.make_async_copy(zero_ref.at[pl.ds(0, bit)], xs_ref.at[pl.ds(off, bit)], zsem))

            end = end - take
            bit //= 2
        for r in range(SUBLANES - 1):
            @pl.when(r < (left & (SUBLANES - 1)))
            def _(r=r):
                act(pltpu.make_async_copy(zero_ref.at[pl.ds(0, 1)], xs_ref.at[pl.ds(first + r, 1)], zsem))

    @pl.when(pl.program_id(0) == 0)
    def _():
        zero_ref[...] = jnp.zeros_like(zero_ref)

        def start(e, c):
            pad_copies(e, lambda cp: cp.start())
            return c

        lax.fori_loop(0, N_EXPERTS, start, 0)

    def copy(t, k):
        return pltpu.make_async_copy(h_ref.at[pl.ds(t, 1)],
                                     xs_ref.at[pl.ds(dest_ref[0, 0, t * TOP_K_EXPERTS + k], 1)], sem)

    def issue(t, c):
        for k in range(TOP_K_EXPERTS):
            copy(t, k).start()
        return c

    def drain(t, c):
        for k in range(TOP_K_EXPERTS):
            copy(t, k).wait()
        return c

    lax.fori_loop(0, tm, issue, 0)
    lax.fori_loop(0, tm, drain, 0)

    @pl.when(pl.program_id(0) == 0)
    def _():
        def finish(e, c):
            pad_copies(e, lambda cp: cp.wait())
            return c

        lax.fori_loop(0, N_EXPERTS, finish, 0)


def _dispatch(h2, dest, pad_lo, pad_len, n_rows):
    t, d = h2.shape
    tm = _tile(t, 256)
    nt = t // tm
    dest3 = dest.reshape(nt, 1, tm * TOP_K_EXPERTS)
    smem = pl.BlockSpec(memory_space=pltpu.SMEM)
    return pl.pallas_call(
        _dispatch_kernel,
        grid=(nt,),
        in_specs=[pl.BlockSpec((1, 1, tm * TOP_K_EXPERTS), lambda i: (i, 0, 0), memory_space=pltpu.SMEM),
                  smem, smem,
                  pl.BlockSpec((tm, d), lambda i: (i, 0))],
        out_specs=pl.BlockSpec(memory_space=pl.ANY),
        out_shape=jax.ShapeDtypeStruct((n_rows, d), h2.dtype),
        scratch_shapes=[pltpu.VMEM((MOE_ROWS // 2, d), h2.dtype),
                        pltpu.SemaphoreType.DMA(()), pltpu.SemaphoreType.DMA(())],
        compiler_params=pltpu.CompilerParams(dimension_semantics=("arbitrary",),
                                             vmem_limit_bytes=VMEM_LIMIT, has_side_effects=True),
        name="moe_dispatch",
    )(dest3, pad_lo, pad_len, h2)


def _deinterleave_kernel(w_ref, p_ref, g_ref, l_ref):
    p = p_ref[...]
    width = p.shape[0]
    half = width // 2
    for c in range(w_ref.shape[2] // width):
        w = w_ref[0, :, c * width:(c + 1) * width].astype(BF16)
        r = jnp.dot(w, p, preferred_element_type=F32)
        g_ref[0, :, c * half:(c + 1) * half] = r[:, :half].astype(BF16)
        l_ref[0, :, c * half:(c + 1) * half] = r[:, half:].astype(BF16)


def _deinterleave(w1):
    n_e, d, f2 = w1.shape
    width = 2 * LANES
    src = jnp.arange(width)
    dst = jnp.where(src % 2 == 0, src // 2, LANES + src // 2)
    perm = (dst[:, None] == jnp.arange(width)[None, :]).astype(BF16)
    tr = _tile(d, 256)
    out = pl.BlockSpec((1, tr, f2 // 2), lambda e, r: (e, r, 0))
    return pl.pallas_call(
        _deinterleave_kernel,
        grid=(n_e, d // tr),
        in_specs=[pl.BlockSpec((1, tr, f2), lambda e, r: (e, r, 0)),
                  pl.BlockSpec((width, width), lambda e, r: (0, 0))],
        out_specs=[out, out],
        out_shape=[jax.ShapeDtypeStruct((n_e, d, f2 // 2), BF16)] * 2,
        compiler_params=_cparams(("arbitrary", "arbitrary")),
        name="w1_deinterleave",
    )(w1, perm)


def _expert_kernel(be_ref, nu_ref, x_ref, w1g_ref, w1l_ref, b1g_ref, b1l_ref, w2_ref, b2_ref,
                   o_ref, xb_ref, acc_ref):
    i = pl.program_id(0)
    j = pl.program_id(1)
    nf = pl.num_programs(1)
    valid = i < nu_ref[0]

    @pl.when(jnp.logical_and(valid, j == 0))
    def _():
        xb_ref[...] = x_ref[...].astype(BF16)
        acc_ref[...] = jnp.zeros_like(acc_ref)

    @pl.when(valid)
    def _():
        xb = xb_ref[...]
        glu = jnp.dot(xb, w1g_ref[0], preferred_element_type=F32) + b1g_ref[0]
        lin = jnp.dot(xb, w1l_ref[0], preferred_element_type=F32) + b1l_ref[0]
        glu = jnp.minimum(glu, SWIGLU_LIMIT)
        lin = jnp.clip(lin, -SWIGLU_LIMIT, SWIGLU_LIMIT)
        act = glu * jax.nn.sigmoid(SWIGLU_ALPHA * glu) * (lin + 1.0)
        acc_ref[...] += jnp.dot(act.astype(BF16), w2_ref[0], preferred_element_type=F32)

    @pl.when(jnp.logical_and(valid, j == nf - 1))
    def _():
        o_ref[...] = acc_ref[...] + b2_ref[0]


def _experts(xs, block_e, n_used, w1g, w1l, b1g, b1l, w2, b2):
    n_rows, d = xs.shape
    n_e, _, f = w1g.shape
    n_blocks = n_rows // MOE_ROWS
    tf = _tile(f, 1024)
    nf = f // tf

    def blk(i, nu):
        return jnp.minimum(i, nu[0] - 1)

    def ftile(i, j, nu):
        return jnp.where(i < nu[0], j, nf - 1)

    grid_spec = pltpu.PrefetchScalarGridSpec(
        num_scalar_prefetch=2,
        grid=(n_blocks, nf),
        in_specs=[
            pl.BlockSpec((MOE_ROWS, d), lambda i, j, be, nu: (blk(i, nu), 0)),
            pl.BlockSpec((1, d, tf), lambda i, j, be, nu: (be[blk(i, nu)], 0, ftile(i, j, nu))),
            pl.BlockSpec((1, d, tf), lambda i, j, be, nu: (be[blk(i, nu)], 0, ftile(i, j, nu))),
            pl.BlockSpec((1, 1, tf), lambda i, j, be, nu: (be[blk(i, nu)], 0, ftile(i, j, nu))),
            pl.BlockSpec((1, 1, tf), lambda i, j, be, nu: (be[blk(i, nu)], 0, ftile(i, j, nu))),
            pl.BlockSpec((1, tf, d), lambda i, j, be, nu: (be[blk(i, nu)], ftile(i, j, nu), 0)),
            pl.BlockSpec((1, 1, d), lambda i, j, be, nu: (be[blk(i, nu)], 0, 0)),
        ],
        out_specs=pl.BlockSpec((MOE_ROWS, d), lambda i, j, be, nu: (blk(i, nu), 0)),
        scratch_shapes=[pltpu.VMEM((MOE_ROWS, d), BF16), pltpu.VMEM((MOE_ROWS, d), F32)],
    )
    return pl.pallas_call(
        _expert_kernel,
        grid_spec=grid_spec,
        out_shape=jax.ShapeDtypeStruct((n_rows, d), F32),
        compiler_params=_cparams(("arbitrary", "arbitrary")),
        name="moe_experts",
    )(block_e, n_used, xs, w1g, w1l, b1g, b1l, w2, b2)


def _combine_kernel(dest_ref, next_ref, tw_ref, x_ref, gpost_ref, ga_ref, ys_ref, o_ref, buf_ref, sems):
    i = pl.program_id(0)
    n = pl.num_programs(0)
    tm = x_ref.shape[0]
    slot = lax.rem(i, 2)

    def copy(idx_ref, s, t, k):
        return pltpu.make_async_copy(ys_ref.at[pl.ds(idx_ref[0, 0, t * TOP_K_EXPERTS + k], 1)],
                                     buf_ref.at[s, k, pl.ds(t, 1)], sems.at[s])

    def issue(idx_ref, s):
        def body(t, c):
            for k in range(TOP_K_EXPERTS):
                copy(idx_ref, s, t, k).start()
            return c

        lax.fori_loop(0, tm, body, 0)

    @pl.when(i == 0)
    def _():
        issue(dest_ref, 0)

    def tile(s):
        @pl.when(i + 1 < n)
        def _():
            issue(next_ref, 1 - s)

        def drain(t, c):
            for k in range(TOP_K_EXPERTS):
                copy(dest_ref, s, t, k).wait()
            return c

        lax.fori_loop(0, tm, drain, 0)

        tw = tw_ref[...]
        f = tw[:, 0:1] * buf_ref[s, 0]
        for k in range(1, TOP_K_EXPERTS):
            f = f + tw[:, k:k + 1] * buf_ref[s, k]
        y = f * lax.rsqrt(jnp.mean(f * f, axis=-1, keepdims=True) + RMS_EPS)
        o_ref[...] = x_ref[...] + ga_ref[0] * (y * gpost_ref[...])

    for s in range(2):
        pl.when(slot == s)(functools.partial(tile, s))


def _combine(ys, dest, top_w, x1, g_post, gate, seq):
    t, d = x1.shape
    bsz = gate.shape[0]
    tm = _tile(seq, 256)
    nt = t // tm
    per_b = seq // tm
    dest3 = dest.reshape(nt, 1, tm * TOP_K_EXPERTS)
    idx_block = (1, 1, tm * TOP_K_EXPERTS)
    return pl.pallas_call(
        _combine_kernel,
        grid=(nt,),
        in_specs=[pl.BlockSpec(idx_block, lambda i: (i, 0, 0), memory_space=pltpu.SMEM),
                  pl.BlockSpec(idx_block, lambda i: (jnp.minimum(i + 1, nt - 1), 0, 0),
                               memory_space=pltpu.SMEM),
                  pl.BlockSpec((tm, LANES), lambda i: (i, 0)),
                  pl.BlockSpec((tm, d), lambda i: (i, 0)),
                  pl.BlockSpec((1, d), lambda i: (0, 0)),
                  pl.BlockSpec((1, 1, d), lambda i: (i // per_b, 0, 0)),
                  pl.BlockSpec(memory_space=pl.ANY)],
        out_specs=pl.BlockSpec((tm, d), lambda i: (i, 0)),
        out_shape=jax.ShapeDtypeStruct((t, d), F32),
        scratch_shapes=[pltpu.VMEM((2, TOP_K_EXPERTS, tm, d), F32), pltpu.SemaphoreType.DMA((2,))],
        compiler_params=_cparams(("arbitrary",)),
        name="moe_combine",
    )(dest3, dest3, top_w, x1, g_post.reshape(1, d), gate.reshape(bsz, 1, d), ys)


def _moe(h2, x1, w_router, b_router, w1, b1, w2, b2, g_post, gate, seq):
    t, d = h2.shape
    top_e, top_w, rank, counts = _route(h2, w_router, b_router)
    top_e = top_e[:, :TOP_K_EXPERTS]
    rank = rank[:, :TOP_K_EXPERTS]
    counts = counts[0, :N_EXPERTS].astype(I32)

    n_blocks = (t * TOP_K_EXPERTS) // MOE_ROWS + N_EXPERTS
    padded = (counts + MOE_ROWS - 1) // MOE_ROWS * MOE_ROWS
    pad_end = jnp.cumsum(padded)
    pad_start = pad_end - padded
    dest = (pad_start[top_e] + rank).astype(I32)
    block_row = jnp.arange(n_blocks, dtype=I32) * MOE_ROWS
    block_e = jnp.minimum(jnp.sum((pad_end[None, :] <= block_row[:, None]).astype(I32), axis=1),
                          N_EXPERTS - 1).astype(I32)
    n_used = (pad_end[-1:] // MOE_ROWS).astype(I32)

    xs = _dispatch(h2, dest, (pad_start + counts).astype(I32), (padded - counts).astype(I32),
                   n_blocks * MOE_ROWS)
    w1g, w1l = _deinterleave(w1)
    n_e, f2 = b1.shape
    b1g = b1[:, 0::2].reshape(n_e, 1, f2 // 2)
    b1l = b1[:, 1::2].reshape(n_e, 1, f2 // 2)
    ys = _experts(xs, block_e, n_used, w1g, w1l, b1g, b1l, w2.astype(BF16), b2.reshape(n_e, 1, d))
    return _combine(ys, dest, top_w, x1, g_post, gate, seq)


def _layer(x2, c, tables, bsz, seq, w_ada, b_ada, g_pre_mix, g_post_mix, w_in, w_branch_a,
           w_branch_b, w_out, g_pre_ffn, g_post_ffn, w_router, b_router, w1, b1, w2, b2):
    d = x2.shape[1]
    mod = _modulation(c, w_ada, b_ada)
    sh1, sc1, ga1, sh2, sc2, ga2 = [mod[:, k * d:(k + 1) * d] for k in range(6)]
    cos_a, sin_a, cos_i, sin_i = tables

    h = _prenorm(x2, g_pre_mix, sc1, sh1, seq, BF16)
    wb = w_in.astype(BF16)
    o = 0
    w_qka = wb[:, o:o + A_Q_W + A_KV_W]
    o += A_Q_W + A_KV_W
    w_plain = wb[:, o:o + A_KV_W + 3 * B_W]
    o += A_KV_W + 3 * B_W
    w_qi = wb[:, o:o + IDX_Q_W]
    o += IDX_Q_W
    w_kw = jnp.zeros((d, LANES), BF16).at[:, :IDX_DIM + IDX_HEADS].set(wb[:, o:o + IDX_DIM + IDX_HEADS])
    o += IDX_DIM + IDX_HEADS
    w_gate = wb[:, o:o + 2 * d]

    z_a = _project(h, w_qka, "rope_a", BF16, (cos_a, sin_a), tn_pref=640)
    z_p = _project(h, w_plain, "plain", BF16, tn_pref=1664)
    z_i = _project(h, w_qi, "rope_idx", BF16, (cos_i, sin_i))
    z_k = _project(h, w_kw, "rope_key_idx", F32, (cos_i, sin_i))
    z_g = _project(h, w_gate, "sigmoid", BF16, tn_pref=1024)

    o_a = _dsa_mixer(z_a, z_p, z_i, z_k, bsz, seq)
    o_b = _stick_mixer(z_p, bsz, seq)
    x1, h2 = _merge(o_a, o_b, z_g, x2, w_branch_a.astype(BF16), w_branch_b.astype(BF16),
                    w_out.astype(BF16), g_post_mix, ga1, g_pre_ffn, sc2, sh2, seq)

    return _moe(h2, x1, w_router, b_router, w1, b1, w2, b2, g_post_ffn, ga2, seq)


def kernel(x, c, positions, w_ada, b_ada, g_pre_mix, g_post_mix, w_in, w_branch_a, w_branch_b, w_out, g_pre_ffn, g_post_ffn, w_router, b_router, w1, b1, w2, b2):
    bsz, seq, d = x.shape
    x2 = x.reshape(bsz * seq, d)
    tables = _rope_tables(positions)
    for l in range(w_ada.shape[0]):
        x2 = _layer(x2, c, tables, bsz, seq, w_ada[l], b_ada[l], g_pre_mix[l], g_post_mix[l], w_in[l],
                    w_branch_a[l], w_branch_b[l], w_out[l], g_pre_ffn[l], g_post_ffn[l],
                    w_router[l], b_router[l], w1[l], b1[l], w2[l], b2[l])
    return x2.reshape(bsz, seq, d)
```

```python
import functools

import jax
import jax.numpy as jnp
from jax import lax
from jax.experimental import pallas as pl
from jax.experimental.pallas import tpu as pltpu

F32 = jnp.float32
BF16 = jnp.bfloat16
I32 = jnp.int32

CHUNK = 64
RMS_EPS = 1e-6
ROPE_THETA = 500000.0
A_HEADS = 8
A_KV_HEADS = 2
A_HEAD_DIM = 128
A_ROT_DIM = A_HEAD_DIM // 4
IDX_HEADS = 16
IDX_DIM = 64
IDX_ROT_DIM = IDX_DIM // 4
TOPK_MAX = 256
B_HEADS = 8
B_HEAD_DIM = 128
N_EXPERTS = 32
TOP_K_EXPERTS = 4
SWIGLU_LIMIT = 7.0
SWIGLU_ALPHA = 1.702

A_Q_W = A_HEADS * A_HEAD_DIM
A_KV_W = A_KV_HEADS * A_HEAD_DIM
B_W = B_HEADS * B_HEAD_DIM
IDX_Q_W = IDX_HEADS * IDX_DIM

LANES = 128
SUBLANES = 8
VMEM_LIMIT = 56 * 1024 * 1024
NEG_BIG = -1e30
INT_MAX = 2 ** 31 - 1
LOG2_E = 1.4426950408889634
F32_MAX = 3.4028234663852886e38
BISECT_CAP = 320
STICK_DEAD = -106.0
MOE_ROWS = 512


def _cparams(sem):
    return pltpu.CompilerParams(dimension_semantics=sem, vmem_limit_bytes=VMEM_LIMIT)


def _tile(n, pref):
    t = min(n, pref)
    assert n % t == 0, (n, t)
    return t


def _mod_kernel(c_ref, w_ref, b_ref, o_ref):
    c = c_ref[...]
    s = c * jax.nn.sigmoid(c)
    o_ref[...] = jnp.dot(s, w_ref[...], preferred_element_type=F32,
                         precision=lax.Precision.HIGHEST) + b_ref[...]


def _modulation(c, w_ada, b_ada):
    bsz, d = c.shape
    n = w_ada.shape[1]
    rows = 8
    c_pad = jnp.zeros((rows, d), F32).at[:bsz].set(c)
    tn = _tile(n, 1024)
    out = pl.pallas_call(
        _mod_kernel,
        grid=(n // tn,),
        in_specs=[pl.BlockSpec((rows, d), lambda j: (0, 0)),
                  pl.BlockSpec((d, tn), lambda j: (0, j)),
                  pl.BlockSpec((1, tn), lambda j: (0, j))],
        out_specs=pl.BlockSpec((rows, tn), lambda j: (0, j)),
        out_shape=jax.ShapeDtypeStruct((rows, n), F32),
        compiler_params=_cparams(("arbitrary",)),
        name="modulation",
    )(c_pad, w_ada, b_ada.reshape(1, n))
    return out[:bsz]


def _rope_table_kernel(pos_ref, fa_ref, sa_ref, fi_ref, si_ref, ca_o, sa_o, ci_o, si_o):
    pos = pos_ref[...].astype(F32)
    ang_a = pos * fa_ref[...]
    ca_o[...] = jnp.cos(ang_a)
    sa_o[...] = jnp.sin(ang_a) * sa_ref[...]
    ang_i = pos * fi_ref[...]
    ci_o[...] = jnp.cos(ang_i)
    si_o[...] = jnp.sin(ang_i) * si_ref[...]


def _rope_pattern(rot_dim, head_dim):
    half = rot_dim // 2
    inv_freq = ROPE_THETA ** (-jnp.arange(half, dtype=F32) / half)
    freq = jnp.concatenate([inv_freq, inv_freq, jnp.zeros((head_dim - rot_dim,), F32)])
    sign = jnp.concatenate([-jnp.ones((half,), F32), jnp.ones((half,), F32),
                            jnp.zeros((head_dim - rot_dim,), F32)])
    reps = LANES // head_dim
    return jnp.tile(freq, reps).reshape(1, LANES), jnp.tile(sign, reps).reshape(1, LANES)


def _rope_tables(positions):
    t = positions.size
    pos = positions.reshape(t, 1).astype(I32)
    fa, sa = _rope_pattern(A_ROT_DIM, A_HEAD_DIM)
    fi, si = _rope_pattern(IDX_ROT_DIM, IDX_DIM)
    tm = _tile(t, 1024)
    pat = pl.BlockSpec((1, LANES), lambda i: (0, 0))
    tab = pl.BlockSpec((tm, LANES), lambda i: (i, 0))
    return pl.pallas_call(
        _rope_table_kernel,
        grid=(t // tm,),
        in_specs=[pl.BlockSpec((tm, 1), lambda i: (i, 0)), pat, pat, pat, pat],
        out_specs=[tab, tab, tab, tab],
        out_shape=[jax.ShapeDtypeStruct((t, LANES), F32)] * 4,
        compiler_params=_cparams(("arbitrary",)),
        name="rope_tables",
    )(pos, fa, sa, fi, si)


def _prenorm_kernel(x_ref, g_ref, sc_ref, sh_ref, o_ref):
    x = x_ref[...]
    y = x * lax.rsqrt(jnp.mean(x * x, axis=-1, keepdims=True) + RMS_EPS)
    h = (y * g_ref[...]) * (1.0 + sc_ref[0]) + sh_ref[0]
    o_ref[...] = h.astype(o_ref.dtype)


def _prenorm(x2, g, scale, shift, seq, out_dtype):
    t, d = x2.shape
    bsz = scale.shape[0]
    tm = _tile(seq, 512)
    per_b = seq // tm
    row = pl.BlockSpec((tm, d), lambda i: (i, 0))
    bvec = pl.BlockSpec((1, 1, d), lambda i: (i // per_b, 0, 0))
    return pl.pallas_call(
        _prenorm_kernel,
        grid=(t // tm,),
        in_specs=[row, pl.BlockSpec((1, d), lambda i: (0, 0)), bvec, bvec],
        out_specs=row,
        out_shape=jax.ShapeDtypeStruct((t, d), out_dtype),
        compiler_params=_cparams(("arbitrary",)),
        name="prenorm",
    )(x2, g.reshape(1, d), scale.reshape(bsz, 1, d), shift.reshape(bsz, 1, d))


def _rotate(x, cos, sin, period, half):
    lane = lax.broadcasted_iota(I32, x.shape, 1) & (period - 1)
    swapped = jnp.where(lane < half, pltpu.roll(x, LANES - half, 1), pltpu.roll(x, half, 1))
    return x * cos + swapped * sin


def _proj_kernel(*refs, epilogue):
    if epilogue in ("plain", "sigmoid"):
        h_ref, w_ref, o_ref = refs
    else:
        h_ref, w_ref, cos_ref, sin_ref, o_ref = refs
    acc = jnp.dot(h_ref[...], w_ref[...], preferred_element_type=F32)
    if epilogue == "plain":
        o_ref[...] = acc.astype(o_ref.dtype)
    elif epilogue == "sigmoid":
        o_ref[...] = jax.nn.sigmoid(acc).astype(o_ref.dtype)
    else:
        cos = cos_ref[...]
        sin = sin_ref[...]
        if epilogue == "rope_a":
            period, half = A_HEAD_DIM, A_ROT_DIM // 2
        else:
            period, half = IDX_DIM, IDX_ROT_DIM // 2
        if epilogue == "rope_key_idx":
            lane = lax.broadcasted_iota(I32, cos.shape, 1)
            cos = jnp.where(lane < IDX_DIM, cos, 1.0)
            sin = jnp.where(lane < IDX_DIM, sin, 0.0)
        for c in range(acc.shape[1] // LANES):
            sl = slice(c * LANES, (c + 1) * LANES)
            o_ref[:, sl] = _rotate(acc[:, sl], cos, sin, period, half).astype(o_ref.dtype)


def _project(h, w, epilogue, out_dtype, tables=None, tn_pref=512):
    t, d = h.shape
    n = w.shape[1]
    tm = _tile(t, 1024)
    tn = n
    for cand in (tn_pref, 256, 128):
        if n % cand == 0:
            tn = cand
            break
    in_specs = [pl.BlockSpec((tm, d), lambda i, j: (i, 0)),
                pl.BlockSpec((d, tn), lambda i, j: (0, j))]
    args = [h, w]
    if tables is not None:
        tab = pl.BlockSpec((tm, LANES), lambda i, j: (i, 0))
        in_specs += [tab, tab]
        args += list(tables)
    return pl.pallas_call(
        functools.partial(_proj_kernel, epilogue=epilogue),
        grid=(t // tm, n // tn),
        in_specs=in_specs,
        out_specs=pl.BlockSpec((tm, tn), lambda i, j: (i, j)),
        out_shape=jax.ShapeDtypeStruct((t, n), out_dtype),
        compiler_params=_cparams(("arbitrary", "arbitrary")),
        name="proj_" + epilogue,
    )(*args)


def _dsa_kernel(qi_ref, kwq_ref, kwk_ref, qa_ref, ka_ref, va_ref, o_ref,
                keys_ref, kib_ref, *, tq, tk, topk):
    i = pl.program_id(1)
    t0 = i * tq
    n_kt = (t0 + tq + tk - 1) // tk
    grp = A_HEADS // A_KV_HEADS
    n_cut_steps = (kwk_ref.shape[0] - 1).bit_length() + 1

    @pl.when(i == 0)
    def _():
        kib_ref[...] = kwk_ref[:, :IDX_DIM].astype(BF16)

    qi = qi_ref[...]
    wq = kwq_ref[...]
    row = t0 + lax.broadcasted_iota(I32, (tq, 1), 0)
    limit = (row // CHUNK + 1) * CHUNK

    def score_tile(kt, carry):
        hi_acc, lo_acc = carry
        k0 = pl.multiple_of(kt * tk, tk)
        kb = kib_ref[pl.ds(k0, tk), :]
        acc = jnp.zeros((tq, tk), F32)
        for h in range(IDX_HEADS):
            d = lax.dot_general(qi[:, h * IDX_DIM:(h + 1) * IDX_DIM], kb,
                                (((1,), (1,)), ((), ())), preferred_element_type=F32)
            acc = acc + jnp.maximum(d, 0.0) * wq[:, IDX_DIM + h:IDX_DIM + h + 1]
        adm = (k0 + lax.broadcasted_iota(I32, (tq, tk), 1)) < limit
        s_hi = jnp.where(adm, acc, -jnp.inf)
        s_lo = jnp.where(adm, acc, jnp.inf)
        keys_ref[kt] = s_hi
        for c in range(tk // LANES):
            hi_acc = jnp.maximum(hi_acc, s_hi[:, c * LANES:(c + 1) * LANES])
            lo_acc = jnp.minimum(lo_acc, s_lo[:, c * LANES:(c + 1) * LANES])
        return hi_acc, lo_acc

    hi_acc, lo_acc = lax.fori_loop(
        0, n_kt, score_tile,
        (jnp.full((tq, LANES), -jnp.inf, F32), jnp.full((tq, LANES), jnp.inf, F32)))
    row_max = jnp.max(hi_acc, axis=1, keepdims=True)
    row_min = jnp.min(lo_acc, axis=1, keepdims=True)

    def count_where(pred):
        def count_tile(kt, cnt):
            hit = pred(keys_ref[kt], kt).astype(I32)
            for c in range(tk // LANES):
                cnt = cnt + hit[:, c * LANES:(c + 1) * LANES]
            return cnt

        cnt = lax.fori_loop(0, n_kt, count_tile, jnp.zeros((tq, LANES), I32))
        return jnp.sum(cnt, axis=1, keepdims=True)

    def bisect_cond(state):
        it, pending = state[0], state[1]
        return jnp.logical_and(it < BISECT_CAP, pending > 0)

    def bisect_step(state):
        it, _, lo, hi, c_lo, tau, surplus, todo = state
        mid = 0.5 * lo + 0.5 * hi
        stuck = jnp.logical_or(mid <= lo, mid >= hi)
        cnt = count_where(lambda s, kt: s >= mid)
        fin = jnp.logical_or(stuck, cnt == topk)
        done_now = jnp.logical_and(todo > 0, fin)
        tau = jnp.where(done_now, jnp.where(stuck, lo, mid), tau)
        surplus = jnp.where(done_now, jnp.where(stuck, c_lo - topk, 0), surplus)
        up = cnt >= topk
        lo = jnp.where(up, mid, lo)
        c_lo = jnp.where(up, cnt, c_lo)
        hi = jnp.where(up, hi, mid)
        todo = jnp.where(fin, 0, todo)
        return it + 1, jnp.max(todo), lo, hi, c_lo, tau, surplus, todo

    todo0 = jnp.where(limit > topk, 1, 0).astype(I32)
    hi0 = row_max + jnp.maximum(jnp.abs(row_max) * 1e-6, 1e-30)
    state = (jnp.int32(0), jnp.max(todo0), row_min, hi0, limit,
             jnp.full((tq, 1), -F32_MAX, F32), jnp.zeros((tq, 1), I32), todo0)
    _, _, lo, _, c_lo, tau, surplus, todo = lax.while_loop(bisect_cond, bisect_step, state)
    tau = jnp.where(todo > 0, lo, tau)
    surplus = jnp.where(todo > 0, c_lo - topk, surplus)

    def key_pos(kt):
        return kt * tk + lax.broadcasted_iota(I32, (tq, tk), 1)

    @pl.when(jnp.max(surplus) > 0)
    def _():
        need = topk - count_where(lambda s, kt: s > tau)

        def cut_step(_, st):
            lo_c, hi_c = st
            mid = lax.shift_right_logical(lo_c + hi_c, 1)
            kept = count_where(lambda s, kt: jnp.logical_and(s == tau, key_pos(kt) < mid))
            ok = kept >= need
            return jnp.where(ok, lo_c, mid), jnp.where(ok, mid, hi_c)

        _, hi_c = lax.fori_loop(0, n_cut_steps, cut_step, (jnp.zeros((tq, 1), I32), limit))
        cutoff = jnp.where(surplus > 0, hi_c, INT_MAX)

        def strike(kt, c):
            s = keys_ref[kt]
            drop = jnp.logical_and(s == tau, key_pos(kt) >= cutoff)
            keys_ref[kt] = jnp.where(drop, -jnp.inf, s)
            return c

        lax.fori_loop(0, n_kt, strike, 0)

    qscale = A_HEAD_DIM ** -0.5 * LOG2_E
    qs = [jnp.concatenate(
        [(qa_ref[:, (j * grp + g) * A_HEAD_DIM:(j * grp + g + 1) * A_HEAD_DIM].astype(F32)
          * qscale).astype(BF16) for g in range(grp)], axis=0)
        for j in range(A_KV_HEADS)]

    def attn_tile(kt, carry):
        k0 = pl.multiple_of(kt * tk, tk)
        bias = jnp.where(keys_ref[kt] >= tau, 0.0, NEG_BIG)
        out = []
        for j in range(A_KV_HEADS):
            m, l, acc = carry[3 * j:3 * j + 3]
            kk = ka_ref[pl.ds(k0, tk), j * A_HEAD_DIM:(j + 1) * A_HEAD_DIM]
            vv = va_ref[pl.ds(k0, tk), j * A_HEAD_DIM:(j + 1) * A_HEAD_DIM]
            s = lax.dot_general(qs[j], kk, (((1,), (1,)), ((), ())),
                                preferred_element_type=F32)
            s = (s.reshape(grp, tq, tk) + bias[None]).reshape(grp * tq, tk)
            m_new = jnp.maximum(m, jnp.max(s, axis=1, keepdims=True))
            alpha = jnp.exp2(m - m_new)
            p = jnp.exp2(s - m_new)
            l = alpha * l + jnp.sum(p, axis=1, keepdims=True)
            acc = alpha * acc + jnp.dot(p.astype(BF16), vv, preferred_element_type=F32)
            out += [m_new, l, acc]
        return tuple(out)

    init = []
    for j in range(A_KV_HEADS):
        init += [jnp.full((grp * tq, 1), NEG_BIG, F32), jnp.zeros((grp * tq, 1), F32),
                 jnp.zeros((grp * tq, A_HEAD_DIM), F32)]
    fin = lax.fori_loop(0, n_kt, attn_tile, tuple(init))
    for j in range(A_KV_HEADS):
        out = fin[3 * j + 2] / fin[3 * j + 1]
        for g in range(grp):
            h = j * grp + g
            o_ref[:, h * A_HEAD_DIM:(h + 1) * A_HEAD_DIM] = out[g * tq:(g + 1) * tq].astype(o_ref.dtype)


def _dsa_mixer(z_a, z_p, z_i, z_k, bsz, seq):
    t = bsz * seq
    tq = _tile(seq, 256)
    tk = _tile(seq, 1024)
    nq = seq // tq
    topk = min(TOPK_MAX, seq // 4)
    return pl.pallas_call(
        functools.partial(_dsa_kernel, tq=tq, tk=tk, topk=topk),
        grid=(bsz, nq),
        in_specs=[
            pl.BlockSpec((tq, IDX_Q_W), lambda b, i: (b * nq + i, 0)),
            pl.BlockSpec((tq, LANES), lambda b, i: (b * nq + i, 0)),
            pl.BlockSpec((seq, LANES), lambda b, i: (b, 0)),
            pl.BlockSpec((tq, A_Q_W), lambda b, i: (b * nq + i, 0)),
            pl.BlockSpec((seq, A_KV_W), lambda b, i: (b, A_Q_W // A_KV_W)),
            pl.BlockSpec((seq, A_KV_W), lambda b, i: (b, 0)),
        ],
        out_specs=pl.BlockSpec((tq, A_Q_W), lambda b, i: (b * nq + i, 0)),
        out_shape=jax.ShapeDtypeStruct((t, A_Q_W), BF16),
        scratch_shapes=[pltpu.VMEM((seq // tk, tq, tk), F32),
                        pltpu.VMEM((seq, IDX_DIM), BF16)],
        compiler_params=_cparams(("arbitrary", "arbitrary")),
        name="dsa_mixer",
    )(z_i, z_k, z_k, z_a, z_a, z_p)


def _stick_kernel(q_ref, k_ref, v_ref, o_ref, *, tq, tk, heads):
    i = pl.program_id(2)
    d_idx = (i * tq) // tk
    hd = B_HEAD_DIM
    scale = hd ** -0.5
    qs = [(q_ref[:, h * hd:(h + 1) * hd].astype(F32) * scale).astype(BF16) for h in range(heads)]
    r_io = lax.broadcasted_iota(I32, (tk, tk), 0)
    c_io = lax.broadcasted_iota(I32, (tk, tk), 1)
    upper = jnp.where(r_io > c_io, 1.0, 0.0).astype(BF16)
    diagonal = (d_idx * tk + lax.broadcasted_iota(I32, (tq, tk), 1)
                < i * tq + lax.broadcasted_iota(I32, (tq, tk), 0))

    def scores(kt):
        k0 = pl.multiple_of(kt * tk, tk)
        return tuple(lax.dot_general(qs[h], k_ref[pl.ds(k0, tk), h * hd:(h + 1) * hd],
                                     (((1,), (1,)), ((), ())), preferred_element_type=F32)
                     for h in range(heads))

    def front(z, causal):
        soft = jnp.log(1.0 + jnp.exp(-jnp.abs(z)))
        log_beta = jnp.minimum(z, 0.0) - soft
        log_keep = log_beta - z
        if causal is not None:
            log_keep = jnp.where(causal, log_keep, 0.0)
        hi = log_keep.astype(BF16)
        lo = (log_keep - hi.astype(F32)).astype(BF16)
        gap = (jnp.dot(hi, upper, preferred_element_type=F32)
               + jnp.dot(lo, upper, preferred_element_type=F32))
        logit = log_beta + gap
        if causal is not None:
            logit = jnp.where(causal, logit, NEG_BIG)
        return logit, jnp.sum(log_keep, axis=1, keepdims=True)

    def back(kt, pend, carry):
        k0 = pl.multiple_of(kt * tk, tk)
        out = []
        for h in range(heads):
            logit, keep_sum = pend[2 * h], pend[2 * h + 1]
            run, acc = carry[2 * h], carry[2 * h + 1]
            attn = jnp.exp(logit + run).astype(BF16)
            vv = v_ref[pl.ds(k0, tk), h * hd:(h + 1) * hd]
            out += [run + keep_sum, acc + jnp.dot(attn, vv, preferred_element_type=F32)]
        return tuple(out)

    def fronts(zs, causal):
        out = []
        for h in range(heads):
            out.extend(front(zs[h], causal))
        return tuple(out)

    def step(state):
        n, _, zs, pend, carry = state
        kt = d_idx - 1 - n
        zs_next = scores(jnp.maximum(kt - 1, 0))
        pend_next = fronts(zs, None)
        carry = back(kt + 1, pend, carry)
        return n + 1, run_below(carry, pend_next), zs_next, pend_next, carry

    def run_below(carry, pend):
        top = carry[0] + pend[1]
        for h in range(1, heads):
            top = jnp.maximum(top, carry[2 * h] + pend[2 * h + 1])
        return jnp.max(top)

    def alive(state):
        return jnp.logical_and(state[0] < d_idx, state[1] > STICK_DEAD)

    init = []
    for h in range(heads):
        init += [jnp.zeros((tq, 1), F32), jnp.zeros((tq, hd), F32)]
    init = tuple(init)
    pend = fronts(scores(d_idx), diagonal)
    zs = scores(jnp.maximum(d_idx - 1, 0))
    n_done, _, _, pend, carry = lax.while_loop(
        alive, step, (jnp.int32(0), run_below(init, pend), zs, pend, init))
    carry = back(d_idx - n_done, pend, carry)
    for h in range(heads):
        o_ref[:, h * hd:(h + 1) * hd] = carry[2 * h + 1].astype(o_ref.dtype)


def _stick_mixer(z_p, bsz, seq):
    t = bsz * seq
    tq = _tile(seq, 256)
    tk = _tile(seq, 256)
    nq = seq // tq
    heads = 2
    width = heads * B_HEAD_DIM
    assert A_KV_W % width == 0 and B_W % width == 0
    q_off = A_KV_W // width
    k_off = q_off + B_W // width
    v_off = k_off + B_W // width
    return pl.pallas_call(
        functools.partial(_stick_kernel, tq=tq, tk=tk, heads=heads),
        grid=(bsz, B_HEADS // heads, nq),
        in_specs=[
            pl.BlockSpec((tq, width), lambda b, h, i: (b * nq + i, q_off + h)),
            pl.BlockSpec((seq, width), lambda b, h, i: (b, k_off + h)),
            pl.BlockSpec((seq, width), lambda b, h, i: (b, v_off + h)),
        ],
        out_specs=pl.BlockSpec((tq, width), lambda b, h, i: (b * nq + i, h)),
        out_shape=jax.ShapeDtypeStruct((t, B_W), BF16),
        compiler_params=_cparams(("arbitrary", "arbitrary", "arbitrary")),
        name="stick_mixer",
    )(z_p, z_p, z_p)


def _merge_kernel(oa_ref, ob_ref, g_ref, x_ref, wa_ref, wb_ref, wo_ref, gpost_ref, ga_ref,
                  gpre_ref, sc_ref, sh_ref, o_ref, h_ref):
    d = x_ref.shape[1]
    a = jnp.dot(oa_ref[...], wa_ref[...], preferred_element_type=F32)
    b = jnp.dot(ob_ref[...], wb_ref[...], preferred_element_type=F32)
    merged = g_ref[:, :d].astype(F32) * a + g_ref[:, d:].astype(F32) * b
    mix = jnp.dot(merged.astype(BF16), wo_ref[...], preferred_element_type=F32)
    y = mix * lax.rsqrt(jnp.mean(mix * mix, axis=-1, keepdims=True) + RMS_EPS)
    x1 = x_ref[...] + ga_ref[0] * (y * gpost_ref[...])
    o_ref[...] = x1
    y2 = x1 * lax.rsqrt(jnp.mean(x1 * x1, axis=-1, keepdims=True) + RMS_EPS)
    h_ref[...] = (y2 * gpre_ref[...]) * (1.0 + sc_ref[0]) + sh_ref[0]


def _merge(o_a, o_b, z_g, x2, w_a, w_b, w_o, g_post, gate, g_pre, scale, shift, seq):
    t, d = x2.shape
    bsz = gate.shape[0]
    tm = _tile(seq, 256)
    per_b = seq // tm
    const = lambda i: (0, 0)
    row = pl.BlockSpec((tm, d), lambda i: (i, 0))
    bvec = pl.BlockSpec((1, 1, d), lambda i: (i // per_b, 0, 0))
    return pl.pallas_call(
        _merge_kernel,
        grid=(t // tm,),
        in_specs=[
            pl.BlockSpec((tm, A_Q_W), lambda i: (i, 0)),
            pl.BlockSpec((tm, B_W), lambda i: (i, 0)),
            pl.BlockSpec((tm, 2 * d), lambda i: (i, 0)),
            row,
            pl.BlockSpec((A_Q_W, d), const),
            pl.BlockSpec((B_W, d), const),
            pl.BlockSpec((d, d), const),
            pl.BlockSpec((1, d), const),
            bvec,
            pl.BlockSpec((1, d), const),
            bvec,
            bvec,
        ],
        out_specs=[row, row],
        out_shape=[jax.ShapeDtypeStruct((t, d), F32)] * 2,
        compiler_params=_cparams(("arbitrary",)),
        name="merge",
    )(o_a, o_b, z_g, x2, w_a, w_b, w_o, g_post.reshape(1, d), gate.reshape(bsz, 1, d),
      g_pre.reshape(1, d), scale.reshape(bsz, 1, d), shift.reshape(bsz, 1, d))


def _router_kernel(h_ref, wr_ref, br_ref, e_ref, w_ref, r_ref, cnt_ref, base_ref):
    i = pl.program_id(0)
    tm = h_ref.shape[0]

    @pl.when(i == 0)
    def _():
        base_ref[...] = jnp.zeros_like(base_ref)

    logits = jnp.dot(h_ref[...], wr_ref[...], preferred_element_type=F32,
                     precision=lax.Precision.HIGHEST) + br_ref[...]
    lane = lax.broadcasted_iota(I32, (tm, LANES), 1)
    work = logits
    picks, vals = [], []
    for _ in range(TOP_K_EXPERTS):
        m = jnp.max(work, axis=1, keepdims=True)
        idx = jnp.min(jnp.where(work == m, lane, LANES), axis=1, keepdims=True)
        picks.append(idx)
        vals.append(m)
        work = jnp.where(lane == idx, -jnp.inf, work)
    exps = [jnp.exp(v - vals[0]) for v in vals]
    den = exps[0]
    for e in exps[1:]:
        den = den + e

    onehot = jnp.zeros((tm, LANES), F32)
    for idx in picks:
        onehot = onehot + jnp.where(lane == idx, 1.0, 0.0)
    r_io = lax.broadcasted_iota(I32, (tm, tm), 0)
    c_io = lax.broadcasted_iota(I32, (tm, tm), 1)
    lower = jnp.where(c_io < r_io, 1.0, 0.0).astype(BF16)
    prefix = jnp.dot(lower, onehot.astype(BF16), preferred_element_type=F32)
    total = prefix + base_ref[0:1, :]

    e_out = jnp.zeros((tm, LANES), I32)
    w_out = jnp.zeros((tm, LANES), F32)
    r_out = jnp.zeros((tm, LANES), I32)
    for k in range(TOP_K_EXPERTS):
        rank = jnp.sum(jnp.where(lane == picks[k], total, 0.0), axis=1, keepdims=True)
        e_out = jnp.where(lane == k, picks[k], e_out)
        w_out = jnp.where(lane == k, exps[k] / den, w_out)
        r_out = jnp.where(lane == k, rank.astype(I32), r_out)
    e_ref[...] = e_out
    w_ref[...] = w_out
    r_ref[...] = r_out
    new_base = base_ref[0:1, :] + jnp.sum(onehot, axis=0, keepdims=True)
    base_ref[...] = jnp.broadcast_to(new_base, base_ref.shape)
    cnt_ref[...] = jnp.broadcast_to(new_base, cnt_ref.shape)


def _route(h2, w_router, b_router):
    t, d = h2.shape
    tm = _tile(t, 512)
    wr = jnp.zeros((d, LANES), F32).at[:, :N_EXPERTS].set(w_router)
    br = jnp.full((1, LANES), NEG_BIG, F32).at[0, :N_EXPERTS].set(b_router)
    row = pl.BlockSpec((tm, LANES), lambda i: (i, 0))
    return pl.pallas_call(
        _router_kernel,
        grid=(t // tm,),
        in_specs=[pl.BlockSpec((tm, d), lambda i: (i, 0)),
                  pl.BlockSpec((d, LANES), lambda i: (0, 0)),
                  pl.BlockSpec((1, LANES), lambda i: (0, 0))],
        out_specs=[row, row, row, pl.BlockSpec((8, LANES), lambda i: (0, 0))],
        out_shape=[jax.ShapeDtypeStruct((t, LANES), I32),
                   jax.ShapeDtypeStruct((t, LANES), F32),
                   jax.ShapeDtypeStruct((t, LANES), I32),
                   jax.ShapeDtypeStruct((8, LANES), F32)],
        scratch_shapes=[pltpu.VMEM((8, LANES), F32)],
        compiler_params=_cparams(("arbitrary",)),
        name="router",
    )(h2, wr, br)


def _dispatch_kernel(dest_ref, padlo_ref, padlen_ref, nused_ref, h_ref, xs_ref, zero_ref, sem, zsem):
    tm = h_ref.shape[0]

    def pad_copies(e, act):
        first = padlo_ref[e]
        left = padlen_ref[e]
        end = first + left
        bit = MOE_ROWS // 2
        while bit >= SUBLANES:
            take = left & bit
            off = pl.multiple_of(end - bit, bit)

            @pl.when(take != 0)
            def _(off=off, bit=bit):
                act(pltpu.make_async_copy(zero_ref.at[pl.ds(0, bit)], xs_ref.at[pl.ds(off, bit)], zsem))

            end = end - take
            bit //= 2
        for r in range(SUBLANES - 1):
            @pl.when(r < (left & (SUBLANES - 1)))
            def _(r=r):
                act(pltpu.make_async_copy(zero_ref.at[pl.ds(0, 1)], xs_ref.at[pl.ds(first + r, 1)], zsem))

    half = MOE_ROWS // 2
    n_blocks = xs_ref.shape[0] // MOE_ROWS

    def tail_copies(b, act):
        base = pl.multiple_of(b * MOE_ROWS, MOE_ROWS)
        for part in range(2):
            act(pltpu.make_async_copy(zero_ref, xs_ref.at[pl.ds(base + part * half, half)], zsem))

    @pl.when(pl.program_id(0) == 0)
    def _():
        zero_ref[...] = jnp.zeros_like(zero_ref)

        def start(e, c):
            pad_copies(e, lambda cp: cp.start())
            return c

        def start_tail(b, c):
            tail_copies(b, lambda cp: cp.start())
            return c

        lax.fori_loop(0, N_EXPERTS, start, 0)
        lax.fori_loop(nused_ref[0], n_blocks, start_tail, 0)

    def copy(t, k):
        return pltpu.make_async_copy(h_ref.at[pl.ds(t, 1)],
                                     xs_ref.at[pl.ds(dest_ref[0, 0, t * TOP_K_EXPERTS + k], 1)], sem)

    def issue(t, c):
        for k in range(TOP_K_EXPERTS):
            copy(t, k).start()
        return c

    def drain(t, c):
        for k in range(TOP_K_EXPERTS):
            copy(t, k).wait()
        return c

    lax.fori_loop(0, tm, issue, 0)
    lax.fori_loop(0, tm, drain, 0)

    @pl.when(pl.program_id(0) == 0)
    def _():
        def finish(e, c):
            pad_copies(e, lambda cp: cp.wait())
            return c

        def finish_tail(b, c):
            tail_copies(b, lambda cp: cp.wait())
            return c

        lax.fori_loop(0, N_EXPERTS, finish, 0)
        lax.fori_loop(nused_ref[0], n_blocks, finish_tail, 0)


def _dispatch(h2, dest, pad_lo, pad_len, n_used, n_rows):
    t, d = h2.shape
    tm = _tile(t, 256)
    nt = t // tm
    dest3 = dest.reshape(nt, 1, tm * TOP_K_EXPERTS)
    smem = pl.BlockSpec(memory_space=pltpu.SMEM)
    return pl.pallas_call(
        _dispatch_kernel,
        grid=(nt,),
        in_specs=[pl.BlockSpec((1, 1, tm * TOP_K_EXPERTS), lambda i: (i, 0, 0), memory_space=pltpu.SMEM),
                  smem, smem, smem,
                  pl.BlockSpec((tm, d), lambda i: (i, 0))],
        out_specs=pl.BlockSpec(memory_space=pl.ANY),
        out_shape=jax.ShapeDtypeStruct((n_rows, d), h2.dtype),
        scratch_shapes=[pltpu.VMEM((MOE_ROWS // 2, d), h2.dtype),
                        pltpu.SemaphoreType.DMA(()), pltpu.SemaphoreType.DMA(())],
        compiler_params=pltpu.CompilerParams(dimension_semantics=("arbitrary",),
                                             vmem_limit_bytes=VMEM_LIMIT, has_side_effects=True),
        name="moe_dispatch",
    )(dest3, pad_lo, pad_len, n_used, h2)


def _deinterleave_kernel(w_ref, p_ref, g_ref, l_ref):
    p = p_ref[...]
    width = p.shape[0]
    half = width // 2
    for c in range(w_ref.shape[2] // width):
        w = w_ref[0, :, c * width:(c + 1) * width].astype(BF16)
        r = jnp.dot(w, p, preferred_element_type=F32)
        g_ref[0, :, c * half:(c + 1) * half] = r[:, :half].astype(BF16)
        l_ref[0, :, c * half:(c + 1) * half] = r[:, half:].astype(BF16)


def _deinterleave(w1):
    n_e, d, f2 = w1.shape
    width = 2 * LANES
    src = jnp.arange(width)
    dst = jnp.where(src % 2 == 0, src // 2, LANES + src // 2)
    perm = (dst[:, None] == jnp.arange(width)[None, :]).astype(BF16)
    tr = _tile(d, 256)
    out = pl.BlockSpec((1, tr, f2 // 2), lambda e, r: (e, r, 0))
    return pl.pallas_call(
        _deinterleave_kernel,
        grid=(n_e, d // tr),
        in_specs=[pl.BlockSpec((1, tr, f2), lambda e, r: (e, r, 0)),
                  pl.BlockSpec((width, width), lambda e, r: (0, 0))],
        out_specs=[out, out],
        out_shape=[jax.ShapeDtypeStruct((n_e, d, f2 // 2), BF16)] * 2,
        compiler_params=_cparams(("arbitrary", "arbitrary")),
        name="w1_deinterleave",
    )(w1, perm)


def _expert_kernel(be_ref, nu_ref, x_ref, w1g_ref, w1l_ref, b1g_ref, b1l_ref, w2_ref, b2_ref,
                   o_ref, xb_ref, acc_ref, *, nf):
    i = pl.program_id(0)
    j = pl.program_id(1)
    valid = i < nu_ref[0]

    def f_tile(xb):
        glu = jnp.dot(xb, w1g_ref[0], preferred_element_type=F32) + b1g_ref[0]
        lin = jnp.dot(xb, w1l_ref[0], preferred_element_type=F32) + b1l_ref[0]
        glu = jnp.minimum(glu, SWIGLU_LIMIT)
        lin = jnp.clip(lin, -SWIGLU_LIMIT, SWIGLU_LIMIT)
        act = glu * jax.nn.sigmoid(SWIGLU_ALPHA * glu) * (lin + 1.0)
        return jnp.dot(act.astype(BF16), w2_ref[0], preferred_element_type=F32)

    @pl.when(jnp.logical_and(valid, j == 0))
    def _():
        xb = x_ref[...].astype(BF16)
        if nf == 1:
            o_ref[...] = f_tile(xb) + b2_ref[0]
        else:
            xb_ref[...] = xb
            acc_ref[...] = f_tile(xb)

    if nf > 2:
        @pl.when(jnp.logical_and(valid, jnp.logical_and(j > 0, j < nf - 1)))
        def _():
            acc_ref[...] += f_tile(xb_ref[...])

    if nf > 1:
        @pl.when(jnp.logical_and(valid, j == nf - 1))
        def _():
            o_ref[...] = acc_ref[...] + f_tile(xb_ref[...]) + b2_ref[0]

    @pl.when(jnp.logical_and(jnp.logical_not(valid), j == 0))
    def _():
        o_ref[...] = jnp.zeros_like(o_ref)


def _experts(xs, block_e, n_used, w1g, w1l, b1g, b1l, w2, b2):
    n_rows, d = xs.shape
    n_e, _, f = w1g.shape
    n_blocks = n_rows // MOE_ROWS
    tf = _tile(f, 1024)
    nf = f // tf

    def blk(i, nu):
        return jnp.minimum(i, nu[0] - 1)

    def ftile(i, j, nu):
        return jnp.where(i < nu[0], j, nf - 1)

    grid_spec = pltpu.PrefetchScalarGridSpec(
        num_scalar_prefetch=2,
        grid=(n_blocks, nf),
        in_specs=[
            pl.BlockSpec((MOE_ROWS, d), lambda i, j, be, nu: (blk(i, nu), 0)),
            pl.BlockSpec((1, d, tf), lambda i, j, be, nu: (be[blk(i, nu)], 0, ftile(i, j, nu))),
            pl.BlockSpec((1, d, tf), lambda i, j, be, nu: (be[blk(i, nu)], 0, ftile(i, j, nu))),
            pl.BlockSpec((1, 1, tf), lambda i, j, be, nu: (be[blk(i, nu)], 0, ftile(i, j, nu))),
            pl.BlockSpec((1, 1, tf), lambda i, j, be, nu: (be[blk(i, nu)], 0, ftile(i, j, nu))),
            pl.BlockSpec((1, tf, d), lambda i, j, be, nu: (be[blk(i, nu)], ftile(i, j, nu), 0)),
            pl.BlockSpec((1, 1, d), lambda i, j, be, nu: (be[blk(i, nu)], 0, 0)),
        ],
        out_specs=pl.BlockSpec((MOE_ROWS, d), lambda i, j, be, nu: (i, 0)),
        scratch_shapes=[pltpu.VMEM((MOE_ROWS, d), BF16), pltpu.VMEM((MOE_ROWS, d), F32)],
    )
    return pl.pallas_call(
        functools.partial(_expert_kernel, nf=nf),
        grid_spec=grid_spec,
        out_shape=jax.ShapeDtypeStruct((n_rows, d), F32),
        compiler_params=_cparams(("arbitrary", "arbitrary")),
        name="moe_experts",
    )(block_e, n_used, xs, w1g, w1l, b1g, b1l, w2, b2)


def _combine_kernel(dest_ref, tw_ref, x_ref, gpost_ref, ga_ref, ys_ref, o_ref, buf_ref, sem):
    tm = x_ref.shape[0]

    def copy(t, k):
        return pltpu.make_async_copy(ys_ref.at[pl.ds(dest_ref[0, 0, t * TOP_K_EXPERTS + k], 1)],
                                     buf_ref.at[k, pl.ds(t, 1)], sem)

    def issue(t, c):
        for k in range(TOP_K_EXPERTS):
            copy(t, k).start()
        return c

    def drain(t, c):
        for k in range(TOP_K_EXPERTS):
            copy(t, k).wait()
        return c

    lax.fori_loop(0, tm, issue, 0)
    lax.fori_loop(0, tm, drain, 0)

    tw = tw_ref[...]
    f = tw[:, 0:1] * buf_ref[0]
    for k in range(1, TOP_K_EXPERTS):
        f = f + tw[:, k:k + 1] * buf_ref[k]
    y = f * lax.rsqrt(jnp.mean(f * f, axis=-1, keepdims=True) + RMS_EPS)
    o_ref[...] = x_ref[...] + ga_ref[0] * (y * gpost_ref[...])


def _combine(ys, dest, top_w, x1, g_post, gate, seq):
    t, d = x1.shape
    bsz = gate.shape[0]
    tm = _tile(seq, 256)
    nt = t // tm
    per_b = seq // tm
    dest3 = dest.reshape(nt, 1, tm * TOP_K_EXPERTS)
    return pl.pallas_call(
        _combine_kernel,
        grid=(nt,),
        in_specs=[pl.BlockSpec((1, 1, tm * TOP_K_EXPERTS), lambda i: (i, 0, 0), memory_space=pltpu.SMEM),
                  pl.BlockSpec((tm, LANES), lambda i: (i, 0)),
                  pl.BlockSpec((tm, d), lambda i: (i, 0)),
                  pl.BlockSpec((1, d), lambda i: (0, 0)),
                  pl.BlockSpec((1, 1, d), lambda i: (i // per_b, 0, 0)),
                  pl.BlockSpec(memory_space=pl.ANY)],
        out_specs=pl.BlockSpec((tm, d), lambda i: (i, 0)),
        out_shape=jax.ShapeDtypeStruct((t, d), F32),
        scratch_shapes=[pltpu.VMEM((TOP_K_EXPERTS, tm, d), F32), pltpu.SemaphoreType.DMA(())],
        compiler_params=_cparams(("arbitrary",)),
        name="moe_combine",
    )(dest3, top_w, x1, g_post.reshape(1, d), gate.reshape(bsz, 1, d), ys)


def _moe(h2, x1, w_router, b_router, w1, b1, w2, b2, g_post, gate, seq):
    t, d = h2.shape
    top_e, top_w, rank, counts = _route(h2, w_router, b_router)
    top_e = top_e[:, :TOP_K_EXPERTS]
    rank = rank[:, :TOP_K_EXPERTS]
    counts = counts[0, :N_EXPERTS].astype(I32)

    n_blocks = (t * TOP_K_EXPERTS) // MOE_ROWS + N_EXPERTS
    padded = (counts + MOE_ROWS - 1) // MOE_ROWS * MOE_ROWS
    pad_end = jnp.cumsum(padded)
    pad_start = pad_end - padded
    dest = (pad_start[top_e] + rank).astype(I32)
    block_row = jnp.arange(n_blocks, dtype=I32) * MOE_ROWS
    block_e = jnp.minimum(jnp.sum((pad_end[None, :] <= block_row[:, None]).astype(I32), axis=1),
                          N_EXPERTS - 1).astype(I32)
    n_used = (pad_end[-1:] // MOE_ROWS).astype(I32)

    xs = _dispatch(h2, dest, (pad_start + counts).astype(I32), (padded - counts).astype(I32),
                   n_used, n_blocks * MOE_ROWS)
    w1g, w1l = _deinterleave(w1)
    n_e, f2 = b1.shape
    b1g = b1[:, 0::2].reshape(n_e, 1, f2 // 2)
    b1l = b1[:, 1::2].reshape(n_e, 1, f2 // 2)
    ys = _experts(xs, block_e, n_used, w1g, w1l, b1g, b1l, w2.astype(BF16), b2.reshape(n_e, 1, d))
    return _combine(ys, dest, top_w, x1, g_post, gate, seq)


def _layer(x2, c, tables, bsz, seq, w_ada, b_ada, g_pre_mix, g_post_mix, w_in, w_branch_a,
           w_branch_b, w_out, g_pre_ffn, g_post_ffn, w_router, b_router, w1, b1, w2, b2):
    d = x2.shape[1]
    mod = _modulation(c, w_ada, b_ada)
    sh1, sc1, ga1, sh2, sc2, ga2 = [mod[:, k * d:(k + 1) * d] for k in range(6)]
    cos_a, sin_a, cos_i, sin_i = tables

    h = _prenorm(x2, g_pre_mix, sc1, sh1, seq, BF16)
    wb = w_in.astype(BF16)
    o = 0
    w_qka = wb[:, o:o + A_Q_W + A_KV_W]
    o += A_Q_W + A_KV_W
    w_plain = wb[:, o:o + A_KV_W + 3 * B_W]
    o += A_KV_W + 3 * B_W
    w_qi = wb[:, o:o + IDX_Q_W]
    o += IDX_Q_W
    w_kw = jnp.zeros((d, LANES), BF16).at[:, :IDX_DIM + IDX_HEADS].set(wb[:, o:o + IDX_DIM + IDX_HEADS])
    o += IDX_DIM + IDX_HEADS
    w_gate = wb[:, o:o + 2 * d]

    z_a = _project(h, w_qka, "rope_a", BF16, (cos_a, sin_a), tn_pref=640)
    z_p = _project(h, w_plain, "plain", BF16, tn_pref=1664)
    z_i = _project(h, w_qi, "rope_idx", BF16, (cos_i, sin_i))
    z_k = _project(h, w_kw, "rope_key_idx", F32, (cos_i, sin_i))
    z_g = _project(h, w_gate, "sigmoid", BF16, tn_pref=1024)

    o_a = _dsa_mixer(z_a, z_p, z_i, z_k, bsz, seq)
    o_b = _stick_mixer(z_p, bsz, seq)
    x1, h2 = _merge(o_a, o_b, z_g, x2, w_branch_a.astype(BF16), w_branch_b.astype(BF16),
                    w_out.astype(BF16), g_post_mix, ga1, g_pre_ffn, sc2, sh2, seq)

    return _moe(h2, x1, w_router, b_router, w1, b1, w2, b2, g_post_ffn, ga2, seq)


def kernel(x, c, positions, w_ada, b_ada, g_pre_mix, g_post_mix, w_in, w_branch_a, w_branch_b, w_out, g_pre_ffn, g_post_ffn, w_router, b_router, w1, b1, w2, b2):
    bsz, seq, d = x.shape
    x2 = x.reshape(bsz * seq, d)
    tables = _rope_tables(positions)
    for l in range(w_ada.shape[0]):
        x2 = _layer(x2, c, tables, bsz, seq, w_ada[l], b_ada[l], g_pre_mix[l], g_post_mix[l], w_in[l],
                    w_branch_a[l], w_branch_b[l], w_out[l], g_pre_ffn[l], g_post_ffn[l],
                    w_router[l], b_router[l], w1[l], b1[l], w2[l], b2[l])
    return x2.reshape(bsz, seq, d)
```

```python
import functools

import jax
import jax.numpy as jnp
from jax import lax
from jax.experimental import pallas as pl
from jax.experimental.pallas import tpu as pltpu

F32 = jnp.float32
BF16 = jnp.bfloat16
I32 = jnp.int32

CHUNK = 64
RMS_EPS = 1e-6
ROPE_THETA = 500000.0
A_HEADS = 8
A_KV_HEADS = 2
A_HEAD_DIM = 128
A_ROT_DIM = A_HEAD_DIM // 4
IDX_HEADS = 16
IDX_DIM = 64
IDX_ROT_DIM = IDX_DIM // 4
TOPK_MAX = 256
B_HEADS = 8
B_HEAD_DIM = 128
N_EXPERTS = 32
TOP_K_EXPERTS = 4
SWIGLU_LIMIT = 7.0
SWIGLU_ALPHA = 1.702

A_Q_W = A_HEADS * A_HEAD_DIM
A_KV_W = A_KV_HEADS * A_HEAD_DIM
B_W = B_HEADS * B_HEAD_DIM
IDX_Q_W = IDX_HEADS * IDX_DIM

LANES = 128
SUBLANES = 8
VMEM_LIMIT = 56 * 1024 * 1024
NEG_BIG = -1e30
INT_MAX = 2 ** 31 - 1
LOG2_E = 1.4426950408889634
F32_MAX = 3.4028234663852886e38
BISECT_CAP = 320
HI_MARGIN, HI_FLOOR = 1e-6, 1e-30
STICK_DEAD = -106.0
MOE_ROWS = 512

TILE = dict(mod_cols=1024, rope_rows=1024, norm_rows=512, proj_rows=1024, proj_cols=512,
            proj_cols_qk=640, proj_cols_plain=1664, proj_cols_gate=1024, dsa_queries=256, dsa_keys=1024, stick_queries=256, stick_keys=256, merge_rows=256,
            router_rows=512, dispatch_rows=256, deinterleave_rows=256, expert_hidden=1024,
            combine_rows=256)


def _cparams(sem):
    return pltpu.CompilerParams(dimension_semantics=sem, vmem_limit_bytes=VMEM_LIMIT)


def _tile(n, pref):
    t = min(n, pref)
    assert n % t == 0, (n, t)
    return t


def _mod_kernel(c_ref, w_ref, b_ref, o_ref):
    c = c_ref[...]
    s = c * jax.nn.sigmoid(c)
    o_ref[...] = jnp.dot(s, w_ref[...], preferred_element_type=F32,
                         precision=lax.Precision.HIGHEST) + b_ref[...]


def _modulation(c, w_ada, b_ada):
    bsz, d = c.shape
    n = w_ada.shape[1]
    rows = SUBLANES
    c_pad = jnp.zeros((rows, d), F32).at[:bsz].set(c)
    tn = _tile(n, TILE["mod_cols"])
    out = pl.pallas_call(
        _mod_kernel,
        grid=(n // tn,),
        in_specs=[pl.BlockSpec((rows, d), lambda j: (0, 0)),
                  pl.BlockSpec((d, tn), lambda j: (0, j)),
                  pl.BlockSpec((1, tn), lambda j: (0, j))],
        out_specs=pl.BlockSpec((rows, tn), lambda j: (0, j)),
        out_shape=jax.ShapeDtypeStruct((rows, n), F32),
        compiler_params=_cparams(("arbitrary",)),
        name="modulation",
    )(c_pad, w_ada, b_ada.reshape(1, n))
    return out[:bsz]


def _rope_table_kernel(pos_ref, fa_ref, sa_ref, fi_ref, si_ref, ca_o, sa_o, ci_o, si_o):
    pos = pos_ref[...].astype(F32)
    ang_a = pos * fa_ref[...]
    ca_o[...] = jnp.cos(ang_a)
    sa_o[...] = jnp.sin(ang_a) * sa_ref[...]
    ang_i = pos * fi_ref[...]
    ci_o[...] = jnp.cos(ang_i)
    si_o[...] = jnp.sin(ang_i) * si_ref[...]


def _rope_pattern(rot_dim, head_dim):
    half = rot_dim // 2
    inv_freq = ROPE_THETA ** (-jnp.arange(half, dtype=F32) / half)
    freq = jnp.concatenate([inv_freq, inv_freq, jnp.zeros((head_dim - rot_dim,), F32)])
    sign = jnp.concatenate([-jnp.ones((half,), F32), jnp.ones((half,), F32),
                            jnp.zeros((head_dim - rot_dim,), F32)])
    reps = LANES // head_dim
    return jnp.tile(freq, reps).reshape(1, LANES), jnp.tile(sign, reps).reshape(1, LANES)


def _rope_tables(positions):
    t = positions.size
    pos = positions.reshape(t, 1).astype(I32)
    fa, sa = _rope_pattern(A_ROT_DIM, A_HEAD_DIM)
    fi, si = _rope_pattern(IDX_ROT_DIM, IDX_DIM)
    tm = _tile(t, TILE["rope_rows"])
    pat = pl.BlockSpec((1, LANES), lambda i: (0, 0))
    tab = pl.BlockSpec((tm, LANES), lambda i: (i, 0))
    return pl.pallas_call(
        _rope_table_kernel,
        grid=(t // tm,),
        in_specs=[pl.BlockSpec((tm, 1), lambda i: (i, 0)), pat, pat, pat, pat],
        out_specs=[tab, tab, tab, tab],
        out_shape=[jax.ShapeDtypeStruct((t, LANES), F32)] * 4,
        compiler_params=_cparams(("arbitrary",)),
        name="rope_tables",
    )(pos, fa, sa, fi, si)


def _prenorm_kernel(x_ref, g_ref, sc_ref, sh_ref, o_ref):
    x = x_ref[...]
    y = x * lax.rsqrt(jnp.mean(x * x, axis=-1, keepdims=True) + RMS_EPS)
    h = (y * g_ref[...]) * (1.0 + sc_ref[0]) + sh_ref[0]
    o_ref[...] = h.astype(o_ref.dtype)


def _prenorm(x2, g, scale, shift, seq, out_dtype):
    t, d = x2.shape
    bsz = scale.shape[0]
    tm = _tile(seq, TILE["norm_rows"])
    per_b = seq // tm
    row = pl.BlockSpec((tm, d), lambda i: (i, 0))
    bvec = pl.BlockSpec((1, 1, d), lambda i: (i // per_b, 0, 0))
    return pl.pallas_call(
        _prenorm_kernel,
        grid=(t // tm,),
        in_specs=[row, pl.BlockSpec((1, d), lambda i: (0, 0)), bvec, bvec],
        out_specs=row,
        out_shape=jax.ShapeDtypeStruct((t, d), out_dtype),
        compiler_params=_cparams(("arbitrary",)),
        name="prenorm",
    )(x2, g.reshape(1, d), scale.reshape(bsz, 1, d), shift.reshape(bsz, 1, d))


def _rotate(x, cos, sin, period, half):
    lane = lax.broadcasted_iota(I32, x.shape, 1) & (period - 1)
    swapped = jnp.where(lane < half, pltpu.roll(x, LANES - half, 1), pltpu.roll(x, half, 1))
    return x * cos + swapped * sin


def _proj_kernel(*refs, epilogue):
    if epilogue in ("plain", "sigmoid"):
        h_ref, w_ref, o_ref = refs
    else:
        h_ref, w_ref, cos_ref, sin_ref, o_ref = refs
    acc = jnp.dot(h_ref[...], w_ref[...], preferred_element_type=F32)
    if epilogue == "plain":
        o_ref[...] = acc.astype(o_ref.dtype)
    elif epilogue == "sigmoid":
        o_ref[...] = jax.nn.sigmoid(acc).astype(o_ref.dtype)
    else:
        cos = cos_ref[...]
        sin = sin_ref[...]
        if epilogue == "rope_a":
            period, half = A_HEAD_DIM, A_ROT_DIM // 2
        else:
            period, half = IDX_DIM, IDX_ROT_DIM // 2
        if epilogue == "rope_key_idx":
            lane = lax.broadcasted_iota(I32, cos.shape, 1)
            cos = jnp.where(lane < IDX_DIM, cos, 1.0)
            sin = jnp.where(lane < IDX_DIM, sin, 0.0)
        for c in range(acc.shape[1] // LANES):
            sl = slice(c * LANES, (c + 1) * LANES)
            o_ref[:, sl] = _rotate(acc[:, sl], cos, sin, period, half).astype(o_ref.dtype)


def _project(h, w, epilogue, out_dtype, tables=None, tn_pref=TILE["proj_cols"]):
    t, d = h.shape
    n = w.shape[1]
    tm = _tile(t, TILE["proj_rows"])
    tn = n
    for cand in (tn_pref, 256, 128):
        if n % cand == 0:
            tn = cand
            break
    in_specs = [pl.BlockSpec((tm, d), lambda i, j: (i, 0)),
                pl.BlockSpec((d, tn), lambda i, j: (0, j))]
    args = [h, w]
    if tables is not None:
        tab = pl.BlockSpec((tm, LANES), lambda i, j: (i, 0))
        in_specs += [tab, tab]
        args += list(tables)
    return pl.pallas_call(
        functools.partial(_proj_kernel, epilogue=epilogue),
        grid=(t // tm, n // tn),
        in_specs=in_specs,
        out_specs=pl.BlockSpec((tm, tn), lambda i, j: (i, j)),
        out_shape=jax.ShapeDtypeStruct((t, n), out_dtype),
        compiler_params=_cparams(("arbitrary", "arbitrary")),
        name="proj_" + epilogue,
    )(*args)


def _dsa_kernel(qi_ref, kwq_ref, kwk_ref, qa_ref, ka_ref, va_ref, o_ref,
                keys_ref, kib_ref, *, tq, tk, topk):
    i = pl.program_id(1)
    t0 = i * tq
    n_kt = (t0 + tq + tk - 1) // tk
    grp = A_HEADS // A_KV_HEADS
    n_cut_steps = (kwk_ref.shape[0] - 1).bit_length() + 1

    @pl.when(i == 0)
    def _():
        kib_ref[...] = kwk_ref[:, :IDX_DIM].astype(BF16)

    qi = qi_ref[...]
    wq = kwq_ref[...]
    row = t0 + lax.broadcasted_iota(I32, (tq, 1), 0)
    limit = (row // CHUNK + 1) * CHUNK

    def score_tile(kt, carry):
        hi_acc, second, lo_acc = carry
        k0 = pl.multiple_of(kt * tk, tk)
        kb = kib_ref[pl.ds(k0, tk), :]
        acc = jnp.zeros((tq, tk), F32)
        for h in range(IDX_HEADS):
            d = lax.dot_general(qi[:, h * IDX_DIM:(h + 1) * IDX_DIM], kb,
                                (((1,), (1,)), ((), ())), preferred_element_type=F32)
            acc = acc + jnp.maximum(d, 0.0) * wq[:, IDX_DIM + h:IDX_DIM + h + 1]
        adm = (k0 + lax.broadcasted_iota(I32, (tq, tk), 1)) < limit
        s_hi = jnp.where(adm, acc, -jnp.inf)
        s_lo = jnp.where(adm, acc, jnp.inf)
        keys_ref[kt] = s_hi
        for c in range(tk // LANES):
            x = s_hi[:, c * LANES:(c + 1) * LANES]
            second = jnp.maximum(second, jnp.minimum(hi_acc, x))
            hi_acc = jnp.maximum(hi_acc, x)
            lo_acc = jnp.minimum(lo_acc, s_lo[:, c * LANES:(c + 1) * LANES])
        return hi_acc, second, lo_acc

    neg_inf = jnp.full((tq, LANES), -jnp.inf, F32)
    hi_acc, second, lo_acc = lax.fori_loop(
        0, n_kt, score_tile, (neg_inf, neg_inf, jnp.full((tq, LANES), jnp.inf, F32)))
    row_max = jnp.max(hi_acc, axis=1, keepdims=True)
    row_min = jnp.min(lo_acc, axis=1, keepdims=True)

    def count_where(pred):
        def count_tile(kt, cnt):
            hit = pred(keys_ref[kt], kt).astype(I32)
            for c in range(tk // LANES):
                cnt = cnt + hit[:, c * LANES:(c + 1) * LANES]
            return cnt

        cnt = lax.fori_loop(0, n_kt, count_tile, jnp.zeros((tq, LANES), I32))
        return jnp.sum(cnt, axis=1, keepdims=True)

    def bisect_cond(state):
        it, pending = state[0], state[1]
        return jnp.logical_and(it < BISECT_CAP, pending > 0)

    def bisect_step(state):
        it, _, lo, hi, c_lo, tau, surplus, todo = state
        mid = 0.5 * lo + 0.5 * hi
        stuck = jnp.logical_or(mid <= lo, mid >= hi)
        cnt = count_where(lambda s, kt: s >= mid)
        fin = jnp.logical_or(stuck, cnt == topk)
        done_now = jnp.logical_and(todo > 0, fin)
        tau = jnp.where(done_now, jnp.where(stuck, lo, mid), tau)
        surplus = jnp.where(done_now, jnp.where(stuck, c_lo - topk, 0), surplus)
        up = cnt >= topk
        lo = jnp.where(up, mid, lo)
        c_lo = jnp.where(up, cnt, c_lo)
        hi = jnp.where(up, hi, mid)
        todo = jnp.where(fin, 0, todo)
        return it + 1, jnp.max(todo), lo, hi, c_lo, tau, surplus, todo

    todo0 = jnp.where(limit > topk, 1, 0).astype(I32)
    if LANES < topk <= 2 * LANES:
        hi_from = jnp.max(second, axis=1, keepdims=True)
        lo0 = jnp.maximum(row_min, jnp.min(second, axis=1, keepdims=True))
        c_lo0 = count_where(lambda s, kt: s >= lo0)
    else:
        hi_from, lo0, c_lo0 = row_max, row_min, limit
    hi0 = hi_from + jnp.maximum(jnp.abs(hi_from) * HI_MARGIN, HI_FLOOR)
    state = (jnp.int32(0), jnp.max(todo0), lo0, hi0, c_lo0,
             jnp.full((tq, 1), -F32_MAX, F32), jnp.zeros((tq, 1), I32), todo0)
    _, _, lo, _, c_lo, tau, surplus, todo = lax.while_loop(bisect_cond, bisect_step, state)
    tau = jnp.where(todo > 0, lo, tau)
    surplus = jnp.where(todo > 0, c_lo - topk, surplus)

    def key_pos(kt):
        return kt * tk + lax.broadcasted_iota(I32, (tq, tk), 1)

    @pl.when(jnp.max(surplus) > 0)
    def _():
        need = topk - count_where(lambda s, kt: s > tau)

        def cut_step(_, st):
            lo_c, hi_c = st
            mid = lax.shift_right_logical(lo_c + hi_c, 1)
            kept = count_where(lambda s, kt: jnp.logical_and(s == tau, key_pos(kt) < mid))
            ok = kept >= need
            return jnp.where(ok, lo_c, mid), jnp.where(ok, mid, hi_c)

        _, hi_c = lax.fori_loop(0, n_cut_steps, cut_step, (jnp.zeros((tq, 1), I32), limit))
        cutoff = jnp.where(surplus > 0, hi_c, INT_MAX)

        def strike(kt, c):
            s = keys_ref[kt]
            drop = jnp.logical_and(s == tau, key_pos(kt) >= cutoff)
            keys_ref[kt] = jnp.where(drop, -jnp.inf, s)
            return c

        lax.fori_loop(0, n_kt, strike, 0)

    qscale = A_HEAD_DIM ** -0.5 * LOG2_E
    qs = [jnp.concatenate(
        [(qa_ref[:, (j * grp + g) * A_HEAD_DIM:(j * grp + g + 1) * A_HEAD_DIM].astype(F32)
          * qscale).astype(BF16) for g in range(grp)], axis=0)
        for j in range(A_KV_HEADS)]

    def attn_tile(kt, carry):
        k0 = pl.multiple_of(kt * tk, tk)
        bias = jnp.where(keys_ref[kt] >= tau, 0.0, NEG_BIG)
        out = []
        for j in range(A_KV_HEADS):
            m, l, acc = carry[3 * j:3 * j + 3]
            kk = ka_ref[pl.ds(k0, tk), j * A_HEAD_DIM:(j + 1) * A_HEAD_DIM]
            vv = va_ref[pl.ds(k0, tk), j * A_HEAD_DIM:(j + 1) * A_HEAD_DIM]
            s = lax.dot_general(qs[j], kk, (((1,), (1,)), ((), ())),
                                preferred_element_type=F32)
            s = (s.reshape(grp, tq, tk) + bias[None]).reshape(grp * tq, tk)
            m_new = jnp.maximum(m, jnp.max(s, axis=1, keepdims=True))
            alpha = jnp.exp2(m - m_new)
            p = jnp.exp2(s - m_new)
            l = alpha * l + jnp.sum(p, axis=1, keepdims=True)
            acc = alpha * acc + jnp.dot(p.astype(BF16), vv, preferred_element_type=F32)
            out += [m_new, l, acc]
        return tuple(out)

    init = []
    for j in range(A_KV_HEADS):
        init += [jnp.full((grp * tq, 1), NEG_BIG, F32), jnp.zeros((grp * tq, 1), F32),
                 jnp.zeros((grp * tq, A_HEAD_DIM), F32)]
    fin = lax.fori_loop(0, n_kt, attn_tile, tuple(init))
    for j in range(A_KV_HEADS):
        out = fin[3 * j + 2] / fin[3 * j + 1]
        for g in range(grp):
            h = j * grp + g
            o_ref[:, h * A_HEAD_DIM:(h + 1) * A_HEAD_DIM] = out[g * tq:(g + 1) * tq].astype(o_ref.dtype)


def _dsa_mixer(z_a, z_p, z_i, z_k, bsz, seq):
    t = bsz * seq
    tq = _tile(seq, TILE["dsa_queries"])
    tk = _tile(seq, TILE["dsa_keys"])
    nq = seq // tq
    topk = min(TOPK_MAX, seq // 4)
    return pl.pallas_call(
        functools.partial(_dsa_kernel, tq=tq, tk=tk, topk=topk),
        grid=(bsz, nq),
        in_specs=[
            pl.BlockSpec((tq, IDX_Q_W), lambda b, i: (b * nq + i, 0)),
            pl.BlockSpec((tq, LANES), lambda b, i: (b * nq + i, 0)),
            pl.BlockSpec((seq, LANES), lambda b, i: (b, 0)),
            pl.BlockSpec((tq, A_Q_W), lambda b, i: (b * nq + i, 0)),
            pl.BlockSpec((seq, A_KV_W), lambda b, i: (b, A_Q_W // A_KV_W)),
            pl.BlockSpec((seq, A_KV_W), lambda b, i: (b, 0)),
        ],
        out_specs=pl.BlockSpec((tq, A_Q_W), lambda b, i: (b * nq + i, 0)),
        out_shape=jax.ShapeDtypeStruct((t, A_Q_W), BF16),
        scratch_shapes=[pltpu.VMEM((seq // tk, tq, tk), F32),
                        pltpu.VMEM((seq, IDX_DIM), BF16)],
        compiler_params=_cparams(("arbitrary", "arbitrary")),
        name="dsa_mixer",
    )(z_i, z_k, z_k, z_a, z_a, z_p)


def _stick_kernel(q_ref, k_ref, v_ref, o_ref, *, tq, tk, heads):
    i = pl.program_id(2)
    d_idx = (i * tq) // tk
    hd = B_HEAD_DIM
    scale = hd ** -0.5
    qs = [(q_ref[:, h * hd:(h + 1) * hd].astype(F32) * scale).astype(BF16) for h in range(heads)]
    r_io = lax.broadcasted_iota(I32, (tk, tk), 0)
    c_io = lax.broadcasted_iota(I32, (tk, tk), 1)
    upper = jnp.where(r_io > c_io, 1.0, 0.0).astype(BF16)
    diagonal = (d_idx * tk + lax.broadcasted_iota(I32, (tq, tk), 1)
                < i * tq + lax.broadcasted_iota(I32, (tq, tk), 0))

    def scores(kt):
        k0 = pl.multiple_of(kt * tk, tk)
        return tuple(lax.dot_general(qs[h], k_ref[pl.ds(k0, tk), h * hd:(h + 1) * hd],
                                     (((1,), (1,)), ((), ())), preferred_element_type=F32)
                     for h in range(heads))

    def front(z, causal):
        soft = jnp.log(1.0 + jnp.exp(-jnp.abs(z)))
        log_beta = jnp.minimum(z, 0.0) - soft
        log_keep = log_beta - z
        if causal is not None:
            log_keep = jnp.where(causal, log_keep, 0.0)
        hi = log_keep.astype(BF16)
        lo = (log_keep - hi.astype(F32)).astype(BF16)
        gap = (jnp.dot(hi, upper, preferred_element_type=F32)
               + jnp.dot(lo, upper, preferred_element_type=F32))
        logit = log_beta + gap
        if causal is not None:
            logit = jnp.where(causal, logit, NEG_BIG)
        return logit, jnp.sum(log_keep, axis=1, keepdims=True)

    def back(kt, pend, carry, gate=None):
        k0 = pl.multiple_of(kt * tk, tk)
        out = []
        for h in range(heads):
            logit, keep_sum = pend[2 * h], pend[2 * h + 1]
            run, acc = carry[2 * h], carry[2 * h + 1]
            if gate is None:
                attn = jnp.exp(logit + run)
            else:
                attn = jnp.exp(logit + (run + gate[0]))
                keep_sum = keep_sum * gate[1]
            vv = v_ref[pl.ds(k0, tk), h * hd:(h + 1) * hd]
            out += [run + keep_sum, acc + jnp.dot(attn.astype(BF16), vv, preferred_element_type=F32)]
        return tuple(out)

    def fronts(zs, causal):
        out = []
        for h in range(heads):
            out.extend(front(zs[h], causal))
        return tuple(out)

    def top_run(carry):
        top = carry[0]
        for h in range(1, heads):
            top = jnp.maximum(top, carry[2 * h])
        return jnp.max(top)

    init = []
    for h in range(heads):
        init += [jnp.zeros((tq, 1), F32), jnp.zeros((tq, hd), F32)]
    has_prev = d_idx >= 1
    prev = jnp.maximum(d_idx - 1, 0)
    pend_diag = fronts(scores(d_idx), diagonal)
    pend_prev = fronts(scores(prev), None)
    zs = scores(jnp.maximum(d_idx - 2, 0))
    carry = back(d_idx, pend_diag, tuple(init))
    carry = back(prev, pend_prev, carry,
                 gate=(jnp.where(has_prev, 0.0, NEG_BIG), jnp.where(has_prev, 1.0, 0.0)))

    def step(state):
        n, _, zs, carry = state
        kt = d_idx - 2 - n
        zs_next = scores(jnp.maximum(kt - 1, 0))
        carry = back(kt, fronts(zs, None), carry)
        return n + 1, top_run(carry), zs_next, carry

    def alive(state):
        return jnp.logical_and(state[0] < d_idx - 1, state[1] > STICK_DEAD)

    _, _, _, carry = lax.while_loop(alive, step, (jnp.int32(0), top_run(carry), zs, carry))
    for h in range(heads):
        o_ref[:, h * hd:(h + 1) * hd] = carry[2 * h + 1].astype(o_ref.dtype)


def _stick_mixer(z_p, bsz, seq):
    t = bsz * seq
    tq = _tile(seq, TILE["stick_queries"])
    tk = _tile(seq, TILE["stick_keys"])
    nq = seq // tq
    heads = 2
    width = heads * B_HEAD_DIM
    assert A_KV_W % width == 0 and B_W % width == 0
    q_off = A_KV_W // width
    k_off = q_off + B_W // width
    v_off = k_off + B_W // width
    return pl.pallas_call(
        functools.partial(_stick_kernel, tq=tq, tk=tk, heads=heads),
        grid=(bsz, B_HEADS // heads, nq),
        in_specs=[
            pl.BlockSpec((tq, width), lambda b, h, i: (b * nq + i, q_off + h)),
            pl.BlockSpec((seq, width), lambda b, h, i: (b, k_off + h)),
            pl.BlockSpec((seq, width), lambda b, h, i: (b, v_off + h)),
        ],
        out_specs=pl.BlockSpec((tq, width), lambda b, h, i: (b * nq + i, h)),
        out_shape=jax.ShapeDtypeStruct((t, B_W), BF16),
        compiler_params=_cparams(("arbitrary", "arbitrary", "arbitrary")),
        name="stick_mixer",
    )(z_p, z_p, z_p)


def _merge_kernel(oa_ref, ob_ref, g_ref, x_ref, wa_ref, wb_ref, wo_ref, gpost_ref, ga_ref,
                  gpre_ref, sc_ref, sh_ref, o_ref, h_ref):
    d = x_ref.shape[1]
    a = jnp.dot(oa_ref[...], wa_ref[...], preferred_element_type=F32)
    b = jnp.dot(ob_ref[...], wb_ref[...], preferred_element_type=F32)
    merged = g_ref[:, :d].astype(F32) * a + g_ref[:, d:].astype(F32) * b
    mix = jnp.dot(merged.astype(BF16), wo_ref[...], preferred_element_type=F32)
    y = mix * lax.rsqrt(jnp.mean(mix * mix, axis=-1, keepdims=True) + RMS_EPS)
    x1 = x_ref[...] + ga_ref[0] * (y * gpost_ref[...])
    o_ref[...] = x1
    y2 = x1 * lax.rsqrt(jnp.mean(x1 * x1, axis=-1, keepdims=True) + RMS_EPS)
    h_ref[...] = (y2 * gpre_ref[...]) * (1.0 + sc_ref[0]) + sh_ref[0]


def _merge(o_a, o_b, z_g, x2, w_a, w_b, w_o, g_post, gate, g_pre, scale, shift, seq):
    t, d = x2.shape
    bsz = gate.shape[0]
    tm = _tile(seq, TILE["merge_rows"])
    per_b = seq // tm
    const = lambda i: (0, 0)
    row = pl.BlockSpec((tm, d), lambda i: (i, 0))
    bvec = pl.BlockSpec((1, 1, d), lambda i: (i // per_b, 0, 0))
    return pl.pallas_call(
        _merge_kernel,
        grid=(t // tm,),
        in_specs=[
            pl.BlockSpec((tm, A_Q_W), lambda i: (i, 0)),
            pl.BlockSpec((tm, B_W), lambda i: (i, 0)),
            pl.BlockSpec((tm, 2 * d), lambda i: (i, 0)),
            row,
            pl.BlockSpec((A_Q_W, d), const),
            pl.BlockSpec((B_W, d), const),
            pl.BlockSpec((d, d), const),
            pl.BlockSpec((1, d), const),
            bvec,
            pl.BlockSpec((1, d), const),
            bvec,
            bvec,
        ],
        out_specs=[row, row],
        out_shape=[jax.ShapeDtypeStruct((t, d), F32)] * 2,
        compiler_params=_cparams(("arbitrary",)),
        name="merge",
    )(o_a, o_b, z_g, x2, w_a, w_b, w_o, g_post.reshape(1, d), gate.reshape(bsz, 1, d),
      g_pre.reshape(1, d), scale.reshape(bsz, 1, d), shift.reshape(bsz, 1, d))


def _router_kernel(h_ref, wr_ref, br_ref, e_ref, w_ref, r_ref, cnt_ref, base_ref):
    i = pl.program_id(0)
    tm = h_ref.shape[0]

    @pl.when(i == 0)
    def _():
        base_ref[...] = jnp.zeros_like(base_ref)

    logits = jnp.dot(h_ref[...], wr_ref[...], preferred_element_type=F32,
                     precision=lax.Precision.HIGHEST) + br_ref[...]
    lane = lax.broadcasted_iota(I32, (tm, LANES), 1)
    work = logits
    picks, vals = [], []
    for _ in range(TOP_K_EXPERTS):
        m = jnp.max(work, axis=1, keepdims=True)
        idx = jnp.min(jnp.where(work == m, lane, LANES), axis=1, keepdims=True)
        picks.append(idx)
        vals.append(m)
        work = jnp.where(lane == idx, -jnp.inf, work)
    exps = [jnp.exp(v - vals[0]) for v in vals]
    den = exps[0]
    for e in exps[1:]:
        den = den + e

    onehot = jnp.zeros((tm, LANES), F32)
    for idx in picks:
        onehot = onehot + jnp.where(lane == idx, 1.0, 0.0)
    r_io = lax.broadcasted_iota(I32, (tm, tm), 0)
    c_io = lax.broadcasted_iota(I32, (tm, tm), 1)
    lower = jnp.where(c_io < r_io, 1.0, 0.0).astype(BF16)
    prefix = jnp.dot(lower, onehot.astype(BF16), preferred_element_type=F32)
    total = prefix + base_ref[0:1, :]

    e_out = jnp.zeros((tm, LANES), I32)
    w_out = jnp.zeros((tm, LANES), F32)
    r_out = jnp.zeros((tm, LANES), I32)
    for k in range(TOP_K_EXPERTS):
        rank = jnp.sum(jnp.where(lane == picks[k], total, 0.0), axis=1, keepdims=True)
        e_out = jnp.where(lane == k, picks[k], e_out)
        w_out = jnp.where(lane == k, exps[k] / den, w_out)
        r_out = jnp.where(lane == k, rank.astype(I32), r_out)
    e_ref[...] = e_out
    w_ref[...] = w_out
    r_ref[...] = r_out
    new_base = base_ref[0:1, :] + jnp.sum(onehot, axis=0, keepdims=True)
    base_ref[...] = jnp.broadcast_to(new_base, base_ref.shape)
    cnt_ref[...] = jnp.broadcast_to(new_base, cnt_ref.shape)


def _route(h2, w_router, b_router):
    t, d = h2.shape
    tm = _tile(t, TILE["router_rows"])
    wr = jnp.zeros((d, LANES), F32).at[:, :N_EXPERTS].set(w_router)
    br = jnp.full((1, LANES), NEG_BIG, F32).at[0, :N_EXPERTS].set(b_router)
    row = pl.BlockSpec((tm, LANES), lambda i: (i, 0))
    return pl.pallas_call(
        _router_kernel,
        grid=(t // tm,),
        in_specs=[pl.BlockSpec((tm, d), lambda i: (i, 0)),
                  pl.BlockSpec((d, LANES), lambda i: (0, 0)),
                  pl.BlockSpec((1, LANES), lambda i: (0, 0))],
        out_specs=[row, row, row, pl.BlockSpec((8, LANES), lambda i: (0, 0))],
        out_shape=[jax.ShapeDtypeStruct((t, LANES), I32),
                   jax.ShapeDtypeStruct((t, LANES), F32),
                   jax.ShapeDtypeStruct((t, LANES), I32),
                   jax.ShapeDtypeStruct((8, LANES), F32)],
        scratch_shapes=[pltpu.VMEM((8, LANES), F32)],
        compiler_params=_cparams(("arbitrary",)),
        name="router",
    )(h2, wr, br)


def _dispatch_kernel(dest_ref, padlo_ref, padlen_ref, nused_ref, h_ref, xs_ref, zero_ref, sem, zsem):
    tm = h_ref.shape[0]

    def pad_copies(e, act):
        first = padlo_ref[e]
        left = padlen_ref[e]
        end = first + left
        bit = MOE_ROWS // 2
        while bit >= SUBLANES:
            take = left & bit
            off = pl.multiple_of(end - bit, bit)

            @pl.when(take != 0)
            def _(off=off, bit=bit):
                act(pltpu.make_async_copy(zero_ref.at[pl.ds(0, bit)], xs_ref.at[pl.ds(off, bit)], zsem))

            end = end - take
            bit //= 2
        for r in range(SUBLANES - 1):
            @pl.when(r < (left & (SUBLANES - 1)))
            def _(r=r):
                act(pltpu.make_async_copy(zero_ref.at[pl.ds(0, 1)], xs_ref.at[pl.ds(first + r, 1)], zsem))

    half = MOE_ROWS // 2
    n_blocks = xs_ref.shape[0] // MOE_ROWS

    def tail_copies(b, act):
        base = pl.multiple_of(b * MOE_ROWS, MOE_ROWS)
        for part in range(2):
            act(pltpu.make_async_copy(zero_ref, xs_ref.at[pl.ds(base + part * half, half)], zsem))

    @pl.when(pl.program_id(0) == 0)
    def _():
        zero_ref[...] = jnp.zeros_like(zero_ref)

        def start(e, c):
            pad_copies(e, lambda cp: cp.start())
            return c

        def start_tail(b, c):
            tail_copies(b, lambda cp: cp.start())
            return c

        lax.fori_loop(0, N_EXPERTS, start, 0)
        lax.fori_loop(nused_ref[0], n_blocks, start_tail, 0)

    def copy(t, k):
        return pltpu.make_async_copy(h_ref.at[pl.ds(t, 1)],
                                     xs_ref.at[pl.ds(dest_ref[0, 0, t * TOP_K_EXPERTS + k], 1)], sem)

    def issue(t, c):
        for k in range(TOP_K_EXPERTS):
            copy(t, k).start()
        return c

    def drain(t, c):
        for k in range(TOP_K_EXPERTS):
            copy(t, k).wait()
        return c

    lax.fori_loop(0, tm, issue, 0)
    lax.fori_loop(0, tm, drain, 0)

    @pl.when(pl.program_id(0) == 0)
    def _():
        def finish(e, c):
            pad_copies(e, lambda cp: cp.wait())
            return c

        def finish_tail(b, c):
            tail_copies(b, lambda cp: cp.wait())
            return c

        lax.fori_loop(0, N_EXPERTS, finish, 0)
        lax.fori_loop(nused_ref[0], n_blocks, finish_tail, 0)


def _dispatch(h2, dest, pad_lo, pad_len, n_used, n_rows):
    t, d = h2.shape
    tm = _tile(t, TILE["dispatch_rows"])
    nt = t // tm
    dest3 = dest.reshape(nt, 1, tm * TOP_K_EXPERTS)
    smem = pl.BlockSpec(memory_space=pltpu.SMEM)
    return pl.pallas_call(
        _dispatch_kernel,
        grid=(nt,),
        in_specs=[pl.BlockSpec((1, 1, tm * TOP_K_EXPERTS), lambda i: (i, 0, 0), memory_space=pltpu.SMEM),
                  smem, smem, smem,
                  pl.BlockSpec((tm, d), lambda i: (i, 0))],
        out_specs=pl.BlockSpec(memory_space=pl.ANY),
        out_shape=jax.ShapeDtypeStruct((n_rows, d), h2.dtype),
        scratch_shapes=[pltpu.VMEM((MOE_ROWS // 2, d), h2.dtype),
                        pltpu.SemaphoreType.DMA(()), pltpu.SemaphoreType.DMA(())],
        compiler_params=pltpu.CompilerParams(dimension_semantics=("arbitrary",),
                                             vmem_limit_bytes=VMEM_LIMIT, has_side_effects=True),
        name="moe_dispatch",
    )(dest3, pad_lo, pad_len, n_used, h2)


def _deinterleave_kernel(w_ref, p_ref, g_ref, l_ref):
    p = p_ref[...]
    width = p.shape[0]
    half = width // 2
    for c in range(w_ref.shape[2] // width):
        w = w_ref[0, :, c * width:(c + 1) * width].astype(BF16)
        r = jnp.dot(w, p, preferred_element_type=F32)
        g_ref[0, :, c * half:(c + 1) * half] = r[:, :half].astype(BF16)
        l_ref[0, :, c * half:(c + 1) * half] = r[:, half:].astype(BF16)


def _deinterleave(w1):
    n_e, d, f2 = w1.shape
    width = 2 * LANES
    src = jnp.arange(width)
    dst = jnp.where(src % 2 == 0, src // 2, LANES + src // 2)
    perm = (dst[:, None] == jnp.arange(width)[None, :]).astype(BF16)
    tr = _tile(d, TILE["deinterleave_rows"])
    out = pl.BlockSpec((1, tr, f2 // 2), lambda e, r: (e, r, 0))
    return pl.pallas_call(
        _deinterleave_kernel,
        grid=(n_e, d // tr),
        in_specs=[pl.BlockSpec((1, tr, f2), lambda e, r: (e, r, 0)),
                  pl.BlockSpec((width, width), lambda e, r: (0, 0))],
        out_specs=[out, out],
        out_shape=[jax.ShapeDtypeStruct((n_e, d, f2 // 2), BF16)] * 2,
        compiler_params=_cparams(("arbitrary", "arbitrary")),
        name="w1_deinterleave",
    )(w1, perm)


def _expert_kernel(be_ref, nu_ref, x_ref, w1g_ref, w1l_ref, b1g_ref, b1l_ref, w2_ref, b2_ref,
                   o_ref, xb_ref, acc_ref, *, nf):
    i = pl.program_id(0)
    j = pl.program_id(1)
    valid = i < nu_ref[0]

    def f_tile(xb):
        glu = jnp.dot(xb, w1g_ref[0], preferred_element_type=F32) + b1g_ref[0]
        lin = jnp.dot(xb, w1l_ref[0], preferred_element_type=F32) + b1l_ref[0]
        glu = jnp.minimum(glu, SWIGLU_LIMIT)
        lin = jnp.clip(lin, -SWIGLU_LIMIT, SWIGLU_LIMIT)
        act = glu * jax.nn.sigmoid(SWIGLU_ALPHA * glu) * (lin + 1.0)
        return jnp.dot(act.astype(BF16), w2_ref[0], preferred_element_type=F32)

    @pl.when(jnp.logical_and(valid, j == 0))
    def _():
        xb = x_ref[...].astype(BF16)
        if nf == 1:
            o_ref[...] = f_tile(xb) + b2_ref[0]
        else:
            xb_ref[...] = xb
            acc_ref[...] = f_tile(xb)

    if nf > 2:
        @pl.when(jnp.logical_and(valid, jnp.logical_and(j > 0, j < nf - 1)))
        def _():
            acc_ref[...] += f_tile(xb_ref[...])

    if nf > 1:
        @pl.when(jnp.logical_and(valid, j == nf - 1))
        def _():
            o_ref[...] = acc_ref[...] + f_tile(xb_ref[...]) + b2_ref[0]

    @pl.when(jnp.logical_and(jnp.logical_not(valid), j == 0))
    def _():
        o_ref[...] = jnp.zeros_like(o_ref)


def _experts(xs, block_e, n_used, w1g, w1l, b1g, b1l, w2, b2):
    n_rows, d = xs.shape
    n_e, _, f = w1g.shape
    n_blocks = n_rows // MOE_ROWS
    tf = _tile(f, TILE["expert_hidden"])
    nf = f // tf

    def blk(i, nu):
        return jnp.minimum(i, nu[0] - 1)

    def ftile(i, j, nu):
        return jnp.where(i < nu[0], j, nf - 1)

    grid_spec = pltpu.PrefetchScalarGridSpec(
        num_scalar_prefetch=2,
        grid=(n_blocks, nf),
        in_specs=[
            pl.BlockSpec((MOE_ROWS, d), lambda i, j, be, nu: (blk(i, nu), 0)),
            pl.BlockSpec((1, d, tf), lambda i, j, be, nu: (be[blk(i, nu)], 0, ftile(i, j, nu))),
            pl.BlockSpec((1, d, tf), lambda i, j, be, nu: (be[blk(i, nu)], 0, ftile(i, j, nu))),
            pl.BlockSpec((1, 1, tf), lambda i, j, be, nu: (be[blk(i, nu)], 0, ftile(i, j, nu))),
            pl.BlockSpec((1, 1, tf), lambda i, j, be, nu: (be[blk(i, nu)], 0, ftile(i, j, nu))),
            pl.BlockSpec((1, tf, d), lambda i, j, be, nu: (be[blk(i, nu)], ftile(i, j, nu), 0)),
            pl.BlockSpec((1, 1, d), lambda i, j, be, nu: (be[blk(i, nu)], 0, 0)),
        ],
        out_specs=pl.BlockSpec((MOE_ROWS, d), lambda i, j, be, nu: (i, 0)),
        scratch_shapes=[pltpu.VMEM((MOE_ROWS, d), BF16), pltpu.VMEM((MOE_ROWS, d), F32)],
    )
    return pl.pallas_call(
        functools.partial(_expert_kernel, nf=nf),
        grid_spec=grid_spec,
        out_shape=jax.ShapeDtypeStruct((n_rows, d), F32),
        compiler_params=_cparams(("arbitrary", "arbitrary")),
        name="moe_experts",
    )(block_e, n_used, xs, w1g, w1l, b1g, b1l, w2, b2)


def _combine_kernel(dest_ref, tw_ref, x_ref, gpost_ref, ga_ref, ys_ref, o_ref, buf_ref, sem):
    tm = x_ref.shape[0]

    def copy(t, k):
        return pltpu.make_async_copy(ys_ref.at[pl.ds(dest_ref[0, 0, t * TOP_K_EXPERTS + k], 1)],
                                     buf_ref.at[k, pl.ds(t, 1)], sem)

    def issue(t, c):
        for k in range(TOP_K_EXPERTS):
            copy(t, k).start()
        return c

    def drain(t, c):
        for k in range(TOP_K_EXPERTS):
            copy(t, k).wait()
        return c

    lax.fori_loop(0, tm, issue, 0)
    lax.fori_loop(0, tm, drain, 0)

    tw = tw_ref[...]
    f = tw[:, 0:1] * buf_ref[0]
    for k in range(1, TOP_K_EXPERTS):
        f = f + tw[:, k:k + 1] * buf_ref[k]
    y = f * lax.rsqrt(jnp.mean(f * f, axis=-1, keepdims=True) + RMS_EPS)
    o_ref[...] = x_ref[...] + ga_ref[0] * (y * gpost_ref[...])


def _combine(ys, dest, top_w, x1, g_post, gate, seq):
    t, d = x1.shape
    bsz = gate.shape[0]
    tm = _tile(seq, TILE["combine_rows"])
    nt = t // tm
    per_b = seq // tm
    dest3 = dest.reshape(nt, 1, tm * TOP_K_EXPERTS)
    return pl.pallas_call(
        _combine_kernel,
        grid=(nt,),
        in_specs=[pl.BlockSpec((1, 1, tm * TOP_K_EXPERTS), lambda i: (i, 0, 0), memory_space=pltpu.SMEM),
                  pl.BlockSpec((tm, LANES), lambda i: (i, 0)),
                  pl.BlockSpec((tm, d), lambda i: (i, 0)),
                  pl.BlockSpec((1, d), lambda i: (0, 0)),
                  pl.BlockSpec((1, 1, d), lambda i: (i // per_b, 0, 0)),
                  pl.BlockSpec(memory_space=pl.ANY)],
        out_specs=pl.BlockSpec((tm, d), lambda i: (i, 0)),
        out_shape=jax.ShapeDtypeStruct((t, d), F32),
        scratch_shapes=[pltpu.VMEM((TOP_K_EXPERTS, tm, d), F32), pltpu.SemaphoreType.DMA(())],
        compiler_params=_cparams(("arbitrary",)),
        name="moe_combine",
    )(dest3, top_w, x1, g_post.reshape(1, d), gate.reshape(bsz, 1, d), ys)


def _moe(h2, x1, w_router, b_router, w1, b1, w2, b2, g_post, gate, seq):
    t, d = h2.shape
    top_e, top_w, rank, counts = _route(h2, w_router, b_router)
    top_e = top_e[:, :TOP_K_EXPERTS]
    rank = rank[:, :TOP_K_EXPERTS]
    counts = counts[0, :N_EXPERTS].astype(I32)

    n_blocks = (t * TOP_K_EXPERTS) // MOE_ROWS + N_EXPERTS
    padded = (counts + MOE_ROWS - 1) // MOE_ROWS * MOE_ROWS
    pad_end = jnp.cumsum(padded)
    pad_start = pad_end - padded
    dest = (pad_start[top_e] + rank).astype(I32)
    block_row = jnp.arange(n_blocks, dtype=I32) * MOE_ROWS
    block_e = jnp.minimum(jnp.sum((pad_end[None, :] <= block_row[:, None]).astype(I32), axis=1),
                          N_EXPERTS - 1).astype(I32)
    n_used = (pad_end[-1:] // MOE_ROWS).astype(I32)

    xs = _dispatch(h2, dest, (pad_start + counts).astype(I32), (padded - counts).astype(I32),
                   n_used, n_blocks * MOE_ROWS)
    w1g, w1l = _deinterleave(w1)
    n_e, f2 = b1.shape
    b1g = b1[:, 0::2].reshape(n_e, 1, f2 // 2)
    b1l = b1[:, 1::2].reshape(n_e, 1, f2 // 2)
    ys = _experts(xs, block_e, n_used, w1g, w1l, b1g, b1l, w2.astype(BF16), b2.reshape(n_e, 1, d))
    return _combine(ys, dest, top_w, x1, g_post, gate, seq)


def _layer(x2, c, tables, bsz, seq, w_ada, b_ada, g_pre_mix, g_post_mix, w_in, w_branch_a,
           w_branch_b, w_out, g_pre_ffn, g_post_ffn, w_router, b_router, w1, b1, w2, b2):
    d = x2.shape[1]
    mod = _modulation(c, w_ada, b_ada)
    sh1, sc1, ga1, sh2, sc2, ga2 = [mod[:, k * d:(k + 1) * d] for k in range(6)]
    cos_a, sin_a, cos_i, sin_i = tables

    h = _prenorm(x2, g_pre_mix, sc1, sh1, seq, BF16)
    wb = w_in.astype(BF16)
    o = 0
    w_qka = wb[:, o:o + A_Q_W + A_KV_W]
    o += A_Q_W + A_KV_W
    w_plain = wb[:, o:o + A_KV_W + 3 * B_W]
    o += A_KV_W + 3 * B_W
    w_qi = wb[:, o:o + IDX_Q_W]
    o += IDX_Q_W
    w_kw = jnp.zeros((d, LANES), BF16).at[:, :IDX_DIM + IDX_HEADS].set(wb[:, o:o + IDX_DIM + IDX_HEADS])
    o += IDX_DIM + IDX_HEADS
    w_gate = wb[:, o:o + 2 * d]

    z_a = _project(h, w_qka, "rope_a", BF16, (cos_a, sin_a), tn_pref=TILE["proj_cols_qk"])
    z_p = _project(h, w_plain, "plain", BF16, tn_pref=TILE["proj_cols_plain"])
    z_i = _project(h, w_qi, "rope_idx", BF16, (cos_i, sin_i))
    z_k = _project(h, w_kw, "rope_key_idx", F32, (cos_i, sin_i))
    z_g = _project(h, w_gate, "sigmoid", BF16, tn_pref=TILE["proj_cols_gate"])

    o_a = _dsa_mixer(z_a, z_p, z_i, z_k, bsz, seq)
    o_b = _stick_mixer(z_p, bsz, seq)
    x1, h2 = _merge(o_a, o_b, z_g, x2, w_branch_a.astype(BF16), w_branch_b.astype(BF16),
                    w_out.astype(BF16), g_post_mix, ga1, g_pre_ffn, sc2, sh2, seq)

    return _moe(h2, x1, w_router, b_router, w1, b1, w2, b2, g_post_ffn, ga2, seq)


def kernel(x, c, positions, w_ada, b_ada, g_pre_mix, g_post_mix, w_in, w_branch_a, w_branch_b, w_out, g_pre_ffn, g_post_ffn, w_router, b_router, w1, b1, w2, b2):
    bsz, seq, d = x.shape
    x2 = x.reshape(bsz * seq, d)
    tables = _rope_tables(positions)
    for l in range(w_ada.shape[0]):
        x2 = _layer(x2, c, tables, bsz, seq, w_ada[l], b_ada[l], g_pre_mix[l], g_post_mix[l], w_in[l],
                    w_branch_a[l], w_branch_b[l], w_out[l], g_pre_ffn[l], g_post_ffn[l],
                    w_router[l], b_router[l], w1[l], b1[l], w2[l], b2[l])
    return x2.reshape(bsz, seq, d)
```

```python
import functools

import jax
import jax.numpy as jnp
from jax import lax
from jax.experimental import pallas as pl
from jax.experimental.pallas import tpu as pltpu

F32 = jnp.float32
BF16 = jnp.bfloat16
I32 = jnp.int32

CHUNK = 64
RMS_EPS = 1e-6
ROPE_THETA = 500000.0
A_HEADS = 8
A_KV_HEADS = 2
A_HEAD_DIM = 128
A_ROT_DIM = A_HEAD_DIM // 4
IDX_HEADS = 16
IDX_DIM = 64
IDX_ROT_DIM = IDX_DIM // 4
TOPK_MAX = 256
B_HEADS = 8
B_HEAD_DIM = 128
N_EXPERTS = 32
TOP_K_EXPERTS = 4
SWIGLU_LIMIT = 7.0
SWIGLU_ALPHA = 1.702

A_Q_W = A_HEADS * A_HEAD_DIM
A_KV_W = A_KV_HEADS * A_HEAD_DIM
B_W = B_HEADS * B_HEAD_DIM
IDX_Q_W = IDX_HEADS * IDX_DIM

LANES = 128
SUBLANES = 8
VMEM_LIMIT = 56 * 1024 * 1024
NEG_BIG = -1e30
INT_MAX = 2 ** 31 - 1
LOG2_E = 1.4426950408889634
F32_MAX = 3.4028234663852886e38
BISECT_CAP = 320
HI_MARGIN, HI_FLOOR = 1e-6, 1e-30
STICK_DEAD = -106.0
MOE_ROWS = 512

TILE = dict(mod_cols=1024, rope_rows=1024, norm_rows=512, proj_rows=1024, proj_cols=512,
            proj_cols_qk=640, proj_cols_plain=1664, proj_cols_gate=1024, dsa_queries=256, dsa_keys=1024, stick_queries=256, stick_keys=256, merge_rows=256,
            router_rows=512, dispatch_rows=256, deinterleave_rows=256, expert_hidden=1024,
            combine_rows=256)


def _cparams(sem):
    return pltpu.CompilerParams(dimension_semantics=sem, vmem_limit_bytes=VMEM_LIMIT)


def _tile(n, pref):
    t = min(n, pref)
    assert n % t == 0, (n, t)
    return t


def _mod_kernel(c_ref, w_ref, b_ref, o_ref):
    c = c_ref[...]
    s = c * jax.nn.sigmoid(c)
    o_ref[...] = jnp.dot(s, w_ref[...], preferred_element_type=F32,
                         precision=lax.Precision.HIGHEST) + b_ref[...]


def _modulation(c, w_ada, b_ada):
    bsz, d = c.shape
    n = w_ada.shape[1]
    rows = SUBLANES
    c_pad = jnp.zeros((rows, d), F32).at[:bsz].set(c)
    tn = _tile(n, TILE["mod_cols"])
    out = pl.pallas_call(
        _mod_kernel,
        grid=(n // tn,),
        in_specs=[pl.BlockSpec((rows, d), lambda j: (0, 0)),
                  pl.BlockSpec((d, tn), lambda j: (0, j)),
                  pl.BlockSpec((1, tn), lambda j: (0, j))],
        out_specs=pl.BlockSpec((rows, tn), lambda j: (0, j)),
        out_shape=jax.ShapeDtypeStruct((rows, n), F32),
        compiler_params=_cparams(("arbitrary",)),
        name="modulation",
    )(c_pad, w_ada, b_ada.reshape(1, n))
    return out[:bsz]


def _rope_table_kernel(pos_ref, fa_ref, sa_ref, fi_ref, si_ref, ca_o, sa_o, ci_o, si_o):
    pos = pos_ref[...].astype(F32)
    ang_a = pos * fa_ref[...]
    ca_o[...] = jnp.cos(ang_a)
    sa_o[...] = jnp.sin(ang_a) * sa_ref[...]
    ang_i = pos * fi_ref[...]
    ci_o[...] = jnp.cos(ang_i)
    si_o[...] = jnp.sin(ang_i) * si_ref[...]


def _rope_pattern(rot_dim, head_dim):
    half = rot_dim // 2
    inv_freq = ROPE_THETA ** (-jnp.arange(half, dtype=F32) / half)
    freq = jnp.concatenate([inv_freq, inv_freq, jnp.zeros((head_dim - rot_dim,), F32)])
    sign = jnp.concatenate([-jnp.ones((half,), F32), jnp.ones((half,), F32),
                            jnp.zeros((head_dim - rot_dim,), F32)])
    reps = LANES // head_dim
    return jnp.tile(freq, reps).reshape(1, LANES), jnp.tile(sign, reps).reshape(1, LANES)


def _rope_tables(positions):
    t = positions.size
    pos = positions.reshape(t, 1).astype(I32)
    fa, sa = _rope_pattern(A_ROT_DIM, A_HEAD_DIM)
    fi, si = _rope_pattern(IDX_ROT_DIM, IDX_DIM)
    tm = _tile(t, TILE["rope_rows"])
    pat = pl.BlockSpec((1, LANES), lambda i: (0, 0))
    tab = pl.BlockSpec((tm, LANES), lambda i: (i, 0))
    return pl.pallas_call(
        _rope_table_kernel,
        grid=(t // tm,),
        in_specs=[pl.BlockSpec((tm, 1), lambda i: (i, 0)), pat, pat, pat, pat],
        out_specs=[tab, tab, tab, tab],
        out_shape=[jax.ShapeDtypeStruct((t, LANES), F32)] * 4,
        compiler_params=_cparams(("arbitrary",)),
        name="rope_tables",
    )(pos, fa, sa, fi, si)


def _prenorm_kernel(x_ref, g_ref, sc_ref, sh_ref, o_ref):
    x = x_ref[...]
    y = x * lax.rsqrt(jnp.mean(x * x, axis=-1, keepdims=True) + RMS_EPS)
    h = (y * g_ref[...]) * (1.0 + sc_ref[0]) + sh_ref[0]
    o_ref[...] = h.astype(o_ref.dtype)


def _prenorm(x2, g, scale, shift, seq, out_dtype):
    t, d = x2.shape
    bsz = scale.shape[0]
    tm = _tile(seq, TILE["norm_rows"])
    per_b = seq // tm
    row = pl.BlockSpec((tm, d), lambda i: (i, 0))
    bvec = pl.BlockSpec((1, 1, d), lambda i: (i // per_b, 0, 0))
    return pl.pallas_call(
        _prenorm_kernel,
        grid=(t // tm,),
        in_specs=[row, pl.BlockSpec((1, d), lambda i: (0, 0)), bvec, bvec],
        out_specs=row,
        out_shape=jax.ShapeDtypeStruct((t, d), out_dtype),
        compiler_params=_cparams(("arbitrary",)),
        name="prenorm",
    )(x2, g.reshape(1, d), scale.reshape(bsz, 1, d), shift.reshape(bsz, 1, d))


def _rotate(x, cos, sin, period, half):
    lane = lax.broadcasted_iota(I32, x.shape, 1) & (period - 1)
    swapped = jnp.where(lane < half, pltpu.roll(x, LANES - half, 1), pltpu.roll(x, half, 1))
    return x * cos + swapped * sin


def _proj_kernel(*refs, epilogue):
    if epilogue in ("plain", "sigmoid"):
        h_ref, w_ref, o_ref = refs
    else:
        h_ref, w_ref, cos_ref, sin_ref, o_ref = refs
    acc = jnp.dot(h_ref[...], w_ref[...], preferred_element_type=F32)
    if epilogue == "plain":
        o_ref[...] = acc.astype(o_ref.dtype)
    elif epilogue == "sigmoid":
        o_ref[...] = jax.nn.sigmoid(acc).astype(o_ref.dtype)
    else:
        cos = cos_ref[...]
        sin = sin_ref[...]
        if epilogue == "rope_a":
            period, half = A_HEAD_DIM, A_ROT_DIM // 2
        else:
            period, half = IDX_DIM, IDX_ROT_DIM // 2
        if epilogue == "rope_key_idx":
            lane = lax.broadcasted_iota(I32, cos.shape, 1)
            cos = jnp.where(lane < IDX_DIM, cos, 1.0)
            sin = jnp.where(lane < IDX_DIM, sin, 0.0)
        for c in range(acc.shape[1] // LANES):
            sl = slice(c * LANES, (c + 1) * LANES)
            o_ref[:, sl] = _rotate(acc[:, sl], cos, sin, period, half).astype(o_ref.dtype)


def _project(h, w, epilogue, out_dtype, tables=None, tn_pref=TILE["proj_cols"]):
    t, d = h.shape
    n = w.shape[1]
    tm = _tile(t, TILE["proj_rows"])
    tn = n
    for cand in (tn_pref, 256, 128):
        if n % cand == 0:
            tn = cand
            break
    in_specs = [pl.BlockSpec((tm, d), lambda i, j: (i, 0)),
                pl.BlockSpec((d, tn), lambda i, j: (0, j))]
    args = [h, w]
    if tables is not None:
        tab = pl.BlockSpec((tm, LANES), lambda i, j: (i, 0))
        in_specs += [tab, tab]
        args += list(tables)
    return pl.pallas_call(
        functools.partial(_proj_kernel, epilogue=epilogue),
        grid=(t // tm, n // tn),
        in_specs=in_specs,
        out_specs=pl.BlockSpec((tm, tn), lambda i, j: (i, j)),
        out_shape=jax.ShapeDtypeStruct((t, n), out_dtype),
        compiler_params=_cparams(("arbitrary", "arbitrary")),
        name="proj_" + epilogue,
    )(*args)


def _dsa_kernel(qi_ref, kwq_ref, kwk_ref, qa_ref, ka_ref, va_ref, o_ref,
                keys_ref, kib_ref, *, tq, tk, topk):
    i = pl.program_id(1)
    t0 = i * tq
    n_kt = (t0 + tq + tk - 1) // tk
    grp = A_HEADS // A_KV_HEADS
    n_cut_steps = (kwk_ref.shape[0] - 1).bit_length() + 1

    @pl.when(i == 0)
    def _():
        kib_ref[...] = kwk_ref[:, :IDX_DIM].astype(BF16)

    qi = qi_ref[...]
    wq = kwq_ref[...]
    row = t0 + lax.broadcasted_iota(I32, (tq, 1), 0)
    limit = (row // CHUNK + 1) * CHUNK

    def score_tile(kt, carry):
        hi_acc, lo_acc = carry
        k0 = pl.multiple_of(kt * tk, tk)
        kb = kib_ref[pl.ds(k0, tk), :]
        acc = jnp.zeros((tq, tk), F32)
        for h in range(IDX_HEADS):
            d = lax.dot_general(qi[:, h * IDX_DIM:(h + 1) * IDX_DIM], kb,
                                (((1,), (1,)), ((), ())), preferred_element_type=F32)
            acc = acc + jnp.maximum(d, 0.0) * wq[:, IDX_DIM + h:IDX_DIM + h + 1]
        adm = (k0 + lax.broadcasted_iota(I32, (tq, tk), 1)) < limit
        s_hi = jnp.where(adm, acc, -jnp.inf)
        s_lo = jnp.where(adm, acc, jnp.inf)
        keys_ref[kt] = s_hi
        for c in range(tk // LANES):
            hi_acc = jnp.maximum(hi_acc, s_hi[:, c * LANES:(c + 1) * LANES])
            lo_acc = jnp.minimum(lo_acc, s_lo[:, c * LANES:(c + 1) * LANES])
        return hi_acc, lo_acc

    hi_acc, lo_acc = lax.fori_loop(
        0, n_kt, score_tile,
        (jnp.full((tq, LANES), -jnp.inf, F32), jnp.full((tq, LANES), jnp.inf, F32)))
    row_max = jnp.max(hi_acc, axis=1, keepdims=True)
    row_min = jnp.min(lo_acc, axis=1, keepdims=True)

    def count_where(pred):
        def count_tile(kt, cnt):
            hit = pred(keys_ref[kt], kt).astype(I32)
            for c in range(tk // LANES):
                cnt = cnt + hit[:, c * LANES:(c + 1) * LANES]
            return cnt

        cnt = lax.fori_loop(0, n_kt, count_tile, jnp.zeros((tq, LANES), I32))
        return jnp.sum(cnt, axis=1, keepdims=True)

    def bisect_cond(state):
        it, pending = state[0], state[1]
        return jnp.logical_and(it < BISECT_CAP, pending > 0)

    def bisect_step(state):
        it, _, lo, hi, c_lo, tau, surplus, todo = state
        mid = 0.5 * lo + 0.5 * hi
        stuck = jnp.logical_or(mid <= lo, mid >= hi)
        cnt = count_where(lambda s, kt: s >= mid)
        fin = jnp.logical_or(stuck, cnt == topk)
        done_now = jnp.logical_and(todo > 0, fin)
        tau = jnp.where(done_now, jnp.where(stuck, lo, mid), tau)
        surplus = jnp.where(done_now, jnp.where(stuck, c_lo - topk, 0), surplus)
        up = cnt >= topk
        lo = jnp.where(up, mid, lo)
        c_lo = jnp.where(up, cnt, c_lo)
        hi = jnp.where(up, hi, mid)
        todo = jnp.where(fin, 0, todo)
        return it + 1, jnp.max(todo), lo, hi, c_lo, tau, surplus, todo

    todo0 = jnp.where(limit > topk, 1, 0).astype(I32)
    hi0 = row_max + jnp.maximum(jnp.abs(row_max) * HI_MARGIN, HI_FLOOR)
    state = (jnp.int32(0), jnp.max(todo0), row_min, hi0, limit,
             jnp.full((tq, 1), -F32_MAX, F32), jnp.zeros((tq, 1), I32), todo0)
    _, _, lo, _, c_lo, tau, surplus, todo = lax.while_loop(bisect_cond, bisect_step, state)
    tau = jnp.where(todo > 0, lo, tau)
    surplus = jnp.where(todo > 0, c_lo - topk, surplus)

    def key_pos(kt):
        return kt * tk + lax.broadcasted_iota(I32, (tq, tk), 1)

    @pl.when(jnp.max(surplus) > 0)
    def _():
        need = topk - count_where(lambda s, kt: s > tau)

        def cut_step(_, st):
            lo_c, hi_c = st
            mid = lax.shift_right_logical(lo_c + hi_c, 1)
            kept = count_where(lambda s, kt: jnp.logical_and(s == tau, key_pos(kt) < mid))
            ok = kept >= need
            return jnp.where(ok, lo_c, mid), jnp.where(ok, mid, hi_c)

        _, hi_c = lax.fori_loop(0, n_cut_steps, cut_step, (jnp.zeros((tq, 1), I32), limit))
        cutoff = jnp.where(surplus > 0, hi_c, INT_MAX)

        def strike(kt, c):
            s = keys_ref[kt]
            drop = jnp.logical_and(s == tau, key_pos(kt) >= cutoff)
            keys_ref[kt] = jnp.where(drop, -jnp.inf, s)
            return c

        lax.fori_loop(0, n_kt, strike, 0)

    qscale = A_HEAD_DIM ** -0.5 * LOG2_E
    qs = [jnp.concatenate(
        [(qa_ref[:, (j * grp + g) * A_HEAD_DIM:(j * grp + g + 1) * A_HEAD_DIM].astype(F32)
          * qscale).astype(BF16) for g in range(grp)], axis=0)
        for j in range(A_KV_HEADS)]

    def attn_tile(kt, carry):
        k0 = pl.multiple_of(kt * tk, tk)
        bias = jnp.where(keys_ref[kt] >= tau, 0.0, NEG_BIG)
        out = []
        for j in range(A_KV_HEADS):
            m, l, acc = carry[3 * j:3 * j + 3]
            kk = ka_ref[pl.ds(k0, tk), j * A_HEAD_DIM:(j + 1) * A_HEAD_DIM]
            vv = va_ref[pl.ds(k0, tk), j * A_HEAD_DIM:(j + 1) * A_HEAD_DIM]
            s = lax.dot_general(qs[j], kk, (((1,), (1,)), ((), ())),
                                preferred_element_type=F32)
            s = (s.reshape(grp, tq, tk) + bias[None]).reshape(grp * tq, tk)
            m_new = jnp.maximum(m, jnp.max(s, axis=1, keepdims=True))
            alpha = jnp.exp2(m - m_new)
            p = jnp.exp2(s - m_new)
            l = alpha * l + jnp.sum(p, axis=1, keepdims=True)
            acc = alpha * acc + jnp.dot(p.astype(BF16), vv, preferred_element_type=F32)
            out += [m_new, l, acc]
        return tuple(out)

    init = []
    for j in range(A_KV_HEADS):
        init += [jnp.full((grp * tq, 1), NEG_BIG, F32), jnp.zeros((grp * tq, 1), F32),
                 jnp.zeros((grp * tq, A_HEAD_DIM), F32)]
    fin = lax.fori_loop(0, n_kt, attn_tile, tuple(init))
    for j in range(A_KV_HEADS):
        out = fin[3 * j + 2] / fin[3 * j + 1]
        for g in range(grp):
            h = j * grp + g
            o_ref[:, h * A_HEAD_DIM:(h + 1) * A_HEAD_DIM] = out[g * tq:(g + 1) * tq].astype(o_ref.dtype)


def _dsa_mixer(z_a, z_p, z_i, z_k, bsz, seq):
    t = bsz * seq
    tq = _tile(seq, TILE["dsa_queries"])
    tk = _tile(seq, TILE["dsa_keys"])
    nq = seq // tq
    topk = min(TOPK_MAX, seq // 4)
    return pl.pallas_call(
        functools.partial(_dsa_kernel, tq=tq, tk=tk, topk=topk),
        grid=(bsz, nq),
        in_specs=[
            pl.BlockSpec((tq, IDX_Q_W), lambda b, i: (b * nq + i, 0)),
            pl.BlockSpec((tq, LANES), lambda b, i: (b * nq + i, 0)),
            pl.BlockSpec((seq, LANES), lambda b, i: (b, 0)),
            pl.BlockSpec((tq, A_Q_W), lambda b, i: (b * nq + i, 0)),
            pl.BlockSpec((seq, A_KV_W), lambda b, i: (b, A_Q_W // A_KV_W)),
            pl.BlockSpec((seq, A_KV_W), lambda b, i: (b, 0)),
        ],
        out_specs=pl.BlockSpec((tq, A_Q_W), lambda b, i: (b * nq + i, 0)),
        out_shape=jax.ShapeDtypeStruct((t, A_Q_W), BF16),
        scratch_shapes=[pltpu.VMEM((seq // tk, tq, tk), F32),
                        pltpu.VMEM((seq, IDX_DIM), BF16)],
        compiler_params=_cparams(("arbitrary", "arbitrary")),
        name="dsa_mixer",
    )(z_i, z_k, z_k, z_a, z_a, z_p)


def _stick_kernel(q_ref, k_ref, v_ref, o_ref, *, tq, tk, heads):
    i = pl.program_id(2)
    d_idx = (i * tq) // tk
    hd = B_HEAD_DIM
    scale = hd ** -0.5
    qs = [(q_ref[:, h * hd:(h + 1) * hd].astype(F32) * scale).astype(BF16) for h in range(heads)]
    r_io = lax.broadcasted_iota(I32, (tk, tk), 0)
    c_io = lax.broadcasted_iota(I32, (tk, tk), 1)
    upper = jnp.where(r_io > c_io, 1.0, 0.0).astype(BF16)
    diagonal = (d_idx * tk + lax.broadcasted_iota(I32, (tq, tk), 1)
                < i * tq + lax.broadcasted_iota(I32, (tq, tk), 0))

    def scores(kt):
        k0 = pl.multiple_of(kt * tk, tk)
        return tuple(lax.dot_general(qs[h], k_ref[pl.ds(k0, tk), h * hd:(h + 1) * hd],
                                     (((1,), (1,)), ((), ())), preferred_element_type=F32)
                     for h in range(heads))

    def front(z, causal):
        soft = jnp.log(1.0 + jnp.exp(-jnp.abs(z)))
        log_beta = jnp.minimum(z, 0.0) - soft
        log_keep = log_beta - z
        if causal is not None:
            log_keep = jnp.where(causal, log_keep, 0.0)
        hi = log_keep.astype(BF16)
        lo = (log_keep - hi.astype(F32)).astype(BF16)
        gap = (jnp.dot(hi, upper, preferred_element_type=F32)
               + jnp.dot(lo, upper, preferred_element_type=F32))
        logit = log_beta + gap
        if causal is not None:
            logit = jnp.where(causal, logit, NEG_BIG)
        return logit, jnp.sum(log_keep, axis=1, keepdims=True)

    def back(kt, pend, carry, gate=None):
        k0 = pl.multiple_of(kt * tk, tk)
        out = []
        for h in range(heads):
            logit, keep_sum = pend[2 * h], pend[2 * h + 1]
            run, acc = carry[2 * h], carry[2 * h + 1]
            if gate is None:
                attn = jnp.exp(logit + run)
            else:
                attn = jnp.exp(logit + (run + gate[0]))
                keep_sum = keep_sum * gate[1]
            vv = v_ref[pl.ds(k0, tk), h * hd:(h + 1) * hd]
            out += [run + keep_sum, acc + jnp.dot(attn.astype(BF16), vv, preferred_element_type=F32)]
        return tuple(out)

    def fronts(zs, causal):
        out = []
        for h in range(heads):
            out.extend(front(zs[h], causal))
        return tuple(out)

    def top_run(carry):
        top = carry[0]
        for h in range(1, heads):
            top = jnp.maximum(top, carry[2 * h])
        return jnp.max(top)

    init = []
    for h in range(heads):
        init += [jnp.zeros((tq, 1), F32), jnp.zeros((tq, hd), F32)]
    has_prev = d_idx >= 1
    prev = jnp.maximum(d_idx - 1, 0)
    pend_diag = fronts(scores(d_idx), diagonal)
    pend_prev = fronts(scores(prev), None)
    zs = scores(jnp.maximum(d_idx - 2, 0))
    carry = back(d_idx, pend_diag, tuple(init))
    carry = back(prev, pend_prev, carry,
                 gate=(jnp.where(has_prev, 0.0, NEG_BIG), jnp.where(has_prev, 1.0, 0.0)))

    def step(state):
        n, _, zs, carry = state
        kt = d_idx - 2 - n
        zs_next = scores(jnp.maximum(kt - 1, 0))
        carry = back(kt, fronts(zs, None), carry)
        return n + 1, top_run(carry), zs_next, carry

    def alive(state):
        return jnp.logical_and(state[0] < d_idx - 1, state[1] > STICK_DEAD)

    _, _, _, carry = lax.while_loop(alive, step, (jnp.int32(0), top_run(carry), zs, carry))
    for h in range(heads):
        o_ref[:, h * hd:(h + 1) * hd] = carry[2 * h + 1].astype(o_ref.dtype)


def _stick_mixer(z_p, bsz, seq):
    t = bsz * seq
    tq = _tile(seq, TILE["stick_queries"])
    tk = _tile(seq, TILE["stick_keys"])
    nq = seq // tq
    heads = 2
    width = heads * B_HEAD_DIM
    assert A_KV_W % width == 0 and B_W % width == 0
    q_off = A_KV_W // width
    k_off = q_off + B_W // width
    v_off = k_off + B_W // width
    return pl.pallas_call(
        functools.partial(_stick_kernel, tq=tq, tk=tk, heads=heads),
        grid=(bsz, B_HEADS // heads, nq),
        in_specs=[
            pl.BlockSpec((tq, width), lambda b, h, i: (b * nq + i, q_off + h)),
            pl.BlockSpec((seq, width), lambda b, h, i: (b, k_off + h)),
            pl.BlockSpec((seq, width), lambda b, h, i: (b, v_off + h)),
        ],
        out_specs=pl.BlockSpec((tq, width), lambda b, h, i: (b * nq + i, h)),
        out_shape=jax.ShapeDtypeStruct((t, B_W), BF16),
        compiler_params=_cparams(("arbitrary", "arbitrary", "arbitrary")),
        name="stick_mixer",
    )(z_p, z_p, z_p)


def _merge_kernel(oa_ref, ob_ref, g_ref, x_ref, wa_ref, wb_ref, wo_ref, gpost_ref, ga_ref,
                  gpre_ref, sc_ref, sh_ref, o_ref, h_ref, hp_ref):
    d = x_ref.shape[1]
    a = jnp.dot(oa_ref[...], wa_ref[...], preferred_element_type=F32)
    b = jnp.dot(ob_ref[...], wb_ref[...], preferred_element_type=F32)
    merged = g_ref[:, :d].astype(F32) * a + g_ref[:, d:].astype(F32) * b
    mix = jnp.dot(merged.astype(BF16), wo_ref[...], preferred_element_type=F32)
    y = mix * lax.rsqrt(jnp.mean(mix * mix, axis=-1, keepdims=True) + RMS_EPS)
    x1 = x_ref[...] + ga_ref[0] * (y * gpost_ref[...])
    o_ref[...] = x1
    y2 = x1 * lax.rsqrt(jnp.mean(x1 * x1, axis=-1, keepdims=True) + RMS_EPS)
    h2 = (y2 * gpre_ref[...]) * (1.0 + sc_ref[0]) + sh_ref[0]
    h_ref[...] = h2
    hp_ref[...] = _pack_bf16_pairs(h2)


def _pack_bf16_pairs(x):
    n = x.shape[1] // 2
    bits = pltpu.bitcast(x.astype(BF16).astype(F32), jnp.uint32)
    return lax.shift_right_logical(bits[:, :n], jnp.uint32(16)) | bits[:, n:]


def _unpack_bf16_pairs(w, dtype):
    lo = pltpu.bitcast(lax.shift_left(w, jnp.uint32(16)), F32)
    hi = pltpu.bitcast(w & jnp.uint32(0xFFFF0000), F32)
    return jnp.concatenate([lo, hi], axis=1).astype(dtype)


def _merge(o_a, o_b, z_g, x2, w_a, w_b, w_o, g_post, gate, g_pre, scale, shift, seq):
    t, d = x2.shape
    bsz = gate.shape[0]
    tm = _tile(seq, TILE["merge_rows"])
    per_b = seq // tm
    const = lambda i: (0, 0)
    row = pl.BlockSpec((tm, d), lambda i: (i, 0))
    bvec = pl.BlockSpec((1, 1, d), lambda i: (i // per_b, 0, 0))
    return pl.pallas_call(
        _merge_kernel,
        grid=(t // tm,),
        in_specs=[
            pl.BlockSpec((tm, A_Q_W), lambda i: (i, 0)),
            pl.BlockSpec((tm, B_W), lambda i: (i, 0)),
            pl.BlockSpec((tm, 2 * d), lambda i: (i, 0)),
            row,
            pl.BlockSpec((A_Q_W, d), const),
            pl.BlockSpec((B_W, d), const),
            pl.BlockSpec((d, d), const),
            pl.BlockSpec((1, d), const),
            bvec,
            pl.BlockSpec((1, d), const),
            bvec,
            bvec,
        ],
        out_specs=[row, row, pl.BlockSpec((tm, d // 2), lambda i: (i, 0))],
        out_shape=[jax.ShapeDtypeStruct((t, d), F32), jax.ShapeDtypeStruct((t, d), F32),
                   jax.ShapeDtypeStruct((t, d // 2), jnp.uint32)],
        compiler_params=_cparams(("arbitrary",)),
        name="merge",
    )(o_a, o_b, z_g, x2, w_a, w_b, w_o, g_post.reshape(1, d), gate.reshape(bsz, 1, d),
      g_pre.reshape(1, d), scale.reshape(bsz, 1, d), shift.reshape(bsz, 1, d))


def _router_kernel(h_ref, wr_ref, br_ref, e_ref, w_ref, r_ref, cnt_ref, base_ref):
    i = pl.program_id(0)
    tm = h_ref.shape[0]

    @pl.when(i == 0)
    def _():
        base_ref[...] = jnp.zeros_like(base_ref)

    logits = jnp.dot(h_ref[...], wr_ref[...], preferred_element_type=F32,
                     precision=lax.Precision.HIGHEST) + br_ref[...]
    lane = lax.broadcasted_iota(I32, (tm, LANES), 1)
    work = logits
    picks, vals = [], []
    for _ in range(TOP_K_EXPERTS):
        m = jnp.max(work, axis=1, keepdims=True)
        idx = jnp.min(jnp.where(work == m, lane, LANES), axis=1, keepdims=True)
        picks.append(idx)
        vals.append(m)
        work = jnp.where(lane == idx, -jnp.inf, work)
    exps = [jnp.exp(v - vals[0]) for v in vals]
    den = exps[0]
    for e in exps[1:]:
        den = den + e

    onehot = jnp.zeros((tm, LANES), F32)
    for idx in picks:
        onehot = onehot + jnp.where(lane == idx, 1.0, 0.0)
    r_io = lax.broadcasted_iota(I32, (tm, tm), 0)
    c_io = lax.broadcasted_iota(I32, (tm, tm), 1)
    lower = jnp.where(c_io < r_io, 1.0, 0.0).astype(BF16)
    prefix = jnp.dot(lower, onehot.astype(BF16), preferred_element_type=F32)
    total = prefix + base_ref[0:1, :]

    e_out = jnp.zeros((tm, LANES), I32)
    w_out = jnp.zeros((tm, LANES), F32)
    r_out = jnp.zeros((tm, LANES), I32)
    for k in range(TOP_K_EXPERTS):
        rank = jnp.sum(jnp.where(lane == picks[k], total, 0.0), axis=1, keepdims=True)
        e_out = jnp.where(lane == k, picks[k], e_out)
        w_out = jnp.where(lane == k, exps[k] / den, w_out)
        r_out = jnp.where(lane == k, rank.astype(I32), r_out)
    e_ref[...] = e_out
    w_ref[...] = w_out
    r_ref[...] = r_out
    new_base = base_ref[0:1, :] + jnp.sum(onehot, axis=0, keepdims=True)
    base_ref[...] = jnp.broadcast_to(new_base, base_ref.shape)
    cnt_ref[...] = jnp.broadcast_to(new_base, cnt_ref.shape)


def _route(h2, w_router, b_router):
    t, d = h2.shape
    tm = _tile(t, TILE["router_rows"])
    wr = jnp.zeros((d, LANES), F32).at[:, :N_EXPERTS].set(w_router)
    br = jnp.full((1, LANES), NEG_BIG, F32).at[0, :N_EXPERTS].set(b_router)
    row = pl.BlockSpec((tm, LANES), lambda i: (i, 0))
    return pl.pallas_call(
        _router_kernel,
        grid=(t // tm,),
        in_specs=[pl.BlockSpec((tm, d), lambda i: (i, 0)),
                  pl.BlockSpec((d, LANES), lambda i: (0, 0)),
                  pl.BlockSpec((1, LANES), lambda i: (0, 0))],
        out_specs=[row, row, row, pl.BlockSpec((8, LANES), lambda i: (0, 0))],
        out_shape=[jax.ShapeDtypeStruct((t, LANES), I32),
                   jax.ShapeDtypeStruct((t, LANES), F32),
                   jax.ShapeDtypeStruct((t, LANES), I32),
                   jax.ShapeDtypeStruct((8, LANES), F32)],
        scratch_shapes=[pltpu.VMEM((8, LANES), F32)],
        compiler_params=_cparams(("arbitrary",)),
        name="router",
    )(h2, wr, br)


def _dispatch_kernel(dest_ref, padlo_ref, padlen_ref, nused_ref, h_ref, xs_ref, zero_ref, sem, zsem):
    tm = h_ref.shape[0]

    def pad_copies(e, act):
        first = padlo_ref[e]
        left = padlen_ref[e]
        end = first + left
        bit = MOE_ROWS // 2
        while bit >= SUBLANES:
            take = left & bit
            off = pl.multiple_of(end - bit, bit)

            @pl.when(take != 0)
            def _(off=off, bit=bit):
                act(pltpu.make_async_copy(zero_ref.at[pl.ds(0, bit)], xs_ref.at[pl.ds(off, bit)], zsem))

            end = end - take
            bit //= 2
        for r in range(SUBLANES - 1):
            @pl.when(r < (left & (SUBLANES - 1)))
            def _(r=r):
                act(pltpu.make_async_copy(zero_ref.at[pl.ds(0, 1)], xs_ref.at[pl.ds(first + r, 1)], zsem))

    half = MOE_ROWS // 2
    n_blocks = xs_ref.shape[0] // MOE_ROWS

    def tail_copies(b, act):
        base = pl.multiple_of(b * MOE_ROWS, MOE_ROWS)
        for part in range(2):
            act(pltpu.make_async_copy(zero_ref, xs_ref.at[pl.ds(base + part * half, half)], zsem))

    @pl.when(pl.program_id(0) == 0)
    def _():
        zero_ref[...] = jnp.zeros_like(zero_ref)

        def start(e, c):
            pad_copies(e, lambda cp: cp.start())
            return c

        def start_tail(b, c):
            tail_copies(b, lambda cp: cp.start())
            return c

        lax.fori_loop(0, N_EXPERTS, start, 0)
        lax.fori_loop(nused_ref[0], n_blocks, start_tail, 0)

    def copy(t, k):
        return pltpu.make_async_copy(h_ref.at[pl.ds(t, 1)],
                                     xs_ref.at[pl.ds(dest_ref[0, 0, t * TOP_K_EXPERTS + k], 1)], sem)

    def issue(t, c):
        for k in range(TOP_K_EXPERTS):
            copy(t, k).start()
        return c

    def drain(t, c):
        for k in range(TOP_K_EXPERTS):
            copy(t, k).wait()
        return c

    lax.fori_loop(0, tm, issue, 0)
    lax.fori_loop(0, tm, drain, 0)

    @pl.when(pl.program_id(0) == 0)
    def _():
        def finish(e, c):
            pad_copies(e, lambda cp: cp.wait())
            return c

        def finish_tail(b, c):
            tail_copies(b, lambda cp: cp.wait())
            return c

        lax.fori_loop(0, N_EXPERTS, finish, 0)
        lax.fori_loop(nused_ref[0], n_blocks, finish_tail, 0)


def _dispatch(h2, dest, pad_lo, pad_len, n_used, n_rows):
    t, d = h2.shape
    tm = _tile(t, TILE["dispatch_rows"])
    nt = t // tm
    dest3 = dest.reshape(nt, 1, tm * TOP_K_EXPERTS)
    smem = pl.BlockSpec(memory_space=pltpu.SMEM)
    return pl.pallas_call(
        _dispatch_kernel,
        grid=(nt,),
        in_specs=[pl.BlockSpec((1, 1, tm * TOP_K_EXPERTS), lambda i: (i, 0, 0), memory_space=pltpu.SMEM),
                  smem, smem, smem,
                  pl.BlockSpec((tm, d), lambda i: (i, 0))],
        out_specs=pl.BlockSpec(memory_space=pl.ANY),
        out_shape=jax.ShapeDtypeStruct((n_rows, d), h2.dtype),
        scratch_shapes=[pltpu.VMEM((MOE_ROWS // 2, d), h2.dtype),
                        pltpu.SemaphoreType.DMA(()), pltpu.SemaphoreType.DMA(())],
        compiler_params=pltpu.CompilerParams(dimension_semantics=("arbitrary",),
                                             vmem_limit_bytes=VMEM_LIMIT, has_side_effects=True),
        name="moe_dispatch",
    )(dest3, pad_lo, pad_len, n_used, h2)


def _deinterleave_kernel(w_ref, p_ref, g_ref, l_ref):
    p = p_ref[...]
    width = p.shape[0]
    half = width // 2
    for c in range(w_ref.shape[2] // width):
        w = w_ref[0, :, c * width:(c + 1) * width].astype(BF16)
        r = jnp.dot(w, p, preferred_element_type=F32)
        g_ref[0, :, c * half:(c + 1) * half] = r[:, :half].astype(BF16)
        l_ref[0, :, c * half:(c + 1) * half] = r[:, half:].astype(BF16)


def _deinterleave(w1):
    n_e, d, f2 = w1.shape
    width = 2 * LANES
    src = jnp.arange(width)
    dst = jnp.where(src % 2 == 0, src // 2, LANES + src // 2)
    perm = (dst[:, None] == jnp.arange(width)[None, :]).astype(BF16)
    tr = _tile(d, TILE["deinterleave_rows"])
    out = pl.BlockSpec((1, tr, f2 // 2), lambda e, r: (e, r, 0))
    return pl.pallas_call(
        _deinterleave_kernel,
        grid=(n_e, d // tr),
        in_specs=[pl.BlockSpec((1, tr, f2), lambda e, r: (e, r, 0)),
                  pl.BlockSpec((width, width), lambda e, r: (0, 0))],
        out_specs=[out, out],
        out_shape=[jax.ShapeDtypeStruct((n_e, d, f2 // 2), BF16)] * 2,
        compiler_params=_cparams(("arbitrary", "arbitrary")),
        name="w1_deinterleave",
    )(w1, perm)


def _expert_kernel(be_ref, nu_ref, x_ref, w1g_ref, w1l_ref, b1g_ref, b1l_ref, w2_ref, b2_ref,
                   o_ref, xb_ref, acc_ref, *, nf):
    i = pl.program_id(0)
    j = pl.program_id(1)
    valid = i < nu_ref[0]

    def f_tile(xb):
        glu = jnp.dot(xb, w1g_ref[0], preferred_element_type=F32) + b1g_ref[0]
        lin = jnp.dot(xb, w1l_ref[0], preferred_element_type=F32) + b1l_ref[0]
        glu = jnp.minimum(glu, SWIGLU_LIMIT)
        lin = jnp.clip(lin, -SWIGLU_LIMIT, SWIGLU_LIMIT)
        act = glu * jax.nn.sigmoid(SWIGLU_ALPHA * glu) * (lin + 1.0)
        return jnp.dot(act.astype(BF16), w2_ref[0], preferred_element_type=F32)

    @pl.when(jnp.logical_and(valid, j == 0))
    def _():
        xb = _unpack_bf16_pairs(x_ref[...], BF16)
        if nf == 1:
            o_ref[...] = _pack_bf16_pairs(f_tile(xb) + b2_ref[0])
        else:
            xb_ref[...] = xb
            acc_ref[...] = f_tile(xb)

    if nf > 2:
        @pl.when(jnp.logical_and(valid, jnp.logical_and(j > 0, j < nf - 1)))
        def _():
            acc_ref[...] += f_tile(xb_ref[...])

    if nf > 1:
        @pl.when(jnp.logical_and(valid, j == nf - 1))
        def _():
            o_ref[...] = _pack_bf16_pairs(acc_ref[...] + f_tile(xb_ref[...]) + b2_ref[0])

    @pl.when(jnp.logical_and(jnp.logical_not(valid), j == 0))
    def _():
        o_ref[...] = jnp.zeros_like(o_ref)


def _experts(xs, block_e, n_used, w1g, w1l, b1g, b1l, w2, b2):
    n_rows = xs.shape[0]
    n_e, d, f = w1g.shape
    n_blocks = n_rows // MOE_ROWS
    tf = _tile(f, TILE["expert_hidden"])
    nf = f // tf

    def blk(i, nu):
        return jnp.minimum(i, nu[0] - 1)

    def ftile(i, j, nu):
        return jnp.where(i < nu[0], j, nf - 1)

    grid_spec = pltpu.PrefetchScalarGridSpec(
        num_scalar_prefetch=2,
        grid=(n_blocks, nf),
        in_specs=[
            pl.BlockSpec((MOE_ROWS, d // 2), lambda i, j, be, nu: (blk(i, nu), 0)),
            pl.BlockSpec((1, d, tf), lambda i, j, be, nu: (be[blk(i, nu)], 0, ftile(i, j, nu))),
            pl.BlockSpec((1, d, tf), lambda i, j, be, nu: (be[blk(i, nu)], 0, ftile(i, j, nu))),
            pl.BlockSpec((1, 1, tf), lambda i, j, be, nu: (be[blk(i, nu)], 0, ftile(i, j, nu))),
            pl.BlockSpec((1, 1, tf), lambda i, j, be, nu: (be[blk(i, nu)], 0, ftile(i, j, nu))),
            pl.BlockSpec((1, tf, d), lambda i, j, be, nu: (be[blk(i, nu)], ftile(i, j, nu), 0)),
            pl.BlockSpec((1, 1, d), lambda i, j, be, nu: (be[blk(i, nu)], 0, 0)),
        ],
        out_specs=pl.BlockSpec((MOE_ROWS, d // 2), lambda i, j, be, nu: (i, 0)),
        scratch_shapes=[pltpu.VMEM((MOE_ROWS, d), BF16), pltpu.VMEM((MOE_ROWS, d), F32)],
    )
    return pl.pallas_call(
        functools.partial(_expert_kernel, nf=nf),
        grid_spec=grid_spec,
        out_shape=jax.ShapeDtypeStruct((n_rows, d // 2), jnp.uint32),
        compiler_params=_cparams(("arbitrary", "arbitrary")),
        name="moe_experts",
    )(block_e, n_used, xs, w1g, w1l, b1g, b1l, w2, b2)


def _combine_kernel(dest_ref, tw_ref, x_ref, gpost_ref, ga_ref, ys_ref, o_ref, buf_ref, sem):
    tm = x_ref.shape[0]

    def copy(t, k):
        return pltpu.make_async_copy(ys_ref.at[pl.ds(dest_ref[0, 0, t * TOP_K_EXPERTS + k], 1)],
                                     buf_ref.at[k, pl.ds(t, 1)], sem)

    def issue(t, c):
        for k in range(TOP_K_EXPERTS):
            copy(t, k).start()
        return c

    def drain(t, c):
        for k in range(TOP_K_EXPERTS):
            copy(t, k).wait()
        return c

    lax.fori_loop(0, tm, issue, 0)
    lax.fori_loop(0, tm, drain, 0)

    tw = tw_ref[...]
    f = tw[:, 0:1] * _unpack_bf16_pairs(buf_ref[0], F32)
    for k in range(1, TOP_K_EXPERTS):
        f = f + tw[:, k:k + 1] * _unpack_bf16_pairs(buf_ref[k], F32)
    y = f * lax.rsqrt(jnp.mean(f * f, axis=-1, keepdims=True) + RMS_EPS)
    o_ref[...] = x_ref[...] + ga_ref[0] * (y * gpost_ref[...])


def _combine(ys, dest, top_w, x1, g_post, gate, seq):
    t, d = x1.shape
    bsz = gate.shape[0]
    tm = _tile(seq, TILE["combine_rows"])
    nt = t // tm
    per_b = seq // tm
    dest3 = dest.reshape(nt, 1, tm * TOP_K_EXPERTS)
    return pl.pallas_call(
        _combine_kernel,
        grid=(nt,),
        in_specs=[pl.BlockSpec((1, 1, tm * TOP_K_EXPERTS), lambda i: (i, 0, 0), memory_space=pltpu.SMEM),
                  pl.BlockSpec((tm, LANES), lambda i: (i, 0)),
                  pl.BlockSpec((tm, d), lambda i: (i, 0)),
                  pl.BlockSpec((1, d), lambda i: (0, 0)),
                  pl.BlockSpec((1, 1, d), lambda i: (i // per_b, 0, 0)),
                  pl.BlockSpec(memory_space=pl.ANY)],
        out_specs=pl.BlockSpec((tm, d), lambda i: (i, 0)),
        out_shape=jax.ShapeDtypeStruct((t, d), F32),
        scratch_shapes=[pltpu.VMEM((TOP_K_EXPERTS, tm, d // 2), jnp.uint32), pltpu.SemaphoreType.DMA(())],
        compiler_params=_cparams(("arbitrary",)),
        name="moe_combine",
    )(dest3, top_w, x1, g_post.reshape(1, d), gate.reshape(bsz, 1, d), ys)


def _moe(h2, h2_packed, x1, w_router, b_router, w1, b1, w2, b2, g_post, gate, seq):
    t, d = h2.shape
    top_e, top_w, rank, counts = _route(h2, w_router, b_router)
    top_e = top_e[:, :TOP_K_EXPERTS]
    rank = rank[:, :TOP_K_EXPERTS]
    counts = counts[0, :N_EXPERTS].astype(I32)

    n_blocks = (t * TOP_K_EXPERTS) // MOE_ROWS + N_EXPERTS
    padded = (counts + MOE_ROWS - 1) // MOE_ROWS * MOE_ROWS
    pad_end = jnp.cumsum(padded)
    pad_start = pad_end - padded
    dest = (pad_start[top_e] + rank).astype(I32)
    block_row = jnp.arange(n_blocks, dtype=I32) * MOE_ROWS
    block_e = jnp.minimum(jnp.sum((pad_end[None, :] <= block_row[:, None]).astype(I32), axis=1),
                          N_EXPERTS - 1).astype(I32)
    n_used = (pad_end[-1:] // MOE_ROWS).astype(I32)

    xs = _dispatch(h2_packed, dest, (pad_start + counts).astype(I32), (padded - counts).astype(I32),
                   n_used, n_blocks * MOE_ROWS)
    w1g, w1l = _deinterleave(w1)
    n_e, f2 = b1.shape
    b1g = b1[:, 0::2].reshape(n_e, 1, f2 // 2)
    b1l = b1[:, 1::2].reshape(n_e, 1, f2 // 2)
    ys = _experts(xs, block_e, n_used, w1g, w1l, b1g, b1l, w2.astype(BF16), b2.reshape(n_e, 1, d))
    return _combine(ys, dest, top_w, x1, g_post, gate, seq)


def _layer(x2, c, tables, bsz, seq, w_ada, b_ada, g_pre_mix, g_post_mix, w_in, w_branch_a,
           w_branch_b, w_out, g_pre_ffn, g_post_ffn, w_router, b_router, w1, b1, w2, b2):
    d = x2.shape[1]
    mod = _modulation(c, w_ada, b_ada)
    sh1, sc1, ga1, sh2, sc2, ga2 = [mod[:, k * d:(k + 1) * d] for k in range(6)]
    cos_a, sin_a, cos_i, sin_i = tables

    h = _prenorm(x2, g_pre_mix, sc1, sh1, seq, BF16)
    wb = w_in.astype(BF16)
    o = 0
    w_qka = wb[:, o:o + A_Q_W + A_KV_W]
    o += A_Q_W + A_KV_W
    w_plain = wb[:, o:o + A_KV_W + 3 * B_W]
    o += A_KV_W + 3 * B_W
    w_qi = wb[:, o:o + IDX_Q_W]
    o += IDX_Q_W
    w_kw = jnp.zeros((d, LANES), BF16).at[:, :IDX_DIM + IDX_HEADS].set(wb[:, o:o + IDX_DIM + IDX_HEADS])
    o += IDX_DIM + IDX_HEADS
    w_gate = wb[:, o:o + 2 * d]

    z_a = _project(h, w_qka, "rope_a", BF16, (cos_a, sin_a), tn_pref=TILE["proj_cols_qk"])
    z_p = _project(h, w_plain, "plain", BF16, tn_pref=TILE["proj_cols_plain"])
    z_i = _project(h, w_qi, "rope_idx", BF16, (cos_i, sin_i))
    z_k = _project(h, w_kw, "rope_key_idx", F32, (cos_i, sin_i))
    z_g = _project(h, w_gate, "sigmoid", BF16, tn_pref=TILE["proj_cols_gate"])

    o_a = _dsa_mixer(z_a, z_p, z_i, z_k, bsz, seq)
    o_b = _stick_mixer(z_p, bsz, seq)
    x1, h2, h2_packed = _merge(o_a, o_b, z_g, x2, w_branch_a.astype(BF16), w_branch_b.astype(BF16),
                               w_out.astype(BF16), g_post_mix, ga1, g_pre_ffn, sc2, sh2, seq)

    return _moe(h2, h2_packed, x1, w_router, b_router, w1, b1, w2, b2, g_post_ffn, ga2, seq)


def kernel(x, c, positions, w_ada, b_ada, g_pre_mix, g_post_mix, w_in, w_branch_a, w_branch_b, w_out, g_pre_ffn, g_post_ffn, w_router, b_router, w1, b1, w2, b2):
    bsz, seq, d = x.shape
    x2 = x.reshape(bsz * seq, d)
    tables = _rope_tables(positions)
    for l in range(w_ada.shape[0]):
        x2 = _layer(x2, c, tables, bsz, seq, w_ada[l], b_ada[l], g_pre_mix[l], g_post_mix[l], w_in[l],
                    w_branch_a[l], w_branch_b[l], w_out[l], g_pre_ffn[l], g_post_ffn[l],
                    w_router[l], b_router[l], w1[l], b1[l], w2[l], b2[l])
    return x2.reshape(bsz, seq, d)
```

```python
import functools

import jax
import jax.numpy as jnp
from jax import lax
from jax.experimental import pallas as pl
from jax.experimental.pallas import tpu as pltpu

F32 = jnp.float32
BF16 = jnp.bfloat16
I32 = jnp.int32

CHUNK = 64
RMS_EPS = 1e-6
ROPE_THETA = 500000.0
A_HEADS = 8
A_KV_HEADS = 2
A_HEAD_DIM = 128
A_ROT_DIM = A_HEAD_DIM // 4
IDX_HEADS = 16
IDX_DIM = 64
IDX_ROT_DIM = IDX_DIM // 4
TOPK_MAX = 256
B_HEADS = 8
B_HEAD_DIM = 128
N_EXPERTS = 32
TOP_K_EXPERTS = 4
SWIGLU_LIMIT = 7.0
SWIGLU_ALPHA = 1.702

A_Q_W = A_HEADS * A_HEAD_DIM
A_KV_W = A_KV_HEADS * A_HEAD_DIM
B_W = B_HEADS * B_HEAD_DIM
IDX_Q_W = IDX_HEADS * IDX_DIM

LANES = 128
SUBLANES = 8
VMEM_LIMIT = 56 * 1024 * 1024
NEG_BIG = -1e30
INT_MAX = 2 ** 31 - 1
LOG2_E = 1.4426950408889634
F32_MAX = 3.4028234663852886e38
BISECT_CAP = 320
HI_MARGIN, HI_FLOOR = 1e-6, 1e-30
STICK_DEAD = -106.0
MOE_ROWS = 512

TILE = dict(mod_cols=1024, rope_rows=1024, norm_rows=512, proj_rows=1024, proj_cols=512,
            proj_cols_qk=640, proj_cols_plain=1664, proj_cols_gate=1024, dsa_queries=256, dsa_keys=1024, stick_queries=256, stick_keys=256, merge_rows=256,
            router_rows=512, dispatch_rows=256, deinterleave_rows=256, expert_hidden=1024,
            combine_rows=256)


def _cparams(sem):
    return pltpu.CompilerParams(dimension_semantics=sem, vmem_limit_bytes=VMEM_LIMIT)


def _tile(n, pref):
    t = min(n, pref)
    assert n % t == 0, (n, t)
    return t


def _mod_kernel(c_ref, w_ref, b_ref, o_ref):
    c = c_ref[...]
    s = c * jax.nn.sigmoid(c)
    o_ref[...] = jnp.dot(s, w_ref[...], preferred_element_type=F32,
                         precision=lax.Precision.HIGHEST) + b_ref[...]


def _modulation(c, w_ada, b_ada):
    bsz, d = c.shape
    n = w_ada.shape[1]
    rows = SUBLANES
    c_pad = jnp.zeros((rows, d), F32).at[:bsz].set(c)
    tn = _tile(n, TILE["mod_cols"])
    out = pl.pallas_call(
        _mod_kernel,
        grid=(n // tn,),
        in_specs=[pl.BlockSpec((rows, d), lambda j: (0, 0)),
                  pl.BlockSpec((d, tn), lambda j: (0, j)),
                  pl.BlockSpec((1, tn), lambda j: (0, j))],
        out_specs=pl.BlockSpec((rows, tn), lambda j: (0, j)),
        out_shape=jax.ShapeDtypeStruct((rows, n), F32),
        compiler_params=_cparams(("arbitrary",)),
        name="modulation",
    )(c_pad, w_ada, b_ada.reshape(1, n))
    return out[:bsz]


def _rope_table_kernel(pos_ref, fa_ref, sa_ref, fi_ref, si_ref, ca_o, sa_o, ci_o, si_o):
    pos = pos_ref[...].astype(F32)
    ang_a = pos * fa_ref[...]
    ca_o[...] = jnp.cos(ang_a)
    sa_o[...] = jnp.sin(ang_a) * sa_ref[...]
    ang_i = pos * fi_ref[...]
    ci_o[...] = jnp.cos(ang_i)
    si_o[...] = jnp.sin(ang_i) * si_ref[...]


def _rope_pattern(rot_dim, head_dim):
    half = rot_dim // 2
    inv_freq = ROPE_THETA ** (-jnp.arange(half, dtype=F32) / half)
    freq = jnp.concatenate([inv_freq, inv_freq, jnp.zeros((head_dim - rot_dim,), F32)])
    sign = jnp.concatenate([-jnp.ones((half,), F32), jnp.ones((half,), F32),
                            jnp.zeros((head_dim - rot_dim,), F32)])
    reps = LANES // head_dim
    return jnp.tile(freq, reps).reshape(1, LANES), jnp.tile(sign, reps).reshape(1, LANES)


def _rope_tables(positions):
    t = positions.size
    pos = positions.reshape(t, 1).astype(I32)
    fa, sa = _rope_pattern(A_ROT_DIM, A_HEAD_DIM)
    fi, si = _rope_pattern(IDX_ROT_DIM, IDX_DIM)
    tm = _tile(t, TILE["rope_rows"])
    pat = pl.BlockSpec((1, LANES), lambda i: (0, 0))
    tab = pl.BlockSpec((tm, LANES), lambda i: (i, 0))
    return pl.pallas_call(
        _rope_table_kernel,
        grid=(t // tm,),
        in_specs=[pl.BlockSpec((tm, 1), lambda i: (i, 0)), pat, pat, pat, pat],
        out_specs=[tab, tab, tab, tab],
        out_shape=[jax.ShapeDtypeStruct((t, LANES), F32)] * 4,
        compiler_params=_cparams(("arbitrary",)),
        name="rope_tables",
    )(pos, fa, sa, fi, si)


def _prenorm_kernel(x_ref, g_ref, sc_ref, sh_ref, o_ref):
    x = x_ref[...]
    y = x * lax.rsqrt(jnp.mean(x * x, axis=-1, keepdims=True) + RMS_EPS)
    h = (y * g_ref[...]) * (1.0 + sc_ref[0]) + sh_ref[0]
    o_ref[...] = h.astype(o_ref.dtype)


def _prenorm(x2, g, scale, shift, seq, out_dtype):
    t, d = x2.shape
    bsz = scale.shape[0]
    tm = _tile(seq, TILE["norm_rows"])
    per_b = seq // tm
    row = pl.BlockSpec((tm, d), lambda i: (i, 0))
    bvec = pl.BlockSpec((1, 1, d), lambda i: (i // per_b, 0, 0))
    return pl.pallas_call(
        _prenorm_kernel,
        grid=(t // tm,),
        in_specs=[row, pl.BlockSpec((1, d), lambda i: (0, 0)), bvec, bvec],
        out_specs=row,
        out_shape=jax.ShapeDtypeStruct((t, d), out_dtype),
        compiler_params=_cparams(("arbitrary",)),
        name="prenorm",
    )(x2, g.reshape(1, d), scale.reshape(bsz, 1, d), shift.reshape(bsz, 1, d))


def _rotate(x, cos, sin, period, half):
    lane = lax.broadcasted_iota(I32, x.shape, 1) & (period - 1)
    swapped = jnp.where(lane < half, pltpu.roll(x, LANES - half, 1), pltpu.roll(x, half, 1))
    return x * cos + swapped * sin


def _proj_kernel(*refs, epilogue):
    if epilogue in ("plain", "sigmoid"):
        h_ref, w_ref, o_ref = refs
    else:
        h_ref, w_ref, cos_ref, sin_ref, o_ref = refs
    acc = jnp.dot(h_ref[...], w_ref[...], preferred_element_type=F32)
    if epilogue == "plain":
        o_ref[...] = acc.astype(o_ref.dtype)
    elif epilogue == "sigmoid":
        o_ref[...] = jax.nn.sigmoid(acc).astype(o_ref.dtype)
    else:
        cos = cos_ref[...]
        sin = sin_ref[...]
        if epilogue == "rope_a":
            period, half = A_HEAD_DIM, A_ROT_DIM // 2
        else:
            period, half = IDX_DIM, IDX_ROT_DIM // 2
        if epilogue == "rope_key_idx":
            lane = lax.broadcasted_iota(I32, cos.shape, 1)
            cos = jnp.where(lane < IDX_DIM, cos, 1.0)
            sin = jnp.where(lane < IDX_DIM, sin, 0.0)
        for c in range(acc.shape[1] // LANES):
            sl = slice(c * LANES, (c + 1) * LANES)
            o_ref[:, sl] = _rotate(acc[:, sl], cos, sin, period, half).astype(o_ref.dtype)


def _project(h, w, epilogue, out_dtype, tables=None, tn_pref=TILE["proj_cols"]):
    t, d = h.shape
    n = w.shape[1]
    tm = _tile(t, TILE["proj_rows"])
    tn = n
    for cand in (tn_pref, 256, 128):
        if n % cand == 0:
            tn = cand
            break
    in_specs = [pl.BlockSpec((tm, d), lambda i, j: (i, 0)),
                pl.BlockSpec((d, tn), lambda i, j: (0, j))]
    args = [h, w]
    if tables is not None:
        tab = pl.BlockSpec((tm, LANES), lambda i, j: (i, 0))
        in_specs += [tab, tab]
        args += list(tables)
    return pl.pallas_call(
        functools.partial(_proj_kernel, epilogue=epilogue),
        grid=(t // tm, n // tn),
        in_specs=in_specs,
        out_specs=pl.BlockSpec((tm, tn), lambda i, j: (i, j)),
        out_shape=jax.ShapeDtypeStruct((t, n), out_dtype),
        compiler_params=_cparams(("arbitrary", "arbitrary")),
        name="proj_" + epilogue,
    )(*args)


def _dsa_kernel(qi_ref, kwq_ref, kwk_ref, qa_ref, ka_ref, va_ref, o_ref,
                keys_ref, kib_ref, *, tq, tk, topk):
    i = pl.program_id(1)
    t0 = i * tq
    n_kt = (t0 + tq + tk - 1) // tk
    grp = A_HEADS // A_KV_HEADS
    n_cut_steps = (kwk_ref.shape[0] - 1).bit_length() + 1

    @pl.when(i == 0)
    def _():
        kib_ref[...] = kwk_ref[:, :IDX_DIM].astype(BF16)

    qi = qi_ref[...]
    wq = kwq_ref[...]
    row = t0 + lax.broadcasted_iota(I32, (tq, 1), 0)
    limit = (row // CHUNK + 1) * CHUNK

    def score_tile(kt, carry):
        hi_acc, lo_acc = carry
        k0 = pl.multiple_of(kt * tk, tk)
        kb = kib_ref[pl.ds(k0, tk), :]
        acc = jnp.zeros((tq, tk), F32)
        for h in range(IDX_HEADS):
            d = lax.dot_general(qi[:, h * IDX_DIM:(h + 1) * IDX_DIM], kb,
                                (((1,), (1,)), ((), ())), preferred_element_type=F32)
            acc = acc + jnp.maximum(d, 0.0) * wq[:, IDX_DIM + h:IDX_DIM + h + 1]
        adm = (k0 + lax.broadcasted_iota(I32, (tq, tk), 1)) < limit
        s_hi = jnp.where(adm, acc, -jnp.inf)
        s_lo = jnp.where(adm, acc, jnp.inf)
        keys_ref[kt] = s_hi
        for c in range(tk // LANES):
            hi_acc = jnp.maximum(hi_acc, s_hi[:, c * LANES:(c + 1) * LANES])
            lo_acc = jnp.minimum(lo_acc, s_lo[:, c * LANES:(c + 1) * LANES])
        return hi_acc, lo_acc

    hi_acc, lo_acc = lax.fori_loop(
        0, n_kt, score_tile,
        (jnp.full((tq, LANES), -jnp.inf, F32), jnp.full((tq, LANES), jnp.inf, F32)))
    row_max = jnp.max(hi_acc, axis=1, keepdims=True)
    row_min = jnp.min(lo_acc, axis=1, keepdims=True)

    def count_where(pred):
        def count_tile(kt, cnt):
            hit = pred(keys_ref[kt], kt).astype(I32)
            for c in range(tk // LANES):
                cnt = cnt + hit[:, c * LANES:(c + 1) * LANES]
            return cnt

        cnt = lax.fori_loop(0, n_kt, count_tile, jnp.zeros((tq, LANES), I32))
        return jnp.sum(cnt, axis=1, keepdims=True)

    def bisect_cond(state):
        it, pending = state[0], state[1]
        return jnp.logical_and(it < BISECT_CAP, pending > 0)

    def bisect_step(state):
        it, _, lo, hi, c_lo, tau, surplus, todo = state
        mid = 0.5 * lo + 0.5 * hi
        stuck = jnp.logical_or(mid <= lo, mid >= hi)
        cnt = count_where(lambda s, kt: s >= mid)
        fin = jnp.logical_or(stuck, cnt == topk)
        done_now = jnp.logical_and(todo > 0, fin)
        tau = jnp.where(done_now, jnp.where(stuck, lo, mid), tau)
        surplus = jnp.where(done_now, jnp.where(stuck, c_lo - topk, 0), surplus)
        up = cnt >= topk
        lo = jnp.where(up, mid, lo)
        c_lo = jnp.where(up, cnt, c_lo)
        hi = jnp.where(up, hi, mid)
        todo = jnp.where(fin, 0, todo)
        return it + 1, jnp.max(todo), lo, hi, c_lo, tau, surplus, todo

    todo0 = jnp.where(limit > topk, 1, 0).astype(I32)
    hi0 = row_max + jnp.maximum(jnp.abs(row_max) * HI_MARGIN, HI_FLOOR)
    state = (jnp.int32(0), jnp.max(todo0), row_min, hi0, limit,
             jnp.full((tq, 1), -F32_MAX, F32), jnp.zeros((tq, 1), I32), todo0)
    _, _, lo, _, c_lo, tau, surplus, todo = lax.while_loop(bisect_cond, bisect_step, state)
    tau = jnp.where(todo > 0, lo, tau)
    surplus = jnp.where(todo > 0, c_lo - topk, surplus)

    def key_pos(kt):
        return kt * tk + lax.broadcasted_iota(I32, (tq, tk), 1)

    @pl.when(jnp.max(surplus) > 0)
    def _():
        need = topk - count_where(lambda s, kt: s > tau)

        def cut_step(_, st):
            lo_c, hi_c = st
            mid = lax.shift_right_logical(lo_c + hi_c, 1)
            kept = count_where(lambda s, kt: jnp.logical_and(s == tau, key_pos(kt) < mid))
            ok = kept >= need
            return jnp.where(ok, lo_c, mid), jnp.where(ok, mid, hi_c)

        _, hi_c = lax.fori_loop(0, n_cut_steps, cut_step, (jnp.zeros((tq, 1), I32), limit))
        cutoff = jnp.where(surplus > 0, hi_c, INT_MAX)

        def strike(kt, c):
            s = keys_ref[kt]
            drop = jnp.logical_and(s == tau, key_pos(kt) >= cutoff)
            keys_ref[kt] = jnp.where(drop, -jnp.inf, s)
            return c

        lax.fori_loop(0, n_kt, strike, 0)

    qscale = A_HEAD_DIM ** -0.5 * LOG2_E
    qs = [jnp.concatenate(
        [(qa_ref[:, (j * grp + g) * A_HEAD_DIM:(j * grp + g + 1) * A_HEAD_DIM].astype(F32)
          * qscale).astype(BF16) for g in range(grp)], axis=0)
        for j in range(A_KV_HEADS)]

    def attn_tile(kt, carry):
        k0 = pl.multiple_of(kt * tk, tk)
        bias = jnp.where(keys_ref[kt] >= tau, 0.0, NEG_BIG)
        out = []
        for j in range(A_KV_HEADS):
            m, l, acc = carry[3 * j:3 * j + 3]
            kk = ka_ref[pl.ds(k0, tk), j * A_HEAD_DIM:(j + 1) * A_HEAD_DIM]
            vv = va_ref[pl.ds(k0, tk), j * A_HEAD_DIM:(j + 1) * A_HEAD_DIM]
            s = lax.dot_general(qs[j], kk, (((1,), (1,)), ((), ())),
                                preferred_element_type=F32)
            s = (s.reshape(grp, tq, tk) + bias[None]).reshape(grp * tq, tk)
            m_new = jnp.maximum(m, jnp.max(s, axis=1, keepdims=True))
            alpha = jnp.exp2(m - m_new)
            p = jnp.exp2(s - m_new)
            l = alpha * l + jnp.sum(p, axis=1, keepdims=True)
            acc = alpha * acc + jnp.dot(p.astype(BF16), vv, preferred_element_type=F32)
            out += [m_new, l, acc]
        return tuple(out)

    init = []
    for j in range(A_KV_HEADS):
        init += [jnp.full((grp * tq, 1), NEG_BIG, F32), jnp.zeros((grp * tq, 1), F32),
                 jnp.zeros((grp * tq, A_HEAD_DIM), F32)]
    fin = lax.fori_loop(0, n_kt, attn_tile, tuple(init))
    for j in range(A_KV_HEADS):
        out = fin[3 * j + 2] / fin[3 * j + 1]
        for g in range(grp):
            h = j * grp + g
            o_ref[:, h * A_HEAD_DIM:(h + 1) * A_HEAD_DIM] = out[g * tq:(g + 1) * tq].astype(o_ref.dtype)


def _dsa_mixer(z_a, z_p, z_i, z_k, bsz, seq):
    t = bsz * seq
    tq = _tile(seq, TILE["dsa_queries"])
    tk = _tile(seq, TILE["dsa_keys"])
    nq = seq // tq
    topk = min(TOPK_MAX, seq // 4)
    return pl.pallas_call(
        functools.partial(_dsa_kernel, tq=tq, tk=tk, topk=topk),
        grid=(bsz, nq),
        in_specs=[
            pl.BlockSpec((tq, IDX_Q_W), lambda b, i: (b * nq + i, 0)),
            pl.BlockSpec((tq, LANES), lambda b, i: (b * nq + i, 0)),
            pl.BlockSpec((seq, LANES), lambda b, i: (b, 0)),
            pl.BlockSpec((tq, A_Q_W), lambda b, i: (b * nq + i, 0)),
            pl.BlockSpec((seq, A_KV_W), lambda b, i: (b, A_Q_W // A_KV_W)),
            pl.BlockSpec((seq, A_KV_W), lambda b, i: (b, 0)),
        ],
        out_specs=pl.BlockSpec((tq, A_Q_W), lambda b, i: (b * nq + i, 0)),
        out_shape=jax.ShapeDtypeStruct((t, A_Q_W), BF16),
        scratch_shapes=[pltpu.VMEM((seq // tk, tq, tk), F32),
                        pltpu.VMEM((seq, IDX_DIM), BF16)],
        compiler_params=_cparams(("arbitrary", "arbitrary")),
        name="dsa_mixer",
    )(z_i, z_k, z_k, z_a, z_a, z_p)


def _stick_kernel(q_ref, k_ref, v_ref, o_ref, *, tq, tk, heads):
    i = pl.program_id(2)
    d_idx = (i * tq) // tk
    hd = B_HEAD_DIM
    scale = hd ** -0.5
    qs = [(q_ref[:, h * hd:(h + 1) * hd].astype(F32) * scale).astype(BF16) for h in range(heads)]
    r_io = lax.broadcasted_iota(I32, (tk, tk), 0)
    c_io = lax.broadcasted_iota(I32, (tk, tk), 1)
    upper = jnp.where(r_io > c_io, 1.0, 0.0).astype(BF16)
    diagonal = (d_idx * tk + lax.broadcasted_iota(I32, (tq, tk), 1)
                < i * tq + lax.broadcasted_iota(I32, (tq, tk), 0))

    def scores(kt):
        k0 = pl.multiple_of(kt * tk, tk)
        return tuple(lax.dot_general(qs[h], k_ref[pl.ds(k0, tk), h * hd:(h + 1) * hd],
                                     (((1,), (1,)), ((), ())), preferred_element_type=F32)
                     for h in range(heads))

    def front(z, causal):
        soft = jnp.log(1.0 + jnp.exp(-jnp.abs(z)))
        log_beta = jnp.minimum(z, 0.0) - soft
        log_keep = log_beta - z
        if causal is not None:
            log_keep = jnp.where(causal, log_keep, 0.0)
        hi = log_keep.astype(BF16)
        lo = (log_keep - hi.astype(F32)).astype(BF16)
        gap = (jnp.dot(hi, upper, preferred_element_type=F32)
               + jnp.dot(lo, upper, preferred_element_type=F32))
        logit = log_beta + gap
        if causal is not None:
            logit = jnp.where(causal, logit, NEG_BIG)
        return logit, jnp.sum(log_keep, axis=1, keepdims=True)

    def back(kt, pend, carry, gate=None):
        k0 = pl.multiple_of(kt * tk, tk)
        out = []
        for h in range(heads):
            logit, keep_sum = pend[2 * h], pend[2 * h + 1]
            run, acc = carry[2 * h], carry[2 * h + 1]
            if gate is None:
                attn = jnp.exp(logit + run)
            else:
                attn = jnp.exp(logit + (run + gate[0]))
                keep_sum = keep_sum * gate[1]
            vv = v_ref[pl.ds(k0, tk), h * hd:(h + 1) * hd]
            out += [run + keep_sum, acc + jnp.dot(attn.astype(BF16), vv, preferred_element_type=F32)]
        return tuple(out)

    def fronts(zs, causal):
        out = []
        for h in range(heads):
            out.extend(front(zs[h], causal))
        return tuple(out)

    def top_run(carry):
        top = carry[0]
        for h in range(1, heads):
            top = jnp.maximum(top, carry[2 * h])
        return jnp.max(top)

    init = []
    for h in range(heads):
        init += [jnp.zeros((tq, 1), F32), jnp.zeros((tq, hd), F32)]
    has_prev = d_idx >= 1
    prev = jnp.maximum(d_idx - 1, 0)
    pend_diag = fronts(scores(d_idx), diagonal)
    pend_prev = fronts(scores(prev), None)
    zs = scores(jnp.maximum(d_idx - 2, 0))
    carry = back(d_idx, pend_diag, tuple(init))
    carry = back(prev, pend_prev, carry,
                 gate=(jnp.where(has_prev, 0.0, NEG_BIG), jnp.where(has_prev, 1.0, 0.0)))

    def step(state):
        n, _, zs, carry = state
        kt = d_idx - 2 - n
        zs_next = scores(jnp.maximum(kt - 1, 0))
        carry = back(kt, fronts(zs, None), carry)
        return n + 1, top_run(carry), zs_next, carry

    def alive(state):
        return jnp.logical_and(state[0] < d_idx - 1, state[1] > STICK_DEAD)

    _, _, _, carry = lax.while_loop(alive, step, (jnp.int32(0), top_run(carry), zs, carry))
    for h in range(heads):
        o_ref[:, h * hd:(h + 1) * hd] = carry[2 * h + 1].astype(o_ref.dtype)


def _stick_mixer(z_p, bsz, seq):
    t = bsz * seq
    tq = _tile(seq, TILE["stick_queries"])
    tk = _tile(seq, TILE["stick_keys"])
    nq = seq // tq
    heads = 2
    width = heads * B_HEAD_DIM
    assert A_KV_W % width == 0 and B_W % width == 0
    q_off = A_KV_W // width
    k_off = q_off + B_W // width
    v_off = k_off + B_W // width
    return pl.pallas_call(
        functools.partial(_stick_kernel, tq=tq, tk=tk, heads=heads),
        grid=(bsz, B_HEADS // heads, nq),
        in_specs=[
            pl.BlockSpec((tq, width), lambda b, h, i: (b * nq + i, q_off + h)),
            pl.BlockSpec((seq, width), lambda b, h, i: (b, k_off + h)),
            pl.BlockSpec((seq, width), lambda b, h, i: (b, v_off + h)),
        ],
        out_specs=pl.BlockSpec((tq, width), lambda b, h, i: (b * nq + i, h)),
        out_shape=jax.ShapeDtypeStruct((t, B_W), BF16),
        compiler_params=_cparams(("arbitrary", "arbitrary", "arbitrary")),
        name="stick_mixer",
    )(z_p, z_p, z_p)


def _merge_kernel(oa_ref, ob_ref, g_ref, x_ref, wa_ref, wb_ref, wo_ref, gpost_ref, ga_ref,
                  gpre_ref, sc_ref, sh_ref, o_ref, h_ref):
    d = x_ref.shape[1]
    a = jnp.dot(oa_ref[...], wa_ref[...], preferred_element_type=F32)
    b = jnp.dot(ob_ref[...], wb_ref[...], preferred_element_type=F32)
    merged = g_ref[:, :d].astype(F32) * a + g_ref[:, d:].astype(F32) * b
    mix = jnp.dot(merged.astype(BF16), wo_ref[...], preferred_element_type=F32)
    y = mix * lax.rsqrt(jnp.mean(mix * mix, axis=-1, keepdims=True) + RMS_EPS)
    x1 = x_ref[...] + ga_ref[0] * (y * gpost_ref[...])
    o_ref[...] = x1
    y2 = x1 * lax.rsqrt(jnp.mean(x1 * x1, axis=-1, keepdims=True) + RMS_EPS)
    h_ref[...] = (y2 * gpre_ref[...]) * (1.0 + sc_ref[0]) + sh_ref[0]


def _merge(o_a, o_b, z_g, x2, w_a, w_b, w_o, g_post, gate, g_pre, scale, shift, seq):
    t, d = x2.shape
    bsz = gate.shape[0]
    tm = _tile(seq, TILE["merge_rows"])
    per_b = seq // tm
    const = lambda i: (0, 0)
    row = pl.BlockSpec((tm, d), lambda i: (i, 0))
    bvec = pl.BlockSpec((1, 1, d), lambda i: (i // per_b, 0, 0))
    return pl.pallas_call(
        _merge_kernel,
        grid=(t // tm,),
        in_specs=[
            pl.BlockSpec((tm, A_Q_W), lambda i: (i, 0)),
            pl.BlockSpec((tm, B_W), lambda i: (i, 0)),
            pl.BlockSpec((tm, 2 * d), lambda i: (i, 0)),
            row,
            pl.BlockSpec((A_Q_W, d), const),
            pl.BlockSpec((B_W, d), const),
            pl.BlockSpec((d, d), const),
            pl.BlockSpec((1, d), const),
            bvec,
            pl.BlockSpec((1, d), const),
            bvec,
            bvec,
        ],
        out_specs=[row, row],
        out_shape=[jax.ShapeDtypeStruct((t, d), F32)] * 2,
        compiler_params=_cparams(("arbitrary",)),
        name="merge",
    )(o_a, o_b, z_g, x2, w_a, w_b, w_o, g_post.reshape(1, d), gate.reshape(bsz, 1, d),
      g_pre.reshape(1, d), scale.reshape(bsz, 1, d), shift.reshape(bsz, 1, d))


def _router_kernel(h_ref, whi_ref, wlo_ref, br_ref, e_ref, w_ref, r_ref, cnt_ref, base_ref):
    i = pl.program_id(0)
    tm = h_ref.shape[0]

    @pl.when(i == 0)
    def _():
        base_ref[...] = jnp.zeros_like(base_ref)

    h = h_ref[...]
    h_hi = h.astype(BF16)
    h_lo = (h - h_hi.astype(F32)).astype(BF16)
    w_hi = whi_ref[...]
    logits = (jnp.dot(h_hi, w_hi, preferred_element_type=F32)
              + jnp.dot(h_lo, w_hi, preferred_element_type=F32)
              + jnp.dot(h_hi, wlo_ref[...], preferred_element_type=F32)) + br_ref[...]
    lane = lax.broadcasted_iota(I32, (tm, LANES), 1)
    work = logits
    picks, vals = [], []
    for _ in range(TOP_K_EXPERTS):
        m = jnp.max(work, axis=1, keepdims=True)
        idx = jnp.min(jnp.where(work == m, lane, LANES), axis=1, keepdims=True)
        picks.append(idx)
        vals.append(m)
        work = jnp.where(lane == idx, -jnp.inf, work)
    exps = [jnp.exp(v - vals[0]) for v in vals]
    den = exps[0]
    for e in exps[1:]:
        den = den + e

    onehot = jnp.zeros((tm, LANES), F32)
    for idx in picks:
        onehot = onehot + jnp.where(lane == idx, 1.0, 0.0)
    r_io = lax.broadcasted_iota(I32, (tm, tm), 0)
    c_io = lax.broadcasted_iota(I32, (tm, tm), 1)
    lower = jnp.where(c_io < r_io, 1.0, 0.0).astype(BF16)
    prefix = jnp.dot(lower, onehot.astype(BF16), preferred_element_type=F32)
    total = prefix + base_ref[0:1, :]

    e_out = jnp.zeros((tm, LANES), I32)
    w_out = jnp.zeros((tm, LANES), F32)
    r_out = jnp.zeros((tm, LANES), I32)
    for k in range(TOP_K_EXPERTS):
        rank = jnp.sum(jnp.where(lane == picks[k], total, 0.0), axis=1, keepdims=True)
        e_out = jnp.where(lane == k, picks[k], e_out)
        w_out = jnp.where(lane == k, exps[k] / den, w_out)
        r_out = jnp.where(lane == k, rank.astype(I32), r_out)
    e_ref[...] = e_out
    w_ref[...] = w_out
    r_ref[...] = r_out
    new_base = base_ref[0:1, :] + jnp.sum(onehot, axis=0, keepdims=True)
    base_ref[...] = jnp.broadcast_to(new_base, base_ref.shape)
    cnt_ref[...] = jnp.broadcast_to(new_base, cnt_ref.shape)


def _route(h2, w_router, b_router):
    t, d = h2.shape
    tm = _tile(t, TILE["router_rows"])
    wr = jnp.zeros((d, LANES), F32).at[:, :N_EXPERTS].set(w_router)
    wr_hi = wr.astype(BF16)
    wr_lo = (wr - wr_hi.astype(F32)).astype(BF16)
    br = jnp.full((1, LANES), NEG_BIG, F32).at[0, :N_EXPERTS].set(b_router)
    row = pl.BlockSpec((tm, LANES), lambda i: (i, 0))
    return pl.pallas_call(
        _router_kernel,
        grid=(t // tm,),
        in_specs=[pl.BlockSpec((tm, d), lambda i: (i, 0)),
                  pl.BlockSpec((d, LANES), lambda i: (0, 0)),
                  pl.BlockSpec((d, LANES), lambda i: (0, 0)),
                  pl.BlockSpec((1, LANES), lambda i: (0, 0))],
        out_specs=[row, row, row, pl.BlockSpec((8, LANES), lambda i: (0, 0))],
        out_shape=[jax.ShapeDtypeStruct((t, LANES), I32),
                   jax.ShapeDtypeStruct((t, LANES), F32),
                   jax.ShapeDtypeStruct((t, LANES), I32),
                   jax.ShapeDtypeStruct((8, LANES), F32)],
        scratch_shapes=[pltpu.VMEM((8, LANES), F32)],
        compiler_params=_cparams(("arbitrary",)),
        name="router",
    )(h2, wr_hi, wr_lo, br)


def _dispatch_kernel(dest_ref, padlo_ref, padlen_ref, nused_ref, h_ref, xs_ref, zero_ref, sem, zsem):
    tm = h_ref.shape[0]

    def pad_copies(e, act):
        first = padlo_ref[e]
        left = padlen_ref[e]
        end = first + left
        bit = MOE_ROWS // 2
        while bit >= SUBLANES:
            take = left & bit
            off = pl.multiple_of(end - bit, bit)

            @pl.when(take != 0)
            def _(off=off, bit=bit):
                act(pltpu.make_async_copy(zero_ref.at[pl.ds(0, bit)], xs_ref.at[pl.ds(off, bit)], zsem))

            end = end - take
            bit //= 2
        for r in range(SUBLANES - 1):
            @pl.when(r < (left & (SUBLANES - 1)))
            def _(r=r):
                act(pltpu.make_async_copy(zero_ref.at[pl.ds(0, 1)], xs_ref.at[pl.ds(first + r, 1)], zsem))

    half = MOE_ROWS // 2
    n_blocks = xs_ref.shape[0] // MOE_ROWS

    def tail_copies(b, act):
        base = pl.multiple_of(b * MOE_ROWS, MOE_ROWS)
        for part in range(2):
            act(pltpu.make_async_copy(zero_ref, xs_ref.at[pl.ds(base + part * half, half)], zsem))

    @pl.when(pl.program_id(0) == 0)
    def _():
        zero_ref[...] = jnp.zeros_like(zero_ref)

        def start(e, c):
            pad_copies(e, lambda cp: cp.start())
            return c

        def start_tail(b, c):
            tail_copies(b, lambda cp: cp.start())
            return c

        lax.fori_loop(0, N_EXPERTS, start, 0)
        lax.fori_loop(nused_ref[0], n_blocks, start_tail, 0)

    def copy(t, k):
        return pltpu.make_async_copy(h_ref.at[pl.ds(t, 1)],
                                     xs_ref.at[pl.ds(dest_ref[0, 0, t * TOP_K_EXPERTS + k], 1)], sem)

    def issue(t, c):
        for k in range(TOP_K_EXPERTS):
            copy(t, k).start()
        return c

    def drain(t, c):
        for k in range(TOP_K_EXPERTS):
            copy(t, k).wait()
        return c

    lax.fori_loop(0, tm, issue, 0)
    lax.fori_loop(0, tm, drain, 0)

    @pl.when(pl.program_id(0) == 0)
    def _():
        def finish(e, c):
            pad_copies(e, lambda cp: cp.wait())
            return c

        def finish_tail(b, c):
            tail_copies(b, lambda cp: cp.wait())
            return c

        lax.fori_loop(0, N_EXPERTS, finish, 0)
        lax.fori_loop(nused_ref[0], n_blocks, finish_tail, 0)


def _dispatch(h2, dest, pad_lo, pad_len, n_used, n_rows):
    t, d = h2.shape
    tm = _tile(t, TILE["dispatch_rows"])
    nt = t // tm
    dest3 = dest.reshape(nt, 1, tm * TOP_K_EXPERTS)
    smem = pl.BlockSpec(memory_space=pltpu.SMEM)
    return pl.pallas_call(
        _dispatch_kernel,
        grid=(nt,),
        in_specs=[pl.BlockSpec((1, 1, tm * TOP_K_EXPERTS), lambda i: (i, 0, 0), memory_space=pltpu.SMEM),
                  smem, smem, smem,
                  pl.BlockSpec((tm, d), lambda i: (i, 0))],
        out_specs=pl.BlockSpec(memory_space=pl.ANY),
        out_shape=jax.ShapeDtypeStruct((n_rows, d), h2.dtype),
        scratch_shapes=[pltpu.VMEM((MOE_ROWS // 2, d), h2.dtype),
                        pltpu.SemaphoreType.DMA(()), pltpu.SemaphoreType.DMA(())],
        compiler_params=pltpu.CompilerParams(dimension_semantics=("arbitrary",),
                                             vmem_limit_bytes=VMEM_LIMIT, has_side_effects=True),
        name="moe_dispatch",
    )(dest3, pad_lo, pad_len, n_used, h2)


def _deinterleave_kernel(w_ref, p_ref, g_ref, l_ref):
    p = p_ref[...]
    width = p.shape[0]
    half = width // 2
    for c in range(w_ref.shape[2] // width):
        w = w_ref[0, :, c * width:(c + 1) * width].astype(BF16)
        r = jnp.dot(w, p, preferred_element_type=F32)
        g_ref[0, :, c * half:(c + 1) * half] = r[:, :half].astype(BF16)
        l_ref[0, :, c * half:(c + 1) * half] = r[:, half:].astype(BF16)


def _deinterleave(w1):
    n_e, d, f2 = w1.shape
    width = 2 * LANES
    src = jnp.arange(width)
    dst = jnp.where(src % 2 == 0, src // 2, LANES + src // 2)
    perm = (dst[:, None] == jnp.arange(width)[None, :]).astype(BF16)
    tr = _tile(d, TILE["deinterleave_rows"])
    out = pl.BlockSpec((1, tr, f2 // 2), lambda e, r: (e, r, 0))
    return pl.pallas_call(
        _deinterleave_kernel,
        grid=(n_e, d // tr),
        in_specs=[pl.BlockSpec((1, tr, f2), lambda e, r: (e, r, 0)),
                  pl.BlockSpec((width, width), lambda e, r: (0, 0))],
        out_specs=[out, out],
        out_shape=[jax.ShapeDtypeStruct((n_e, d, f2 // 2), BF16)] * 2,
        compiler_params=_cparams(("arbitrary", "arbitrary")),
        name="w1_deinterleave",
    )(w1, perm)


def _expert_kernel(be_ref, nu_ref, x_ref, w1g_ref, w1l_ref, b1g_ref, b1l_ref, w2_ref, b2_ref,
                   o_ref, xb_ref, acc_ref, *, nf):
    i = pl.program_id(0)
    j = pl.program_id(1)
    valid = i < nu_ref[0]

    def f_tile(xb):
        glu = jnp.dot(xb, w1g_ref[0], preferred_element_type=F32) + b1g_ref[0]
        lin = jnp.dot(xb, w1l_ref[0], preferred_element_type=F32) + b1l_ref[0]
        glu = jnp.minimum(glu, SWIGLU_LIMIT)
        lin = jnp.clip(lin, -SWIGLU_LIMIT, SWIGLU_LIMIT)
        act = glu * jax.nn.sigmoid(SWIGLU_ALPHA * glu) * (lin + 1.0)
        return jnp.dot(act.astype(BF16), w2_ref[0], preferred_element_type=F32)

    @pl.when(jnp.logical_and(valid, j == 0))
    def _():
        xb = x_ref[...].astype(BF16)
        if nf == 1:
            o_ref[...] = f_tile(xb) + b2_ref[0]
        else:
            xb_ref[...] = xb
            acc_ref[...] = f_tile(xb)

    if nf > 2:
        @pl.when(jnp.logical_and(valid, jnp.logical_and(j > 0, j < nf - 1)))
        def _():
            acc_ref[...] += f_tile(xb_ref[...])

    if nf > 1:
        @pl.when(jnp.logical_and(valid, j == nf - 1))
        def _():
            o_ref[...] = acc_ref[...] + f_tile(xb_ref[...]) + b2_ref[0]

    @pl.when(jnp.logical_and(jnp.logical_not(valid), j == 0))
    def _():
        o_ref[...] = jnp.zeros_like(o_ref)


def _experts(xs, block_e, n_used, w1g, w1l, b1g, b1l, w2, b2):
    n_rows, d = xs.shape
    n_e, _, f = w1g.shape
    n_blocks = n_rows // MOE_ROWS
    tf = _tile(f, TILE["expert_hidden"])
    nf = f // tf

    def blk(i, nu):
        return jnp.minimum(i, nu[0] - 1)

    def ftile(i, j, nu):
        return jnp.where(i < nu[0], j, nf - 1)

    grid_spec = pltpu.PrefetchScalarGridSpec(
        num_scalar_prefetch=2,
        grid=(n_blocks, nf),
        in_specs=[
            pl.BlockSpec((MOE_ROWS, d), lambda i, j, be, nu: (blk(i, nu), 0)),
            pl.BlockSpec((1, d, tf), lambda i, j, be, nu: (be[blk(i, nu)], 0, ftile(i, j, nu))),
            pl.BlockSpec((1, d, tf), lambda i, j, be, nu: (be[blk(i, nu)], 0, ftile(i, j, nu))),
            pl.BlockSpec((1, 1, tf), lambda i, j, be, nu: (be[blk(i, nu)], 0, ftile(i, j, nu))),
            pl.BlockSpec((1, 1, tf), lambda i, j, be, nu: (be[blk(i, nu)], 0, ftile(i, j, nu))),
            pl.BlockSpec((1, tf, d), lambda i, j, be, nu: (be[blk(i, nu)], ftile(i, j, nu), 0)),
            pl.BlockSpec((1, 1, d), lambda i, j, be, nu: (be[blk(i, nu)], 0, 0)),
        ],
        out_specs=pl.BlockSpec((MOE_ROWS, d), lambda i, j, be, nu: (i, 0)),
        scratch_shapes=[pltpu.VMEM((MOE_ROWS, d), BF16), pltpu.VMEM((MOE_ROWS, d), F32)],
    )
    return pl.pallas_call(
        functools.partial(_expert_kernel, nf=nf),
        grid_spec=grid_spec,
        out_shape=jax.ShapeDtypeStruct((n_rows, d), F32),
        compiler_params=_cparams(("arbitrary", "arbitrary")),
        name="moe_experts",
    )(block_e, n_used, xs, w1g, w1l, b1g, b1l, w2, b2)


def _combine_kernel(dest_ref, tw_ref, x_ref, gpost_ref, ga_ref, ys_ref, o_ref, buf_ref, sem):
    tm = x_ref.shape[0]

    def copy(t, k):
        return pltpu.make_async_copy(ys_ref.at[pl.ds(dest_ref[0, 0, t * TOP_K_EXPERTS + k], 1)],
                                     buf_ref.at[k, pl.ds(t, 1)], sem)

    def issue(t, c):
        for k in range(TOP_K_EXPERTS):
            copy(t, k).start()
        return c

    def drain(t, c):
        for k in range(TOP_K_EXPERTS):
            copy(t, k).wait()
        return c

    lax.fori_loop(0, tm, issue, 0)
    lax.fori_loop(0, tm, drain, 0)

    tw = tw_ref[...]
    f = tw[:, 0:1] * buf_ref[0]
    for k in range(1, TOP_K_EXPERTS):
        f = f + tw[:, k:k + 1] * buf_ref[k]
    y = f * lax.rsqrt(jnp.mean(f * f, axis=-1, keepdims=True) + RMS_EPS)
    o_ref[...] = x_ref[...] + ga_ref[0] * (y * gpost_ref[...])


def _combine(ys, dest, top_w, x1, g_post, gate, seq):
    t, d = x1.shape
    bsz = gate.shape[0]
    tm = _tile(seq, TILE["combine_rows"])
    nt = t // tm
    per_b = seq // tm
    dest3 = dest.reshape(nt, 1, tm * TOP_K_EXPERTS)
    return pl.pallas_call(
        _combine_kernel,
        grid=(nt,),
        in_specs=[pl.BlockSpec((1, 1, tm * TOP_K_EXPERTS), lambda i: (i, 0, 0), memory_space=pltpu.SMEM),
                  pl.BlockSpec((tm, LANES), lambda i: (i, 0)),
                  pl.BlockSpec((tm, d), lambda i: (i, 0)),
                  pl.BlockSpec((1, d), lambda i: (0, 0)),
                  pl.BlockSpec((1, 1, d), lambda i: (i // per_b, 0, 0)),
                  pl.BlockSpec(memory_space=pl.ANY)],
        out_specs=pl.BlockSpec((tm, d), lambda i: (i, 0)),
        out_shape=jax.ShapeDtypeStruct((t, d), F32),
        scratch_shapes=[pltpu.VMEM((TOP_K_EXPERTS, tm, d), F32), pltpu.SemaphoreType.DMA(())],
        compiler_params=_cparams(("arbitrary",)),
        name="moe_combine",
    )(dest3, top_w, x1, g_post.reshape(1, d), gate.reshape(bsz, 1, d), ys)


def _moe(h2, x1, w_router, b_router, w1, b1, w2, b2, g_post, gate, seq):
    t, d = h2.shape
    top_e, top_w, rank, counts = _route(h2, w_router, b_router)
    top_e = top_e[:, :TOP_K_EXPERTS]
    rank = rank[:, :TOP_K_EXPERTS]
    counts = counts[0, :N_EXPERTS].astype(I32)

    n_blocks = (t * TOP_K_EXPERTS) // MOE_ROWS + N_EXPERTS
    padded = (counts + MOE_ROWS - 1) // MOE_ROWS * MOE_ROWS
    pad_end = jnp.cumsum(padded)
    pad_start = pad_end - padded
    dest = (pad_start[top_e] + rank).astype(I32)
    block_row = jnp.arange(n_blocks, dtype=I32) * MOE_ROWS
    block_e = jnp.minimum(jnp.sum((pad_end[None, :] <= block_row[:, None]).astype(I32), axis=1),
                          N_EXPERTS - 1).astype(I32)
    n_used = (pad_end[-1:] // MOE_ROWS).astype(I32)

    xs = _dispatch(h2, dest, (pad_start + counts).astype(I32), (padded - counts).astype(I32),
                   n_used, n_blocks * MOE_ROWS)
    w1g, w1l = _deinterleave(w1)
    n_e, f2 = b1.shape
    b1g = b1[:, 0::2].reshape(n_e, 1, f2 // 2)
    b1l = b1[:, 1::2].reshape(n_e, 1, f2 // 2)
    ys = _experts(xs, block_e, n_used, w1g, w1l, b1g, b1l, w2.astype(BF16), b2.reshape(n_e, 1, d))
    return _combine(ys, dest, top_w, x1, g_post, gate, seq)


def _layer(x2, c, tables, bsz, seq, w_ada, b_ada, g_pre_mix, g_post_mix, w_in, w_branch_a,
           w_branch_b, w_out, g_pre_ffn, g_post_ffn, w_router, b_router, w1, b1, w2, b2):
    d = x2.shape[1]
    mod = _modulation(c, w_ada, b_ada)
    sh1, sc1, ga1, sh2, sc2, ga2 = [mod[:, k * d:(k + 1) * d] for k in range(6)]
    cos_a, sin_a, cos_i, sin_i = tables

    h = _prenorm(x2, g_pre_mix, sc1, sh1, seq, BF16)
    wb = w_in.astype(BF16)
    o = 0
    w_qka = wb[:, o:o + A_Q_W + A_KV_W]
    o += A_Q_W + A_KV_W
    w_plain = wb[:, o:o + A_KV_W + 3 * B_W]
    o += A_KV_W + 3 * B_W
    w_qi = wb[:, o:o + IDX_Q_W]
    o += IDX_Q_W
    w_kw = jnp.zeros((d, LANES), BF16).at[:, :IDX_DIM + IDX_HEADS].set(wb[:, o:o + IDX_DIM + IDX_HEADS])
    o += IDX_DIM + IDX_HEADS
    w_gate = wb[:, o:o + 2 * d]

    z_a = _project(h, w_qka, "rope_a", BF16, (cos_a, sin_a), tn_pref=TILE["proj_cols_qk"])
    z_p = _project(h, w_plain, "plain", BF16, tn_pref=TILE["proj_cols_plain"])
    z_i = _project(h, w_qi, "rope_idx", BF16, (cos_i, sin_i))
    z_k = _project(h, w_kw, "rope_key_idx", F32, (cos_i, sin_i))
    z_g = _project(h, w_gate, "sigmoid", BF16, tn_pref=TILE["proj_cols_gate"])

    o_a = _dsa_mixer(z_a, z_p, z_i, z_k, bsz, seq)
    o_b = _stick_mixer(z_p, bsz, seq)
    x1, h2 = _merge(o_a, o_b, z_g, x2, w_branch_a.astype(BF16), w_branch_b.astype(BF16),
                    w_out.astype(BF16), g_post_mix, ga1, g_pre_ffn, sc2, sh2, seq)

    return _moe(h2, x1, w_router, b_router, w1, b1, w2, b2, g_post_ffn, ga2, seq)


def kernel(x, c, positions, w_ada, b_ada, g_pre_mix, g_post_mix, w_in, w_branch_a, w_branch_b, w_out, g_pre_ffn, g_post_ffn, w_router, b_router, w1, b1, w2, b2):
    bsz, seq, d = x.shape
    x2 = x.reshape(bsz * seq, d)
    tables = _rope_tables(positions)
    for l in range(w_ada.shape[0]):
        x2 = _layer(x2, c, tables, bsz, seq, w_ada[l], b_ada[l], g_pre_mix[l], g_post_mix[l], w_in[l],
                    w_branch_a[l], w_branch_b[l], w_out[l], g_pre_ffn[l], g_post_ffn[l],
                    w_router[l], b_router[l], w1[l], b1[l], w2[l], b2[l])
    return x2.reshape(bsz, seq, d)
```

```python
import functools

import jax
import jax.numpy as jnp
from jax import lax
from jax.experimental import pallas as pl
from jax.experimental.pallas import tpu as pltpu

F32 = jnp.float32
BF16 = jnp.bfloat16
I32 = jnp.int32

CHUNK = 64
RMS_EPS = 1e-6
ROPE_THETA = 500000.0
A_HEADS = 8
A_KV_HEADS = 2
A_HEAD_DIM = 128
A_ROT_DIM = A_HEAD_DIM // 4
IDX_HEADS = 16
IDX_DIM = 64
IDX_ROT_DIM = IDX_DIM // 4
TOPK_MAX = 256
B_HEADS = 8
B_HEAD_DIM = 128
N_EXPERTS = 32
TOP_K_EXPERTS = 4
SWIGLU_LIMIT = 7.0
SWIGLU_ALPHA = 1.702

A_Q_W = A_HEADS * A_HEAD_DIM
A_KV_W = A_KV_HEADS * A_HEAD_DIM
B_W = B_HEADS * B_HEAD_DIM
IDX_Q_W = IDX_HEADS * IDX_DIM

LANES = 128
SUBLANES = 8
VMEM_LIMIT = 56 * 1024 * 1024
NEG_BIG = -1e30
INT_MAX = 2 ** 31 - 1
LOG2_E = 1.4426950408889634
F32_MAX = 3.4028234663852886e38
BISECT_CAP = 320
HI_MARGIN, HI_FLOOR = 1e-6, 1e-30
STICK_DEAD = -106.0
MOE_ROWS = 512

TILE = dict(mod_cols=1024, rope_rows=1024, norm_rows=512, proj_rows=1024, proj_cols=512,
            proj_cols_qk=640, proj_cols_plain=1664, proj_cols_gate=1024, dsa_queries=256, dsa_keys=1024, stick_queries=256, stick_keys=256, merge_rows=256,
            router_rows=512, dispatch_rows=256, deinterleave_rows=256, expert_hidden=1024,
            combine_rows=256)


def _cparams(sem):
    return pltpu.CompilerParams(dimension_semantics=sem, vmem_limit_bytes=VMEM_LIMIT)


def _tile(n, pref):
    t = min(n, pref)
    assert n % t == 0, (n, t)
    return t


def _mod_kernel(c_ref, w_ref, b_ref, o_ref):
    c = c_ref[...]
    s = c * jax.nn.sigmoid(c)
    o_ref[...] = jnp.dot(s, w_ref[...], preferred_element_type=F32,
                         precision=lax.Precision.HIGHEST) + b_ref[...]


def _modulation(c, w_ada, b_ada):
    bsz, d = c.shape
    n = w_ada.shape[1]
    rows = SUBLANES
    c_pad = jnp.zeros((rows, d), F32).at[:bsz].set(c)
    tn = _tile(n, TILE["mod_cols"])
    out = pl.pallas_call(
        _mod_kernel,
        grid=(n // tn,),
        in_specs=[pl.BlockSpec((rows, d), lambda j: (0, 0)),
                  pl.BlockSpec((d, tn), lambda j: (0, j)),
                  pl.BlockSpec((1, tn), lambda j: (0, j))],
        out_specs=pl.BlockSpec((rows, tn), lambda j: (0, j)),
        out_shape=jax.ShapeDtypeStruct((rows, n), F32),
        compiler_params=_cparams(("arbitrary",)),
        name="modulation",
    )(c_pad, w_ada, b_ada.reshape(1, n))
    return out[:bsz]


def _rope_table_kernel(pos_ref, fa_ref, sa_ref, fi_ref, si_ref, ca_o, sa_o, ci_o, si_o):
    pos = pos_ref[...].astype(F32)
    ang_a = pos * fa_ref[...]
    ca_o[...] = jnp.cos(ang_a)
    sa_o[...] = jnp.sin(ang_a) * sa_ref[...]
    ang_i = pos * fi_ref[...]
    ci_o[...] = jnp.cos(ang_i)
    si_o[...] = jnp.sin(ang_i) * si_ref[...]


def _rope_pattern(rot_dim, head_dim):
    half = rot_dim // 2
    inv_freq = ROPE_THETA ** (-jnp.arange(half, dtype=F32) / half)
    freq = jnp.concatenate([inv_freq, inv_freq, jnp.zeros((head_dim - rot_dim,), F32)])
    sign = jnp.concatenate([-jnp.ones((half,), F32), jnp.ones((half,), F32),
                            jnp.zeros((head_dim - rot_dim,), F32)])
    reps = LANES // head_dim
    return jnp.tile(freq, reps).reshape(1, LANES), jnp.tile(sign, reps).reshape(1, LANES)


def _rope_tables(positions):
    t = positions.size
    pos = positions.reshape(t, 1).astype(I32)
    fa, sa = _rope_pattern(A_ROT_DIM, A_HEAD_DIM)
    fi, si = _rope_pattern(IDX_ROT_DIM, IDX_DIM)
    tm = _tile(t, TILE["rope_rows"])
    pat = pl.BlockSpec((1, LANES), lambda i: (0, 0))
    tab = pl.BlockSpec((tm, LANES), lambda i: (i, 0))
    return pl.pallas_call(
        _rope_table_kernel,
        grid=(t // tm,),
        in_specs=[pl.BlockSpec((tm, 1), lambda i: (i, 0)), pat, pat, pat, pat],
        out_specs=[tab, tab, tab, tab],
        out_shape=[jax.ShapeDtypeStruct((t, LANES), F32)] * 4,
        compiler_params=_cparams(("arbitrary",)),
        name="rope_tables",
    )(pos, fa, sa, fi, si)


def _prenorm_kernel(x_ref, g_ref, sc_ref, sh_ref, o_ref):
    x = x_ref[...]
    y = x * lax.rsqrt(jnp.mean(x * x, axis=-1, keepdims=True) + RMS_EPS)
    h = (y * g_ref[...]) * (1.0 + sc_ref[0]) + sh_ref[0]
    o_ref[...] = h.astype(o_ref.dtype)


def _prenorm(x2, g, scale, shift, seq, out_dtype):
    t, d = x2.shape
    bsz = scale.shape[0]
    tm = _tile(seq, TILE["norm_rows"])
    per_b = seq // tm
    row = pl.BlockSpec((tm, d), lambda i: (i, 0))
    bvec = pl.BlockSpec((1, 1, d), lambda i: (i // per_b, 0, 0))
    return pl.pallas_call(
        _prenorm_kernel,
        grid=(t // tm,),
        in_specs=[row, pl.BlockSpec((1, d), lambda i: (0, 0)), bvec, bvec],
        out_specs=row,
        out_shape=jax.ShapeDtypeStruct((t, d), out_dtype),
        compiler_params=_cparams(("arbitrary",)),
        name="prenorm",
    )(x2, g.reshape(1, d), scale.reshape(bsz, 1, d), shift.reshape(bsz, 1, d))


def _rotate(x, cos, sin, period, half):
    lane = lax.broadcasted_iota(I32, x.shape, 1) & (period - 1)
    swapped = jnp.where(lane < half, pltpu.roll(x, LANES - half, 1), pltpu.roll(x, half, 1))
    return x * cos + swapped * sin


def _proj_kernel(*refs, epilogue):
    if epilogue in ("plain", "sigmoid"):
        h_ref, w_ref, o_ref = refs
    else:
        h_ref, w_ref, cos_ref, sin_ref, o_ref = refs
    acc = jnp.dot(h_ref[...], w_ref[...], preferred_element_type=F32)
    if epilogue == "plain":
        o_ref[...] = acc.astype(o_ref.dtype)
    elif epilogue == "sigmoid":
        o_ref[...] = jax.nn.sigmoid(acc).astype(o_ref.dtype)
    else:
        cos = cos_ref[...]
        sin = sin_ref[...]
        if epilogue == "rope_a":
            period, half = A_HEAD_DIM, A_ROT_DIM // 2
        else:
            period, half = IDX_DIM, IDX_ROT_DIM // 2
        if epilogue == "rope_key_idx":
            lane = lax.broadcasted_iota(I32, cos.shape, 1)
            cos = jnp.where(lane < IDX_DIM, cos, 1.0)
            sin = jnp.where(lane < IDX_DIM, sin, 0.0)
        for c in range(acc.shape[1] // LANES):
            sl = slice(c * LANES, (c + 1) * LANES)
            o_ref[:, sl] = _rotate(acc[:, sl], cos, sin, period, half).astype(o_ref.dtype)


def _project(h, w, epilogue, out_dtype, tables=None, tn_pref=TILE["proj_cols"]):
    t, d = h.shape
    n = w.shape[1]
    tm = _tile(t, TILE["proj_rows"])
    tn = n
    for cand in (tn_pref, 256, 128):
        if n % cand == 0:
            tn = cand
            break
    in_specs = [pl.BlockSpec((tm, d), lambda i, j: (i, 0)),
                pl.BlockSpec((d, tn), lambda i, j: (0, j))]
    args = [h, w]
    if tables is not None:
        tab = pl.BlockSpec((tm, LANES), lambda i, j: (i, 0))
        in_specs += [tab, tab]
        args += list(tables)
    return pl.pallas_call(
        functools.partial(_proj_kernel, epilogue=epilogue),
        grid=(t // tm, n // tn),
        in_specs=in_specs,
        out_specs=pl.BlockSpec((tm, tn), lambda i, j: (i, j)),
        out_shape=jax.ShapeDtypeStruct((t, n), out_dtype),
        compiler_params=_cparams(("arbitrary", "arbitrary")),
        name="proj_" + epilogue,
    )(*args)


def _dsa_kernel(qi_ref, kwq_ref, kwk_ref, qa_ref, ka_ref, va_ref, o_ref,
                keys_ref, kib_ref, *, tq, tk, topk):
    i = pl.program_id(1)
    t0 = i * tq
    n_kt = (t0 + tq + tk - 1) // tk
    grp = A_HEADS // A_KV_HEADS
    n_cut_steps = (kwk_ref.shape[0] - 1).bit_length() + 1

    @pl.when(i == 0)
    def _():
        kib_ref[...] = kwk_ref[:, :IDX_DIM].astype(BF16)

    qi = qi_ref[...]
    wq = kwq_ref[...]
    row = t0 + lax.broadcasted_iota(I32, (tq, 1), 0)
    limit = (row // CHUNK + 1) * CHUNK

    def score_tile(kt, carry):
        hi_acc, lo_acc = carry
        k0 = pl.multiple_of(kt * tk, tk)
        kb = kib_ref[pl.ds(k0, tk), :]
        acc = jnp.zeros((tq, tk), F32)
        for h in range(IDX_HEADS):
            d = lax.dot_general(qi[:, h * IDX_DIM:(h + 1) * IDX_DIM], kb,
                                (((1,), (1,)), ((), ())), preferred_element_type=F32)
            acc = acc + jnp.maximum(d, 0.0) * wq[:, IDX_DIM + h:IDX_DIM + h + 1]
        adm = (k0 + lax.broadcasted_iota(I32, (tq, tk), 1)) < limit
        s_hi = jnp.where(adm, acc, -jnp.inf)
        s_lo = jnp.where(adm, acc, jnp.inf)
        keys_ref[kt] = s_hi
        for c in range(tk // LANES):
            hi_acc = jnp.maximum(hi_acc, s_hi[:, c * LANES:(c + 1) * LANES])
            lo_acc = jnp.minimum(lo_acc, s_lo[:, c * LANES:(c + 1) * LANES])
        return hi_acc, lo_acc

    hi_acc, lo_acc = lax.fori_loop(
        0, n_kt, score_tile,
        (jnp.full((tq, LANES), -jnp.inf, F32), jnp.full((tq, LANES), jnp.inf, F32)))
    row_max = jnp.max(hi_acc, axis=1, keepdims=True)
    row_min = jnp.min(lo_acc, axis=1, keepdims=True)

    def count_where(pred):
        def count_tile(kt, cnt):
            hit = pred(keys_ref[kt], kt).astype(I32)
            for c in range(tk // LANES):
                cnt = cnt + hit[:, c * LANES:(c + 1) * LANES]
            return cnt

        cnt = lax.fori_loop(0, n_kt, count_tile, jnp.zeros((tq, LANES), I32))
        return jnp.sum(cnt, axis=1, keepdims=True)

    def bisect_cond(state):
        it, pending = state[0], state[1]
        return jnp.logical_and(it < BISECT_CAP, pending > 0)

    def bisect_step(state):
        it, _, lo, hi, c_lo, tau, surplus, todo = state
        mid = 0.5 * lo + 0.5 * hi
        stuck = jnp.logical_or(mid <= lo, mid >= hi)
        cnt = count_where(lambda s, kt: s >= mid)
        fin = jnp.logical_or(stuck, cnt == topk)
        done_now = jnp.logical_and(todo > 0, fin)
        tau = jnp.where(done_now, jnp.where(stuck, lo, mid), tau)
        surplus = jnp.where(done_now, jnp.where(stuck, c_lo - topk, 0), surplus)
        up = cnt >= topk
        lo = jnp.where(up, mid, lo)
        c_lo = jnp.where(up, cnt, c_lo)
        hi = jnp.where(up, hi, mid)
        todo = jnp.where(fin, 0, todo)
        return it + 1, jnp.max(todo), lo, hi, c_lo, tau, surplus, todo

    todo0 = jnp.where(limit > topk, 1, 0).astype(I32)
    hi0 = row_max + jnp.maximum(jnp.abs(row_max) * HI_MARGIN, HI_FLOOR)
    state = (jnp.int32(0), jnp.max(todo0), row_min, hi0, limit,
             jnp.full((tq, 1), -F32_MAX, F32), jnp.zeros((tq, 1), I32), todo0)
    _, _, lo, _, c_lo, tau, surplus, todo = lax.while_loop(bisect_cond, bisect_step, state)
    tau = jnp.where(todo > 0, lo, tau)
    surplus = jnp.where(todo > 0, c_lo - topk, surplus)

    def key_pos(kt):
        return kt * tk + lax.broadcasted_iota(I32, (tq, tk), 1)

    @pl.when(jnp.max(surplus) > 0)
    def _():
        need = topk - count_where(lambda s, kt: s > tau)

        def cut_step(_, st):
            lo_c, hi_c = st
            mid = lax.shift_right_logical(lo_c + hi_c, 1)
            kept = count_where(lambda s, kt: jnp.logical_and(s == tau, key_pos(kt) < mid))
            ok = kept >= need
            return jnp.where(ok, lo_c, mid), jnp.where(ok, mid, hi_c)

        _, hi_c = lax.fori_loop(0, n_cut_steps, cut_step, (jnp.zeros((tq, 1), I32), limit))
        cutoff = jnp.where(surplus > 0, hi_c, INT_MAX)

        def strike(kt, c):
            s = keys_ref[kt]
            drop = jnp.logical_and(s == tau, key_pos(kt) >= cutoff)
            keys_ref[kt] = jnp.where(drop, -jnp.inf, s)
            return c

        lax.fori_loop(0, n_kt, strike, 0)

    qscale = A_HEAD_DIM ** -0.5 * LOG2_E
    qs = [jnp.concatenate(
        [(qa_ref[:, (j * grp + g) * A_HEAD_DIM:(j * grp + g + 1) * A_HEAD_DIM].astype(F32)
          * qscale).astype(BF16) for g in range(grp)], axis=0)
        for j in range(A_KV_HEADS)]

    def attn_tile(kt, carry):
        k0 = pl.multiple_of(kt * tk, tk)
        bias = jnp.where(keys_ref[kt] >= tau, 0.0, NEG_BIG)
        out = []
        for j in range(A_KV_HEADS):
            m, l, acc = carry[3 * j:3 * j + 3]
            kk = ka_ref[pl.ds(k0, tk), j * A_HEAD_DIM:(j + 1) * A_HEAD_DIM]
            vv = va_ref[pl.ds(k0, tk), j * A_HEAD_DIM:(j + 1) * A_HEAD_DIM]
            s = lax.dot_general(qs[j], kk, (((1,), (1,)), ((), ())),
                                preferred_element_type=F32)
            s = (s.reshape(grp, tq, tk) + bias[None]).reshape(grp * tq, tk)
            m_new = jnp.maximum(m, jnp.max(s, axis=1, keepdims=True))
            alpha = jnp.exp2(m - m_new)
            p = jnp.exp2(s - m_new)
            l = alpha * l + jnp.sum(p, axis=1, keepdims=True)
            acc = alpha * acc + jnp.dot(p.astype(BF16), vv, preferred_element_type=F32)
            out += [m_new, l, acc]
        return tuple(out)

    init = []
    for j in range(A_KV_HEADS):
        init += [jnp.full((grp * tq, 1), NEG_BIG, F32), jnp.zeros((grp * tq, 1), F32),
                 jnp.zeros((grp * tq, A_HEAD_DIM), F32)]
    fin = lax.fori_loop(0, n_kt, attn_tile, tuple(init))
    for j in range(A_KV_HEADS):
        out = fin[3 * j + 2] / fin[3 * j + 1]
        for g in range(grp):
            h = j * grp + g
            o_ref[:, h * A_HEAD_DIM:(h + 1) * A_HEAD_DIM] = out[g * tq:(g + 1) * tq].astype(o_ref.dtype)


def _dsa_mixer(z_a, z_p, z_i, z_k, bsz, seq):
    t = bsz * seq
    tq = _tile(seq, TILE["dsa_queries"])
    tk = _tile(seq, TILE["dsa_keys"])
    nq = seq // tq
    topk = min(TOPK_MAX, seq // 4)
    return pl.pallas_call(
        functools.partial(_dsa_kernel, tq=tq, tk=tk, topk=topk),
        grid=(bsz, nq),
        in_specs=[
            pl.BlockSpec((tq, IDX_Q_W), lambda b, i: (b * nq + i, 0)),
            pl.BlockSpec((tq, LANES), lambda b, i: (b * nq + i, 0)),
            pl.BlockSpec((seq, LANES), lambda b, i: (b, 0)),
            pl.BlockSpec((tq, A_Q_W), lambda b, i: (b * nq + i, 0)),
            pl.BlockSpec((seq, A_KV_W), lambda b, i: (b, A_Q_W // A_KV_W)),
            pl.BlockSpec((seq, A_KV_W), lambda b, i: (b, 0)),
        ],
        out_specs=pl.BlockSpec((tq, A_Q_W), lambda b, i: (b * nq + i, 0)),
        out_shape=jax.ShapeDtypeStruct((t, A_Q_W), BF16),
        scratch_shapes=[pltpu.VMEM((seq // tk, tq, tk), F32),
                        pltpu.VMEM((seq, IDX_DIM), BF16)],
        compiler_params=_cparams(("arbitrary", "arbitrary")),
        name="dsa_mixer",
    )(z_i, z_k, z_k, z_a, z_a, z_p)


def _stick_kernel(q_ref, k_ref, v_ref, o_ref, *, tq, tk, heads):
    i = pl.program_id(2)
    d_idx = (i * tq) // tk
    hd = B_HEAD_DIM
    scale = hd ** -0.5
    qs = [(q_ref[:, h * hd:(h + 1) * hd].astype(F32) * scale).astype(BF16) for h in range(heads)]
    r_io = lax.broadcasted_iota(I32, (tk, tk), 0)
    c_io = lax.broadcasted_iota(I32, (tk, tk), 1)
    upper = jnp.where(r_io > c_io, 1.0, 0.0).astype(BF16)
    diagonal = (d_idx * tk + lax.broadcasted_iota(I32, (tq, tk), 1)
                < i * tq + lax.broadcasted_iota(I32, (tq, tk), 0))

    def scores(kt):
        k0 = pl.multiple_of(kt * tk, tk)
        return tuple(lax.dot_general(qs[h], k_ref[pl.ds(k0, tk), h * hd:(h + 1) * hd],
                                     (((1,), (1,)), ((), ())), preferred_element_type=F32)
                     for h in range(heads))

    def front(z, causal):
        soft = jnp.log(1.0 + jnp.exp(-jnp.abs(z)))
        log_beta = jnp.minimum(z, 0.0) - soft
        log_keep = log_beta - z
        if causal is not None:
            log_keep = jnp.where(causal, log_keep, 0.0)
        hi = log_keep.astype(BF16)
        lo = (log_keep - hi.astype(F32)).astype(BF16)
        gap = (jnp.dot(hi, upper, preferred_element_type=F32)
               + jnp.dot(lo, upper, preferred_element_type=F32))
        logit = log_beta + gap
        if causal is not None:
            logit = jnp.where(causal, logit, NEG_BIG)
        return logit, jnp.sum(log_keep, axis=1, keepdims=True)

    def back(kt, pend, carry, gate=None):
        k0 = pl.multiple_of(kt * tk, tk)
        out = []
        for h in range(heads):
            logit, keep_sum = pend[2 * h], pend[2 * h + 1]
            run, acc = carry[2 * h], carry[2 * h + 1]
            if gate is None:
                attn = jnp.exp(logit + run)
            else:
                attn = jnp.exp(logit + (run + gate[0]))
                keep_sum = keep_sum * gate[1]
            vv = v_ref[pl.ds(k0, tk), h * hd:(h + 1) * hd]
            out += [run + keep_sum, acc + jnp.dot(attn.astype(BF16), vv, preferred_element_type=F32)]
        return tuple(out)

    def fronts(zs, causal):
        out = []
        for h in range(heads):
            out.extend(front(zs[h], causal))
        return tuple(out)

    def top_run(carry):
        top = carry[0]
        for h in range(1, heads):
            top = jnp.maximum(top, carry[2 * h])
        return jnp.max(top)

    init = []
    for h in range(heads):
        init += [jnp.zeros((tq, 1), F32), jnp.zeros((tq, hd), F32)]
    has_prev = d_idx >= 1
    prev = jnp.maximum(d_idx - 1, 0)
    pend_diag = fronts(scores(d_idx), diagonal)
    pend_prev = fronts(scores(prev), None)
    zs = scores(jnp.maximum(d_idx - 2, 0))
    carry = back(d_idx, pend_diag, tuple(init))
    carry = back(prev, pend_prev, carry,
                 gate=(jnp.where(has_prev, 0.0, NEG_BIG), jnp.where(has_prev, 1.0, 0.0)))

    def step(state):
        n, _, zs, carry = state
        kt = d_idx - 2 - n
        zs_next = scores(jnp.maximum(kt - 1, 0))
        carry = back(kt, fronts(zs, None), carry)
        return n + 1, top_run(carry), zs_next, carry

    def alive(state):
        return jnp.logical_and(state[0] < d_idx - 1, state[1] > STICK_DEAD)

    _, _, _, carry = lax.while_loop(alive, step, (jnp.int32(0), top_run(carry), zs, carry))
    for h in range(heads):
        o_ref[:, h * hd:(h + 1) * hd] = carry[2 * h + 1].astype(o_ref.dtype)


def _stick_mixer(z_p, bsz, seq):
    t = bsz * seq
    tq = _tile(seq, TILE["stick_queries"])
    tk = _tile(seq, TILE["stick_keys"])
    nq = seq // tq
    heads = 2
    width = heads * B_HEAD_DIM
    assert A_KV_W % width == 0 and B_W % width == 0
    q_off = A_KV_W // width
    k_off = q_off + B_W // width
    v_off = k_off + B_W // width
    return pl.pallas_call(
        functools.partial(_stick_kernel, tq=tq, tk=tk, heads=heads),
        grid=(bsz, B_HEADS // heads, nq),
        in_specs=[
            pl.BlockSpec((tq, width), lambda b, h, i: (b * nq + i, q_off + h)),
            pl.BlockSpec((seq, width), lambda b, h, i: (b, k_off + h)),
            pl.BlockSpec((seq, width), lambda b, h, i: (b, v_off + h)),
        ],
        out_specs=pl.BlockSpec((tq, width), lambda b, h, i: (b * nq + i, h)),
        out_shape=jax.ShapeDtypeStruct((t, B_W), BF16),
        compiler_params=_cparams(("arbitrary", "arbitrary", "arbitrary")),
        name="stick_mixer",
    )(z_p, z_p, z_p)


def _merge_kernel(oa_ref, ob_ref, g_ref, x_ref, wa_ref, wb_ref, wo_ref, gpost_ref, ga_ref,
                  gpre_ref, sc_ref, sh_ref, o_ref, h_ref):
    d = x_ref.shape[1]
    a = jnp.dot(oa_ref[...], wa_ref[...], preferred_element_type=F32)
    b = jnp.dot(ob_ref[...], wb_ref[...], preferred_element_type=F32)
    merged = g_ref[:, :d].astype(F32) * a + g_ref[:, d:].astype(F32) * b
    mix = jnp.dot(merged.astype(BF16), wo_ref[...], preferred_element_type=F32)
    y = mix * lax.rsqrt(jnp.mean(mix * mix, axis=-1, keepdims=True) + RMS_EPS)
    x1 = x_ref[...] + ga_ref[0] * (y * gpost_ref[...])
    o_ref[...] = x1
    y2 = x1 * lax.rsqrt(jnp.mean(x1 * x1, axis=-1, keepdims=True) + RMS_EPS)
    h_ref[...] = (y2 * gpre_ref[...]) * (1.0 + sc_ref[0]) + sh_ref[0]


def _merge(o_a, o_b, z_g, x2, w_a, w_b, w_o, g_post, gate, g_pre, scale, shift, seq):
    t, d = x2.shape
    bsz = gate.shape[0]
    tm = _tile(seq, TILE["merge_rows"])
    per_b = seq // tm
    const = lambda i: (0, 0)
    row = pl.BlockSpec((tm, d), lambda i: (i, 0))
    bvec = pl.BlockSpec((1, 1, d), lambda i: (i // per_b, 0, 0))
    return pl.pallas_call(
        _merge_kernel,
        grid=(t // tm,),
        in_specs=[
            pl.BlockSpec((tm, A_Q_W), lambda i: (i, 0)),
            pl.BlockSpec((tm, B_W), lambda i: (i, 0)),
            pl.BlockSpec((tm, 2 * d), lambda i: (i, 0)),
            row,
            pl.BlockSpec((A_Q_W, d), const),
            pl.BlockSpec((B_W, d), const),
            pl.BlockSpec((d, d), const),
            pl.BlockSpec((1, d), const),
            bvec,
            pl.BlockSpec((1, d), const),
            bvec,
            bvec,
        ],
        out_specs=[row, row],
        out_shape=[jax.ShapeDtypeStruct((t, d), F32)] * 2,
        compiler_params=_cparams(("arbitrary",)),
        name="merge",
    )(o_a, o_b, z_g, x2, w_a, w_b, w_o, g_post.reshape(1, d), gate.reshape(bsz, 1, d),
      g_pre.reshape(1, d), scale.reshape(bsz, 1, d), shift.reshape(bsz, 1, d))


def _router_kernel(h_ref, wr_ref, br_ref, e_ref, w_ref, r_ref, cnt_ref, base_ref):
    i = pl.program_id(0)
    tm = h_ref.shape[0]

    @pl.when(i == 0)
    def _():
        base_ref[...] = jnp.zeros_like(base_ref)

    h = h_ref[...]
    h_hi = h.astype(BF16)
    h_lo = (h - h_hi.astype(F32)).astype(BF16)
    w = wr_ref[...]
    w_hi = w.astype(BF16)
    w_lo = (w - w_hi.astype(F32)).astype(BF16)
    logits = (jnp.dot(h_hi, w_hi, preferred_element_type=F32)
              + jnp.dot(h_lo, w_hi, preferred_element_type=F32)
              + jnp.dot(h_hi, w_lo, preferred_element_type=F32)) + br_ref[...]
    lane = lax.broadcasted_iota(I32, (tm, LANES), 1)
    work = logits
    picks, vals = [], []
    for _ in range(TOP_K_EXPERTS):
        m = jnp.max(work, axis=1, keepdims=True)
        idx = jnp.min(jnp.where(work == m, lane, LANES), axis=1, keepdims=True)
        picks.append(idx)
        vals.append(m)
        work = jnp.where(lane == idx, -jnp.inf, work)
    exps = [jnp.exp(v - vals[0]) for v in vals]
    den = exps[0]
    for e in exps[1:]:
        den = den + e

    onehot = jnp.zeros((tm, LANES), F32)
    for idx in picks:
        onehot = onehot + jnp.where(lane == idx, 1.0, 0.0)
    r_io = lax.broadcasted_iota(I32, (tm, tm), 0)
    c_io = lax.broadcasted_iota(I32, (tm, tm), 1)
    lower = jnp.where(c_io < r_io, 1.0, 0.0).astype(BF16)
    prefix = jnp.dot(lower, onehot.astype(BF16), preferred_element_type=F32)
    total = prefix + base_ref[0:1, :]

    e_out = jnp.zeros((tm, LANES), I32)
    w_out = jnp.zeros((tm, LANES), F32)
    r_out = jnp.zeros((tm, LANES), I32)
    for k in range(TOP_K_EXPERTS):
        rank = jnp.sum(jnp.where(lane == picks[k], total, 0.0), axis=1, keepdims=True)
        e_out = jnp.where(lane == k, picks[k], e_out)
        w_out = jnp.where(lane == k, exps[k] / den, w_out)
        r_out = jnp.where(lane == k, rank.astype(I32), r_out)
    e_ref[...] = e_out
    w_ref[...] = w_out
    r_ref[...] = r_out
    new_base = base_ref[0:1, :] + jnp.sum(onehot, axis=0, keepdims=True)
    base_ref[...] = jnp.broadcast_to(new_base, base_ref.shape)
    cnt_ref[...] = jnp.broadcast_to(new_base, cnt_ref.shape)


def _route(h2, w_router, b_router):
    t, d = h2.shape
    tm = _tile(t, TILE["router_rows"])
    wr = jnp.zeros((d, LANES), F32).at[:, :N_EXPERTS].set(w_router)
    br = jnp.full((1, LANES), NEG_BIG, F32).at[0, :N_EXPERTS].set(b_router)
    row = pl.BlockSpec((tm, LANES), lambda i: (i, 0))
    return pl.pallas_call(
        _router_kernel,
        grid=(t // tm,),
        in_specs=[pl.BlockSpec((tm, d), lambda i: (i, 0)),
                  pl.BlockSpec((d, LANES), lambda i: (0, 0)),
                  pl.BlockSpec((1, LANES), lambda i: (0, 0))],
        out_specs=[row, row, row, pl.BlockSpec((8, LANES), lambda i: (0, 0))],
        out_shape=[jax.ShapeDtypeStruct((t, LANES), I32),
                   jax.ShapeDtypeStruct((t, LANES), F32),
                   jax.ShapeDtypeStruct((t, LANES), I32),
                   jax.ShapeDtypeStruct((8, LANES), F32)],
        scratch_shapes=[pltpu.VMEM((8, LANES), F32)],
        compiler_params=_cparams(("arbitrary",)),
        name="router",
    )(h2, wr, br)


def _dispatch_kernel(dest_ref, padlo_ref, padlen_ref, nused_ref, h_ref, xs_ref, zero_ref, sem, zsem):
    tm = h_ref.shape[0]

    def pad_copies(e, act):
        first = padlo_ref[e]
        left = padlen_ref[e]
        end = first + left
        bit = MOE_ROWS // 2
        while bit >= SUBLANES:
            take = left & bit
            off = pl.multiple_of(end - bit, bit)

            @pl.when(take != 0)
            def _(off=off, bit=bit):
                act(pltpu.make_async_copy(zero_ref.at[pl.ds(0, bit)], xs_ref.at[pl.ds(off, bit)], zsem))

            end = end - take
            bit //= 2
        for r in range(SUBLANES - 1):
            @pl.when(r < (left & (SUBLANES - 1)))
            def _(r=r):
                act(pltpu.make_async_copy(zero_ref.at[pl.ds(0, 1)], xs_ref.at[pl.ds(first + r, 1)], zsem))

    half = MOE_ROWS // 2
    n_blocks = xs_ref.shape[0] // MOE_ROWS

    def tail_copies(b, act):
        base = pl.multiple_of(b * MOE_ROWS, MOE_ROWS)
        for part in range(2):
            act(pltpu.make_async_copy(zero_ref, xs_ref.at[pl.ds(base + part * half, half)], zsem))

    @pl.when(pl.program_id(0) == 0)
    def _():
        zero_ref[...] = jnp.zeros_like(zero_ref)

        def start(e, c):
            pad_copies(e, lambda cp: cp.start())
            return c

        def start_tail(b, c):
            tail_copies(b, lambda cp: cp.start())
            return c

        lax.fori_loop(0, N_EXPERTS, start, 0)
        lax.fori_loop(nused_ref[0], n_blocks, start_tail, 0)

    def copy(t, k):
        return pltpu.make_async_copy(h_ref.at[pl.ds(t, 1)],
                                     xs_ref.at[pl.ds(dest_ref[0, 0, t * TOP_K_EXPERTS + k], 1)], sem)

    def issue(t, c):
        for k in range(TOP_K_EXPERTS):
            copy(t, k).start()
        return c

    def drain(t, c):
        for k in range(TOP_K_EXPERTS):
            copy(t, k).wait()
        return c

    lax.fori_loop(0, tm, issue, 0)
    lax.fori_loop(0, tm, drain, 0)

    @pl.when(pl.program_id(0) == 0)
    def _():
        def finish(e, c):
            pad_copies(e, lambda cp: cp.wait())
            return c

        def finish_tail(b, c):
            tail_copies(b, lambda cp: cp.wait())
            return c

        lax.fori_loop(0, N_EXPERTS, finish, 0)
        lax.fori_loop(nused_ref[0], n_blocks, finish_tail, 0)


def _dispatch(h2, dest, pad_lo, pad_len, n_used, n_rows):
    t, d = h2.shape
    tm = _tile(t, TILE["dispatch_rows"])
    nt = t // tm
    dest3 = dest.reshape(nt, 1, tm * TOP_K_EXPERTS)
    smem = pl.BlockSpec(memory_space=pltpu.SMEM)
    return pl.pallas_call(
        _dispatch_kernel,
        grid=(nt,),
        in_specs=[pl.BlockSpec((1, 1, tm * TOP_K_EXPERTS), lambda i: (i, 0, 0), memory_space=pltpu.SMEM),
                  smem, smem, smem,
                  pl.BlockSpec((tm, d), lambda i: (i, 0))],
        out_specs=pl.BlockSpec(memory_space=pl.ANY),
        out_shape=jax.ShapeDtypeStruct((n_rows, d), h2.dtype),
        scratch_shapes=[pltpu.VMEM((MOE_ROWS // 2, d), h2.dtype),
                        pltpu.SemaphoreType.DMA(()), pltpu.SemaphoreType.DMA(())],
        compiler_params=pltpu.CompilerParams(dimension_semantics=("arbitrary",),
                                             vmem_limit_bytes=VMEM_LIMIT, has_side_effects=True),
        name="moe_dispatch",
    )(dest3, pad_lo, pad_len, n_used, h2)


def _deinterleave_kernel(w_ref, p_ref, g_ref, l_ref):
    p = p_ref[...]
    width = p.shape[0]
    half = width // 2
    for c in range(w_ref.shape[2] // width):
        w = w_ref[0, :, c * width:(c + 1) * width].astype(BF16)
        r = jnp.dot(w, p, preferred_element_type=F32)
        g_ref[0, :, c * half:(c + 1) * half] = r[:, :half].astype(BF16)
        l_ref[0, :, c * half:(c + 1) * half] = r[:, half:].astype(BF16)


def _deinterleave(w1):
    n_e, d, f2 = w1.shape
    width = 2 * LANES
    src = jnp.arange(width)
    dst = jnp.where(src % 2 == 0, src // 2, LANES + src // 2)
    perm = (dst[:, None] == jnp.arange(width)[None, :]).astype(BF16)
    tr = _tile(d, TILE["deinterleave_rows"])
    out = pl.BlockSpec((1, tr, f2 // 2), lambda e, r: (e, r, 0))
    return pl.pallas_call(
        _deinterleave_kernel,
        grid=(n_e, d // tr),
        in_specs=[pl.BlockSpec((1, tr, f2), lambda e, r: (e, r, 0)),
                  pl.BlockSpec((width, width), lambda e, r: (0, 0))],
        out_specs=[out, out],
        out_shape=[jax.ShapeDtypeStruct((n_e, d, f2 // 2), BF16)] * 2,
        compiler_params=_cparams(("arbitrary", "arbitrary")),
        name="w1_deinterleave",
    )(w1, perm)


def _expert_kernel(be_ref, nu_ref, x_ref, w1g_ref, w1l_ref, b1g_ref, b1l_ref, w2_ref, b2_ref,
                   o_ref, xb_ref, acc_ref, *, nf):
    i = pl.program_id(0)
    j = pl.program_id(1)
    valid = i < nu_ref[0]

    def f_tile(xb):
        glu = jnp.dot(xb, w1g_ref[0], preferred_element_type=F32) + b1g_ref[0]
        lin = jnp.dot(xb, w1l_ref[0], preferred_element_type=F32) + b1l_ref[0]
        glu = jnp.minimum(glu, SWIGLU_LIMIT)
        lin = jnp.clip(lin, -SWIGLU_LIMIT, SWIGLU_LIMIT)
        act = glu * jax.nn.sigmoid(SWIGLU_ALPHA * glu) * (lin + 1.0)
        return jnp.dot(act.astype(BF16), w2_ref[0], preferred_element_type=F32)

    @pl.when(jnp.logical_and(valid, j == 0))
    def _():
        xb = x_ref[...].astype(BF16)
        if nf == 1:
            o_ref[...] = f_tile(xb) + b2_ref[0]
        else:
            xb_ref[...] = xb
            acc_ref[...] = f_tile(xb)

    if nf > 2:
        @pl.when(jnp.logical_and(valid, jnp.logical_and(j > 0, j < nf - 1)))
        def _():
            acc_ref[...] += f_tile(xb_ref[...])

    if nf > 1:
        @pl.when(jnp.logical_and(valid, j == nf - 1))
        def _():
            o_ref[...] = acc_ref[...] + f_tile(xb_ref[...]) + b2_ref[0]

    @pl.when(jnp.logical_and(jnp.logical_not(valid), j == 0))
    def _():
        o_ref[...] = jnp.zeros_like(o_ref)


def _experts(xs, block_e, n_used, w1g, w1l, b1g, b1l, w2, b2):
    n_rows, d = xs.shape
    n_e, _, f = w1g.shape
    n_blocks = n_rows // MOE_ROWS
    tf = _tile(f, TILE["expert_hidden"])
    nf = f // tf

    def blk(i, nu):
        return jnp.minimum(i, nu[0] - 1)

    def ftile(i, j, nu):
        return jnp.where(i < nu[0], j, nf - 1)

    grid_spec = pltpu.PrefetchScalarGridSpec(
        num_scalar_prefetch=2,
        grid=(n_blocks, nf),
        in_specs=[
            pl.BlockSpec((MOE_ROWS, d), lambda i, j, be, nu: (blk(i, nu), 0)),
            pl.BlockSpec((1, d, tf), lambda i, j, be, nu: (be[blk(i, nu)], 0, ftile(i, j, nu))),
            pl.BlockSpec((1, d, tf), lambda i, j, be, nu: (be[blk(i, nu)], 0, ftile(i, j, nu))),
            pl.BlockSpec((1, 1, tf), lambda i, j, be, nu: (be[blk(i, nu)], 0, ftile(i, j, nu))),
            pl.BlockSpec((1, 1, tf), lambda i, j, be, nu: (be[blk(i, nu)], 0, ftile(i, j, nu))),
            pl.BlockSpec((1, tf, d), lambda i, j, be, nu: (be[blk(i, nu)], ftile(i, j, nu), 0)),
            pl.BlockSpec((1, 1, d), lambda i, j, be, nu: (be[blk(i, nu)], 0, 0)),
        ],
        out_specs=pl.BlockSpec((MOE_ROWS, d), lambda i, j, be, nu: (i, 0)),
        scratch_shapes=[pltpu.VMEM((MOE_ROWS, d), BF16), pltpu.VMEM((MOE_ROWS, d), F32)],
    )
    return pl.pallas_call(
        functools.partial(_expert_kernel, nf=nf),
        grid_spec=grid_spec,
        out_shape=jax.ShapeDtypeStruct((n_rows, d), F32),
        compiler_params=_cparams(("arbitrary", "arbitrary")),
        name="moe_experts",
    )(block_e, n_used, xs, w1g, w1l, b1g, b1l, w2, b2)


def _combine_kernel(dest_ref, tw_ref, x_ref, gpost_ref, ga_ref, ys_ref, o_ref, buf_ref, sem):
    tm = x_ref.shape[0]

    def copy(t, k):
        return pltpu.make_async_copy(ys_ref.at[pl.ds(dest_ref[0, 0, t * TOP_K_EXPERTS + k], 1)],
                                     buf_ref.at[k, pl.ds(t, 1)], sem)

    def issue(t, c):
        for k in range(TOP_K_EXPERTS):
            copy(t, k).start()
        return c

    def drain(t, c):
        for k in range(TOP_K_EXPERTS):
            copy(t, k).wait()
        return c

    lax.fori_loop(0, tm, issue, 0)
    lax.fori_loop(0, tm, drain, 0)

    tw = tw_ref[...]
    f = tw[:, 0:1] * buf_ref[0]
    for k in range(1, TOP_K_EXPERTS):
        f = f + tw[:, k:k + 1] * buf_ref[k]
    y = f * lax.rsqrt(jnp.mean(f * f, axis=-1, keepdims=True) + RMS_EPS)
    o_ref[...] = x_ref[...] + ga_ref[0] * (y * gpost_ref[...])


def _combine(ys, dest, top_w, x1, g_post, gate, seq):
    t, d = x1.shape
    bsz = gate.shape[0]
    tm = _tile(seq, TILE["combine_rows"])
    nt = t // tm
    per_b = seq // tm
    dest3 = dest.reshape(nt, 1, tm * TOP_K_EXPERTS)
    return pl.pallas_call(
        _combine_kernel,
        grid=(nt,),
        in_specs=[pl.BlockSpec((1, 1, tm * TOP_K_EXPERTS), lambda i: (i, 0, 0), memory_space=pltpu.SMEM),
                  pl.BlockSpec((tm, LANES), lambda i: (i, 0)),
                  pl.BlockSpec((tm, d), lambda i: (i, 0)),
                  pl.BlockSpec((1, d), lambda i: (0, 0)),
                  pl.BlockSpec((1, 1, d), lambda i: (i // per_b, 0, 0)),
                  pl.BlockSpec(memory_space=pl.ANY)],
        out_specs=pl.BlockSpec((tm, d), lambda i: (i, 0)),
        out_shape=jax.ShapeDtypeStruct((t, d), F32),
        scratch_shapes=[pltpu.VMEM((TOP_K_EXPERTS, tm, d), F32), pltpu.SemaphoreType.DMA(())],
        compiler_params=_cparams(("arbitrary",)),
        name="moe_combine",
    )(dest3, top_w, x1, g_post.reshape(1, d), gate.reshape(bsz, 1, d), ys)


def _moe(h2, x1, w_router, b_router, w1, b1, w2, b2, g_post, gate, seq):
    t, d = h2.shape
    top_e, top_w, rank, counts = _route(h2, w_router, b_router)
    top_e = top_e[:, :TOP_K_EXPERTS]
    rank = rank[:, :TOP_K_EXPERTS]
    counts = counts[0, :N_EXPERTS].astype(I32)

    n_blocks = (t * TOP_K_EXPERTS) // MOE_ROWS + N_EXPERTS
    padded = (counts + MOE_ROWS - 1) // MOE_ROWS * MOE_ROWS
    pad_end = jnp.cumsum(padded)
    pad_start = pad_end - padded
    dest = (pad_start[top_e] + rank).astype(I32)
    block_row = jnp.arange(n_blocks, dtype=I32) * MOE_ROWS
    block_e = jnp.minimum(jnp.sum((pad_end[None, :] <= block_row[:, None]).astype(I32), axis=1),
                          N_EXPERTS - 1).astype(I32)
    n_used = (pad_end[-1:] // MOE_ROWS).astype(I32)

    xs = _dispatch(h2, dest, (pad_start + counts).astype(I32), (padded - counts).astype(I32),
                   n_used, n_blocks * MOE_ROWS)
    w1g, w1l = _deinterleave(w1)
    n_e, f2 = b1.shape
    b1g = b1[:, 0::2].reshape(n_e, 1, f2 // 2)
    b1l = b1[:, 1::2].reshape(n_e, 1, f2 // 2)
    ys = _experts(xs, block_e, n_used, w1g, w1l, b1g, b1l, w2.astype(BF16), b2.reshape(n_e, 1, d))
    return _combine(ys, dest, top_w, x1, g_post, gate, seq)


def _layer(x2, c, tables, bsz, seq, w_ada, b_ada, g_pre_mix, g_post_mix, w_in, w_branch_a,
           w_branch_b, w_out, g_pre_ffn, g_post_ffn, w_router, b_router, w1, b1, w2, b2):
    d = x2.shape[1]
    mod = _modulation(c, w_ada, b_ada)
    sh1, sc1, ga1, sh2, sc2, ga2 = [mod[:, k * d:(k + 1) * d] for k in range(6)]
    cos_a, sin_a, cos_i, sin_i = tables

    h = _prenorm(x2, g_pre_mix, sc1, sh1, seq, BF16)
    wb = w_in.astype(BF16)
    o = 0
    w_qka = wb[:, o:o + A_Q_W + A_KV_W]
    o += A_Q_W + A_KV_W
    w_plain = wb[:, o:o + A_KV_W + 3 * B_W]
    o += A_KV_W + 3 * B_W
    w_qi = wb[:, o:o + IDX_Q_W]
    o += IDX_Q_W
    w_kw = jnp.zeros((d, LANES), BF16).at[:, :IDX_DIM + IDX_HEADS].set(wb[:, o:o + IDX_DIM + IDX_HEADS])
    o += IDX_DIM + IDX_HEADS
    w_gate = wb[:, o:o + 2 * d]

    z_a = _project(h, w_qka, "rope_a", BF16, (cos_a, sin_a), tn_pref=TILE["proj_cols_qk"])
    z_p = _project(h, w_plain, "plain", BF16, tn_pref=TILE["proj_cols_plain"])
    z_i = _project(h, w_qi, "rope_idx", BF16, (cos_i, sin_i))
    z_k = _project(h, w_kw, "rope_key_idx", F32, (cos_i, sin_i))
    z_g = _project(h, w_gate, "sigmoid", BF16, tn_pref=TILE["proj_cols_gate"])

    o_a = _dsa_mixer(z_a, z_p, z_i, z_k, bsz, seq)
    o_b = _stick_mixer(z_p, bsz, seq)
    x1, h2 = _merge(o_a, o_b, z_g, x2, w_branch_a.astype(BF16), w_branch_b.astype(BF16),
                    w_out.astype(BF16), g_post_mix, ga1, g_pre_ffn, sc2, sh2, seq)

    return _moe(h2, x1, w_router, b_router, w1, b1, w2, b2, g_post_ffn, ga2, seq)


def kernel(x, c, positions, w_ada, b_ada, g_pre_mix, g_post_mix, w_in, w_branch_a, w_branch_b, w_out, g_pre_ffn, g_post_ffn, w_router, b_router, w1, b1, w2, b2):
    bsz, seq, d = x.shape
    x2 = x.reshape(bsz * seq, d)
    tables = _rope_tables(positions)
    for l in range(w_ada.shape[0]):
        x2 = _layer(x2, c, tables, bsz, seq, w_ada[l], b_ada[l], g_pre_mix[l], g_post_mix[l], w_in[l],
                    w_branch_a[l], w_branch_b[l], w_out[l], g_pre_ffn[l], g_post_ffn[l],
                    w_router[l], b_router[l], w1[l], b1[l], w2[l], b2[l])
    return x2.reshape(bsz, seq, d)
```
